```python
import math
import jax
import jax.numpy as jnp
from jax import lax
import numpy as np

D_MODEL = 1024
BATCH = 8
SEQ = 4096
DEPTH = 2

CHUNK = 64
N_META = 16
Q_BLOCK = 128
EPS = 1e-6
NEG_INF = -1e30

HEAD_DIM = 64
SB_HEADS = 8
MLA_HEADS = 8
MLA_NOPE = 64
MLA_ROPE = 32
MLA_V = 64
MLA_Q_RANK = 256
MLA_KV_RANK = 128
MLA_THETA = 10000.0
DIFF_HEADS = 4
DIFF_DIM = 64
FOX_HEADS = 8
ROPE_THETA = 500000.0
ROT_DIM = HEAD_DIM // 4
D_FF = 2816
N_EXPERTS = 8
TOP_K = 2
D_FF_EXPERT = 3584

SB_W = SB_HEADS * HEAD_DIM
EVEN_IN = 3 * SB_W + MLA_Q_RANK + MLA_KV_RANK + MLA_ROPE
EVEN_MIX = SB_W + MLA_HEADS * MLA_V
DIFF_W = DIFF_HEADS * 2 * DIFF_DIM
FOX_W = FOX_HEADS * HEAD_DIM
ODD_IN = 3 * DIFF_W + 3 * FOX_W + FOX_HEADS
ODD_MIX = DIFF_W + FOX_W

kernel_name = "hybrid_sb_mla_diff_fox_moe_stream"


def rmsnorm(x, g):
    xf = x.astype(jnp.float32)
    y = xf * lax.rsqrt(jnp.mean(xf * xf, axis=-1, keepdims=True) + EPS)
    return (y * g.astype(jnp.float32)).astype(x.dtype)


def rope(x, pos, rot_dim, theta):
    half = rot_dim // 2
    inv = theta ** (-jnp.arange(half, dtype=jnp.float32) * 2.0 / rot_dim)
    ang = pos.astype(jnp.float32)[:, None] * inv[None, :]
    cos = jnp.cos(ang).astype(x.dtype)
    sin = jnp.sin(ang).astype(x.dtype)
    x1 = x[..., :half]
    x2 = x[..., half:rot_dim]
    return jnp.concatenate([x1 * cos - x2 * sin, x2 * cos + x1 * sin, x[..., rot_dim:]], axis=-1)


def chunk_id(p):
    return jnp.where(p < N_META, 0, 1 + (p - N_META) // CHUNK)


def _blocks(t, axis):
    s = t.shape
    t = t.reshape(s[:axis] + (s[axis] // Q_BLOCK, Q_BLOCK) + s[axis + 1:])
    return jnp.moveaxis(t, axis, 0)


def _unblocks(o):
    nb, b, h, qb, d = o.shape
    return o.transpose(1, 0, 3, 2, 4).reshape(b, nb * qb, h * d)


def _starts(length):
    return jnp.arange(length // Q_BLOCK, dtype=jnp.int32) * Q_BLOCK


def swiglu(x, wg, wu, wd):
    return (jax.nn.silu(x @ wg) * (x @ wu)) @ wd


def stick_breaking(q, k, v, pos):
    scale = q.shape[-1] ** -0.5

    def blk(args):
        s0, q_i = args
        tq = s0 + jnp.arange(Q_BLOCK, dtype=jnp.int32)
        z = jnp.einsum('bhqd,bhkd->bhqk', q_i, k).astype(jnp.float32) * scale
        strict = pos[None, :] < tq[:, None]
        log_1mb = jnp.where(strict, jax.nn.log_sigmoid(-z), 0.0)
        between = lax.cumsum(log_1mb, axis=3, reverse=True) - log_1mb
        w = jnp.where(strict, jnp.exp(jax.nn.log_sigmoid(z) + between), 0.0)
        return jnp.einsum('bhqk,bhkd->bhqd', w.astype(v.dtype), v)

    return _unblocks(lax.map(blk, (_starts(q.shape[2]), _blocks(q, 2))))


def mla(c_q, c_kv, k_r, pos, g_q, w_uq, g_kv, w_ukv):
    b, length, _ = c_q.shape
    q = (rmsnorm(c_q, g_q) @ w_uq).reshape(b, length, MLA_HEADS, MLA_NOPE + MLA_ROPE).transpose(0, 2, 1, 3)
    q_nope = q[..., :MLA_NOPE]
    q_rope = rope(q[..., MLA_NOPE:], pos, MLA_ROPE, MLA_THETA)
    kv = (rmsnorm(c_kv, g_kv) @ w_ukv).reshape(b, length, MLA_HEADS, MLA_NOPE + MLA_V).transpose(0, 2, 1, 3)
    k_nope = kv[..., :MLA_NOPE]
    v = kv[..., MLA_NOPE:]
    k_rope = rope(k_r, pos, MLA_ROPE, MLA_THETA)
    scale = (MLA_NOPE + MLA_ROPE) ** -0.5
    kchunk = chunk_id(pos)

    def blk(args):
        s0, qn_i, qr_i = args
        tq = s0 + jnp.arange(Q_BLOCK, dtype=jnp.int32)
        s = (jnp.einsum('bhqd,bhkd->bhqk', qn_i, k_nope)
             + jnp.einsum('bhqr,bkr->bhqk', qr_i, k_rope)).astype(jnp.float32) * scale
        mask = kchunk[None, :] <= chunk_id(tq)[:, None]
        p = jax.nn.softmax(jnp.where(mask, s, NEG_INF), axis=-1)
        return jnp.einsum('bhqk,bhkd->bhqd', p.astype(v.dtype), v)

    return _unblocks(lax.map(blk, (_starts(length), _blocks(q_nope, 2), _blocks(q_rope, 2))))


def diff_attention(q, k, v, pos, lam, lam_init, g_sub):
    b, h, _, length, d = q.shape
    q = rope(q, pos, ROT_DIM, ROPE_THETA)
    k = rope(k, pos, ROT_DIM, ROPE_THETA)
    scale = d ** -0.5
    kchunk = chunk_id(pos)

    def blk(args):
        s0, q_i = args
        tq = s0 + jnp.arange(Q_BLOCK, dtype=jnp.int32)
        s = jnp.einsum('bhmqd,bhmkd->bhmqk', q_i, k).astype(jnp.float32) * scale
        mask = kchunk[None, :] <= chunk_id(tq)[:, None]
        p = jax.nn.softmax(jnp.where(mask, s, NEG_INF), axis=-1)
        w = p[:, :, 0] - lam * p[:, :, 1]
        return jnp.einsum('bhqk,bhkd->bhqd', w.astype(v.dtype), v)

    o = _unblocks(lax.map(blk, (_starts(length), _blocks(q, 3))))
    o = rmsnorm(o.reshape(b, length, h, 2 * d), g_sub) * (1.0 - lam_init)
    return o.reshape(b, length, h * 2 * d)


def forgetting_attention(q, k, v, f_logit, pos):
    length = q.shape[2]
    c = jnp.cumsum(jax.nn.log_sigmoid(f_logit.astype(jnp.float32)), axis=1).transpose(0, 2, 1)
    scale = q.shape[-1] ** -0.5

    def blk(args):
        s0, q_i, cq_i = args
        tq = s0 + jnp.arange(Q_BLOCK, dtype=jnp.int32)
        s = jnp.einsum('bhqd,bhkd->bhqk', q_i, k).astype(jnp.float32) * scale
        s = s + (cq_i[..., :, None] - c[..., None, :])
        mask = pos[None, :] <= tq[:, None]
        p = jax.nn.softmax(jnp.where(mask, s, NEG_INF), axis=-1)
        return jnp.einsum('bhqk,bhkd->bhqd', p.astype(v.dtype), v)

    return _unblocks(lax.map(blk, (_starts(length), _blocks(q, 2), _blocks(c, 2))))


def even_mixer(hn, pos, w_in, g_q, w_uq, g_kv, w_ukv, w_out):
    b, length, _ = hn.shape
    proj = hn @ w_in
    sb = proj[..., :3 * SB_W].reshape(b, length, 3, SB_HEADS, HEAD_DIM).transpose(2, 0, 3, 1, 4)
    a_out = stick_breaking(sb[0], sb[1], sb[2], pos)
    off = 3 * SB_W
    c_q = proj[..., off:off + MLA_Q_RANK]
    off += MLA_Q_RANK
    c_kv = proj[..., off:off + MLA_KV_RANK]
    off += MLA_KV_RANK
    k_r = proj[..., off:off + MLA_ROPE]
    b_out = mla(c_q, c_kv, k_r, pos, g_q, w_uq, g_kv, w_ukv)
    return jnp.concatenate([a_out, b_out], axis=-1) @ w_out


def odd_mixer(hn, pos, w_in, b_forget, lq1, lk1, lq2, lk2, g_sub, w_out, lam_init):
    b, length, _ = hn.shape
    proj = hn @ w_in
    dq = proj[..., :DIFF_W].reshape(b, length, DIFF_HEADS, 2, DIFF_DIM).transpose(0, 2, 3, 1, 4)
    dk = proj[..., DIFF_W:2 * DIFF_W].reshape(b, length, DIFF_HEADS, 2, DIFF_DIM).transpose(0, 2, 3, 1, 4)
    dv = proj[..., 2 * DIFF_W:3 * DIFF_W].reshape(b, length, DIFF_HEADS, 2 * DIFF_DIM).transpose(0, 2, 1, 3)
    lam = (jnp.exp(jnp.sum(lq1.astype(jnp.float32) * lk1.astype(jnp.float32)))
           - jnp.exp(jnp.sum(lq2.astype(jnp.float32) * lk2.astype(jnp.float32))) + lam_init)
    c_out = diff_attention(dq, dk, dv, pos, lam, lam_init, g_sub)
    off = 3 * DIFF_W
    fx = proj[..., off:off + 3 * FOX_W].reshape(b, length, 3, FOX_HEADS, HEAD_DIM).transpose(2, 0, 3, 1, 4)
    f_logit = proj[..., off + 3 * FOX_W:] + b_forget
    d_out = forgetting_attention(fx[0], fx[1], fx[2], f_logit, pos)
    return jnp.concatenate([c_out, d_out], axis=-1) @ w_out


def moe_ffn(hn, w_router, wg, wu, wd):
    logits = (hn @ w_router).astype(jnp.float32)
    top_vals, top_idx = lax.top_k(logits, TOP_K)
    gates = jax.nn.softmax(top_vals, axis=-1)
    dense_gate = jnp.einsum('blk,blke->ble', gates, jax.nn.one_hot(top_idx, N_EXPERTS, dtype=jnp.float32))
    y = jnp.zeros_like(hn)
    for e in range(N_EXPERTS):
        y = y + dense_gate[..., e:e + 1].astype(hn.dtype) * swiglu(hn, wg[e], wu[e], wd[e])
    return y


def setup_inputs(seed: int = 0) -> dict:
    key = jax.random.key(seed)
    ks = iter(jax.random.split(key, 48))
    ne = (DEPTH + 1) // 2
    no = DEPTH // 2

    def nrm(shape, scale):
        return jax.random.normal(next(ks), shape, jnp.float32) * scale

    def gain(shape):
        return 1.0 + nrm(shape, 0.02)

    d = D_MODEL
    b_forget = jnp.linspace(1.0, 6.0, FOX_HEADS, dtype=jnp.float32)[None, :] + nrm((no, FOX_HEADS), 0.1)
    return {
        "x": nrm((BATCH, SEQ, d), 1.0),
        "meta_tokens": nrm((N_META, d), 1.0),
        "ln_mix_e": gain((ne, d)),
        "w_in_e": nrm((ne, d, EVEN_IN), d ** -0.5),
        "ln_mla_q_e": gain((ne, MLA_Q_RANK)),
        "w_mla_uq_e": nrm((ne, MLA_Q_RANK, MLA_HEADS * (MLA_NOPE + MLA_ROPE)), MLA_Q_RANK ** -0.5),
        "ln_mla_kv_e": gain((ne, MLA_KV_RANK)),
        "w_mla_ukv_e": nrm((ne, MLA_KV_RANK, MLA_HEADS * (MLA_NOPE + MLA_V)), MLA_KV_RANK ** -0.5),
        "w_out_e": nrm((ne, EVEN_MIX, d), EVEN_MIX ** -0.5),
        "ln_ffn_e": gain((ne, d)),
        "w_ffn_gate_e": nrm((ne, d, D_FF), d ** -0.5),
        "w_ffn_up_e": nrm((ne, d, D_FF), d ** -0.5),
        "w_ffn_down_e": nrm((ne, D_FF, d), D_FF ** -0.5),
        "ln_mix_o": gain((no, d)),
        "w_in_o": nrm((no, d, ODD_IN), d ** -0.5),
        "b_forget_o": b_forget,
        "diff_lq1_o": nrm((no, DIFF_DIM), 0.1),
        "diff_lk1_o": nrm((no, DIFF_DIM), 0.1),
        "diff_lq2_o": nrm((no, DIFF_DIM), 0.1),
        "diff_lk2_o": nrm((no, DIFF_DIM), 0.1),
        "diff_subln_o": gain((no, 2 * DIFF_DIM)),
        "w_out_o": nrm((no, ODD_MIX, d), ODD_MIX ** -0.5),
        "ln_ffn_o": gain((no, d)),
        "w_router_o": nrm((no, d, N_EXPERTS), d ** -0.5),
        "w_moe_gate_o": nrm((no, N_EXPERTS, d, D_FF_EXPERT), d ** -0.5),
        "w_moe_up_o": nrm((no, N_EXPERTS, d, D_FF_EXPERT), d ** -0.5),
        "w_moe_down_o": nrm((no, N_EXPERTS, D_FF_EXPERT, d), D_FF_EXPERT ** -0.5),
        "ln_final": gain((d,)),
    }


def reference(x, meta_tokens, ln_mix_e, w_in_e, ln_mla_q_e, w_mla_uq_e, ln_mla_kv_e, w_mla_ukv_e,
              w_out_e, ln_ffn_e, w_ffn_gate_e, w_ffn_up_e, w_ffn_down_e, ln_mix_o, w_in_o, b_forget_o,
              diff_lq1_o, diff_lk1_o, diff_lq2_o, diff_lk2_o, diff_subln_o, w_out_o, ln_ffn_o,
              w_router_o, w_moe_gate_o, w_moe_up_o, w_moe_down_o, ln_final):
    b, seq, d = x.shape
    total = seq + N_META
    length = ((total + Q_BLOCK - 1) // Q_BLOCK) * Q_BLOCK
    meta = jnp.broadcast_to(meta_tokens.astype(x.dtype)[None], (b, N_META, d))
    h = jnp.concatenate([meta, x, jnp.zeros((b, length - total, d), x.dtype)], axis=1)
    pos = jnp.arange(length, dtype=jnp.int32)

    for i in range(DEPTH):
        j = i // 2
        if i % 2 == 0:
            hn = rmsnorm(h, ln_mix_e[j])
            h = h + even_mixer(hn, pos, w_in_e[j], ln_mla_q_e[j], w_mla_uq_e[j], ln_mla_kv_e[j],
                               w_mla_ukv_e[j], w_out_e[j])
            h = h + swiglu(rmsnorm(h, ln_ffn_e[j]), w_ffn_gate_e[j], w_ffn_up_e[j], w_ffn_down_e[j])
        else:
            lam_init = 0.8 - 0.6 * math.exp(-0.3 * i)
            hn = rmsnorm(h, ln_mix_o[j])
            h = h + odd_mixer(hn, pos, w_in_o[j], b_forget_o[j], diff_lq1_o[j], diff_lk1_o[j],
                              diff_lq2_o[j], diff_lk2_o[j], diff_subln_o[j], w_out_o[j], lam_init)
            h = h + moe_ffn(rmsnorm(h, ln_ffn_o[j]), w_router_o[j], w_moe_gate_o[j],
                            w_moe_up_o[j], w_moe_down_o[j])

    h = rmsnorm(h, ln_final)
    return h[:, N_META:N_META + seq]
```

```python
import functools
import math

import jax
import jax.numpy as jnp
from jax import lax
from jax.experimental import pallas as pl
from jax.experimental.pallas import tpu as pltpu

F32 = jnp.float32
BF16 = jnp.bfloat16

D_MODEL = 1024
CHUNK = 64
N_META = 16
EPS = 1e-6
NEG_INF = -1e30

HEAD_DIM = 64
SB_HEADS = 8
MLA_HEADS = 8
MLA_NOPE = 64
MLA_ROPE = 32
MLA_V = 64
MLA_Q_RANK = 256
MLA_KV_RANK = 128
MLA_THETA = 10000.0
DIFF_HEADS = 4
DIFF_DIM = 64
FOX_HEADS = 8
ROPE_THETA = 500000.0
ROT_DIM = HEAD_DIM // 4
N_EXPERTS = 8
TOP_K = 2

SB_W = SB_HEADS * HEAD_DIM
DIFF_W = DIFF_HEADS * 2 * DIFF_DIM
FOX_W = FOX_HEADS * HEAD_DIM

LANES = 128
PAD = 128
FIRST = PAD - N_META
TQ = 128
TK = 128
CHUNK_SHIFT = CHUNK.bit_length() - 1
VMEM_LIMIT = 56 * 1024 * 1024


def _cparams(sem):
    return pltpu.CompilerParams(dimension_semantics=sem, vmem_limit_bytes=VMEM_LIMIT)


def _pick(n, cands):
    for c in cands:
        if n % c == 0:
            return c
    raise ValueError(f"no tile in {cands} divides {n}")


def _rms(x, g):
    return x * lax.rsqrt(jnp.mean(x * x, axis=-1, keepdims=True) + EPS) * g


def _dot(a, b):
    return jnp.dot(a, b, preferred_element_type=F32)


def _dot_nt(a, b):
    return lax.dot_general(a, b, (((1,), (1,)), ((), ())), preferred_element_type=F32)


def _rope_chunk(y, tab, shift):
    c = tab[:, 0:LANES]
    s1 = tab[:, LANES:2 * LANES]
    s2 = tab[:, 2 * LANES:3 * LANES]
    return (y * c + pltpu.roll(y, LANES - shift, 1) * s1 + pltpu.roll(y, shift, 1) * s2)


def _split3(x):
    a = x.astype(BF16)
    r = x - a.astype(F32)
    b = r.astype(BF16)
    c = (r - b.astype(F32)).astype(BF16)
    return a, b, c


def _even_proj_kernel(h_ref, g_ref, w_ref, sb_ref, lat_ref):
    hn = _rms(h_ref[...], g_ref[...]).astype(BF16)
    y = _dot(hn, w_ref[...])
    n_sb = sb_ref.shape[-1]
    sb_ref[...] = y[:, :n_sb].astype(BF16)
    lat_ref[...] = y[:, n_sb:].astype(BF16)


def _even_proj(h, g, w):
    t, d = h.shape
    n = w.shape[1]
    n_sb = 3 * SB_W
    tm = _pick(t, (512, 384, 256, 128))
    return pl.pallas_call(
        _even_proj_kernel,
        grid=(t // tm,),
        in_specs=[pl.BlockSpec((tm, d), lambda i: (i, 0)),
                  pl.BlockSpec((1, d), lambda i: (0, 0)),
                  pl.BlockSpec((d, n), lambda i: (0, 0))],
        out_specs=[pl.BlockSpec((tm, n_sb), lambda i: (i, 0)),
                   pl.BlockSpec((tm, n - n_sb), lambda i: (i, 0))],
        out_shape=[jax.ShapeDtypeStruct((t, n_sb), BF16),
                   jax.ShapeDtypeStruct((t, n - n_sb), BF16)],
        compiler_params=_cparams(("parallel",)),
        name="even_proj",
    )(h, g, w)


def _mla_up_kernel(lat_ref, gq_ref, gkv_ref, wq_ref, wk_ref, wv_ref, tq_ref, tk_ref,
                   q_ref, k_ref, v_ref):
    lat = lat_ref[0]
    nq = _rms(lat[:, :MLA_Q_RANK].astype(F32), gq_ref[...]).astype(BF16)
    nkv = _rms(lat[:, MLA_Q_RANK:MLA_Q_RANK + MLA_KV_RANK].astype(F32), gkv_ref[...]).astype(BF16)
    yq = _dot(nq, wq_ref[...])
    xk = jnp.concatenate([nkv, lat[:, MLA_Q_RANK + MLA_KV_RANK:]], axis=1)
    yk = _dot(xk, wk_ref[...])
    v_ref[0] = _dot(nkv, wv_ref[...]).astype(BF16)
    tabq = tq_ref[...]
    tabk = tk_ref[...]
    for hd in range(MLA_HEADS):
        sl = slice(hd * LANES, (hd + 1) * LANES)
        q_ref[0, :, sl] = _rope_chunk(yq[:, sl], tabq, MLA_ROPE // 2).astype(BF16)
        k_ref[0, :, sl] = _rope_chunk(yk[:, sl], tabk, MLA_ROPE // 2).astype(BF16)


def _mla_up(lat, gq, gkv, wq, wk, wv, tabq, tabk):
    b, lp, nl = lat.shape
    tm = _pick(lp, (384, 128))
    nqk = MLA_HEADS * LANES
    nv = MLA_HEADS * MLA_V
    full = lambda shp: pl.BlockSpec(shp, lambda bi, i: (0,) * len(shp))
    return pl.pallas_call(
        _mla_up_kernel,
        grid=(b, lp // tm),
        in_specs=[pl.BlockSpec((1, tm, nl), lambda bi, i: (bi, i, 0)),
                  full(gq.shape), full(gkv.shape), full(wq.shape), full(wk.shape), full(wv.shape),
                  pl.BlockSpec((tm, 3 * LANES), lambda bi, i: (i, 0)),
                  pl.BlockSpec((tm, 3 * LANES), lambda bi, i: (i, 0))],
        out_specs=[pl.BlockSpec((1, tm, nqk), lambda bi, i: (bi, i, 0)),
                   pl.BlockSpec((1, tm, nqk), lambda bi, i: (bi, i, 0)),
                   pl.BlockSpec((1, tm, nv), lambda bi, i: (bi, i, 0))],
        out_shape=[jax.ShapeDtypeStruct((b, lp, nqk), BF16),
                   jax.ShapeDtypeStruct((b, lp, nqk), BF16),
                   jax.ShapeDtypeStruct((b, lp, nv), BF16)],
        compiler_params=_cparams(("parallel", "parallel")),
        name="mla_up",
    )(lat, gq, gkv, wq, wk, wv, tabq, tabk)


N_ODD_MAIN = 3 * DIFF_W + 3 * FOX_W
F_ROWS = 16


def _odd_proj_kernel(h_ref, g_ref, w_ref, wft_ref, bcol_ref, brow_ref, tab_ref,
                     main_ref, fcol_ref, frow_ref):
    hn = _rms(h_ref[0], g_ref[...]).astype(BF16)
    y = _dot(hn, w_ref[...])
    tab = tab_ref[...]
    n_rope = 2 * DIFF_W // LANES
    for c in range(n_rope):
        sl = slice(c * LANES, (c + 1) * LANES)
        main_ref[0, :, sl] = _rope_chunk(y[:, sl], tab, ROT_DIM // 2).astype(BF16)
    main_ref[0, :, 2 * DIFF_W:] = y[:, 2 * DIFF_W:N_ODD_MAIN].astype(BF16)
    fcol_ref[0] = y[:, N_ODD_MAIN:] + bcol_ref[...]
    frow_ref[0] = _dot_nt(wft_ref[...], hn) + brow_ref[...]


def _odd_proj(h3, g, w, wft, bcol, brow, tab):
    b, lp, d = h3.shape
    tm = _pick(lp, (384, 128))
    full = lambda shp: pl.BlockSpec(shp, lambda bi, i: (0,) * len(shp))
    return pl.pallas_call(
        _odd_proj_kernel,
        grid=(b, lp // tm),
        in_specs=[pl.BlockSpec((1, tm, d), lambda bi, i: (bi, i, 0)),
                  full(g.shape), full(w.shape), full(wft.shape), full(bcol.shape), full(brow.shape),
                  pl.BlockSpec((tm, 3 * LANES), lambda bi, i: (i, 0))],
        out_specs=[pl.BlockSpec((1, tm, N_ODD_MAIN), lambda bi, i: (bi, i, 0)),
                   pl.BlockSpec((1, tm, LANES), lambda bi, i: (bi, i, 0)),
                   pl.BlockSpec((1, F_ROWS, tm), lambda bi, i: (bi, 0, i))],
        out_shape=[jax.ShapeDtypeStruct((b, lp, N_ODD_MAIN), BF16),
                   jax.ShapeDtypeStruct((b, lp, LANES), F32),
                   jax.ShapeDtypeStruct((b, F_ROWS, lp), F32)],
        compiler_params=_cparams(("parallel", "parallel")),
        name="odd_proj",
    )(h3, g, w, wft, bcol, brow, tab)


def _out_proj_kernel(h_ref, a_ref, b_ref, wa_ref, wb_ref, o_ref):
    o_ref[...] = h_ref[...] + _dot(a_ref[...], wa_ref[...]) + _dot(b_ref[...], wb_ref[...])


def _out_proj(h, mix_a, mix_b, wa, wb):
    t, d = h.shape
    tm = _pick(t, (512, 384, 256, 128))
    na, nb = mix_a.shape[1], mix_b.shape[1]
    return pl.pallas_call(
        _out_proj_kernel,
        grid=(t // tm,),
        in_specs=[pl.BlockSpec((tm, d), lambda i: (i, 0)),
                  pl.BlockSpec((tm, na), lambda i: (i, 0)),
                  pl.BlockSpec((tm, nb), lambda i: (i, 0)),
                  pl.BlockSpec((na, d), lambda i: (0, 0)),
                  pl.BlockSpec((nb, d), lambda i: (0, 0))],
        out_specs=pl.BlockSpec((tm, d), lambda i: (i, 0)),
        out_shape=jax.ShapeDtypeStruct((t, d), F32),
        compiler_params=_cparams(("parallel",)),
        name="out_proj",
    )(h, mix_a, mix_b, wa, wb)


def _log_sigmoid(x):
    return jnp.minimum(x, 0.0) - jnp.log(1.0 + jnp.exp(-jnp.abs(x)))


def _fox_cumsum_kernel(fcol_ref, frow_ref, ccol_ref, crow_ref):
    lp = fcol_ref.shape[1]
    blk = LANES
    r = lax.broadcasted_iota(jnp.int32, (blk, blk), 0)
    c = lax.broadcasted_iota(jnp.int32, (blk, blk), 1)
    tri_l = (c <= r).astype(BF16)
    tri_u = (r <= c).astype(BF16)
    row_id = lax.broadcasted_iota(jnp.int32, (blk, LANES), 0)
    col_id = lax.broadcasted_iota(jnp.int32, (F_ROWS, blk), 1)
    carry_c = jnp.zeros((1, LANES), F32)
    carry_r = jnp.zeros((F_ROWS, 1), F32)
    for j in range(lp // blk):
        sl = slice(j * blk, (j + 1) * blk)
        ls = _log_sigmoid(fcol_ref[0, sl, :])
        ls = jnp.where(row_id + j * blk >= FIRST, ls, 0.0)
        a, b, c3 = _split3(ls)
        cs = _dot(tri_l, a) + _dot(tri_l, b) + _dot(tri_l, c3) + carry_c
        ccol_ref[0, sl, :] = cs
        carry_c = cs[blk - 1:blk, :]
        lr = _log_sigmoid(frow_ref[0, :, sl])
        lr = jnp.where(col_id + j * blk >= FIRST, lr, 0.0)
        a, b, c3 = _split3(lr)
        cr = _dot(a, tri_u) + _dot(b, tri_u) + _dot(c3, tri_u) + carry_r
        crow_ref[0, :, sl] = cr
        carry_r = cr[:, blk - 1:blk]


def _fox_cumsum(fcol, frow):
    b, lp, _ = fcol.shape
    return pl.pallas_call(
        _fox_cumsum_kernel,
        grid=(b,),
        in_specs=[pl.BlockSpec((1, lp, LANES), lambda bi: (bi, 0, 0)),
                  pl.BlockSpec((1, F_ROWS, lp), lambda bi: (bi, 0, 0))],
        out_specs=[pl.BlockSpec((1, lp, LANES), lambda bi: (bi, 0, 0)),
                   pl.BlockSpec((1, F_ROWS, lp), lambda bi: (bi, 0, 0))],
        out_shape=[jax.ShapeDtypeStruct((b, lp, LANES), F32),
                   jax.ShapeDtypeStruct((b, F_ROWS, lp), F32)],
        compiler_params=_cparams(("parallel",)),
        name="fox_cumsum",
    )(fcol, frow)


def _rows(i, n):
    if isinstance(i, int):
        return pl.ds(i * n, n)
    return pl.ds(pl.multiple_of(i * n, n), n)


def _tile_ids(q0, k0):
    r = lax.broadcasted_iota(jnp.int32, (2 * TQ, TK), 0)
    qs = q0 + jnp.where(r >= TQ, r - TQ, r)
    ks = k0 + lax.broadcasted_iota(jnp.int32, (2 * TQ, TK), 1)
    return qs, ks


def _mask_chunk(q0, k0):
    qs, ks = _tile_ids(q0, k0)
    return (ks >= FIRST) & ((ks >> CHUNK_SHIFT) <= (qs >> CHUNK_SHIFT))


def _mask_causal(q0, k0):
    qs, ks = _tile_ids(q0, k0)
    return (ks >= FIRST) & (ks <= qs)


def _mask_strict(q0, k0):
    qs, ks = _tile_ids(q0, k0)
    return (ks >= FIRST) & (ks < qs)


def _lane_halves():
    lane = lax.broadcasted_iota(jnp.int32, (TQ, LANES), 1)
    return lane < (LANES // 2)


def _stack_masked(q):
    lo = _lane_halves()
    z = jnp.zeros_like(q)
    return jnp.concatenate([jnp.where(lo, q, z), jnp.where(lo, z, q)], axis=0)


def _softmax_sweep(prep, scores_fn, v_tile_fn, mask_fn):
    def step(carry, s, v, mask):
        m, l, acc = carry
        if mask is not None:
            s = jnp.where(mask, s, NEG_INF)
        m_new = jnp.maximum(m, jnp.max(s, axis=-1, keepdims=True))
        alpha = jnp.exp(m - m_new)
        p = jnp.exp(s - m_new)
        l = alpha * l + jnp.sum(p, axis=-1, keepdims=True)
        acc = alpha * acc + _dot(p.astype(BF16), v)
        return m_new, l, acc

    def run(qi):
        ctx = prep(qi)
        q0 = qi * TQ
        init = (jnp.full((2 * TQ, 1), NEG_INF, F32), jnp.zeros((2 * TQ, 1), F32),
                jnp.zeros((2 * TQ, LANES), F32))
        carry = step(init, scores_fn(ctx, 0), v_tile_fn(0), mask_fn(q0, 0))
        if isinstance(qi, int) and qi == 0:
            return carry
        carry = lax.fori_loop(
            1, qi, lambda j, c: step(c, scores_fn(ctx, j), v_tile_fn(j), None), carry)
        return step(carry, scores_fn(ctx, qi), v_tile_fn(qi), mask_fn(q0, qi * TK))

    return run


def _for_each_query_block(nq, run, emit):
    emit(0, run(0))

    def body(qi, _):
        emit(qi, run(qi))
        return 0

    lax.fori_loop(1, nq, body, 0)


def _merge_heads(top, bottom):
    return jnp.where(_lane_halves(), top, bottom)


def _attn_call(kernel, name, b, lp, n_blocks, in_specs, args):
    return pl.pallas_call(
        kernel,
        grid=(b, n_blocks),
        in_specs=in_specs,
        out_specs=pl.BlockSpec((1, lp, LANES), lambda bi, hp: (bi, 0, hp)),
        out_shape=jax.ShapeDtypeStruct((b, lp, n_blocks * LANES), BF16),
        compiler_params=_cparams(("parallel", "parallel")),
        name=name,
    )(*args)


def _col_spec(lp, off):
    return pl.BlockSpec((1, lp, LANES), lambda bi, hp: (bi, 0, off + hp))


def _mla_attn_kernel(q_ref, k_ref, v_ref, o_ref):
    nq = q_ref.shape[1] // TQ

    def prep(qi):
        return q_ref[0, _rows(qi, TQ), 0:LANES], q_ref[0, _rows(qi, TQ), LANES:2 * LANES]

    def scores(ctx, j):
        qa, qb = ctx
        ka = k_ref[0, _rows(j, TK), 0:LANES]
        kb = k_ref[0, _rows(j, TK), LANES:2 * LANES]
        return jnp.concatenate([_dot_nt(qa, ka), _dot_nt(qb, kb)], axis=0)

    run = _softmax_sweep(prep, scores, lambda j: v_ref[0, _rows(j, TK), :], _mask_chunk)

    def emit(qi, res):
        _, l, acc = res
        o = acc / l
        o_ref[0, _rows(qi, TQ), :] = _merge_heads(o[:TQ], o[TQ:]).astype(o_ref.dtype)

    _for_each_query_block(nq, run, emit)


def _mla_attn(q, k, v):
    b, lp, _ = q.shape
    wide = lambda: pl.BlockSpec((1, lp, 2 * LANES), lambda bi, hp: (bi, 0, hp))
    return _attn_call(_mla_attn_kernel, "mla_attn", b, lp, MLA_HEADS // 2,
                      [wide(), wide(), _col_spec(lp, 0)], (q, k, v))


def _diff_attn_kernel(lam_init, x_q, x_k, x_v, lam_ref, gsub_ref, o_ref):
    nq = x_q.shape[1] // TQ
    lv = lam_ref[...]
    lam = (jnp.exp(jnp.sum(lv[0:1] * lv[1:2], axis=-1, keepdims=True))
           - jnp.exp(jnp.sum(lv[2:3] * lv[3:4], axis=-1, keepdims=True)) + lam_init)
    gsub = gsub_ref[...]

    def prep(qi):
        return _stack_masked(x_q[0, _rows(qi, TQ), :])

    def scores(qs, j):
        return _dot_nt(qs, x_k[0, _rows(j, TK), :])

    run = _softmax_sweep(prep, scores, lambda j: x_v[0, _rows(j, TK), :], _mask_chunk)

    def emit(qi, res):
        _, l, acc = res
        o = acc / l
        o = o[:TQ] - lam * o[TQ:]
        o_ref[0, _rows(qi, TQ), :] = (_rms(o, gsub) * (1.0 - lam_init)).astype(o_ref.dtype)

    _for_each_query_block(nq, run, emit)


def _diff_attn(main, lam_vecs, gsub, lam_init):
    b, lp, _ = main.shape
    nb = DIFF_W // LANES
    full = lambda shp: pl.BlockSpec(shp, lambda bi, hp: (0,) * len(shp))
    return _attn_call(functools.partial(_diff_attn_kernel, lam_init), "diff_attn", b, lp, DIFF_HEADS,
                      [_col_spec(lp, 0), _col_spec(lp, nb), _col_spec(lp, 2 * nb),
                       full(lam_vecs.shape), full(gsub.shape)],
                      (main, main, main, lam_vecs, gsub))


def _fox_attn_kernel(x_q, x_k, x_v, ccol_ref, crow_ref, o_ref):
    nq = x_q.shape[1] // TQ
    hp = pl.program_id(1)
    lane = lax.broadcasted_iota(jnp.int32, (TQ, LANES), 1)

    def prep(qi):
        cc = ccol_ref[0, _rows(qi, TQ), :]
        cq0 = jnp.sum(jnp.where(lane == 2 * hp, cc, 0.0), axis=-1, keepdims=True)
        cq1 = jnp.sum(jnp.where(lane == 2 * hp + 1, cc, 0.0), axis=-1, keepdims=True)
        return _stack_masked(x_q[0, _rows(qi, TQ), :]), cq0, cq1

    def scores(ctx, j):
        qs, cq0, cq1 = ctx
        s = _dot_nt(qs, x_k[0, _rows(j, TK), :])
        ck = crow_ref[0, 0, :, _rows(j, TK)]
        return s + jnp.concatenate([cq0 - ck[0:1, :], cq1 - ck[1:2, :]], axis=0)

    run = _softmax_sweep(prep, scores, lambda j: x_v[0, _rows(j, TK), :], _mask_causal)

    def emit(qi, res):
        _, l, acc = res
        o = acc / l
        o_ref[0, _rows(qi, TQ), :] = _merge_heads(o[:TQ], o[TQ:]).astype(o_ref.dtype)

    _for_each_query_block(nq, run, emit)


def _fox_attn(main, ccol, crow4):
    b, lp, _ = main.shape
    off = 3 * DIFF_W // LANES
    nb = FOX_W // LANES
    return _attn_call(_fox_attn_kernel, "fox_attn", b, lp, nb,
                      [_col_spec(lp, off), _col_spec(lp, off + nb), _col_spec(lp, off + 2 * nb),
                       pl.BlockSpec((1, lp, LANES), lambda bi, hp: (bi, 0, 0)),
                       pl.BlockSpec((1, 1, 2, lp), lambda bi, hp: (bi, hp, 0, 0))],
                      (main, main, main, ccol, crow4))


def _sb_attn_kernel(x_q, x_k, x_v, o_ref):
    nq = x_q.shape[1] // TQ
    kr = lax.broadcasted_iota(jnp.int32, (TK, TK), 0)
    kc = lax.broadcasted_iota(jnp.int32, (TK, TK), 1)
    later = (kr > kc).astype(BF16)

    def step(carry, qs, j, mask):
        rsum, acc = carry
        z = _dot_nt(qs, x_k[0, _rows(j, TK), :])
        log_1mb = -(jnp.maximum(z, 0.0) + jnp.log(1.0 + jnp.exp(-jnp.abs(z))))
        if mask is not None:
            log_1mb = jnp.where(mask, log_1mb, 0.0)
        hi = log_1mb.astype(BF16)
        lo = (log_1mb - hi.astype(F32)).astype(BF16)
        between = _dot(hi, later) + _dot(lo, later) + rsum
        w = jnp.exp(z + log_1mb + between)
        if mask is not None:
            w = jnp.where(mask, w, 0.0)
        acc = acc + _dot(w.astype(BF16), x_v[0, _rows(j, TK), :])
        rsum = rsum + jnp.sum(log_1mb, axis=-1, keepdims=True)
        return rsum, acc

    def run(qi):
        q0 = qi * TQ
        qs = _stack_masked(x_q[0, _rows(qi, TQ), :])
        carry = (jnp.zeros((2 * TQ, 1), F32), jnp.zeros((2 * TQ, LANES), F32))
        carry = step(carry, qs, qi, _mask_strict(q0, qi * TK))
        if isinstance(qi, int) and qi == 0:
            return carry
        carry = lax.fori_loop(1, qi, lambda it, c: step(c, qs, qi - it, None), carry)
        return step(carry, qs, 0, _mask_strict(q0, 0))

    def emit(qi, res):
        _, acc = res
        o_ref[0, _rows(qi, TQ), :] = _merge_heads(acc[:TQ], acc[TQ:]).astype(o_ref.dtype)

    _for_each_query_block(nq, run, emit)


def _sb_attn(sb):
    b, lp, _ = sb.shape
    nb = SB_W // LANES
    return _attn_call(_sb_attn_kernel, "sb_attn", b, lp, nb,
                      [_col_spec(lp, 0), _col_spec(lp, nb), _col_spec(lp, 2 * nb)],
                      (sb, sb, sb))


def _silu(x):
    return x / (1.0 + jnp.exp(-x))


def _ffn_kernel(h_ref, g_ref, wg_ref, wu_ref, wd_ref, o_ref, hn_ref, acc_ref):
    f = pl.program_id(1)

    @pl.when(f == 0)
    def _():
        hn_ref[...] = _rms(h_ref[...], g_ref[...]).astype(BF16)
        acc_ref[...] = jnp.zeros_like(acc_ref)

    hn = hn_ref[...]
    a = _silu(_dot(hn, wg_ref[...])) * _dot(hn, wu_ref[...])
    acc_ref[...] += _dot(a.astype(BF16), wd_ref[...])

    @pl.when(f == pl.num_programs(1) - 1)
    def _():
        o_ref[...] = h_ref[...] + acc_ref[...]


def _ffn(h, g, wg, wu, wd):
    t, d = h.shape
    ff = wg.shape[1]
    tm = _pick(t, (512, 384, 256, 128))
    tf = _pick(ff, (1408, 512, 256, 128))
    return pl.pallas_call(
        _ffn_kernel,
        grid=(t // tm, ff // tf),
        in_specs=[pl.BlockSpec((tm, d), lambda i, f: (i, 0)),
                  pl.BlockSpec((1, d), lambda i, f: (0, 0)),
                  pl.BlockSpec((d, tf), lambda i, f: (0, f)),
                  pl.BlockSpec((d, tf), lambda i, f: (0, f)),
                  pl.BlockSpec((tf, d), lambda i, f: (f, 0))],
        out_specs=pl.BlockSpec((tm, d), lambda i, f: (i, 0)),
        out_shape=jax.ShapeDtypeStruct((t, d), F32),
        scratch_shapes=[pltpu.VMEM((tm, d), BF16), pltpu.VMEM((tm, d), F32)],
        compiler_params=_cparams(("parallel", "arbitrary")),
        name="ffn",
    )(h, g, wg, wu, wd)


def _router_gates(hn, wr_ref):
    a, b, c = _split3(hn)
    wa, wb, wc = wr_ref[0], wr_ref[1], wr_ref[2]
    logits = (_dot(a, wa) + _dot(a, wb) + _dot(b, wa) + _dot(a, wc) + _dot(b, wb) + _dot(c, wa))
    lane = lax.broadcasted_iota(jnp.int32, logits.shape, 1).astype(F32)
    logits = jnp.where(lane < N_EXPERTS, logits, -jnp.inf)
    v1 = jnp.max(logits, axis=-1, keepdims=True)
    i1 = jnp.min(jnp.where(logits == v1, lane, float(LANES)), axis=-1, keepdims=True)
    rest = jnp.where(lane == i1, -jnp.inf, logits)
    v2 = jnp.max(rest, axis=-1, keepdims=True)
    i2 = jnp.min(jnp.where(rest == v2, lane, float(LANES)), axis=-1, keepdims=True)
    e2 = jnp.exp(v2 - v1)
    den = 1.0 + e2
    return jnp.where(lane == i1, 1.0 / den, 0.0) + jnp.where(lane == i2, e2 / den, 0.0)


def _moe_kernel(h_ref, g_ref, wr_ref, wg_ref, wu_ref, wd_ref, o_ref, hn_ref, gate_ref, acc_ref):
    e = pl.program_id(1)
    f = pl.program_id(2)

    @pl.when((e == 0) & (f == 0))
    def _():
        hn = _rms(h_ref[...], g_ref[...])
        hn_ref[...] = hn.astype(BF16)
        gate_ref[...] = _router_gates(hn, wr_ref)
        acc_ref[...] = jnp.zeros_like(acc_ref)

    hn = hn_ref[...]
    lane = lax.broadcasted_iota(jnp.int32, gate_ref.shape, 1)
    ge = jnp.sum(jnp.where(lane == e, gate_ref[...], 0.0), axis=-1, keepdims=True)
    a = _silu(_dot(hn, wg_ref[0])) * _dot(hn, wu_ref[0])
    acc_ref[...] += ge * _dot(a.astype(BF16), wd_ref[0])

    @pl.when((e == pl.num_programs(1) - 1) & (f == pl.num_programs(2) - 1))
    def _():
        o_ref[...] = h_ref[...] + acc_ref[...]


def _moe(h, g, wr3, wg, wu, wd):
    t, d = h.shape
    ne, _, ff = wg.shape
    tm = _pick(t, (1024, 768, 512, 384, 256, 128))
    tf = _pick(ff, (512, 256, 128))
    return pl.pallas_call(
        _moe_kernel,
        grid=(t // tm, ne, ff // tf),
        in_specs=[pl.BlockSpec((tm, d), lambda i, e, f: (i, 0)),
                  pl.BlockSpec((1, d), lambda i, e, f: (0, 0)),
                  pl.BlockSpec((3, d, LANES), lambda i, e, f: (0, 0, 0)),
                  pl.BlockSpec((1, d, tf), lambda i, e, f: (e, 0, f)),
                  pl.BlockSpec((1, d, tf), lambda i, e, f: (e, 0, f)),
                  pl.BlockSpec((1, tf, d), lambda i, e, f: (e, f, 0))],
        out_specs=pl.BlockSpec((tm, d), lambda i, e, f: (i, 0)),
        out_shape=jax.ShapeDtypeStruct((t, d), F32),
        scratch_shapes=[pltpu.VMEM((tm, d), BF16), pltpu.VMEM((tm, LANES), F32),
                        pltpu.VMEM((tm, d), F32)],
        compiler_params=_cparams(("parallel", "arbitrary", "arbitrary")),
        name="moe",
    )(h, g, wr3, wg, wu, wd)


def _final_norm_kernel(h_ref, g_ref, o_ref):
    o_ref[0] = _rms(h_ref[0], g_ref[...])


def _final_norm(h3, g, seq):
    b, lp, d = h3.shape
    tm = PAD
    return pl.pallas_call(
        _final_norm_kernel,
        grid=(b, seq // tm),
        in_specs=[pl.BlockSpec((1, tm, d), lambda bi, i: (bi, i + PAD // tm, 0)),
                  pl.BlockSpec((1, d), lambda bi, i: (0, 0))],
        out_specs=pl.BlockSpec((1, tm, d), lambda bi, i: (bi, i, 0)),
        out_shape=jax.ShapeDtypeStruct((b, seq, d), F32),
        compiler_params=_cparams(("parallel", "parallel")),
        name="final_norm",
    )(h3, g)


def _rope_table(lp, rot_dim, theta, width, offset, scale):
    half = rot_dim // 2
    pos = (jnp.arange(lp, dtype=jnp.int32) - FIRST).astype(F32)
    inv = theta ** (-jnp.arange(half, dtype=F32) * 2.0 / rot_dim)
    ang = pos[:, None] * inv[None, :]
    cos, sin = jnp.cos(ang), jnp.sin(ang)
    ones = jnp.ones((lp, 1), F32)
    zeros = jnp.zeros((lp, 1), F32)

    def unit(first, second, fill):
        parts = [jnp.tile(fill, (1, offset)), first, second,
                 jnp.tile(fill, (1, width - offset - rot_dim))]
        return jnp.tile(jnp.concatenate(parts, axis=1), (1, LANES // width))

    c = unit(cos, cos, ones)
    s1 = unit(-sin, jnp.zeros_like(sin), zeros)
    s2 = unit(jnp.zeros_like(sin), sin, zeros)
    return jnp.concatenate([c, s1, s2], axis=1) * scale


def _pad_heads(w, heads, used, width):
    k = w.shape[0]
    w = w.reshape(k, heads, used)
    return jnp.pad(w, ((0, 0), (0, 0), (0, width - used))).reshape(k, heads * width)


def kernel(x, meta_tokens, ln_mix_e, w_in_e, ln_mla_q_e, w_mla_uq_e, ln_mla_kv_e, w_mla_ukv_e,
           w_out_e, ln_ffn_e, w_ffn_gate_e, w_ffn_up_e, w_ffn_down_e, ln_mix_o, w_in_o, b_forget_o,
           diff_lq1_o, diff_lk1_o, diff_lq2_o, diff_lk2_o, diff_subln_o, w_out_o, ln_ffn_o,
           w_router_o, w_moe_gate_o, w_moe_up_o, w_moe_down_o, ln_final):
    b, seq, d = x.shape
    assert d == D_MODEL and seq % TQ == 0
    lp = PAD + seq
    t = b * lp
    depth = 2

    meta = jnp.broadcast_to(meta_tokens.astype(x.dtype)[None], (b, N_META, d))
    h = jnp.concatenate([jnp.zeros((b, FIRST, d), x.dtype), meta, x], axis=1).reshape(t, d)

    row = lambda v: v.reshape(1, -1).astype(F32)
    sb_scale = HEAD_DIM ** -0.5
    tab_mla_q = _rope_table(lp, MLA_ROPE, MLA_THETA, LANES, MLA_NOPE, (MLA_NOPE + MLA_ROPE) ** -0.5)
    tab_mla_k = _rope_table(lp, MLA_ROPE, MLA_THETA, LANES, MLA_NOPE, 1.0)
    tab_diff = _rope_table(lp, ROT_DIM, ROPE_THETA, DIFF_DIM, 0, 1.0)

    for i in range(depth):
        j = i // 2
        if i % 2 == 0:
            w_in = w_in_e[j]
            lat_w = MLA_Q_RANK + MLA_KV_RANK + MLA_ROPE
            w_cat = jnp.concatenate(
                [w_in[:, :SB_W] * sb_scale, w_in[:, SB_W:3 * SB_W + lat_w],
                 jnp.zeros((d, 4 * LANES - lat_w), F32)], axis=1).astype(BF16)
            sb, lat = _even_proj(h, row(ln_mix_e[j]), w_cat)
            sb = sb.reshape(b, lp, -1)
            lat = lat.reshape(b, lp, -1)

            wq = _pad_heads(w_mla_uq_e[j], MLA_HEADS, MLA_NOPE + MLA_ROPE, LANES).astype(BF16)
            ukv = w_mla_ukv_e[j].reshape(MLA_KV_RANK, MLA_HEADS, MLA_NOPE + MLA_V)
            wk_nope = _pad_heads(ukv[:, :, :MLA_NOPE].reshape(MLA_KV_RANK, -1), MLA_HEADS, MLA_NOPE, LANES)
            place = jnp.pad(jnp.eye(MLA_ROPE, dtype=F32), ((0, 0), (MLA_NOPE, LANES - MLA_NOPE - MLA_ROPE)))
            wk_rope = jnp.tile(place, (1, MLA_HEADS))
            wk = jnp.concatenate(
                [wk_nope, wk_rope, jnp.zeros((LANES - MLA_ROPE, MLA_HEADS * LANES), F32)], axis=0).astype(BF16)
            wv = ukv[:, :, MLA_NOPE:].reshape(MLA_KV_RANK, -1).astype(BF16)
            mq, mk, mv = _mla_up(lat, row(ln_mla_q_e[j]), row(ln_mla_kv_e[j]), wq, wk, wv,
                                 tab_mla_q, tab_mla_k)

            a_out = _sb_attn(sb).reshape(t, -1)
            b_out = _mla_attn(mq, mk, mv).reshape(t, -1)
            w_out = w_out_e[j].astype(BF16)
            h = _out_proj(h, a_out, b_out, w_out[:SB_W], w_out[SB_W:])
            h = _ffn(h, row(ln_ffn_e[j]), w_ffn_gate_e[j].astype(BF16), w_ffn_up_e[j].astype(BF16),
                     w_ffn_down_e[j].astype(BF16))
        else:
            lam_init = 0.8 - 0.6 * math.exp(-0.3 * i)
            w_in = w_in_o[j]
            wf = w_in[:, N_ODD_MAIN:]
            w_cat = jnp.concatenate(
                [w_in[:, :DIFF_W] * sb_scale, w_in[:, DIFF_W:3 * DIFF_W],
                 w_in[:, 3 * DIFF_W:3 * DIFF_W + FOX_W] * sb_scale, w_in[:, 3 * DIFF_W + FOX_W:N_ODD_MAIN],
                 wf, jnp.zeros((d, LANES - FOX_HEADS), F32)], axis=1).astype(BF16)
            wft = jnp.pad(wf.T, ((0, F_ROWS - FOX_HEADS), (0, 0))).astype(BF16)
            bf = b_forget_o[j].astype(F32)
            bcol = jnp.pad(bf, (0, LANES - FOX_HEADS)).reshape(1, LANES)
            brow = jnp.pad(bf, (0, F_ROWS - FOX_HEADS)).reshape(F_ROWS, 1)
            main, fcol, frow = _odd_proj(h.reshape(b, lp, d), row(ln_mix_o[j]), w_cat, wft, bcol, brow,
                                         tab_diff)
            ccol, crow = _fox_cumsum(fcol, frow)
            crow4 = crow[:, :FOX_HEADS].reshape(b, FOX_HEADS // 2, 2, lp)
            lam_vecs = jnp.stack([diff_lq1_o[j], diff_lk1_o[j], diff_lq2_o[j], diff_lk2_o[j]]).astype(F32)
            c_out = _diff_attn(main, lam_vecs, row(diff_subln_o[j]), lam_init).reshape(t, -1)
            d_out = _fox_attn(main, ccol, crow4).reshape(t, -1)
            w_out = w_out_o[j].astype(BF16)
            h = _out_proj(h, c_out, d_out, w_out[:DIFF_W], w_out[DIFF_W:])
            wr = jnp.pad(w_router_o[j].astype(F32), ((0, 0), (0, LANES - N_EXPERTS)))
            wr3 = jnp.stack(_split3(wr))
            h = _moe(h, row(ln_ffn_o[j]), wr3, w_moe_gate_o[j].astype(BF16), w_moe_up_o[j].astype(BF16),
                     w_moe_down_o[j].astype(BF16))

    return _final_norm(h.reshape(b, lp, d), row(ln_final), seq)
```

```python
import functools
import math

import jax
import jax.numpy as jnp
from jax import lax
from jax.experimental import pallas as pl
from jax.experimental.pallas import tpu as pltpu

F32 = jnp.float32
BF16 = jnp.bfloat16

D_MODEL = 1024
CHUNK = 64
N_META = 16
EPS = 1e-6
NEG_INF = -1e30

HEAD_DIM = 64
SB_HEADS = 8
MLA_HEADS = 8
MLA_NOPE = 64
MLA_ROPE = 32
MLA_V = 64
MLA_Q_RANK = 256
MLA_KV_RANK = 128
MLA_THETA = 10000.0
DIFF_HEADS = 4
DIFF_DIM = 64
FOX_HEADS = 8
ROPE_THETA = 500000.0
ROT_DIM = HEAD_DIM // 4
N_EXPERTS = 8
TOP_K = 2

SB_W = SB_HEADS * HEAD_DIM
DIFF_W = DIFF_HEADS * 2 * DIFF_DIM
FOX_W = FOX_HEADS * HEAD_DIM

LANES = 128
PAD = 128
FIRST = PAD - N_META
TQ = 128
KEY_WINDOW = 1024
SUFFIX_CHUNK = 256
CHUNK_SHIFT = CHUNK.bit_length() - 1
VMEM_LIMIT = 56 * 1024 * 1024


def _cparams(sem):
    return pltpu.CompilerParams(dimension_semantics=sem, vmem_limit_bytes=VMEM_LIMIT)


def _pick(n, cands):
    for c in cands:
        if n % c == 0:
            return c
    raise ValueError(f"no tile in {cands} divides {n}")


def _rms(x, g):
    return x * lax.rsqrt(jnp.mean(x * x, axis=-1, keepdims=True) + EPS) * g


def _dot(a, b):
    return jnp.dot(a, b, preferred_element_type=F32)


def _dot_nt(a, b):
    return lax.dot_general(a, b, (((1,), (1,)), ((), ())), preferred_element_type=F32)


def _rope_chunk(y, tab, shift):
    c = tab[:, 0:LANES]
    s1 = tab[:, LANES:2 * LANES]
    s2 = tab[:, 2 * LANES:3 * LANES]
    return (y * c + pltpu.roll(y, LANES - shift, 1) * s1 + pltpu.roll(y, shift, 1) * s2)


def _split3(x):
    a = x.astype(BF16)
    r = x - a.astype(F32)
    b = r.astype(BF16)
    c = (r - b.astype(F32)).astype(BF16)
    return a, b, c


def _even_proj_kernel(h_ref, g_ref, w_ref, sb_ref, lat_ref):
    hn = _rms(h_ref[...], g_ref[...]).astype(BF16)
    y = _dot(hn, w_ref[...])
    n_sb = sb_ref.shape[-1]
    sb_ref[...] = y[:, :n_sb].astype(BF16)
    lat_ref[...] = y[:, n_sb:].astype(BF16)


def _even_proj(h, g, w):
    t, d = h.shape
    n = w.shape[1]
    n_sb = 3 * SB_W
    tm = _pick(t, (512, 384, 256, 128))
    return pl.pallas_call(
        _even_proj_kernel,
        grid=(t // tm,),
        in_specs=[pl.BlockSpec((tm, d), lambda i: (i, 0)),
                  pl.BlockSpec((1, d), lambda i: (0, 0)),
                  pl.BlockSpec((d, n), lambda i: (0, 0))],
        out_specs=[pl.BlockSpec((tm, n_sb), lambda i: (i, 0)),
                   pl.BlockSpec((tm, n - n_sb), lambda i: (i, 0))],
        out_shape=[jax.ShapeDtypeStruct((t, n_sb), BF16),
                   jax.ShapeDtypeStruct((t, n - n_sb), BF16)],
        compiler_params=_cparams(("parallel",)),
        name="even_proj",
    )(h, g, w)


def _mla_up_kernel(lat_ref, gq_ref, gkv_ref, wq_ref, wk_ref, wv_ref, tq_ref, tk_ref,
                   q_ref, k_ref, v_ref):
    lat = lat_ref[0]
    nq = _rms(lat[:, :MLA_Q_RANK].astype(F32), gq_ref[...]).astype(BF16)
    nkv = _rms(lat[:, MLA_Q_RANK:MLA_Q_RANK + MLA_KV_RANK].astype(F32), gkv_ref[...]).astype(BF16)
    yq = _dot(nq, wq_ref[...])
    xk = jnp.concatenate([nkv, lat[:, MLA_Q_RANK + MLA_KV_RANK:]], axis=1)
    yk = _dot(xk, wk_ref[...])
    v_ref[0] = _dot(nkv, wv_ref[...]).astype(BF16)
    tabq = tq_ref[...]
    tabk = tk_ref[...]
    for hd in range(MLA_HEADS):
        sl = slice(hd * LANES, (hd + 1) * LANES)
        q_ref[0, :, sl] = _rope_chunk(yq[:, sl], tabq, MLA_ROPE // 2).astype(BF16)
        k_ref[0, :, sl] = _rope_chunk(yk[:, sl], tabk, MLA_ROPE // 2).astype(BF16)


def _mla_up(lat, gq, gkv, wq, wk, wv, tabq, tabk):
    b, lp, nl = lat.shape
    tm = _pick(lp, (384, 128))
    nqk = MLA_HEADS * LANES
    nv = MLA_HEADS * MLA_V
    full = lambda shp: pl.BlockSpec(shp, lambda bi, i: (0,) * len(shp))
    return pl.pallas_call(
        _mla_up_kernel,
        grid=(b, lp // tm),
        in_specs=[pl.BlockSpec((1, tm, nl), lambda bi, i: (bi, i, 0)),
                  full(gq.shape), full(gkv.shape), full(wq.shape), full(wk.shape), full(wv.shape),
                  pl.BlockSpec((tm, 3 * LANES), lambda bi, i: (i, 0)),
                  pl.BlockSpec((tm, 3 * LANES), lambda bi, i: (i, 0))],
        out_specs=[pl.BlockSpec((1, tm, nqk), lambda bi, i: (bi, i, 0)),
                   pl.BlockSpec((1, tm, nqk), lambda bi, i: (bi, i, 0)),
                   pl.BlockSpec((1, tm, nv), lambda bi, i: (bi, i, 0))],
        out_shape=[jax.ShapeDtypeStruct((b, lp, nqk), BF16),
                   jax.ShapeDtypeStruct((b, lp, nqk), BF16),
                   jax.ShapeDtypeStruct((b, lp, nv), BF16)],
        compiler_params=_cparams(("parallel", "parallel")),
        name="mla_up",
    )(lat, gq, gkv, wq, wk, wv, tabq, tabk)


N_ODD_MAIN = 3 * DIFF_W + 3 * FOX_W
F_ROWS = 16


def _odd_proj_kernel(h_ref, g_ref, w_ref, wft_ref, bcol_ref, brow_ref, tab_ref,
                     main_ref, fcol_ref, frow_ref):
    hn = _rms(h_ref[0], g_ref[...]).astype(BF16)
    y = _dot(hn, w_ref[...])
    tab = tab_ref[...]
    n_rope = 2 * DIFF_W // LANES
    for c in range(n_rope):
        sl = slice(c * LANES, (c + 1) * LANES)
        main_ref[0, :, sl] = _rope_chunk(y[:, sl], tab, ROT_DIM // 2).astype(BF16)
    main_ref[0, :, 2 * DIFF_W:] = y[:, 2 * DIFF_W:N_ODD_MAIN].astype(BF16)
    fcol_ref[0] = y[:, N_ODD_MAIN:] + bcol_ref[...]
    frow_ref[0] = _dot_nt(wft_ref[...], hn) + brow_ref[...]


def _odd_proj(h3, g, w, wft, bcol, brow, tab):
    b, lp, d = h3.shape
    tm = _pick(lp, (384, 128))
    full = lambda shp: pl.BlockSpec(shp, lambda bi, i: (0,) * len(shp))
    return pl.pallas_call(
        _odd_proj_kernel,
        grid=(b, lp // tm),
        in_specs=[pl.BlockSpec((1, tm, d), lambda bi, i: (bi, i, 0)),
                  full(g.shape), full(w.shape), full(wft.shape), full(bcol.shape), full(brow.shape),
                  pl.BlockSpec((tm, 3 * LANES), lambda bi, i: (i, 0))],
        out_specs=[pl.BlockSpec((1, tm, N_ODD_MAIN), lambda bi, i: (bi, i, 0)),
                   pl.BlockSpec((1, tm, LANES), lambda bi, i: (bi, i, 0)),
                   pl.BlockSpec((1, F_ROWS, tm), lambda bi, i: (bi, 0, i))],
        out_shape=[jax.ShapeDtypeStruct((b, lp, N_ODD_MAIN), BF16),
                   jax.ShapeDtypeStruct((b, lp, LANES), F32),
                   jax.ShapeDtypeStruct((b, F_ROWS, lp), F32)],
        compiler_params=_cparams(("parallel", "parallel")),
        name="odd_proj",
    )(h3, g, w, wft, bcol, brow, tab)


def _out_proj_kernel(h_ref, a_ref, b_ref, wa_ref, wb_ref, o_ref):
    o_ref[...] = h_ref[...] + _dot(a_ref[...], wa_ref[...]) + _dot(b_ref[...], wb_ref[...])


def _out_proj(h, mix_a, mix_b, wa, wb):
    t, d = h.shape
    tm = _pick(t, (512, 384, 256, 128))
    na, nb = mix_a.shape[1], mix_b.shape[1]
    return pl.pallas_call(
        _out_proj_kernel,
        grid=(t // tm,),
        in_specs=[pl.BlockSpec((tm, d), lambda i: (i, 0)),
                  pl.BlockSpec((tm, na), lambda i: (i, 0)),
                  pl.BlockSpec((tm, nb), lambda i: (i, 0)),
                  pl.BlockSpec((na, d), lambda i: (0, 0)),
                  pl.BlockSpec((nb, d), lambda i: (0, 0))],
        out_specs=pl.BlockSpec((tm, d), lambda i: (i, 0)),
        out_shape=jax.ShapeDtypeStruct((t, d), F32),
        compiler_params=_cparams(("parallel",)),
        name="out_proj",
    )(h, mix_a, mix_b, wa, wb)


def _log_sigmoid(x):
    return jnp.minimum(x, 0.0) - jnp.log(1.0 + jnp.exp(-jnp.abs(x)))


def _fox_cumsum_kernel(fcol_ref, frow_ref, ccol_ref, crow_ref):
    lp = fcol_ref.shape[1]
    blk = LANES
    r = lax.broadcasted_iota(jnp.int32, (blk, blk), 0)
    c = lax.broadcasted_iota(jnp.int32, (blk, blk), 1)
    tri_l = (c <= r).astype(BF16)
    tri_u = (r <= c).astype(BF16)
    row_id = lax.broadcasted_iota(jnp.int32, (blk, LANES), 0)
    col_id = lax.broadcasted_iota(jnp.int32, (F_ROWS, blk), 1)
    carry_c = jnp.zeros((1, LANES), F32)
    carry_r = jnp.zeros((F_ROWS, 1), F32)
    for j in range(lp // blk):
        sl = slice(j * blk, (j + 1) * blk)
        ls = _log_sigmoid(fcol_ref[0, sl, :])
        ls = jnp.where(row_id + j * blk >= FIRST, ls, 0.0)
        a, b, c3 = _split3(ls)
        cs = _dot(tri_l, a) + _dot(tri_l, b) + _dot(tri_l, c3) + carry_c
        ccol_ref[0, sl, :] = cs
        carry_c = cs[blk - 1:blk, :]
        lr = _log_sigmoid(frow_ref[0, :, sl])
        lr = jnp.where(col_id + j * blk >= FIRST, lr, 0.0)
        a, b, c3 = _split3(lr)
        cr = _dot(a, tri_u) + _dot(b, tri_u) + _dot(c3, tri_u) + carry_r
        crow_ref[0, :, sl] = cr
        carry_r = cr[:, blk - 1:blk]


def _fox_cumsum(fcol, frow):
    b, lp, _ = fcol.shape
    return pl.pallas_call(
        _fox_cumsum_kernel,
        grid=(b,),
        in_specs=[pl.BlockSpec((1, lp, LANES), lambda bi: (bi, 0, 0)),
                  pl.BlockSpec((1, F_ROWS, lp), lambda bi: (bi, 0, 0))],
        out_specs=[pl.BlockSpec((1, lp, LANES), lambda bi: (bi, 0, 0)),
                   pl.BlockSpec((1, F_ROWS, lp), lambda bi: (bi, 0, 0))],
        out_shape=[jax.ShapeDtypeStruct((b, lp, LANES), F32),
                   jax.ShapeDtypeStruct((b, F_ROWS, lp), F32)],
        compiler_params=_cparams(("parallel",)),
        name="fox_cumsum",
    )(fcol, frow)


def _rows(i, n):
    if isinstance(i, int):
        return pl.ds(i * n, n)
    return pl.ds(pl.multiple_of(i * n, n), n)


def _keys(k0, w):
    if isinstance(k0, int):
        return pl.ds(k0, w)
    return pl.ds(pl.multiple_of(k0, TQ), w)


def _mask(kind, q0, k0, w, lim=None):
    r = lax.broadcasted_iota(jnp.int32, (2 * TQ, w), 0)
    qs = q0 + (r & (TQ - 1))
    ks = k0 + lax.broadcasted_iota(jnp.int32, (2 * TQ, w), 1)
    if kind == "chunk":
        vis = (ks >> CHUNK_SHIFT) <= (qs >> CHUNK_SHIFT)
    elif kind == "causal":
        vis = ks <= qs
    else:
        vis = ks < qs
    vis = vis & (ks >= FIRST)
    if lim is not None:
        vis = vis & (ks < lim)
    return vis


def _sweep_plan(qi, g):
    m = qi // g
    pre = qi + 1 - m * g
    return m, pre


def _lane_halves():
    lane = lax.broadcasted_iota(jnp.int32, (TQ, LANES), 1)
    return lane < (LANES // 2)


def _stack_masked(q):
    lo = _lane_halves()
    z = jnp.zeros_like(q)
    return jnp.concatenate([jnp.where(lo, q, z), jnp.where(lo, z, q)], axis=0)


def _softmax_sweep(kind, w, prep, scores_fn, v_fn):
    def step(carry, s, v, mask):
        m, l, acc = carry
        if mask is not None:
            s = jnp.where(mask, s, NEG_INF)
        m_new = jnp.maximum(m, jnp.max(s, axis=-1, keepdims=True))
        alpha = jnp.exp(m - m_new)
        p = jnp.exp(s - m_new)
        l = alpha * l + jnp.sum(p, axis=-1, keepdims=True)
        acc = alpha * acc + _dot(p.astype(BF16), v)
        return m_new, l, acc

    def run(qi, near):
        g = w // TQ
        ctx = prep(qi)
        q0 = qi * TQ
        mwin, pre = _sweep_plan(qi, g)
        init = (jnp.full((2 * TQ, 1), NEG_INF, F32), jnp.zeros((2 * TQ, 1), F32),
                jnp.zeros((2 * TQ, LANES), F32))
        carry = step(init, scores_fn(ctx, 0), v_fn(0), _mask(kind, q0, 0, w, lim=pre * TQ))
        if near:
            return carry

        def mid(t, c):
            k0 = (pre + (t - 1) * g) * TQ
            return step(c, scores_fn(ctx, k0), v_fn(k0), None)

        carry = lax.fori_loop(1, mwin, mid, carry)
        k0 = q0 + TQ - w
        return step(carry, scores_fn(ctx, k0), v_fn(k0), _mask(kind, q0, k0, w))

    return run


def _for_each_query_block(nq, w, run, emit):
    n_near = min(w // TQ, nq)

    def near_body(qi, _):
        emit(qi, run(qi, True))
        return 0

    def far_body(qi, _):
        emit(qi, run(qi, False))
        return 0

    lax.fori_loop(0, n_near, near_body, 0)
    lax.fori_loop(n_near, nq, far_body, 0)


def _key_window(lp):
    return min(KEY_WINDOW, lp)


def _merge_heads(top, bottom):
    return jnp.where(_lane_halves(), top, bottom)


def _attn_call(kernel, name, b, lp, n_blocks, in_specs, args):
    return pl.pallas_call(
        kernel,
        grid=(b, n_blocks),
        in_specs=in_specs,
        out_specs=pl.BlockSpec((1, lp, LANES), lambda bi, hp: (bi, 0, hp)),
        out_shape=jax.ShapeDtypeStruct((b, lp, n_blocks * LANES), BF16),
        compiler_params=_cparams(("parallel", "parallel")),
        name=name,
    )(*args)


def _col_spec(lp, off):
    return pl.BlockSpec((1, lp, LANES), lambda bi, hp: (bi, 0, off + hp))


def _mla_attn_kernel(q_ref, k_ref, v_ref, o_ref):
    lp = q_ref.shape[1]
    w = _key_window(lp)

    def prep(qi):
        return q_ref[0, _rows(qi, TQ), 0:LANES], q_ref[0, _rows(qi, TQ), LANES:2 * LANES]

    def scores(ctx, k0):
        qa, qb = ctx
        ka = k_ref[0, _keys(k0, w), 0:LANES]
        kb = k_ref[0, _keys(k0, w), LANES:2 * LANES]
        return jnp.concatenate([_dot_nt(qa, ka), _dot_nt(qb, kb)], axis=0)

    run = _softmax_sweep("chunk", w, prep, scores, lambda k0: v_ref[0, _keys(k0, w), :])

    def emit(qi, res):
        _, l, acc = res
        o = acc / l
        o_ref[0, _rows(qi, TQ), :] = _merge_heads(o[:TQ], o[TQ:]).astype(o_ref.dtype)

    _for_each_query_block(lp // TQ, w, run, emit)


def _mla_attn(q, k, v):
    b, lp, _ = q.shape
    wide = lambda: pl.BlockSpec((1, lp, 2 * LANES), lambda bi, hp: (bi, 0, hp))
    return _attn_call(_mla_attn_kernel, "mla_attn", b, lp, MLA_HEADS // 2,
                      [wide(), wide(), _col_spec(lp, 0)], (q, k, v))


def _diff_attn_kernel(lam_init, x_q, x_k, x_v, lam_ref, gsub_ref, o_ref):
    lp = x_q.shape[1]
    w = _key_window(lp)
    lv = lam_ref[...]
    lam = (jnp.exp(jnp.sum(lv[0:1] * lv[1:2], axis=-1, keepdims=True))
           - jnp.exp(jnp.sum(lv[2:3] * lv[3:4], axis=-1, keepdims=True)) + lam_init)
    gsub = gsub_ref[...]

    def prep(qi):
        return _stack_masked(x_q[0, _rows(qi, TQ), :])

    def scores(qs, k0):
        return _dot_nt(qs, x_k[0, _keys(k0, w), :])

    run = _softmax_sweep("chunk", w, prep, scores, lambda k0: x_v[0, _keys(k0, w), :])

    def emit(qi, res):
        _, l, acc = res
        o = acc / l
        o = o[:TQ] - lam * o[TQ:]
        o_ref[0, _rows(qi, TQ), :] = (_rms(o, gsub) * (1.0 - lam_init)).astype(o_ref.dtype)

    _for_each_query_block(lp // TQ, w, run, emit)


def _diff_attn(main, lam_vecs, gsub, lam_init):
    b, lp, _ = main.shape
    nb = DIFF_W // LANES
    full = lambda shp: pl.BlockSpec(shp, lambda bi, hp: (0,) * len(shp))
    return _attn_call(functools.partial(_diff_attn_kernel, lam_init), "diff_attn", b, lp, DIFF_HEADS,
                      [_col_spec(lp, 0), _col_spec(lp, nb), _col_spec(lp, 2 * nb),
                       full(lam_vecs.shape), full(gsub.shape)],
                      (main, main, main, lam_vecs, gsub))


def _fox_attn_kernel(x_q, x_k, x_v, ccol_ref, crow_ref, o_ref):
    lp = x_q.shape[1]
    w = _key_window(lp)
    hp = pl.program_id(1)
    lane = lax.broadcasted_iota(jnp.int32, (TQ, LANES), 1)

    def prep(qi):
        cc = ccol_ref[0, _rows(qi, TQ), :]
        cq0 = jnp.sum(jnp.where(lane == 2 * hp, cc, 0.0), axis=-1, keepdims=True)
        cq1 = jnp.sum(jnp.where(lane == 2 * hp + 1, cc, 0.0), axis=-1, keepdims=True)
        return _stack_masked(x_q[0, _rows(qi, TQ), :]), cq0, cq1

    def scores(ctx, k0):
        qs, cq0, cq1 = ctx
        s = _dot_nt(qs, x_k[0, _keys(k0, w), :])
        ck = crow_ref[0, 0, :, _keys(k0, w)]
        return s + jnp.concatenate([cq0 - ck[0:1, :], cq1 - ck[1:2, :]], axis=0)

    run = _softmax_sweep("causal", w, prep, scores, lambda k0: x_v[0, _keys(k0, w), :])

    def emit(qi, res):
        _, l, acc = res
        o = acc / l
        o_ref[0, _rows(qi, TQ), :] = _merge_heads(o[:TQ], o[TQ:]).astype(o_ref.dtype)

    _for_each_query_block(lp // TQ, w, run, emit)


def _fox_attn(main, ccol, crow4):
    b, lp, _ = main.shape
    off = 3 * DIFF_W // LANES
    nb = FOX_W // LANES
    return _attn_call(_fox_attn_kernel, "fox_attn", b, lp, nb,
                      [_col_spec(lp, off), _col_spec(lp, off + nb), _col_spec(lp, off + 2 * nb),
                       pl.BlockSpec((1, lp, LANES), lambda bi, hp: (bi, 0, 0)),
                       pl.BlockSpec((1, 1, 2, lp), lambda bi, hp: (bi, hp, 0, 0))],
                      (main, main, main, ccol, crow4))


def _sb_attn_kernel(x_q, x_k, x_v, o_ref):
    lp = x_q.shape[1]
    w = _key_window(lp)
    g = w // TQ
    cs = SUFFIX_CHUNK if w % SUFFIX_CHUNK == 0 else TQ
    kr = lax.broadcasted_iota(jnp.int32, (cs, cs), 0)
    kc = lax.broadcasted_iota(jnp.int32, (cs, cs), 1)
    later = (kr > kc).astype(BF16)

    def step(carry, qs, k0, mask):
        rsum, acc = carry
        z = _dot_nt(qs, x_k[0, _keys(k0, w), :])
        log_1mb = -(jnp.maximum(z, 0.0) + jnp.log(1.0 + jnp.exp(-jnp.abs(z))))
        if mask is not None:
            log_1mb = jnp.where(mask, log_1mb, 0.0)
        hi = log_1mb.astype(BF16)
        lo = (log_1mb - hi.astype(F32)).astype(BF16)
        parts = []
        for c in reversed(range(w // cs)):
            sl = slice(c * cs, (c + 1) * cs)
            parts.append(_dot(hi[:, sl], later) + _dot(lo[:, sl], later) + rsum)
            rsum = rsum + jnp.sum(log_1mb[:, sl], axis=-1, keepdims=True)
        between = jnp.concatenate(parts[::-1], axis=1)
        wgt = jnp.exp(z + log_1mb + between)
        if mask is not None:
            wgt = jnp.where(mask, wgt, 0.0)
        acc = acc + _dot(wgt.astype(BF16), x_v[0, _keys(k0, w), :])
        return rsum, acc

    def run(qi, near):
        q0 = qi * TQ
        qs = _stack_masked(x_q[0, _rows(qi, TQ), :])
        mwin, pre = _sweep_plan(qi, g)
        carry = (jnp.zeros((2 * TQ, 1), F32), jnp.zeros((2 * TQ, LANES), F32))
        if not near:
            k0 = q0 + TQ - w
            carry = step(carry, qs, k0, _mask("strict", q0, k0, w))

            def mid(it, c):
                return step(c, qs, (pre + (mwin - 1 - it) * g) * TQ, None)

            carry = lax.fori_loop(1, mwin, mid, carry)
        return step(carry, qs, 0, _mask("strict", q0, 0, w, lim=pre * TQ))

    def emit(qi, res):
        _, acc = res
        o_ref[0, _rows(qi, TQ), :] = _merge_heads(acc[:TQ], acc[TQ:]).astype(o_ref.dtype)

    _for_each_query_block(lp // TQ, w, run, emit)


def _sb_attn(sb):
    b, lp, _ = sb.shape
    nb = SB_W // LANES
    return _attn_call(_sb_attn_kernel, "sb_attn", b, lp, nb,
                      [_col_spec(lp, 0), _col_spec(lp, nb), _col_spec(lp, 2 * nb)],
                      (sb, sb, sb))


def _silu(x):
    return x / (1.0 + jnp.exp(-x))


def _ffn_kernel(h_ref, g_ref, wg_ref, wu_ref, wd_ref, o_ref, hn_ref, acc_ref):
    f = pl.program_id(1)

    @pl.when(f == 0)
    def _():
        hn_ref[...] = _rms(h_ref[...], g_ref[...]).astype(BF16)
        acc_ref[...] = jnp.zeros_like(acc_ref)

    hn = hn_ref[...]
    a = _silu(_dot(hn, wg_ref[...])) * _dot(hn, wu_ref[...])
    acc_ref[...] += _dot(a.astype(BF16), wd_ref[...])

    @pl.when(f == pl.num_programs(1) - 1)
    def _():
        o_ref[...] = h_ref[...] + acc_ref[...]


def _ffn(h, g, wg, wu, wd):
    t, d = h.shape
    ff = wg.shape[1]
    tm = _pick(t, (512, 384, 256, 128))
    tf = _pick(ff, (1408, 512, 256, 128))
    return pl.pallas_call(
        _ffn_kernel,
        grid=(t // tm, ff // tf),
        in_specs=[pl.BlockSpec((tm, d), lambda i, f: (i, 0)),
                  pl.BlockSpec((1, d), lambda i, f: (0, 0)),
                  pl.BlockSpec((d, tf), lambda i, f: (0, f)),
                  pl.BlockSpec((d, tf), lambda i, f: (0, f)),
                  pl.BlockSpec((tf, d), lambda i, f: (f, 0))],
        out_specs=pl.BlockSpec((tm, d), lambda i, f: (i, 0)),
        out_shape=jax.ShapeDtypeStruct((t, d), F32),
        scratch_shapes=[pltpu.VMEM((tm, d), BF16), pltpu.VMEM((tm, d), F32)],
        compiler_params=_cparams(("parallel", "arbitrary")),
        name="ffn",
    )(h, g, wg, wu, wd)


def _router_gates(hn, wr_ref):
    a, b, c = _split3(hn)
    wa, wb, wc = wr_ref[0], wr_ref[1], wr_ref[2]
    logits = (_dot(a, wa) + _dot(a, wb) + _dot(b, wa) + _dot(a, wc) + _dot(b, wb) + _dot(c, wa))
    lane = lax.broadcasted_iota(jnp.int32, logits.shape, 1).astype(F32)
    logits = jnp.where(lane < N_EXPERTS, logits, -jnp.inf)
    v1 = jnp.max(logits, axis=-1, keepdims=True)
    i1 = jnp.min(jnp.where(logits == v1, lane, float(LANES)), axis=-1, keepdims=True)
    rest = jnp.where(lane == i1, -jnp.inf, logits)
    v2 = jnp.max(rest, axis=-1, keepdims=True)
    i2 = jnp.min(jnp.where(rest == v2, lane, float(LANES)), axis=-1, keepdims=True)
    e2 = jnp.exp(v2 - v1)
    den = 1.0 + e2
    return jnp.where(lane == i1, 1.0 / den, 0.0) + jnp.where(lane == i2, e2 / den, 0.0)


def _moe_kernel(h_ref, g_ref, wr_ref, wg_ref, wu_ref, wd_ref, o_ref, hn_ref, gate_ref, acc_ref):
    e = pl.program_id(1)
    f = pl.program_id(2)

    @pl.when((e == 0) & (f == 0))
    def _():
        hn = _rms(h_ref[...], g_ref[...])
        hn_ref[...] = hn.astype(BF16)
        gate_ref[...] = _router_gates(hn, wr_ref)
        acc_ref[...] = jnp.zeros_like(acc_ref)

    hn = hn_ref[...]
    lane = lax.broadcasted_iota(jnp.int32, gate_ref.shape, 1)
    ge = jnp.sum(jnp.where(lane == e, gate_ref[...], 0.0), axis=-1, keepdims=True)
    a = _silu(_dot(hn, wg_ref[0])) * _dot(hn, wu_ref[0])
    acc_ref[...] += ge * _dot(a.astype(BF16), wd_ref[0])

    @pl.when((e == pl.num_programs(1) - 1) & (f == pl.num_programs(2) - 1))
    def _():
        o_ref[...] = h_ref[...] + acc_ref[...]


def _moe(h, g, wr3, wg, wu, wd):
    t, d = h.shape
    ne, _, ff = wg.shape
    tm = _pick(t, (1024, 768, 512, 384, 256, 128))
    tf = _pick(ff, (512, 256, 128))
    return pl.pallas_call(
        _moe_kernel,
        grid=(t // tm, ne, ff // tf),
        in_specs=[pl.BlockSpec((tm, d), lambda i, e, f: (i, 0)),
                  pl.BlockSpec((1, d), lambda i, e, f: (0, 0)),
                  pl.BlockSpec((3, d, LANES), lambda i, e, f: (0, 0, 0)),
                  pl.BlockSpec((1, d, tf), lambda i, e, f: (e, 0, f)),
                  pl.BlockSpec((1, d, tf), lambda i, e, f: (e, 0, f)),
                  pl.BlockSpec((1, tf, d), lambda i, e, f: (e, f, 0))],
        out_specs=pl.BlockSpec((tm, d), lambda i, e, f: (i, 0)),
        out_shape=jax.ShapeDtypeStruct((t, d), F32),
        scratch_shapes=[pltpu.VMEM((tm, d), BF16), pltpu.VMEM((tm, LANES), F32),
                        pltpu.VMEM((tm, d), F32)],
        compiler_params=_cparams(("parallel", "arbitrary", "arbitrary")),
        name="moe",
    )(h, g, wr3, wg, wu, wd)


def _final_norm_kernel(h_ref, g_ref, o_ref):
    o_ref[0] = _rms(h_ref[0], g_ref[...])


def _final_norm(h3, g, seq):
    b, lp, d = h3.shape
    tm = PAD
    return pl.pallas_call(
        _final_norm_kernel,
        grid=(b, seq // tm),
        in_specs=[pl.BlockSpec((1, tm, d), lambda bi, i: (bi, i + PAD // tm, 0)),
                  pl.BlockSpec((1, d), lambda bi, i: (0, 0))],
        out_specs=pl.BlockSpec((1, tm, d), lambda bi, i: (bi, i, 0)),
        out_shape=jax.ShapeDtypeStruct((b, seq, d), F32),
        compiler_params=_cparams(("parallel", "parallel")),
        name="final_norm",
    )(h3, g)


def _rope_table(lp, rot_dim, theta, width, offset, scale):
    half = rot_dim // 2
    pos = (jnp.arange(lp, dtype=jnp.int32) - FIRST).astype(F32)
    inv = theta ** (-jnp.arange(half, dtype=F32) * 2.0 / rot_dim)
    ang = pos[:, None] * inv[None, :]
    cos, sin = jnp.cos(ang), jnp.sin(ang)
    ones = jnp.ones((lp, 1), F32)
    zeros = jnp.zeros((lp, 1), F32)

    def unit(first, second, fill):
        parts = [jnp.tile(fill, (1, offset)), first, second,
                 jnp.tile(fill, (1, width - offset - rot_dim))]
        return jnp.tile(jnp.concatenate(parts, axis=1), (1, LANES // width))

    c = unit(cos, cos, ones)
    s1 = unit(-sin, jnp.zeros_like(sin), zeros)
    s2 = unit(jnp.zeros_like(sin), sin, zeros)
    return jnp.concatenate([c, s1, s2], axis=1) * scale


def _pad_heads(w, heads, used, width):
    k = w.shape[0]
    w = w.reshape(k, heads, used)
    return jnp.pad(w, ((0, 0), (0, 0), (0, width - used))).reshape(k, heads * width)


def kernel(x, meta_tokens, ln_mix_e, w_in_e, ln_mla_q_e, w_mla_uq_e, ln_mla_kv_e, w_mla_ukv_e,
           w_out_e, ln_ffn_e, w_ffn_gate_e, w_ffn_up_e, w_ffn_down_e, ln_mix_o, w_in_o, b_forget_o,
           diff_lq1_o, diff_lk1_o, diff_lq2_o, diff_lk2_o, diff_subln_o, w_out_o, ln_ffn_o,
           w_router_o, w_moe_gate_o, w_moe_up_o, w_moe_down_o, ln_final):
    b, seq, d = x.shape
    assert d == D_MODEL and seq % TQ == 0
    lp = PAD + seq
    t = b * lp
    depth = 2

    meta = jnp.broadcast_to(meta_tokens.astype(x.dtype)[None], (b, N_META, d))
    h = jnp.concatenate([jnp.zeros((b, FIRST, d), x.dtype), meta, x], axis=1).reshape(t, d)

    row = lambda v: v.reshape(1, -1).astype(F32)
    sb_scale = HEAD_DIM ** -0.5
    tab_mla_q = _rope_table(lp, MLA_ROPE, MLA_THETA, LANES, MLA_NOPE, (MLA_NOPE + MLA_ROPE) ** -0.5)
    tab_mla_k = _rope_table(lp, MLA_ROPE, MLA_THETA, LANES, MLA_NOPE, 1.0)
    tab_diff = _rope_table(lp, ROT_DIM, ROPE_THETA, DIFF_DIM, 0, 1.0)

    for i in range(depth):
        j = i // 2
        if i % 2 == 0:
            w_in = w_in_e[j]
            lat_w = MLA_Q_RANK + MLA_KV_RANK + MLA_ROPE
            w_cat = jnp.concatenate(
                [w_in[:, :SB_W] * sb_scale, w_in[:, SB_W:3 * SB_W + lat_w],
                 jnp.zeros((d, 4 * LANES - lat_w), F32)], axis=1).astype(BF16)
            sb, lat = _even_proj(h, row(ln_mix_e[j]), w_cat)
            sb = sb.reshape(b, lp, -1)
            lat = lat.reshape(b, lp, -1)

            wq = _pad_heads(w_mla_uq_e[j], MLA_HEADS, MLA_NOPE + MLA_ROPE, LANES).astype(BF16)
            ukv = w_mla_ukv_e[j].reshape(MLA_KV_RANK, MLA_HEADS, MLA_NOPE + MLA_V)
            wk_nope = _pad_heads(ukv[:, :, :MLA_NOPE].reshape(MLA_KV_RANK, -1), MLA_HEADS, MLA_NOPE, LANES)
            place = jnp.pad(jnp.eye(MLA_ROPE, dtype=F32), ((0, 0), (MLA_NOPE, LANES - MLA_NOPE - MLA_ROPE)))
            wk_rope = jnp.tile(place, (1, MLA_HEADS))
            wk = jnp.concatenate(
                [wk_nope, wk_rope, jnp.zeros((LANES - MLA_ROPE, MLA_HEADS * LANES), F32)], axis=0).astype(BF16)
            wv = ukv[:, :, MLA_NOPE:].reshape(MLA_KV_RANK, -1).astype(BF16)
            mq, mk, mv = _mla_up(lat, row(ln_mla_q_e[j]), row(ln_mla_kv_e[j]), wq, wk, wv,
                                 tab_mla_q, tab_mla_k)

            a_out = _sb_attn(sb).reshape(t, -1)
            b_out = _mla_attn(mq, mk, mv).reshape(t, -1)
            w_out = w_out_e[j].astype(BF16)
            h = _out_proj(h, a_out, b_out, w_out[:SB_W], w_out[SB_W:])
            h = _ffn(h, row(ln_ffn_e[j]), w_ffn_gate_e[j].astype(BF16), w_ffn_up_e[j].astype(BF16),
                     w_ffn_down_e[j].astype(BF16))
        else:
            lam_init = 0.8 - 0.6 * math.exp(-0.3 * i)
            w_in = w_in_o[j]
            wf = w_in[:, N_ODD_MAIN:]
            w_cat = jnp.concatenate(
                [w_in[:, :DIFF_W] * sb_scale, w_in[:, DIFF_W:3 * DIFF_W],
                 w_in[:, 3 * DIFF_W:3 * DIFF_W + FOX_W] * sb_scale, w_in[:, 3 * DIFF_W + FOX_W:N_ODD_MAIN],
                 wf, jnp.zeros((d, LANES - FOX_HEADS), F32)], axis=1).astype(BF16)
            wft = jnp.pad(wf.T, ((0, F_ROWS - FOX_HEADS), (0, 0))).astype(BF16)
            bf = b_forget_o[j].astype(F32)
            bcol = jnp.pad(bf, (0, LANES - FOX_HEADS)).reshape(1, LANES)
            brow = jnp.pad(bf, (0, F_ROWS - FOX_HEADS)).reshape(F_ROWS, 1)
            main, fcol, frow = _odd_proj(h.reshape(b, lp, d), row(ln_mix_o[j]), w_cat, wft, bcol, brow,
                                         tab_diff)
            ccol, crow = _fox_cumsum(fcol, frow)
            crow4 = crow[:, :FOX_HEADS].reshape(b, FOX_HEADS // 2, 2, lp)
            lam_vecs = jnp.stack([diff_lq1_o[j], diff_lk1_o[j], diff_lq2_o[j], diff_lk2_o[j]]).astype(F32)
            c_out = _diff_attn(main, lam_vecs, row(diff_subln_o[j]), lam_init).reshape(t, -1)
            d_out = _fox_attn(main, ccol, crow4).reshape(t, -1)
            w_out = w_out_o[j].astype(BF16)
            h = _out_proj(h, c_out, d_out, w_out[:DIFF_W], w_out[DIFF_W:])
            wr = jnp.pad(w_router_o[j].astype(F32), ((0, 0), (0, LANES - N_EXPERTS)))
            wr3 = jnp.stack(_split3(wr))
            h = _moe(h, row(ln_ffn_o[j]), wr3, w_moe_gate_o[j].astype(BF16), w_moe_up_o[j].astype(BF16),
                     w_moe_down_o[j].astype(BF16))

    return _final_norm(h.reshape(b, lp, d), row(ln_final), seq)
```

```python
import functools
import math

import jax
import jax.numpy as jnp
from jax import lax
from jax.experimental import pallas as pl
from jax.experimental.pallas import tpu as pltpu

F32 = jnp.float32
BF16 = jnp.bfloat16

D_MODEL = 1024
CHUNK = 64
N_META = 16
EPS = 1e-6
NEG_INF = -1e30

HEAD_DIM = 64
SB_HEADS = 8
MLA_HEADS = 8
MLA_NOPE = 64
MLA_ROPE = 32
MLA_V = 64
MLA_Q_RANK = 256
MLA_KV_RANK = 128
MLA_THETA = 10000.0
DIFF_HEADS = 4
DIFF_DIM = 64
FOX_HEADS = 8
ROPE_THETA = 500000.0
ROT_DIM = HEAD_DIM // 4
N_EXPERTS = 8
TOP_K = 2

SB_W = SB_HEADS * HEAD_DIM
DIFF_W = DIFF_HEADS * 2 * DIFF_DIM
FOX_W = FOX_HEADS * HEAD_DIM

LANES = 128
PAD = 128
FIRST = PAD - N_META
TQ = 128
KEY_WINDOW = 1024
SUFFIX_CHUNK = 256
CHUNK_SHIFT = CHUNK.bit_length() - 1
VMEM_LIMIT = 56 * 1024 * 1024


def _cparams(sem):
    return pltpu.CompilerParams(dimension_semantics=sem, vmem_limit_bytes=VMEM_LIMIT)


def _pick(n, cands):
    for c in cands:
        if n % c == 0:
            return c
    raise ValueError(f"no tile in {cands} divides {n}")


def _rms(x, g):
    return x * lax.rsqrt(jnp.mean(x * x, axis=-1, keepdims=True) + EPS) * g


def _dot(a, b):
    return jnp.dot(a, b, preferred_element_type=F32)


def _dot_nt(a, b):
    return lax.dot_general(a, b, (((1,), (1,)), ((), ())), preferred_element_type=F32)


def _rope_chunk(y, tab, shift):
    c = tab[:, 0:LANES]
    s1 = tab[:, LANES:2 * LANES]
    s2 = tab[:, 2 * LANES:3 * LANES]
    return (y * c + pltpu.roll(y, LANES - shift, 1) * s1 + pltpu.roll(y, shift, 1) * s2)


def _split3(x):
    a = x.astype(BF16)
    r = x - a.astype(F32)
    b = r.astype(BF16)
    c = (r - b.astype(F32)).astype(BF16)
    return a, b, c


def _even_proj_kernel(h_ref, g_ref, w_ref, sb_ref, lat_ref):
    hn = _rms(h_ref[...], g_ref[...]).astype(BF16)
    y = _dot(hn, w_ref[...])
    n_sb = sb_ref.shape[-1]
    sb_ref[...] = y[:, :n_sb].astype(BF16)
    lat_ref[...] = y[:, n_sb:].astype(BF16)


def _even_proj(h, g, w):
    t, d = h.shape
    n = w.shape[1]
    n_sb = 3 * SB_W
    tm = _pick(t, (512, 384, 256, 128))
    return pl.pallas_call(
        _even_proj_kernel,
        grid=(t // tm,),
        in_specs=[pl.BlockSpec((tm, d), lambda i: (i, 0)),
                  pl.BlockSpec((1, d), lambda i: (0, 0)),
                  pl.BlockSpec((d, n), lambda i: (0, 0))],
        out_specs=[pl.BlockSpec((tm, n_sb), lambda i: (i, 0)),
                   pl.BlockSpec((tm, n - n_sb), lambda i: (i, 0))],
        out_shape=[jax.ShapeDtypeStruct((t, n_sb), BF16),
                   jax.ShapeDtypeStruct((t, n - n_sb), BF16)],
        compiler_params=_cparams(("parallel",)),
        name="even_proj",
    )(h, g, w)


def _mla_up_kernel(lat_ref, gq_ref, gkv_ref, wq_ref, wk_ref, wv_ref, tq_ref, tk_ref,
                   q_ref, k_ref, v_ref):
    lat = lat_ref[0]
    nq = _rms(lat[:, :MLA_Q_RANK].astype(F32), gq_ref[...]).astype(BF16)
    nkv = _rms(lat[:, MLA_Q_RANK:MLA_Q_RANK + MLA_KV_RANK].astype(F32), gkv_ref[...]).astype(BF16)
    yq = _dot(nq, wq_ref[...])
    xk = jnp.concatenate([nkv, lat[:, MLA_Q_RANK + MLA_KV_RANK:]], axis=1)
    yk = _dot(xk, wk_ref[...])
    v_ref[0] = _dot(nkv, wv_ref[...]).astype(BF16)
    tabq = tq_ref[...]
    tabk = tk_ref[...]
    for hd in range(MLA_HEADS):
        sl = slice(hd * LANES, (hd + 1) * LANES)
        q_ref[0, :, sl] = _rope_chunk(yq[:, sl], tabq, MLA_ROPE // 2).astype(BF16)
        k_ref[0, :, sl] = _rope_chunk(yk[:, sl], tabk, MLA_ROPE // 2).astype(BF16)


def _mla_up(lat, gq, gkv, wq, wk, wv, tabq, tabk):
    b, lp, nl = lat.shape
    tm = _pick(lp, (384, 128))
    nqk = MLA_HEADS * LANES
    nv = MLA_HEADS * MLA_V
    full = lambda shp: pl.BlockSpec(shp, lambda bi, i: (0,) * len(shp))
    return pl.pallas_call(
        _mla_up_kernel,
        grid=(b, lp // tm),
        in_specs=[pl.BlockSpec((1, tm, nl), lambda bi, i: (bi, i, 0)),
                  full(gq.shape), full(gkv.shape), full(wq.shape), full(wk.shape), full(wv.shape),
                  pl.BlockSpec((tm, 3 * LANES), lambda bi, i: (i, 0)),
                  pl.BlockSpec((tm, 3 * LANES), lambda bi, i: (i, 0))],
        out_specs=[pl.BlockSpec((1, tm, nqk), lambda bi, i: (bi, i, 0)),
                   pl.BlockSpec((1, tm, nqk), lambda bi, i: (bi, i, 0)),
                   pl.BlockSpec((1, tm, nv), lambda bi, i: (bi, i, 0))],
        out_shape=[jax.ShapeDtypeStruct((b, lp, nqk), BF16),
                   jax.ShapeDtypeStruct((b, lp, nqk), BF16),
                   jax.ShapeDtypeStruct((b, lp, nv), BF16)],
        compiler_params=_cparams(("parallel", "parallel")),
        name="mla_up",
    )(lat, gq, gkv, wq, wk, wv, tabq, tabk)


N_ODD_MAIN = 3 * DIFF_W + 3 * FOX_W
F_ROWS = 16


def _odd_proj_kernel(h_ref, g_ref, w_ref, wft_ref, bcol_ref, brow_ref, tab_ref,
                     main_ref, fcol_ref, frow_ref):
    hn = _rms(h_ref[0], g_ref[...]).astype(BF16)
    y = _dot(hn, w_ref[...])
    tab = tab_ref[...]
    n_rope = 2 * DIFF_W // LANES
    for c in range(n_rope):
        sl = slice(c * LANES, (c + 1) * LANES)
        main_ref[0, :, sl] = _rope_chunk(y[:, sl], tab, ROT_DIM // 2).astype(BF16)
    main_ref[0, :, 2 * DIFF_W:] = y[:, 2 * DIFF_W:N_ODD_MAIN].astype(BF16)
    fcol_ref[0] = y[:, N_ODD_MAIN:] + bcol_ref[...]
    frow_ref[0] = _dot_nt(wft_ref[...], hn) + brow_ref[...]


def _odd_proj(h3, g, w, wft, bcol, brow, tab):
    b, lp, d = h3.shape
    tm = _pick(lp, (384, 128))
    full = lambda shp: pl.BlockSpec(shp, lambda bi, i: (0,) * len(shp))
    return pl.pallas_call(
        _odd_proj_kernel,
        grid=(b, lp // tm),
        in_specs=[pl.BlockSpec((1, tm, d), lambda bi, i: (bi, i, 0)),
                  full(g.shape), full(w.shape), full(wft.shape), full(bcol.shape), full(brow.shape),
                  pl.BlockSpec((tm, 3 * LANES), lambda bi, i: (i, 0))],
        out_specs=[pl.BlockSpec((1, tm, N_ODD_MAIN), lambda bi, i: (bi, i, 0)),
                   pl.BlockSpec((1, tm, LANES), lambda bi, i: (bi, i, 0)),
                   pl.BlockSpec((1, F_ROWS, tm), lambda bi, i: (bi, 0, i))],
        out_shape=[jax.ShapeDtypeStruct((b, lp, N_ODD_MAIN), BF16),
                   jax.ShapeDtypeStruct((b, lp, LANES), F32),
                   jax.ShapeDtypeStruct((b, F_ROWS, lp), F32)],
        compiler_params=_cparams(("parallel", "parallel")),
        name="odd_proj",
    )(h3, g, w, wft, bcol, brow, tab)


def _out_proj_kernel(h_ref, a_ref, b_ref, wa_ref, wb_ref, o_ref):
    o_ref[...] = h_ref[...] + _dot(a_ref[...], wa_ref[...]) + _dot(b_ref[...], wb_ref[...])


def _out_proj(h, mix_a, mix_b, wa, wb):
    t, d = h.shape
    tm = _pick(t, (512, 384, 256, 128))
    na, nb = mix_a.shape[1], mix_b.shape[1]
    return pl.pallas_call(
        _out_proj_kernel,
        grid=(t // tm,),
        in_specs=[pl.BlockSpec((tm, d), lambda i: (i, 0)),
                  pl.BlockSpec((tm, na), lambda i: (i, 0)),
                  pl.BlockSpec((tm, nb), lambda i: (i, 0)),
                  pl.BlockSpec((na, d), lambda i: (0, 0)),
                  pl.BlockSpec((nb, d), lambda i: (0, 0))],
        out_specs=pl.BlockSpec((tm, d), lambda i: (i, 0)),
        out_shape=jax.ShapeDtypeStruct((t, d), F32),
        compiler_params=_cparams(("parallel",)),
        name="out_proj",
    )(h, mix_a, mix_b, wa, wb)


def _log_sigmoid(x):
    return jnp.minimum(x, 0.0) - jnp.log(1.0 + jnp.exp(-jnp.abs(x)))


def _fox_cumsum_kernel(fcol_ref, frow_ref, ccol_ref, crow_ref):
    lp = fcol_ref.shape[1]
    blk = LANES
    r = lax.broadcasted_iota(jnp.int32, (blk, blk), 0)
    c = lax.broadcasted_iota(jnp.int32, (blk, blk), 1)
    tri_l = (c <= r).astype(BF16)
    tri_u = (r <= c).astype(BF16)
    row_id = lax.broadcasted_iota(jnp.int32, (blk, LANES), 0)
    col_id = lax.broadcasted_iota(jnp.int32, (F_ROWS, blk), 1)
    carry_c = jnp.zeros((1, LANES), F32)
    carry_r = jnp.zeros((F_ROWS, 1), F32)
    for j in range(lp // blk):
        sl = slice(j * blk, (j + 1) * blk)
        ls = _log_sigmoid(fcol_ref[0, sl, :])
        ls = jnp.where(row_id + j * blk >= FIRST, ls, 0.0)
        a, b, c3 = _split3(ls)
        cs = _dot(tri_l, a) + _dot(tri_l, b) + _dot(tri_l, c3) + carry_c
        ccol_ref[0, sl, :] = cs
        carry_c = cs[blk - 1:blk, :]
        lr = _log_sigmoid(frow_ref[0, :, sl])
        lr = jnp.where(col_id + j * blk >= FIRST, lr, 0.0)
        a, b, c3 = _split3(lr)
        cr = _dot(a, tri_u) + _dot(b, tri_u) + _dot(c3, tri_u) + carry_r
        crow_ref[0, :, sl] = cr
        carry_r = cr[:, blk - 1:blk]


def _fox_cumsum(fcol, frow):
    b, lp, _ = fcol.shape
    return pl.pallas_call(
        _fox_cumsum_kernel,
        grid=(b,),
        in_specs=[pl.BlockSpec((1, lp, LANES), lambda bi: (bi, 0, 0)),
                  pl.BlockSpec((1, F_ROWS, lp), lambda bi: (bi, 0, 0))],
        out_specs=[pl.BlockSpec((1, lp, LANES), lambda bi: (bi, 0, 0)),
                   pl.BlockSpec((1, F_ROWS, lp), lambda bi: (bi, 0, 0))],
        out_shape=[jax.ShapeDtypeStruct((b, lp, LANES), F32),
                   jax.ShapeDtypeStruct((b, F_ROWS, lp), F32)],
        compiler_params=_cparams(("parallel",)),
        name="fox_cumsum",
    )(fcol, frow)


def _rows(i, n):
    if isinstance(i, int):
        return pl.ds(i * n, n)
    return pl.ds(pl.multiple_of(i * n, n), n)


def _keys(k0, w):
    if isinstance(k0, int):
        return pl.ds(k0, w)
    return pl.ds(pl.multiple_of(k0, TQ), w)


def _mask(kind, q0, k0, w, lim=None):
    r = lax.broadcasted_iota(jnp.int32, (2 * TQ, w), 0)
    qs = q0 + (r & (TQ - 1))
    ks = k0 + lax.broadcasted_iota(jnp.int32, (2 * TQ, w), 1)
    if kind == "chunk":
        vis = (ks >> CHUNK_SHIFT) <= (qs >> CHUNK_SHIFT)
    elif kind == "causal":
        vis = ks <= qs
    else:
        vis = ks < qs
    vis = vis & (ks >= FIRST)
    if lim is not None:
        vis = vis & (ks < lim)
    return vis


def _sweep_plan(qi, g):
    m = qi // g
    pre = qi + 1 - m * g
    return m, pre


def _lane_halves():
    lane = lax.broadcasted_iota(jnp.int32, (TQ, LANES), 1)
    return lane < (LANES // 2)


def _stack_masked(q):
    lo = _lane_halves()
    z = jnp.zeros_like(q)
    return jnp.concatenate([jnp.where(lo, q, z), jnp.where(lo, z, q)], axis=0)


def _softmax_sweep(kind, w, prep, scores_fn, v_fn):
    def step(carry, s, v, mask):
        m, l, acc = carry
        if mask is not None:
            s = jnp.where(mask, s, NEG_INF)
        m_new = jnp.maximum(m, jnp.max(s, axis=-1, keepdims=True))
        alpha = jnp.exp(m - m_new)
        p = jnp.exp(s - m_new)
        l = alpha * l + jnp.sum(p, axis=-1, keepdims=True)
        acc = alpha * acc + _dot(p.astype(BF16), v)
        return m_new, l, acc

    def run(qi, near):
        g = w // TQ
        ctx = prep(qi)
        q0 = qi * TQ
        mwin, pre = _sweep_plan(qi, g)
        init = (jnp.full((2 * TQ, 1), NEG_INF, F32), jnp.zeros((2 * TQ, 1), F32),
                jnp.zeros((2 * TQ, LANES), F32))
        carry = step(init, scores_fn(ctx, 0), v_fn(0), _mask(kind, q0, 0, w, lim=pre * TQ))
        if near:
            return carry

        def mid(t, c):
            k0 = (pre + (t - 1) * g) * TQ
            return step(c, scores_fn(ctx, k0), v_fn(k0), None)

        carry = lax.fori_loop(1, mwin, mid, carry)
        k0 = q0 + TQ - w
        return step(carry, scores_fn(ctx, k0), v_fn(k0), _mask(kind, q0, k0, w))

    return run


def _for_each_query_block(nq, w, run, emit):
    n_near = min(w // TQ, nq)

    def near_body(qi, _):
        emit(qi, run(qi, True))
        return 0

    def far_body(qi, _):
        emit(qi, run(qi, False))
        return 0

    lax.fori_loop(0, n_near, near_body, 0)
    lax.fori_loop(n_near, nq, far_body, 0)


def _key_window(lp):
    return min(KEY_WINDOW, lp)


def _merge_heads(top, bottom):
    return jnp.where(_lane_halves(), top, bottom)


def _attn_call(kernel, name, b, lp, n_blocks, in_specs, args):
    return pl.pallas_call(
        kernel,
        grid=(b, n_blocks),
        in_specs=in_specs,
        out_specs=pl.BlockSpec((1, lp, LANES), lambda bi, hp: (bi, 0, hp)),
        out_shape=jax.ShapeDtypeStruct((b, lp, n_blocks * LANES), BF16),
        compiler_params=_cparams(("parallel", "parallel")),
        name=name,
    )(*args)


def _col_spec(lp, off):
    return pl.BlockSpec((1, lp, LANES), lambda bi, hp: (bi, 0, off + hp))


def _mla_attn_kernel(q_ref, k_ref, v_ref, o_ref):
    lp = q_ref.shape[1]
    w = _key_window(lp)

    def prep(qi):
        return q_ref[0, _rows(qi, TQ), 0:LANES], q_ref[0, _rows(qi, TQ), LANES:2 * LANES]

    def scores(ctx, k0):
        qa, qb = ctx
        ka = k_ref[0, _keys(k0, w), 0:LANES]
        kb = k_ref[0, _keys(k0, w), LANES:2 * LANES]
        return jnp.concatenate([_dot_nt(qa, ka), _dot_nt(qb, kb)], axis=0)

    run = _softmax_sweep("chunk", w, prep, scores, lambda k0: v_ref[0, _keys(k0, w), :])

    def emit(qi, res):
        _, l, acc = res
        o = acc / l
        o_ref[0, _rows(qi, TQ), :] = _merge_heads(o[:TQ], o[TQ:]).astype(o_ref.dtype)

    _for_each_query_block(lp // TQ, w, run, emit)


def _mla_attn(q, k, v):
    b, lp, _ = q.shape
    wide = lambda: pl.BlockSpec((1, lp, 2 * LANES), lambda bi, hp: (bi, 0, hp))
    return _attn_call(_mla_attn_kernel, "mla_attn", b, lp, MLA_HEADS // 2,
                      [wide(), wide(), _col_spec(lp, 0)], (q, k, v))


def _diff_attn_kernel(lam_init, x_q, x_k, x_v, lam_ref, gsub_ref, o_ref):
    lp = x_q.shape[1]
    w = _key_window(lp)
    lv = lam_ref[...]
    lam = (jnp.exp(jnp.sum(lv[0:1] * lv[1:2], axis=-1, keepdims=True))
           - jnp.exp(jnp.sum(lv[2:3] * lv[3:4], axis=-1, keepdims=True)) + lam_init)
    gsub = gsub_ref[...]

    def prep(qi):
        return _stack_masked(x_q[0, _rows(qi, TQ), :])

    def scores(qs, k0):
        return _dot_nt(qs, x_k[0, _keys(k0, w), :])

    run = _softmax_sweep("chunk", w, prep, scores, lambda k0: x_v[0, _keys(k0, w), :])

    def emit(qi, res):
        _, l, acc = res
        o = acc / l
        o = o[:TQ] - lam * o[TQ:]
        o_ref[0, _rows(qi, TQ), :] = (_rms(o, gsub) * (1.0 - lam_init)).astype(o_ref.dtype)

    _for_each_query_block(lp // TQ, w, run, emit)


def _diff_attn(main, lam_vecs, gsub, lam_init):
    b, lp, _ = main.shape
    nb = DIFF_W // LANES
    full = lambda shp: pl.BlockSpec(shp, lambda bi, hp: (0,) * len(shp))
    return _attn_call(functools.partial(_diff_attn_kernel, lam_init), "diff_attn", b, lp, DIFF_HEADS,
                      [_col_spec(lp, 0), _col_spec(lp, nb), _col_spec(lp, 2 * nb),
                       full(lam_vecs.shape), full(gsub.shape)],
                      (main, main, main, lam_vecs, gsub))


def _fox_attn_kernel(x_q, x_k, x_v, ccol_ref, crow_ref, o_ref):
    lp = x_q.shape[1]
    w = _key_window(lp)
    hp = pl.program_id(1)
    lane = lax.broadcasted_iota(jnp.int32, (TQ, LANES), 1)

    def prep(qi):
        cc = ccol_ref[0, _rows(qi, TQ), :]
        cq0 = jnp.sum(jnp.where(lane == 2 * hp, cc, 0.0), axis=-1, keepdims=True)
        cq1 = jnp.sum(jnp.where(lane == 2 * hp + 1, cc, 0.0), axis=-1, keepdims=True)
        return _stack_masked(x_q[0, _rows(qi, TQ), :]), cq0, cq1

    def scores(ctx, k0):
        qs, cq0, cq1 = ctx
        s = _dot_nt(qs, x_k[0, _keys(k0, w), :])
        ck = crow_ref[0, 0, :, _keys(k0, w)]
        return s + jnp.concatenate([cq0 - ck[0:1, :], cq1 - ck[1:2, :]], axis=0)

    run = _softmax_sweep("causal", w, prep, scores, lambda k0: x_v[0, _keys(k0, w), :])

    def emit(qi, res):
        _, l, acc = res
        o = acc / l
        o_ref[0, _rows(qi, TQ), :] = _merge_heads(o[:TQ], o[TQ:]).astype(o_ref.dtype)

    _for_each_query_block(lp // TQ, w, run, emit)


def _fox_attn(main, ccol, crow4):
    b, lp, _ = main.shape
    off = 3 * DIFF_W // LANES
    nb = FOX_W // LANES
    return _attn_call(_fox_attn_kernel, "fox_attn", b, lp, nb,
                      [_col_spec(lp, off), _col_spec(lp, off + nb), _col_spec(lp, off + 2 * nb),
                       pl.BlockSpec((1, lp, LANES), lambda bi, hp: (bi, 0, 0)),
                       pl.BlockSpec((1, 1, 2, lp), lambda bi, hp: (bi, hp, 0, 0))],
                      (main, main, main, ccol, crow4))


def _sb_attn_kernel(x_q, x_k, x_v, o_ref):
    lp = x_q.shape[1]
    w = _key_window(lp)
    g = w // TQ
    cs = SUFFIX_CHUNK if w % SUFFIX_CHUNK == 0 else TQ
    kr = lax.broadcasted_iota(jnp.int32, (cs, cs), 0)
    kc = lax.broadcasted_iota(jnp.int32, (cs, cs), 1)
    later = (kr > kc).astype(BF16)

    def step(carry, qs, k0, mask):
        rsum, acc = carry
        z = _dot_nt(qs, x_k[0, _keys(k0, w), :])
        log_1mb = -(jnp.maximum(z, 0.0) + jnp.log(1.0 + jnp.exp(-jnp.abs(z))))
        if mask is not None:
            log_1mb = jnp.where(mask, log_1mb, 0.0)
        hi = log_1mb.astype(BF16)
        lo = (log_1mb - hi.astype(F32)).astype(BF16)
        parts = []
        for c in reversed(range(w // cs)):
            sl = slice(c * cs, (c + 1) * cs)
            parts.append(_dot(hi[:, sl], later) + _dot(lo[:, sl], later) + rsum)
            rsum = rsum + jnp.sum(log_1mb[:, sl], axis=-1, keepdims=True)
        between = jnp.concatenate(parts[::-1], axis=1)
        wgt = jnp.exp(z + log_1mb + between)
        if mask is not None:
            wgt = jnp.where(mask, wgt, 0.0)
        acc = acc + _dot(wgt.astype(BF16), x_v[0, _keys(k0, w), :])
        return rsum, acc

    def run(qi, near):
        q0 = qi * TQ
        qs = _stack_masked(x_q[0, _rows(qi, TQ), :])
        mwin, pre = _sweep_plan(qi, g)
        carry = (jnp.zeros((2 * TQ, 1), F32), jnp.zeros((2 * TQ, LANES), F32))
        if not near:
            k0 = q0 + TQ - w
            carry = step(carry, qs, k0, _mask("strict", q0, k0, w))

            def mid(it, c):
                return step(c, qs, (pre + (mwin - 1 - it) * g) * TQ, None)

            carry = lax.fori_loop(1, mwin, mid, carry)
        return step(carry, qs, 0, _mask("strict", q0, 0, w, lim=pre * TQ))

    def emit(qi, res):
        _, acc = res
        o_ref[0, _rows(qi, TQ), :] = _merge_heads(acc[:TQ], acc[TQ:]).astype(o_ref.dtype)

    _for_each_query_block(lp // TQ, w, run, emit)


def _sb_attn(sb):
    b, lp, _ = sb.shape
    nb = SB_W // LANES
    return _attn_call(_sb_attn_kernel, "sb_attn", b, lp, nb,
                      [_col_spec(lp, 0), _col_spec(lp, nb), _col_spec(lp, 2 * nb)],
                      (sb, sb, sb))


def _silu(x):
    return x / (1.0 + jnp.exp(-x))


def _ffn_kernel(h_ref, g_ref, wg_ref, wu_ref, wd_ref, o_ref, hn_ref, acc_ref):
    f = pl.program_id(1)

    @pl.when(f == 0)
    def _():
        hn_ref[...] = _rms(h_ref[...], g_ref[...]).astype(BF16)
        acc_ref[...] = jnp.zeros_like(acc_ref)

    hn = hn_ref[...]
    a = _silu(_dot(hn, wg_ref[...])) * _dot(hn, wu_ref[...])
    acc_ref[...] += _dot(a.astype(BF16), wd_ref[...])

    @pl.when(f == pl.num_programs(1) - 1)
    def _():
        o_ref[...] = h_ref[...] + acc_ref[...]


def _ffn(h, g, wg, wu, wd):
    t, d = h.shape
    ff = wg.shape[1]
    tm = _pick(t, (512, 384, 256, 128))
    tf = _pick(ff, (1408, 512, 256, 128))
    return pl.pallas_call(
        _ffn_kernel,
        grid=(t // tm, ff // tf),
        in_specs=[pl.BlockSpec((tm, d), lambda i, f: (i, 0)),
                  pl.BlockSpec((1, d), lambda i, f: (0, 0)),
                  pl.BlockSpec((d, tf), lambda i, f: (0, f)),
                  pl.BlockSpec((d, tf), lambda i, f: (0, f)),
                  pl.BlockSpec((tf, d), lambda i, f: (f, 0))],
        out_specs=pl.BlockSpec((tm, d), lambda i, f: (i, 0)),
        out_shape=jax.ShapeDtypeStruct((t, d), F32),
        scratch_shapes=[pltpu.VMEM((tm, d), BF16), pltpu.VMEM((tm, d), F32)],
        compiler_params=_cparams(("parallel", "arbitrary")),
        name="ffn",
    )(h, g, wg, wu, wd)


def _router_gates(hn, wr_ref):
    a, b, c = _split3(hn)
    wa, wb, wc = wr_ref[0], wr_ref[1], wr_ref[2]
    logits = (_dot(a, wa) + _dot(a, wb) + _dot(b, wa) + _dot(a, wc) + _dot(b, wb) + _dot(c, wa))
    lane = lax.broadcasted_iota(jnp.int32, logits.shape, 1).astype(F32)
    logits = jnp.where(lane < N_EXPERTS, logits, -jnp.inf)
    v1 = jnp.max(logits, axis=-1, keepdims=True)
    i1 = jnp.min(jnp.where(logits == v1, lane, float(LANES)), axis=-1, keepdims=True)
    rest = jnp.where(lane == i1, -jnp.inf, logits)
    v2 = jnp.max(rest, axis=-1, keepdims=True)
    i2 = jnp.min(jnp.where(rest == v2, lane, float(LANES)), axis=-1, keepdims=True)
    e2 = jnp.exp(v2 - v1)
    den = 1.0 + e2
    gate = jnp.where(lane == i1, 1.0 / den, 0.0) + jnp.where(lane == i2, e2 / den, 0.0)
    sel = jnp.where((lane == i1) | (lane == i2), 1.0, 0.0)
    return gate, sel


MOE_TM = 512
MOE_CHUNK = 128
MOE_ALIGN = 16
MOE_ROWS = 1024
SEL_LANE = N_EXPERTS
RANK_LANE = 2 * N_EXPERTS
T_ROWS = 16


def _route_kernel(h_ref, g_ref, wr_ref, hn_ref, route_ref, selt_ref, rankt_ref, cnt_ref):
    tm = h_ref.shape[0]
    hn = _rms(h_ref[...], g_ref[...])
    hn_ref[...] = hn.astype(BF16)
    gate, sel = _router_gates(hn, wr_ref)
    selb = sel.astype(BF16)
    r = lax.broadcasted_iota(jnp.int32, (tm, tm), 0)
    c = lax.broadcasted_iota(jnp.int32, (tm, tm), 1)
    rank = _dot(jnp.where(c < r, 1.0, 0.0).astype(BF16), selb)
    route_ref[...] = gate + pltpu.roll(sel, SEL_LANE, 1) + pltpu.roll(rank, RANK_LANE, 1)
    er = lax.broadcasted_iota(jnp.int32, (T_ROWS, LANES), 0)
    el = lax.broadcasted_iota(jnp.int32, (T_ROWS, LANES), 1)
    pick = jnp.where((er == el) & (er < N_EXPERTS), 1.0, 0.0).astype(BF16)
    selt = _dot_nt(pick, selb)
    selt_ref[0] = selt
    rankt_ref[0] = _dot(selt.astype(BF16), jnp.where(r < c, 1.0, 0.0).astype(BF16))
    cnt_ref[0] = jnp.sum(sel, axis=0, keepdims=True)


def _moe_route(h, g, wr3):
    t, d = h.shape
    tm = _pick(t, (MOE_TM, 384, 256, 128))
    nt = t // tm
    return pl.pallas_call(
        _route_kernel,
        grid=(nt,),
        in_specs=[pl.BlockSpec((tm, d), lambda i: (i, 0)),
                  pl.BlockSpec((1, d), lambda i: (0, 0)),
                  pl.BlockSpec((3, d, LANES), lambda i: (0, 0, 0))],
        out_specs=[pl.BlockSpec((tm, d), lambda i: (i, 0)),
                   pl.BlockSpec((tm, LANES), lambda i: (i, 0)),
                   pl.BlockSpec((1, T_ROWS, tm), lambda i: (i, 0, 0)),
                   pl.BlockSpec((1, T_ROWS, tm), lambda i: (i, 0, 0)),
                   pl.BlockSpec((1, 1, LANES), lambda i: (i, 0, 0))],
        out_shape=[jax.ShapeDtypeStruct((t, d), BF16),
                   jax.ShapeDtypeStruct((t, LANES), F32),
                   jax.ShapeDtypeStruct((nt, T_ROWS, tm), F32),
                   jax.ShapeDtypeStruct((nt, T_ROWS, tm), F32),
                   jax.ShapeDtypeStruct((nt, 1, LANES), F32)],
        compiler_params=_cparams(("parallel",)),
        name="moe_route",
    )(h, g, wr3)


def _moe_plan(cnt, n_row_tiles):
    cnt_al = (cnt + MOE_ALIGN - 1) // MOE_ALIGN * MOE_ALIGN
    tot = jnp.sum(cnt_al, axis=0)
    cap = (tot + MOE_CHUNK + MOE_ROWS - 1) // MOE_ROWS * MOE_ROWS
    base = jnp.cumsum(cap) - cap
    offs = base[None, :] + jnp.cumsum(cnt_al, axis=0) - cnt_al
    starts = jnp.arange(n_row_tiles, dtype=jnp.int32) * MOE_ROWS
    tile_e = jnp.clip(jnp.sum(starts[:, None] >= base[None, :], axis=1) - 1, 0, N_EXPERTS - 1)
    tile_ok = starts < (base + tot)[tile_e]
    return (offs.reshape(-1).astype(jnp.int32), tile_e.astype(jnp.int32), tile_ok.astype(jnp.int32))


def _chunk_copies(n_chunks):
    return [(e, c) for e in range(N_EXPERTS) for c in range(n_chunks)]


def _dispatch_kernel(offs_ref, cnts_ref, hn_ref, selt_ref, rankt_ref, xs_in_ref, xs_ref, stage, sems):
    del xs_in_ref
    i = pl.program_id(0)
    tm = hn_ref.shape[0]
    hn = hn_ref[...]
    ridx = lax.broadcasted_iota(jnp.int32, (MOE_CHUNK, tm), 0).astype(F32)

    def copy(e, c):
        slot = e * (tm // MOE_CHUNK) + c
        row0 = pl.multiple_of(offs_ref[i * N_EXPERTS + e] + c * MOE_CHUNK, MOE_ALIGN)
        return pltpu.make_async_copy(stage.at[slot], xs_ref.at[pl.ds(row0, MOE_CHUNK)], sems.at[slot])

    for e, c in _chunk_copies(tm // MOE_CHUNK):
        @pl.when(c * MOE_CHUNK < cnts_ref[i * N_EXPERTS + e])
        def _(e=e, c=c):
            hit = (rankt_ref[0, e:e + 1, :] - float(c * MOE_CHUNK) == ridx) & (selt_ref[0, e:e + 1, :] > 0.5)
            rows = _dot(jnp.where(hit, 1.0, 0.0).astype(BF16), hn)
            stage[e * (tm // MOE_CHUNK) + c] = rows.astype(BF16)
            copy(e, c).start()

    for e, c in _chunk_copies(tm // MOE_CHUNK):
        @pl.when(c * MOE_CHUNK < cnts_ref[i * N_EXPERTS + e])
        def _(e=e, c=c):
            copy(e, c).wait()


def _moe_dispatch(offs, cnts, hn, selt, rankt, n_rows):
    t, d = hn.shape
    nt, _, tm = selt.shape
    n_slots = N_EXPERTS * (tm // MOE_CHUNK)
    xs0 = jnp.zeros((n_rows, d), BF16)
    grid_spec = pltpu.PrefetchScalarGridSpec(
        num_scalar_prefetch=2,
        grid=(nt,),
        in_specs=[pl.BlockSpec((tm, d), lambda i, *_: (i, 0)),
                  pl.BlockSpec((1, T_ROWS, tm), lambda i, *_: (i, 0, 0)),
                  pl.BlockSpec((1, T_ROWS, tm), lambda i, *_: (i, 0, 0)),
                  pl.BlockSpec(memory_space=pl.ANY)],
        out_specs=pl.BlockSpec(memory_space=pl.ANY),
        scratch_shapes=[pltpu.VMEM((n_slots, MOE_CHUNK, d), BF16), pltpu.SemaphoreType.DMA((n_slots,))],
    )
    return pl.pallas_call(
        _dispatch_kernel,
        grid_spec=grid_spec,
        out_shape=jax.ShapeDtypeStruct((n_rows, d), BF16),
        input_output_aliases={5: 0},
        compiler_params=_cparams(("arbitrary",)),
        name="moe_dispatch",
    )(offs, cnts, hn, selt, rankt, xs0)


def _expert_kernel(te_ref, ok_ref, xs_ref, wg_ref, wu_ref, wd_ref, ys_ref, acc_ref):
    s = pl.program_id(0)
    f = pl.program_id(1)
    ok = ok_ref[s] > 0

    @pl.when(ok & (f == 0))
    def _():
        acc_ref[...] = jnp.zeros_like(acc_ref)

    @pl.when(ok)
    def _():
        x = xs_ref[...]
        a = _silu(_dot(x, wg_ref[0])) * _dot(x, wu_ref[0])
        acc_ref[...] += _dot(a.astype(BF16), wd_ref[0])

    last = f == pl.num_programs(1) - 1

    @pl.when(ok & last)
    def _():
        ys_ref[...] = acc_ref[...].astype(ys_ref.dtype)

    @pl.when(jnp.logical_not(ok) & last)
    def _():
        ys_ref[...] = jnp.zeros_like(ys_ref)


def _moe_experts(tile_e, tile_ok, xs, wg, wu, wd):
    n_rows, d = xs.shape
    ff = wg.shape[2]
    tf = _pick(ff, (512, 256, 128))
    nf = ff // tf

    def f_idx(s, f, ok_ref):
        return jnp.where(ok_ref[s] > 0, f, nf - 1)

    grid_spec = pltpu.PrefetchScalarGridSpec(
        num_scalar_prefetch=2,
        grid=(n_rows // MOE_ROWS, nf),
        in_specs=[pl.BlockSpec((MOE_ROWS, d), lambda s, f, te, ok: (s, 0)),
                  pl.BlockSpec((1, d, tf), lambda s, f, te, ok: (te[s], 0, f_idx(s, f, ok))),
                  pl.BlockSpec((1, d, tf), lambda s, f, te, ok: (te[s], 0, f_idx(s, f, ok))),
                  pl.BlockSpec((1, tf, d), lambda s, f, te, ok: (te[s], f_idx(s, f, ok), 0))],
        out_specs=pl.BlockSpec((MOE_ROWS, d), lambda s, f, te, ok: (s, 0)),
        scratch_shapes=[pltpu.VMEM((MOE_ROWS, d), F32)],
    )
    return pl.pallas_call(
        _expert_kernel,
        grid_spec=grid_spec,
        out_shape=jax.ShapeDtypeStruct((n_rows, d), BF16),
        compiler_params=_cparams(("parallel", "arbitrary")),
        name="moe_experts",
    )(tile_e, tile_ok, xs, wg, wu, wd)


def _combine_kernel(offs_ref, cnts_ref, h_ref, route_ref, ys_ref, o_ref, stage, sems):
    i = pl.program_id(0)
    tm = h_ref.shape[0]
    n_chunks = tm // MOE_CHUNK
    ridx = lax.broadcasted_iota(jnp.int32, (tm, MOE_CHUNK), 1).astype(F32)

    def copy(e, c):
        slot = e * n_chunks + c
        row0 = pl.multiple_of(offs_ref[i * N_EXPERTS + e] + c * MOE_CHUNK, MOE_ALIGN)
        return pltpu.make_async_copy(ys_ref.at[pl.ds(row0, MOE_CHUNK)], stage.at[slot], sems.at[slot])

    for e, c in _chunk_copies(n_chunks):
        @pl.when(c * MOE_CHUNK < cnts_ref[i * N_EXPERTS + e])
        def _(e=e, c=c):
            copy(e, c).start()

    o_ref[...] = h_ref[...]
    for e, c in _chunk_copies(n_chunks):
        @pl.when(c * MOE_CHUNK < cnts_ref[i * N_EXPERTS + e])
        def _(e=e, c=c):
            copy(e, c).wait()
            gate = route_ref[:, e:e + 1]
            sel = route_ref[:, SEL_LANE + e:SEL_LANE + e + 1]
            rank = route_ref[:, RANK_LANE + e:RANK_LANE + e + 1]
            hit = (rank - float(c * MOE_CHUNK) == ridx) & (sel > 0.5)
            back = _dot(jnp.where(hit, 1.0, 0.0).astype(BF16), stage[e * n_chunks + c])
            o_ref[...] += gate * back


def _moe_combine(offs, cnts, h, route, ys):
    t, d = h.shape
    tm = _pick(t, (MOE_TM, 384, 256, 128))
    n_slots = N_EXPERTS * (tm // MOE_CHUNK)
    grid_spec = pltpu.PrefetchScalarGridSpec(
        num_scalar_prefetch=2,
        grid=(t // tm,),
        in_specs=[pl.BlockSpec((tm, d), lambda i, *_: (i, 0)),
                  pl.BlockSpec((tm, LANES), lambda i, *_: (i, 0)),
                  pl.BlockSpec(memory_space=pl.ANY)],
        out_specs=pl.BlockSpec((tm, d), lambda i, *_: (i, 0)),
        scratch_shapes=[pltpu.VMEM((n_slots, MOE_CHUNK, d), BF16), pltpu.SemaphoreType.DMA((n_slots,))],
    )
    return pl.pallas_call(
        _combine_kernel,
        grid_spec=grid_spec,
        out_shape=jax.ShapeDtypeStruct((t, d), F32),
        compiler_params=_cparams(("arbitrary",)),
        name="moe_combine",
    )(offs, cnts, h, route, ys)


def _moe(h, g, wr3, wg, wu, wd):
    t, d = h.shape
    hn, route, selt, rankt, cnt = _moe_route(h, g, wr3)
    nt = cnt.shape[0]
    cnt = cnt[:, 0, :N_EXPERTS].astype(jnp.int32)
    worst = TOP_K * t + nt * N_EXPERTS * (MOE_ALIGN - 1) + N_EXPERTS * (MOE_CHUNK + MOE_ROWS - 1)
    n_row_tiles = -(-worst // MOE_ROWS)
    offs, tile_e, tile_ok = _moe_plan(cnt, n_row_tiles)
    cnts = cnt.reshape(-1)
    xs = _moe_dispatch(offs, cnts, hn, selt, rankt, n_row_tiles * MOE_ROWS)
    ys = _moe_experts(tile_e, tile_ok, xs, wg, wu, wd)
    return _moe_combine(offs, cnts, h, route, ys)


def _final_norm_kernel(h_ref, g_ref, o_ref):
    o_ref[0] = _rms(h_ref[0], g_ref[...])


def _final_norm(h3, g, seq):
    b, lp, d = h3.shape
    tm = PAD
    return pl.pallas_call(
        _final_norm_kernel,
        grid=(b, seq // tm),
        in_specs=[pl.BlockSpec((1, tm, d), lambda bi, i: (bi, i + PAD // tm, 0)),
                  pl.BlockSpec((1, d), lambda bi, i: (0, 0))],
        out_specs=pl.BlockSpec((1, tm, d), lambda bi, i: (bi, i, 0)),
        out_shape=jax.ShapeDtypeStruct((b, seq, d), F32),
        compiler_params=_cparams(("parallel", "parallel")),
        name="final_norm",
    )(h3, g)


def _rope_table(lp, rot_dim, theta, width, offset, scale):
    half = rot_dim // 2
    pos = (jnp.arange(lp, dtype=jnp.int32) - FIRST).astype(F32)
    inv = theta ** (-jnp.arange(half, dtype=F32) * 2.0 / rot_dim)
    ang = pos[:, None] * inv[None, :]
    cos, sin = jnp.cos(ang), jnp.sin(ang)
    ones = jnp.ones((lp, 1), F32)
    zeros = jnp.zeros((lp, 1), F32)

    def unit(first, second, fill):
        parts = [jnp.tile(fill, (1, offset)), first, second,
                 jnp.tile(fill, (1, width - offset - rot_dim))]
        return jnp.tile(jnp.concatenate(parts, axis=1), (1, LANES // width))

    c = unit(cos, cos, ones)
    s1 = unit(-sin, jnp.zeros_like(sin), zeros)
    s2 = unit(jnp.zeros_like(sin), sin, zeros)
    return jnp.concatenate([c, s1, s2], axis=1) * scale


def _pad_heads(w, heads, used, width):
    k = w.shape[0]
    w = w.reshape(k, heads, used)
    return jnp.pad(w, ((0, 0), (0, 0), (0, width - used))).reshape(k, heads * width)


def kernel(x, meta_tokens, ln_mix_e, w_in_e, ln_mla_q_e, w_mla_uq_e, ln_mla_kv_e, w_mla_ukv_e,
           w_out_e, ln_ffn_e, w_ffn_gate_e, w_ffn_up_e, w_ffn_down_e, ln_mix_o, w_in_o, b_forget_o,
           diff_lq1_o, diff_lk1_o, diff_lq2_o, diff_lk2_o, diff_subln_o, w_out_o, ln_ffn_o,
           w_router_o, w_moe_gate_o, w_moe_up_o, w_moe_down_o, ln_final):
    b, seq, d = x.shape
    assert d == D_MODEL and seq % TQ == 0
    lp = PAD + seq
    t = b * lp
    depth = 2

    meta = jnp.broadcast_to(meta_tokens.astype(x.dtype)[None], (b, N_META, d))
    h = jnp.concatenate([jnp.zeros((b, FIRST, d), x.dtype), meta, x], axis=1).reshape(t, d)

    row = lambda v: v.reshape(1, -1).astype(F32)
    sb_scale = HEAD_DIM ** -0.5
    tab_mla_q = _rope_table(lp, MLA_ROPE, MLA_THETA, LANES, MLA_NOPE, (MLA_NOPE + MLA_ROPE) ** -0.5)
    tab_mla_k = _rope_table(lp, MLA_ROPE, MLA_THETA, LANES, MLA_NOPE, 1.0)
    tab_diff = _rope_table(lp, ROT_DIM, ROPE_THETA, DIFF_DIM, 0, 1.0)

    for i in range(depth):
        j = i // 2
        if i % 2 == 0:
            w_in = w_in_e[j]
            lat_w = MLA_Q_RANK + MLA_KV_RANK + MLA_ROPE
            w_cat = jnp.concatenate(
                [w_in[:, :SB_W] * sb_scale, w_in[:, SB_W:3 * SB_W + lat_w],
                 jnp.zeros((d, 4 * LANES - lat_w), F32)], axis=1).astype(BF16)
            sb, lat = _even_proj(h, row(ln_mix_e[j]), w_cat)
            sb = sb.reshape(b, lp, -1)
            lat = lat.reshape(b, lp, -1)

            wq = _pad_heads(w_mla_uq_e[j], MLA_HEADS, MLA_NOPE + MLA_ROPE, LANES).astype(BF16)
            ukv = w_mla_ukv_e[j].reshape(MLA_KV_RANK, MLA_HEADS, MLA_NOPE + MLA_V)
            wk_nope = _pad_heads(ukv[:, :, :MLA_NOPE].reshape(MLA_KV_RANK, -1), MLA_HEADS, MLA_NOPE, LANES)
            place = jnp.pad(jnp.eye(MLA_ROPE, dtype=F32), ((0, 0), (MLA_NOPE, LANES - MLA_NOPE - MLA_ROPE)))
            wk_rope = jnp.tile(place, (1, MLA_HEADS))
            wk = jnp.concatenate(
                [wk_nope, wk_rope, jnp.zeros((LANES - MLA_ROPE, MLA_HEADS * LANES), F32)], axis=0).astype(BF16)
            wv = ukv[:, :, MLA_NOPE:].reshape(MLA_KV_RANK, -1).astype(BF16)
            mq, mk, mv = _mla_up(lat, row(ln_mla_q_e[j]), row(ln_mla_kv_e[j]), wq, wk, wv,
                                 tab_mla_q, tab_mla_k)

            a_out = _sb_attn(sb).reshape(t, -1)
            b_out = _mla_attn(mq, mk, mv).reshape(t, -1)
            w_out = w_out_e[j].astype(BF16)
            h = _out_proj(h, a_out, b_out, w_out[:SB_W], w_out[SB_W:])
            h = _ffn(h, row(ln_ffn_e[j]), w_ffn_gate_e[j].astype(BF16), w_ffn_up_e[j].astype(BF16),
                     w_ffn_down_e[j].astype(BF16))
        else:
            lam_init = 0.8 - 0.6 * math.exp(-0.3 * i)
            w_in = w_in_o[j]
            wf = w_in[:, N_ODD_MAIN:]
            w_cat = jnp.concatenate(
                [w_in[:, :DIFF_W] * sb_scale, w_in[:, DIFF_W:3 * DIFF_W],
                 w_in[:, 3 * DIFF_W:3 * DIFF_W + FOX_W] * sb_scale, w_in[:, 3 * DIFF_W + FOX_W:N_ODD_MAIN],
                 wf, jnp.zeros((d, LANES - FOX_HEADS), F32)], axis=1).astype(BF16)
            wft = jnp.pad(wf.T, ((0, F_ROWS - FOX_HEADS), (0, 0))).astype(BF16)
            bf = b_forget_o[j].astype(F32)
            bcol = jnp.pad(bf, (0, LANES - FOX_HEADS)).reshape(1, LANES)
            brow = jnp.pad(bf, (0, F_ROWS - FOX_HEADS)).reshape(F_ROWS, 1)
            main, fcol, frow = _odd_proj(h.reshape(b, lp, d), row(ln_mix_o[j]), w_cat, wft, bcol, brow,
                                         tab_diff)
            ccol, crow = _fox_cumsum(fcol, frow)
            crow4 = crow[:, :FOX_HEADS].reshape(b, FOX_HEADS // 2, 2, lp)
            lam_vecs = jnp.stack([diff_lq1_o[j], diff_lk1_o[j], diff_lq2_o[j], diff_lk2_o[j]]).astype(F32)
            c_out = _diff_attn(main, lam_vecs, row(diff_subln_o[j]), lam_init).reshape(t, -1)
            d_out = _fox_attn(main, ccol, crow4).reshape(t, -1)
            w_out = w_out_o[j].astype(BF16)
            h = _out_proj(h, c_out, d_out, w_out[:DIFF_W], w_out[DIFF_W:])
            wr = jnp.pad(w_router_o[j].astype(F32), ((0, 0), (0, LANES - N_EXPERTS)))
            wr3 = jnp.stack(_split3(wr))
            h = _moe(h, row(ln_ffn_o[j]), wr3, w_moe_gate_o[j].astype(BF16), w_moe_up_o[j].astype(BF16),
                     w_moe_down_o[j].astype(BF16))

    return _final_norm(h.reshape(b, lp, d), row(ln_final), seq)
```

```python
import functools
import math

import jax
import jax.numpy as jnp
from jax import lax
from jax.experimental import pallas as pl
from jax.experimental.pallas import tpu as pltpu

F32 = jnp.float32
BF16 = jnp.bfloat16

D_MODEL = 1024
CHUNK = 64
N_META = 16
EPS = 1e-6
LOG2E = 1.4426950408889634
NEG_INF = -1e30

HEAD_DIM = 64
SB_HEADS = 8
MLA_HEADS = 8
MLA_NOPE = 64
MLA_ROPE = 32
MLA_V = 64
MLA_Q_RANK = 256
MLA_KV_RANK = 128
MLA_THETA = 10000.0
DIFF_HEADS = 4
DIFF_DIM = 64
FOX_HEADS = 8
ROPE_THETA = 500000.0
ROT_DIM = HEAD_DIM // 4
N_EXPERTS = 8
TOP_K = 2

SB_W = SB_HEADS * HEAD_DIM
DIFF_W = DIFF_HEADS * 2 * DIFF_DIM
FOX_W = FOX_HEADS * HEAD_DIM

LANES = 128
PAD = 128
FIRST = PAD - N_META
TQ = 128
KEY_WINDOW = 512
SUFFIX_CHUNK = 256
CHUNK_SHIFT = CHUNK.bit_length() - 1
VMEM_LIMIT = 56 * 1024 * 1024


def _cparams(sem):
    return pltpu.CompilerParams(dimension_semantics=sem, vmem_limit_bytes=VMEM_LIMIT)


def _pick(n, cands):
    for c in cands:
        if n % c == 0:
            return c
    raise ValueError(f"no tile in {cands} divides {n}")


def _rms(x, g):
    return x * lax.rsqrt(jnp.mean(x * x, axis=-1, keepdims=True) + EPS) * g


def _dot(a, b):
    return jnp.dot(a, b, preferred_element_type=F32)


def _dot_nt(a, b):
    return lax.dot_general(a, b, (((1,), (1,)), ((), ())), preferred_element_type=F32)


def _rope_chunk(y, tab, shift):
    c = tab[:, 0:LANES]
    s1 = tab[:, LANES:2 * LANES]
    s2 = tab[:, 2 * LANES:3 * LANES]
    return (y * c + pltpu.roll(y, LANES - shift, 1) * s1 + pltpu.roll(y, shift, 1) * s2)


def _split3(x):
    a = x.astype(BF16)
    r = x - a.astype(F32)
    b = r.astype(BF16)
    c = (r - b.astype(F32)).astype(BF16)
    return a, b, c


def _even_proj_kernel(h_ref, g_ref, w_ref, sb_ref, lat_ref):
    hn = _rms(h_ref[...], g_ref[...]).astype(BF16)
    y = _dot(hn, w_ref[...])
    n_sb = sb_ref.shape[-1]
    sb_ref[:, :SB_W] = (y[:, :SB_W] * LOG2E).astype(BF16)
    sb_ref[:, SB_W:] = y[:, SB_W:n_sb].astype(BF16)
    lat_ref[...] = y[:, n_sb:].astype(BF16)


def _even_proj(h, g, w):
    t, d = h.shape
    n = w.shape[1]
    n_sb = 3 * SB_W
    tm = _pick(t, (512, 384, 256, 128))
    return pl.pallas_call(
        _even_proj_kernel,
        grid=(t // tm,),
        in_specs=[pl.BlockSpec((tm, d), lambda i: (i, 0)),
                  pl.BlockSpec((1, d), lambda i: (0, 0)),
                  pl.BlockSpec((d, n), lambda i: (0, 0))],
        out_specs=[pl.BlockSpec((tm, n_sb), lambda i: (i, 0)),
                   pl.BlockSpec((tm, n - n_sb), lambda i: (i, 0))],
        out_shape=[jax.ShapeDtypeStruct((t, n_sb), BF16),
                   jax.ShapeDtypeStruct((t, n - n_sb), BF16)],
        compiler_params=_cparams(("parallel",)),
        name="even_proj",
    )(h, g, w)


def _mla_up_kernel(lat_ref, gq_ref, gkv_ref, wq_ref, wk_ref, wv_ref, tq_ref, tk_ref,
                   q_ref, k_ref, v_ref):
    lat = lat_ref[0]
    nq = _rms(lat[:, :MLA_Q_RANK].astype(F32), gq_ref[...]).astype(BF16)
    nkv = _rms(lat[:, MLA_Q_RANK:MLA_Q_RANK + MLA_KV_RANK].astype(F32), gkv_ref[...]).astype(BF16)
    yq = _dot(nq, wq_ref[...])
    xk = jnp.concatenate([nkv, lat[:, MLA_Q_RANK + MLA_KV_RANK:]], axis=1)
    yk = _dot(xk, wk_ref[...])
    v_ref[0] = _dot(nkv, wv_ref[...]).astype(BF16)
    tabq = tq_ref[...]
    tabk = tk_ref[...]
    for hd in range(MLA_HEADS):
        sl = slice(hd * LANES, (hd + 1) * LANES)
        q_ref[0, :, sl] = _rope_chunk(yq[:, sl], tabq, MLA_ROPE // 2).astype(BF16)
        k_ref[0, :, sl] = _rope_chunk(yk[:, sl], tabk, MLA_ROPE // 2).astype(BF16)


def _mla_up(lat, gq, gkv, wq, wk, wv, tabq, tabk):
    b, lp, nl = lat.shape
    tm = _pick(lp, (384, 128))
    nqk = MLA_HEADS * LANES
    nv = MLA_HEADS * MLA_V
    full = lambda shp: pl.BlockSpec(shp, lambda bi, i: (0,) * len(shp))
    return pl.pallas_call(
        _mla_up_kernel,
        grid=(b, lp // tm),
        in_specs=[pl.BlockSpec((1, tm, nl), lambda bi, i: (bi, i, 0)),
                  full(gq.shape), full(gkv.shape), full(wq.shape), full(wk.shape), full(wv.shape),
                  pl.BlockSpec((tm, 3 * LANES), lambda bi, i: (i, 0)),
                  pl.BlockSpec((tm, 3 * LANES), lambda bi, i: (i, 0))],
        out_specs=[pl.BlockSpec((1, tm, nqk), lambda bi, i: (bi, i, 0)),
                   pl.BlockSpec((1, tm, nqk), lambda bi, i: (bi, i, 0)),
                   pl.BlockSpec((1, tm, nv), lambda bi, i: (bi, i, 0))],
        out_shape=[jax.ShapeDtypeStruct((b, lp, nqk), BF16),
                   jax.ShapeDtypeStruct((b, lp, nqk), BF16),
                   jax.ShapeDtypeStruct((b, lp, nv), BF16)],
        compiler_params=_cparams(("parallel", "parallel")),
        name="mla_up",
    )(lat, gq, gkv, wq, wk, wv, tabq, tabk)


N_ODD_MAIN = 3 * DIFF_W + 3 * FOX_W
F_ROWS = 16


def _odd_proj_kernel(h_ref, g_ref, w_ref, wft_ref, bcol_ref, brow_ref, tabq_ref, tabk_ref,
                     main_ref, fcol_ref, frow_ref):
    hn = _rms(h_ref[0], g_ref[...]).astype(BF16)
    y = _dot(hn, w_ref[...])
    n_rope = DIFF_W // LANES
    for c in range(2 * n_rope):
        sl = slice(c * LANES, (c + 1) * LANES)
        tab = tabq_ref[...] if c < n_rope else tabk_ref[...]
        main_ref[0, :, sl] = _rope_chunk(y[:, sl], tab, ROT_DIM // 2).astype(BF16)
    fq = slice(3 * DIFF_W, 3 * DIFF_W + FOX_W)
    main_ref[0, :, 2 * DIFF_W:3 * DIFF_W] = y[:, 2 * DIFF_W:3 * DIFF_W].astype(BF16)
    main_ref[0, :, fq] = (y[:, fq] * LOG2E).astype(BF16)
    main_ref[0, :, 3 * DIFF_W + FOX_W:] = y[:, 3 * DIFF_W + FOX_W:N_ODD_MAIN].astype(BF16)
    fcol_ref[0] = y[:, N_ODD_MAIN:] + bcol_ref[...]
    frow_ref[0] = _dot_nt(wft_ref[...], hn) + brow_ref[...]


def _odd_proj(h3, g, w, wft, bcol, brow, tabq, tabk):
    b, lp, d = h3.shape
    tm = _pick(lp, (384, 128))
    full = lambda shp: pl.BlockSpec(shp, lambda bi, i: (0,) * len(shp))
    return pl.pallas_call(
        _odd_proj_kernel,
        grid=(b, lp // tm),
        in_specs=[pl.BlockSpec((1, tm, d), lambda bi, i: (bi, i, 0)),
                  full(g.shape), full(w.shape), full(wft.shape), full(bcol.shape), full(brow.shape),
                  pl.BlockSpec((tm, 3 * LANES), lambda bi, i: (i, 0)),
                  pl.BlockSpec((tm, 3 * LANES), lambda bi, i: (i, 0))],
        out_specs=[pl.BlockSpec((1, tm, N_ODD_MAIN), lambda bi, i: (bi, i, 0)),
                   pl.BlockSpec((1, tm, LANES), lambda bi, i: (bi, i, 0)),
                   pl.BlockSpec((1, F_ROWS, tm), lambda bi, i: (bi, 0, i))],
        out_shape=[jax.ShapeDtypeStruct((b, lp, N_ODD_MAIN), BF16),
                   jax.ShapeDtypeStruct((b, lp, LANES), F32),
                   jax.ShapeDtypeStruct((b, F_ROWS, lp), F32)],
        compiler_params=_cparams(("parallel", "parallel")),
        name="odd_proj",
    )(h3, g, w, wft, bcol, brow, tabq, tabk)


def _out_proj_kernel(h_ref, a_ref, b_ref, wa_ref, wb_ref, o_ref):
    o_ref[...] = h_ref[...] + _dot(a_ref[...], wa_ref[...]) + _dot(b_ref[...], wb_ref[...])


def _out_proj(h, mix_a, mix_b, wa, wb):
    t, d = h.shape
    tm = _pick(t, (512, 384, 256, 128))
    na, nb = mix_a.shape[1], mix_b.shape[1]
    return pl.pallas_call(
        _out_proj_kernel,
        grid=(t // tm,),
        in_specs=[pl.BlockSpec((tm, d), lambda i: (i, 0)),
                  pl.BlockSpec((tm, na), lambda i: (i, 0)),
                  pl.BlockSpec((tm, nb), lambda i: (i, 0)),
                  pl.BlockSpec((na, d), lambda i: (0, 0)),
                  pl.BlockSpec((nb, d), lambda i: (0, 0))],
        out_specs=pl.BlockSpec((tm, d), lambda i: (i, 0)),
        out_shape=jax.ShapeDtypeStruct((t, d), F32),
        compiler_params=_cparams(("parallel",)),
        name="out_proj",
    )(h, mix_a, mix_b, wa, wb)


def _log_sigmoid(x):
    return jnp.minimum(x, 0.0) - jnp.log(1.0 + jnp.exp(-jnp.abs(x)))


def _fox_cumsum_kernel(fcol_ref, frow_ref, ccol_ref, crow_ref):
    lp = fcol_ref.shape[1]
    blk = LANES
    r = lax.broadcasted_iota(jnp.int32, (blk, blk), 0)
    c = lax.broadcasted_iota(jnp.int32, (blk, blk), 1)
    tri_l = (c <= r).astype(BF16)
    tri_u = (r <= c).astype(BF16)
    row_id = lax.broadcasted_iota(jnp.int32, (blk, LANES), 0)
    col_id = lax.broadcasted_iota(jnp.int32, (F_ROWS, blk), 1)
    carry_c = jnp.zeros((1, LANES), F32)
    carry_r = jnp.zeros((F_ROWS, 1), F32)
    for j in range(lp // blk):
        sl = slice(j * blk, (j + 1) * blk)
        ls = _log_sigmoid(fcol_ref[0, sl, :])
        ls = jnp.where(row_id + j * blk >= FIRST, ls, 0.0)
        a, b, c3 = _split3(ls)
        cs = _dot(tri_l, a) + _dot(tri_l, b) + _dot(tri_l, c3) + carry_c
        ccol_ref[0, sl, :] = cs * LOG2E
        carry_c = cs[blk - 1:blk, :]
        lr = _log_sigmoid(frow_ref[0, :, sl])
        lr = jnp.where(col_id + j * blk >= FIRST, lr, 0.0)
        a, b, c3 = _split3(lr)
        cr = _dot(a, tri_u) + _dot(b, tri_u) + _dot(c3, tri_u) + carry_r
        crow_ref[0, :, sl] = cr * LOG2E
        carry_r = cr[:, blk - 1:blk]


def _fox_cumsum(fcol, frow):
    b, lp, _ = fcol.shape
    return pl.pallas_call(
        _fox_cumsum_kernel,
        grid=(b,),
        in_specs=[pl.BlockSpec((1, lp, LANES), lambda bi: (bi, 0, 0)),
                  pl.BlockSpec((1, F_ROWS, lp), lambda bi: (bi, 0, 0))],
        out_specs=[pl.BlockSpec((1, lp, LANES), lambda bi: (bi, 0, 0)),
                   pl.BlockSpec((1, F_ROWS, lp), lambda bi: (bi, 0, 0))],
        out_shape=[jax.ShapeDtypeStruct((b, lp, LANES), F32),
                   jax.ShapeDtypeStruct((b, F_ROWS, lp), F32)],
        compiler_params=_cparams(("parallel",)),
        name="fox_cumsum",
    )(fcol, frow)


def _rows(i, n):
    if isinstance(i, int):
        return pl.ds(i * n, n)
    return pl.ds(pl.multiple_of(i * n, n), n)


def _keys(k0, w):
    if isinstance(k0, int):
        return pl.ds(k0, w)
    return pl.ds(pl.multiple_of(k0, TQ), w)


def _visible(kind, ks, qs):
    if kind == "chunk":
        return (ks >> CHUNK_SHIFT) <= (qs >> CHUNK_SHIFT)
    if kind == "causal":
        return ks <= qs
    return ks < qs


def _first_window_mask(kind, q0, w, lim):
    r = lax.broadcasted_iota(jnp.int32, (2 * TQ, w), 0)
    ks = lax.broadcasted_iota(jnp.int32, (2 * TQ, w), 1)
    return _visible(kind, ks, q0 + (r & (TQ - 1))) & (ks >= FIRST) & (ks < lim)


def _diag_window_mask(kind, w):
    r = lax.broadcasted_iota(jnp.int32, (2 * TQ, w), 0)
    ks = lax.broadcasted_iota(jnp.int32, (2 * TQ, w), 1) + (TQ - w)
    return _visible(kind, ks, r & (TQ - 1))


def _span_mask(w, lim):
    ks = lax.broadcasted_iota(jnp.int32, (1, w), 1)
    return (ks >= FIRST) & (ks < lim)


def _lane_halves():
    lane = lax.broadcasted_iota(jnp.int32, (TQ, LANES), 1)
    return lane < (LANES // 2)


def _stack_masked(q):
    lo = _lane_halves()
    z = jnp.zeros_like(q)
    return jnp.concatenate([jnp.where(lo, q, z), jnp.where(lo, z, q)], axis=0)


def _softmax_sweep(kind, w, prep, scores_fn, v_fn, dbias_ref):
    ones = jnp.ones((w, LANES), BF16)

    def step(carry, s, v):
        m, acc = carry
        m_new = jnp.maximum(m, jnp.max(s, axis=-1, keepdims=True))
        alpha = jnp.exp2(m - m_new)
        p = jnp.exp2(s - m_new)
        acc = alpha * acc + _dot(p.astype(BF16), jnp.concatenate([v, ones], axis=1))
        return m_new, acc

    def run(qi, mwin):
        g = w // TQ
        ctx = prep(qi)
        q0 = qi * TQ
        pre = qi + 1 - mwin * g
        carry = (jnp.full((2 * TQ, 1), NEG_INF, F32), jnp.zeros((2 * TQ, 2 * LANES), F32))
        s = scores_fn(ctx, 0)
        if mwin == 0:
            s = jnp.where(_first_window_mask(kind, q0, w, pre * TQ), s, NEG_INF)
            return step(carry, s, v_fn(0))[1]
        carry = step(carry, s + jnp.where(_span_mask(w, pre * TQ), 0.0, NEG_INF), v_fn(0))
        for t in range(1, mwin):
            k0 = (pre + (t - 1) * g) * TQ
            carry = step(carry, scores_fn(ctx, k0), v_fn(k0))
        k0 = q0 + TQ - w
        return step(carry, scores_fn(ctx, k0) + dbias_ref[...], v_fn(k0))[1]

    return run


def _for_each_query_block(nq, w, run, emit):
    g = w // TQ
    for mwin in range(-(-nq // g)):
        lo, hi = mwin * g, min((mwin + 1) * g, nq)

        def body(it, _, mwin=mwin, lo=lo):
            qi = lo + 2 * it
            first, second = run(qi, mwin), run(qi + 1, mwin)
            emit(qi, first)
            emit(qi + 1, second)
            return 0

        lax.fori_loop(0, (hi - lo) // 2, body, 0)
        if (hi - lo) % 2:
            emit(hi - 1, run(hi - 1, mwin))


def _key_window(lp):
    return min(KEY_WINDOW, lp)


def _merge_heads(top, bottom):
    return jnp.where(_lane_halves(), top, bottom)


def _attn_call(kernel, name, b, lp, n_blocks, in_specs, args, n_masks=1):
    w = _key_window(lp)
    return pl.pallas_call(
        kernel,
        grid=(b, n_blocks),
        in_specs=in_specs,
        out_specs=pl.BlockSpec((1, lp, LANES), lambda bi, hp: (bi, 0, hp)),
        out_shape=jax.ShapeDtypeStruct((b, lp, n_blocks * LANES), BF16),
        scratch_shapes=[pltpu.VMEM((2 * TQ, w), F32)] * n_masks,
        compiler_params=_cparams(("parallel", "parallel")),
        name=name,
    )(*args)


def _normalised(acc):
    return acc[:, :LANES] / acc[:, LANES:]


def _col_spec(lp, off):
    return pl.BlockSpec((1, lp, LANES), lambda bi, hp: (bi, 0, off + hp))


def _mla_attn_kernel(q_ref, k_ref, v_ref, o_ref, dbias_ref):
    lp = q_ref.shape[1]
    w = _key_window(lp)
    dbias_ref[...] = jnp.where(_diag_window_mask("chunk", w), 0.0, NEG_INF)

    def prep(qi):
        return q_ref[0, _rows(qi, TQ), 0:LANES], q_ref[0, _rows(qi, TQ), LANES:2 * LANES]

    def scores(ctx, k0):
        qa, qb = ctx
        ka = k_ref[0, _keys(k0, w), 0:LANES]
        kb = k_ref[0, _keys(k0, w), LANES:2 * LANES]
        return jnp.concatenate([_dot_nt(qa, ka), _dot_nt(qb, kb)], axis=0)

    run = _softmax_sweep("chunk", w, prep, scores, lambda k0: v_ref[0, _keys(k0, w), :], dbias_ref)

    def emit(qi, acc):
        o = _normalised(acc)
        o_ref[0, _rows(qi, TQ), :] = _merge_heads(o[:TQ], o[TQ:]).astype(o_ref.dtype)

    _for_each_query_block(lp // TQ, w, run, emit)


def _mla_attn(q, k, v):
    b, lp, _ = q.shape
    wide = lambda: pl.BlockSpec((1, lp, 2 * LANES), lambda bi, hp: (bi, 0, hp))
    return _attn_call(_mla_attn_kernel, "mla_attn", b, lp, MLA_HEADS // 2,
                      [wide(), wide(), _col_spec(lp, 0)], (q, k, v))


def _diff_attn_kernel(lam_init, x_q, x_k, x_v, lam_ref, gsub_ref, o_ref, dbias_ref):
    lp = x_q.shape[1]
    w = _key_window(lp)
    dbias_ref[...] = jnp.where(_diag_window_mask("chunk", w), 0.0, NEG_INF)
    lv = lam_ref[...]
    lam = (jnp.exp(jnp.sum(lv[0:1] * lv[1:2], axis=-1, keepdims=True))
           - jnp.exp(jnp.sum(lv[2:3] * lv[3:4], axis=-1, keepdims=True)) + lam_init)
    gsub = gsub_ref[...]

    def prep(qi):
        return _stack_masked(x_q[0, _rows(qi, TQ), :])

    def scores(qs, k0):
        return _dot_nt(qs, x_k[0, _keys(k0, w), :])

    run = _softmax_sweep("chunk", w, prep, scores, lambda k0: x_v[0, _keys(k0, w), :], dbias_ref)

    def emit(qi, acc):
        o = _normalised(acc)
        o = o[:TQ] - lam * o[TQ:]
        o_ref[0, _rows(qi, TQ), :] = (_rms(o, gsub) * (1.0 - lam_init)).astype(o_ref.dtype)

    _for_each_query_block(lp // TQ, w, run, emit)


def _diff_attn(main, lam_vecs, gsub, lam_init):
    b, lp, _ = main.shape
    nb = DIFF_W // LANES
    full = lambda shp: pl.BlockSpec(shp, lambda bi, hp: (0,) * len(shp))
    return _attn_call(functools.partial(_diff_attn_kernel, lam_init), "diff_attn", b, lp, DIFF_HEADS,
                      [_col_spec(lp, 0), _col_spec(lp, nb), _col_spec(lp, 2 * nb),
                       full(lam_vecs.shape), full(gsub.shape)],
                      (main, main, main, lam_vecs, gsub))


def _fox_attn_kernel(x_q, x_k, x_v, ccol_ref, crow_ref, o_ref, dbias_ref):
    lp = x_q.shape[1]
    w = _key_window(lp)
    dbias_ref[...] = jnp.where(_diag_window_mask("causal", w), 0.0, NEG_INF)
    hp = pl.program_id(1)
    lane = lax.broadcasted_iota(jnp.int32, (TQ, LANES), 1)

    def prep(qi):
        cc = ccol_ref[0, _rows(qi, TQ), :]
        cq0 = jnp.sum(jnp.where(lane == 2 * hp, cc, 0.0), axis=-1, keepdims=True)
        cq1 = jnp.sum(jnp.where(lane == 2 * hp + 1, cc, 0.0), axis=-1, keepdims=True)
        return _stack_masked(x_q[0, _rows(qi, TQ), :]), cq0, cq1

    def scores(ctx, k0):
        qs, cq0, cq1 = ctx
        s = _dot_nt(qs, x_k[0, _keys(k0, w), :])
        ck = crow_ref[0, 0, :, _keys(k0, w)]
        return s + jnp.concatenate([cq0 - ck[0:1, :], cq1 - ck[1:2, :]], axis=0)

    run = _softmax_sweep("causal", w, prep, scores, lambda k0: x_v[0, _keys(k0, w), :], dbias_ref)

    def emit(qi, acc):
        o = _normalised(acc)
        o_ref[0, _rows(qi, TQ), :] = _merge_heads(o[:TQ], o[TQ:]).astype(o_ref.dtype)

    _for_each_query_block(lp // TQ, w, run, emit)


def _fox_attn(main, ccol, crow4):
    b, lp, _ = main.shape
    off = 3 * DIFF_W // LANES
    nb = FOX_W // LANES
    return _attn_call(_fox_attn_kernel, "fox_attn", b, lp, nb,
                      [_col_spec(lp, off), _col_spec(lp, off + nb), _col_spec(lp, off + 2 * nb),
                       pl.BlockSpec((1, lp, LANES), lambda bi, hp: (bi, 0, 0)),
                       pl.BlockSpec((1, 1, 2, lp), lambda bi, hp: (bi, hp, 0, 0))],
                      (main, main, main, ccol, crow4))


def _sb_attn_kernel(x_q, x_k, x_v, o_ref, dkeep_ref, dbias_ref):
    lp = x_q.shape[1]
    w = _key_window(lp)
    g = w // TQ
    cs = SUFFIX_CHUNK if w % SUFFIX_CHUNK == 0 else TQ
    kr = lax.broadcasted_iota(jnp.int32, (cs, cs), 0)
    kc = lax.broadcasted_iota(jnp.int32, (cs, cs), 1)
    later = jnp.where(kr > kc, 1.0, 0.0).astype(BF16)
    dvis = _diag_window_mask("strict", w)
    dkeep_ref[...] = jnp.where(dvis, 1.0, 0.0)
    dbias_ref[...] = jnp.where(dvis, 0.0, NEG_INF)

    def step(carry, qs, k0, keep=None, bias=None, vis=None):
        rsum, acc = carry
        z = _dot_nt(qs, x_k[0, _keys(k0, w), :])
        nz = -z
        log_1mb = jnp.minimum(nz, 0.0) - jnp.log2(1.0 + jnp.exp2(jnp.minimum(z, nz)))
        if vis is not None:
            log_1mb = jnp.where(vis, log_1mb, 0.0)
        if keep is not None:
            log_1mb = log_1mb * keep
        lb = log_1mb.astype(BF16)
        parts = []
        for c in reversed(range(w // cs)):
            sl = slice(c * cs, (c + 1) * cs)
            parts.append(_dot(lb[:, sl], later) + rsum)
            rsum = rsum + jnp.sum(log_1mb[:, sl], axis=-1, keepdims=True)
        arg = z + log_1mb + jnp.concatenate(parts[::-1], axis=1)
        if bias is not None:
            arg = arg + bias
        wgt = jnp.exp2(arg)
        if vis is not None:
            wgt = jnp.where(vis, wgt, 0.0)
        acc = acc + _dot(wgt.astype(BF16), x_v[0, _keys(k0, w), :])
        return rsum, acc

    def run(qi, mwin):
        q0 = qi * TQ
        qs = _stack_masked(x_q[0, _rows(qi, TQ), :])
        pre = qi + 1 - mwin * g
        carry = (jnp.zeros((2 * TQ, 1), F32), jnp.zeros((2 * TQ, LANES), F32))
        if mwin == 0:
            return step(carry, qs, 0, vis=_first_window_mask("strict", q0, w, pre * TQ))[1]
        carry = step(carry, qs, q0 + TQ - w, keep=dkeep_ref[...], bias=dbias_ref[...])
        for t in range(mwin - 1, 0, -1):
            carry = step(carry, qs, (pre + (t - 1) * g) * TQ)
        span = _span_mask(w, pre * TQ)
        return step(carry, qs, 0, keep=jnp.where(span, 1.0, 0.0), bias=jnp.where(span, 0.0, NEG_INF))[1]

    def emit(qi, acc):
        o_ref[0, _rows(qi, TQ), :] = _merge_heads(acc[:TQ], acc[TQ:]).astype(o_ref.dtype)

    _for_each_query_block(lp // TQ, w, run, emit)


def _sb_attn(sb):
    b, lp, _ = sb.shape
    nb = SB_W // LANES
    return _attn_call(_sb_attn_kernel, "sb_attn", b, lp, nb,
                      [_col_spec(lp, 0), _col_spec(lp, nb), _col_spec(lp, 2 * nb)],
                      (sb, sb, sb), n_masks=2)


def _silu(x):
    return x / (1.0 + jnp.exp(-x))


def _ffn_kernel(h_ref, g_ref, wg_ref, wu_ref, wd_ref, o_ref, hn_ref, acc_ref):
    f = pl.program_id(1)

    @pl.when(f == 0)
    def _():
        hn_ref[...] = _rms(h_ref[...], g_ref[...]).astype(BF16)
        acc_ref[...] = jnp.zeros_like(acc_ref)

    hn = hn_ref[...]
    a = _silu(_dot(hn, wg_ref[...])) * _dot(hn, wu_ref[...])
    acc_ref[...] += _dot(a.astype(BF16), wd_ref[...])

    @pl.when(f == pl.num_programs(1) - 1)
    def _():
        o_ref[...] = h_ref[...] + acc_ref[...]


def _ffn(h, g, wg, wu, wd):
    t, d = h.shape
    ff = wg.shape[1]
    tm = _pick(t, (512, 384, 256, 128))
    tf = _pick(ff, (1408, 512, 256, 128))
    return pl.pallas_call(
        _ffn_kernel,
        grid=(t // tm, ff // tf),
        in_specs=[pl.BlockSpec((tm, d), lambda i, f: (i, 0)),
                  pl.BlockSpec((1, d), lambda i, f: (0, 0)),
                  pl.BlockSpec((d, tf), lambda i, f: (0, f)),
                  pl.BlockSpec((d, tf), lambda i, f: (0, f)),
                  pl.BlockSpec((tf, d), lambda i, f: (f, 0))],
        out_specs=pl.BlockSpec((tm, d), lambda i, f: (i, 0)),
        out_shape=jax.ShapeDtypeStruct((t, d), F32),
        scratch_shapes=[pltpu.VMEM((tm, d), BF16), pltpu.VMEM((tm, d), F32)],
        compiler_params=_cparams(("parallel", "arbitrary")),
        name="ffn",
    )(h, g, wg, wu, wd)


def _router_gates(hn, wr_ref):
    a, b, c = _split3(hn)
    wa, wb, wc = wr_ref[0], wr_ref[1], wr_ref[2]
    logits = (_dot(a, wa) + _dot(a, wb) + _dot(b, wa) + _dot(a, wc) + _dot(b, wb) + _dot(c, wa))
    lane = lax.broadcasted_iota(jnp.int32, logits.shape, 1).astype(F32)
    logits = jnp.where(lane < N_EXPERTS, logits, -jnp.inf)
    v1 = jnp.max(logits, axis=-1, keepdims=True)
    i1 = jnp.min(jnp.where(logits == v1, lane, float(LANES)), axis=-1, keepdims=True)
    rest = jnp.where(lane == i1, -jnp.inf, logits)
    v2 = jnp.max(rest, axis=-1, keepdims=True)
    i2 = jnp.min(jnp.where(rest == v2, lane, float(LANES)), axis=-1, keepdims=True)
    e2 = jnp.exp(v2 - v1)
    den = 1.0 + e2
    gate = jnp.where(lane == i1, 1.0 / den, 0.0) + jnp.where(lane == i2, e2 / den, 0.0)
    sel = jnp.where((lane == i1) | (lane == i2), 1.0, 0.0)
    return gate, sel


MOE_TM = 512
MOE_CHUNK = 128
MOE_ALIGN = 16
MOE_ROWS = 1024
SEL_LANE = N_EXPERTS
RANK_LANE = 2 * N_EXPERTS
T_ROWS = 16


def _route_kernel(h_ref, g_ref, wr_ref, hn_ref, route_ref, selt_ref, rankt_ref, cnt_ref):
    tm = h_ref.shape[0]
    hn = _rms(h_ref[...], g_ref[...])
    hn_ref[...] = hn.astype(BF16)
    gate, sel = _router_gates(hn, wr_ref)
    selb = sel.astype(BF16)
    r = lax.broadcasted_iota(jnp.int32, (tm, tm), 0)
    c = lax.broadcasted_iota(jnp.int32, (tm, tm), 1)
    rank = _dot(jnp.where(c < r, 1.0, 0.0).astype(BF16), selb)
    route_ref[...] = gate + pltpu.roll(sel, SEL_LANE, 1) + pltpu.roll(rank, RANK_LANE, 1)
    er = lax.broadcasted_iota(jnp.int32, (T_ROWS, LANES), 0)
    el = lax.broadcasted_iota(jnp.int32, (T_ROWS, LANES), 1)
    pick = jnp.where((er == el) & (er < N_EXPERTS), 1.0, 0.0).astype(BF16)
    selt = _dot_nt(pick, selb)
    selt_ref[0] = selt
    rankt_ref[0] = _dot(selt.astype(BF16), jnp.where(r < c, 1.0, 0.0).astype(BF16))
    cnt_ref[0] = jnp.sum(sel, axis=0, keepdims=True)


def _moe_route(h, g, wr3):
    t, d = h.shape
    tm = _pick(t, (MOE_TM, 384, 256, 128))
    nt = t // tm
    return pl.pallas_call(
        _route_kernel,
        grid=(nt,),
        in_specs=[pl.BlockSpec((tm, d), lambda i: (i, 0)),
                  pl.BlockSpec((1, d), lambda i: (0, 0)),
                  pl.BlockSpec((3, d, LANES), lambda i: (0, 0, 0))],
        out_specs=[pl.BlockSpec((tm, d), lambda i: (i, 0)),
                   pl.BlockSpec((tm, LANES), lambda i: (i, 0)),
                   pl.BlockSpec((1, T_ROWS, tm), lambda i: (i, 0, 0)),
                   pl.BlockSpec((1, T_ROWS, tm), lambda i: (i, 0, 0)),
                   pl.BlockSpec((1, 1, LANES), lambda i: (i, 0, 0))],
        out_shape=[jax.ShapeDtypeStruct((t, d), BF16),
                   jax.ShapeDtypeStruct((t, LANES), F32),
                   jax.ShapeDtypeStruct((nt, T_ROWS, tm), F32),
                   jax.ShapeDtypeStruct((nt, T_ROWS, tm), F32),
                   jax.ShapeDtypeStruct((nt, 1, LANES), F32)],
        compiler_params=_cparams(("parallel",)),
        name="moe_route",
    )(h, g, wr3)


def _moe_plan(cnt, n_row_tiles):
    cnt_al = (cnt + MOE_ALIGN - 1) // MOE_ALIGN * MOE_ALIGN
    tot = jnp.sum(cnt_al, axis=0)
    cap = (tot + MOE_CHUNK + MOE_ROWS - 1) // MOE_ROWS * MOE_ROWS
    base = jnp.cumsum(cap) - cap
    offs = base[None, :] + jnp.cumsum(cnt_al, axis=0) - cnt_al
    starts = jnp.arange(n_row_tiles, dtype=jnp.int32) * MOE_ROWS
    tile_e = jnp.clip(jnp.sum(starts[:, None] >= base[None, :], axis=1) - 1, 0, N_EXPERTS - 1)
    tile_ok = starts < (base + tot)[tile_e]
    return (offs.reshape(-1).astype(jnp.int32), tile_e.astype(jnp.int32), tile_ok.astype(jnp.int32))


def _chunk_copies(n_chunks):
    return [(e, c) for e in range(N_EXPERTS) for c in range(n_chunks)]


def _dispatch_kernel(offs_ref, cnts_ref, hn_ref, selt_ref, rankt_ref, xs_in_ref, xs_ref, stage, sems):
    del xs_in_ref
    i = pl.program_id(0)
    tm = hn_ref.shape[0]
    hn = hn_ref[...]
    ridx = lax.broadcasted_iota(jnp.int32, (MOE_CHUNK, tm), 0).astype(F32)

    def copy(e, c):
        slot = e * (tm // MOE_CHUNK) + c
        row0 = pl.multiple_of(offs_ref[i * N_EXPERTS + e] + c * MOE_CHUNK, MOE_ALIGN)
        return pltpu.make_async_copy(stage.at[slot], xs_ref.at[pl.ds(row0, MOE_CHUNK)], sems.at[slot])

    for e, c in _chunk_copies(tm // MOE_CHUNK):
        @pl.when(c * MOE_CHUNK < cnts_ref[i * N_EXPERTS + e])
        def _(e=e, c=c):
            hit = (rankt_ref[0, e:e + 1, :] - float(c * MOE_CHUNK) == ridx) & (selt_ref[0, e:e + 1, :] > 0.5)
            rows = _dot(jnp.where(hit, 1.0, 0.0).astype(BF16), hn)
            stage[e * (tm // MOE_CHUNK) + c] = rows.astype(BF16)
            copy(e, c).start()

    for e, c in _chunk_copies(tm // MOE_CHUNK):
        @pl.when(c * MOE_CHUNK < cnts_ref[i * N_EXPERTS + e])
        def _(e=e, c=c):
            copy(e, c).wait()


def _moe_dispatch(offs, cnts, hn, selt, rankt, n_rows):
    t, d = hn.shape
    nt, _, tm = selt.shape
    n_slots = N_EXPERTS * (tm // MOE_CHUNK)
    xs0 = jnp.zeros((n_rows, d), BF16)
    grid_spec = pltpu.PrefetchScalarGridSpec(
        num_scalar_prefetch=2,
        grid=(nt,),
        in_specs=[pl.BlockSpec((tm, d), lambda i, *_: (i, 0)),
                  pl.BlockSpec((1, T_ROWS, tm), lambda i, *_: (i, 0, 0)),
                  pl.BlockSpec((1, T_ROWS, tm), lambda i, *_: (i, 0, 0)),
                  pl.BlockSpec(memory_space=pl.ANY)],
        out_specs=pl.BlockSpec(memory_space=pl.ANY),
        scratch_shapes=[pltpu.VMEM((n_slots, MOE_CHUNK, d), BF16), pltpu.SemaphoreType.DMA((n_slots,))],
    )
    return pl.pallas_call(
        _dispatch_kernel,
        grid_spec=grid_spec,
        out_shape=jax.ShapeDtypeStruct((n_rows, d), BF16),
        input_output_aliases={5: 0},
        compiler_params=_cparams(("arbitrary",)),
        name="moe_dispatch",
    )(offs, cnts, hn, selt, rankt, xs0)


def _expert_kernel(te_ref, ok_ref, xs_ref, wg_ref, wu_ref, wd_ref, ys_ref, acc_ref):
    s = pl.program_id(0)
    f = pl.program_id(1)
    ok = ok_ref[s] > 0

    @pl.when(ok & (f == 0))
    def _():
        acc_ref[...] = jnp.zeros_like(acc_ref)

    @pl.when(ok)
    def _():
        x = xs_ref[...]
        a = _silu(_dot(x, wg_ref[0])) * _dot(x, wu_ref[0])
        acc_ref[...] += _dot(a.astype(BF16), wd_ref[0])

    last = f == pl.num_programs(1) - 1

    @pl.when(ok & last)
    def _():
        ys_ref[...] = acc_ref[...].astype(ys_ref.dtype)

    @pl.when(jnp.logical_not(ok) & last)
    def _():
        ys_ref[...] = jnp.zeros_like(ys_ref)


def _moe_experts(tile_e, tile_ok, xs, wg, wu, wd):
    n_rows, d = xs.shape
    ff = wg.shape[2]
    tf = _pick(ff, (512, 256, 128))
    nf = ff // tf

    def f_idx(s, f, ok_ref):
        return jnp.where(ok_ref[s] > 0, f, nf - 1)

    grid_spec = pltpu.PrefetchScalarGridSpec(
        num_scalar_prefetch=2,
        grid=(n_rows // MOE_ROWS, nf),
        in_specs=[pl.BlockSpec((MOE_ROWS, d), lambda s, f, te, ok: (s, 0)),
                  pl.BlockSpec((1, d, tf), lambda s, f, te, ok: (te[s], 0, f_idx(s, f, ok))),
                  pl.BlockSpec((1, d, tf), lambda s, f, te, ok: (te[s], 0, f_idx(s, f, ok))),
                  pl.BlockSpec((1, tf, d), lambda s, f, te, ok: (te[s], f_idx(s, f, ok), 0))],
        out_specs=pl.BlockSpec((MOE_ROWS, d), lambda s, f, te, ok: (s, 0)),
        scratch_shapes=[pltpu.VMEM((MOE_ROWS, d), F32)],
    )
    return pl.pallas_call(
        _expert_kernel,
        grid_spec=grid_spec,
        out_shape=jax.ShapeDtypeStruct((n_rows, d), BF16),
        compiler_params=_cparams(("parallel", "arbitrary")),
        name="moe_experts",
    )(tile_e, tile_ok, xs, wg, wu, wd)


def _combine_kernel(offs_ref, cnts_ref, h_ref, route_ref, ys_ref, o_ref, stage, sems):
    i = pl.program_id(0)
    tm = h_ref.shape[0]
    n_chunks = tm // MOE_CHUNK
    ridx = lax.broadcasted_iota(jnp.int32, (tm, MOE_CHUNK), 1).astype(F32)

    def copy(e, c):
        slot = e * n_chunks + c
        row0 = pl.multiple_of(offs_ref[i * N_EXPERTS + e] + c * MOE_CHUNK, MOE_ALIGN)
        return pltpu.make_async_copy(ys_ref.at[pl.ds(row0, MOE_CHUNK)], stage.at[slot], sems.at[slot])

    for e, c in _chunk_copies(n_chunks):
        @pl.when(c * MOE_CHUNK < cnts_ref[i * N_EXPERTS + e])
        def _(e=e, c=c):
            copy(e, c).start()

    o_ref[...] = h_ref[...]
    for e, c in _chunk_copies(n_chunks):
        @pl.when(c * MOE_CHUNK < cnts_ref[i * N_EXPERTS + e])
        def _(e=e, c=c):
            copy(e, c).wait()
            gate = route_ref[:, e:e + 1]
            sel = route_ref[:, SEL_LANE + e:SEL_LANE + e + 1]
            rank = route_ref[:, RANK_LANE + e:RANK_LANE + e + 1]
            hit = (rank - float(c * MOE_CHUNK) == ridx) & (sel > 0.5)
            back = _dot(jnp.where(hit, 1.0, 0.0).astype(BF16), stage[e * n_chunks + c])
            o_ref[...] += gate * back


def _moe_combine(offs, cnts, h, route, ys):
    t, d = h.shape
    tm = _pick(t, (MOE_TM, 384, 256, 128))
    n_slots = N_EXPERTS * (tm // MOE_CHUNK)
    grid_spec = pltpu.PrefetchScalarGridSpec(
        num_scalar_prefetch=2,
        grid=(t // tm,),
        in_specs=[pl.BlockSpec((tm, d), lambda i, *_: (i, 0)),
                  pl.BlockSpec((tm, LANES), lambda i, *_: (i, 0)),
                  pl.BlockSpec(memory_space=pl.ANY)],
        out_specs=pl.BlockSpec((tm, d), lambda i, *_: (i, 0)),
        scratch_shapes=[pltpu.VMEM((n_slots, MOE_CHUNK, d), BF16), pltpu.SemaphoreType.DMA((n_slots,))],
    )
    return pl.pallas_call(
        _combine_kernel,
        grid_spec=grid_spec,
        out_shape=jax.ShapeDtypeStruct((t, d), F32),
        compiler_params=_cparams(("arbitrary",)),
        name="moe_combine",
    )(offs, cnts, h, route, ys)


def _moe(h, g, wr3, wg, wu, wd):
    t, d = h.shape
    hn, route, selt, rankt, cnt = _moe_route(h, g, wr3)
    nt = cnt.shape[0]
    cnt = cnt[:, 0, :N_EXPERTS].astype(jnp.int32)
    worst = TOP_K * t + nt * N_EXPERTS * (MOE_ALIGN - 1) + N_EXPERTS * (MOE_CHUNK + MOE_ROWS - 1)
    n_row_tiles = -(-worst // MOE_ROWS)
    offs, tile_e, tile_ok = _moe_plan(cnt, n_row_tiles)
    cnts = cnt.reshape(-1)
    xs = _moe_dispatch(offs, cnts, hn, selt, rankt, n_row_tiles * MOE_ROWS)
    ys = _moe_experts(tile_e, tile_ok, xs, wg, wu, wd)
    return _moe_combine(offs, cnts, h, route, ys)


def _final_norm_kernel(h_ref, g_ref, o_ref):
    o_ref[0] = _rms(h_ref[0], g_ref[...])


def _final_norm(h3, g, seq):
    b, lp, d = h3.shape
    tm = PAD
    return pl.pallas_call(
        _final_norm_kernel,
        grid=(b, seq // tm),
        in_specs=[pl.BlockSpec((1, tm, d), lambda bi, i: (bi, i + PAD // tm, 0)),
                  pl.BlockSpec((1, d), lambda bi, i: (0, 0))],
        out_specs=pl.BlockSpec((1, tm, d), lambda bi, i: (bi, i, 0)),
        out_shape=jax.ShapeDtypeStruct((b, seq, d), F32),
        compiler_params=_cparams(("parallel", "parallel")),
        name="final_norm",
    )(h3, g)


def _rope_table(lp, rot_dim, theta, width, offset, scale):
    half = rot_dim // 2
    pos = (jnp.arange(lp, dtype=jnp.int32) - FIRST).astype(F32)
    inv = theta ** (-jnp.arange(half, dtype=F32) * 2.0 / rot_dim)
    ang = pos[:, None] * inv[None, :]
    cos, sin = jnp.cos(ang), jnp.sin(ang)
    ones = jnp.ones((lp, 1), F32)
    zeros = jnp.zeros((lp, 1), F32)

    def unit(first, second, fill):
        parts = [jnp.tile(fill, (1, offset)), first, second,
                 jnp.tile(fill, (1, width - offset - rot_dim))]
        return jnp.tile(jnp.concatenate(parts, axis=1), (1, LANES // width))

    c = unit(cos, cos, ones)
    s1 = unit(-sin, jnp.zeros_like(sin), zeros)
    s2 = unit(jnp.zeros_like(sin), sin, zeros)
    return jnp.concatenate([c, s1, s2], axis=1) * scale


def _pad_heads(w, heads, used, width):
    k = w.shape[0]
    w = w.reshape(k, heads, used)
    return jnp.pad(w, ((0, 0), (0, 0), (0, width - used))).reshape(k, heads * width)


def kernel(x, meta_tokens, ln_mix_e, w_in_e, ln_mla_q_e, w_mla_uq_e, ln_mla_kv_e, w_mla_ukv_e,
           w_out_e, ln_ffn_e, w_ffn_gate_e, w_ffn_up_e, w_ffn_down_e, ln_mix_o, w_in_o, b_forget_o,
           diff_lq1_o, diff_lk1_o, diff_lq2_o, diff_lk2_o, diff_subln_o, w_out_o, ln_ffn_o,
           w_router_o, w_moe_gate_o, w_moe_up_o, w_moe_down_o, ln_final):
    b, seq, d = x.shape
    assert d == D_MODEL and seq % TQ == 0
    lp = PAD + seq
    t = b * lp
    depth = 2

    meta = jnp.broadcast_to(meta_tokens.astype(x.dtype)[None], (b, N_META, d))
    h = jnp.concatenate([jnp.zeros((b, FIRST, d), x.dtype), meta, x], axis=1).reshape(t, d)

    row = lambda v: v.reshape(1, -1).astype(F32)
    sb_scale = HEAD_DIM ** -0.5
    tab_mla_q = _rope_table(lp, MLA_ROPE, MLA_THETA, LANES, MLA_NOPE,
                            (MLA_NOPE + MLA_ROPE) ** -0.5 * LOG2E)
    tab_mla_k = _rope_table(lp, MLA_ROPE, MLA_THETA, LANES, MLA_NOPE, 1.0)
    tab_diff_q = _rope_table(lp, ROT_DIM, ROPE_THETA, DIFF_DIM, 0, LOG2E)
    tab_diff_k = _rope_table(lp, ROT_DIM, ROPE_THETA, DIFF_DIM, 0, 1.0)

    for i in range(depth):
        j = i // 2
        if i % 2 == 0:
            w_in = w_in_e[j]
            lat_w = MLA_Q_RANK + MLA_KV_RANK + MLA_ROPE
            w_cat = jnp.concatenate(
                [w_in[:, :SB_W] * sb_scale, w_in[:, SB_W:3 * SB_W + lat_w],
                 jnp.zeros((d, 4 * LANES - lat_w), F32)], axis=1).astype(BF16)
            sb, lat = _even_proj(h, row(ln_mix_e[j]), w_cat)
            sb = sb.reshape(b, lp, -1)
            lat = lat.reshape(b, lp, -1)

            wq = _pad_heads(w_mla_uq_e[j], MLA_HEADS, MLA_NOPE + MLA_ROPE, LANES).astype(BF16)
            ukv = w_mla_ukv_e[j].reshape(MLA_KV_RANK, MLA_HEADS, MLA_NOPE + MLA_V)
            wk_nope = _pad_heads(ukv[:, :, :MLA_NOPE].reshape(MLA_KV_RANK, -1), MLA_HEADS, MLA_NOPE, LANES)
            place = jnp.pad(jnp.eye(MLA_ROPE, dtype=F32), ((0, 0), (MLA_NOPE, LANES - MLA_NOPE - MLA_ROPE)))
            wk_rope = jnp.tile(place, (1, MLA_HEADS))
            wk = jnp.concatenate(
                [wk_nope, wk_rope, jnp.zeros((LANES - MLA_ROPE, MLA_HEADS * LANES), F32)], axis=0).astype(BF16)
            wv = ukv[:, :, MLA_NOPE:].reshape(MLA_KV_RANK, -1).astype(BF16)
            mq, mk, mv = _mla_up(lat, row(ln_mla_q_e[j]), row(ln_mla_kv_e[j]), wq, wk, wv,
                                 tab_mla_q, tab_mla_k)

            a_out = _sb_attn(sb).reshape(t, -1)
            b_out = _mla_attn(mq, mk, mv).reshape(t, -1)
            w_out = w_out_e[j].astype(BF16)
            h = _out_proj(h, a_out, b_out, w_out[:SB_W], w_out[SB_W:])
            h = _ffn(h, row(ln_ffn_e[j]), w_ffn_gate_e[j].astype(BF16), w_ffn_up_e[j].astype(BF16),
                     w_ffn_down_e[j].astype(BF16))
        else:
            lam_init = 0.8 - 0.6 * math.exp(-0.3 * i)
            w_in = w_in_o[j]
            wf = w_in[:, N_ODD_MAIN:]
            w_cat = jnp.concatenate(
                [w_in[:, :DIFF_W] * sb_scale, w_in[:, DIFF_W:3 * DIFF_W],
                 w_in[:, 3 * DIFF_W:3 * DIFF_W + FOX_W] * sb_scale, w_in[:, 3 * DIFF_W + FOX_W:N_ODD_MAIN],
                 wf, jnp.zeros((d, LANES - FOX_HEADS), F32)], axis=1).astype(BF16)
            wft = jnp.pad(wf.T, ((0, F_ROWS - FOX_HEADS), (0, 0))).astype(BF16)
            bf = b_forget_o[j].astype(F32)
            bcol = jnp.pad(bf, (0, LANES - FOX_HEADS)).reshape(1, LANES)
            brow = jnp.pad(bf, (0, F_ROWS - FOX_HEADS)).reshape(F_ROWS, 1)
            main, fcol, frow = _odd_proj(h.reshape(b, lp, d), row(ln_mix_o[j]), w_cat, wft, bcol, brow,
                                         tab_diff_q, tab_diff_k)
            ccol, crow = _fox_cumsum(fcol, frow)
            crow4 = crow[:, :FOX_HEADS].reshape(b, FOX_HEADS // 2, 2, lp)
            lam_vecs = jnp.stack([diff_lq1_o[j], diff_lk1_o[j], diff_lq2_o[j], diff_lk2_o[j]]).astype(F32)
            c_out = _diff_attn(main, lam_vecs, row(diff_subln_o[j]), lam_init).reshape(t, -1)
            d_out = _fox_attn(main, ccol, crow4).reshape(t, -1)
            w_out = w_out_o[j].astype(BF16)
            h = _out_proj(h, c_out, d_out, w_out[:DIFF_W], w_out[DIFF_W:])
            wr = jnp.pad(w_router_o[j].astype(F32), ((0, 0), (0, LANES - N_EXPERTS)))
            wr3 = jnp.stack(_split3(wr))
            h = _moe(h, row(ln_ffn_o[j]), wr3, w_moe_gate_o[j].astype(BF16), w_moe_up_o[j].astype(BF16),
                     w_moe_down_o[j].astype(BF16))

    return _final_norm(h.reshape(b, lp, d), row(ln_final), seq)
```

```python
import functools
import math

import jax
import jax.numpy as jnp
from jax import lax
from jax.experimental import pallas as pl
from jax.experimental.pallas import tpu as pltpu

F32 = jnp.float32
BF16 = jnp.bfloat16

D_MODEL = 1024
CHUNK = 64
N_META = 16
EPS = 1e-6
LOG2E = 1.4426950408889634
NEG_INF = -1e30

HEAD_DIM = 64
SB_HEADS = 8
MLA_HEADS = 8
MLA_NOPE = 64
MLA_ROPE = 32
MLA_V = 64
MLA_Q_RANK = 256
MLA_KV_RANK = 128
MLA_THETA = 10000.0
DIFF_HEADS = 4
DIFF_DIM = 64
FOX_HEADS = 8
ROPE_THETA = 500000.0
ROT_DIM = HEAD_DIM // 4
N_EXPERTS = 8
TOP_K = 2

SB_W = SB_HEADS * HEAD_DIM
DIFF_W = DIFF_HEADS * 2 * DIFF_DIM
FOX_W = FOX_HEADS * HEAD_DIM

LANES = 128
PAD = 128
FIRST = PAD - N_META
TQ = 128
KEY_WINDOW = 512
Q_GROUP = 4
SUFFIX_CHUNK = 256
CHUNK_SHIFT = CHUNK.bit_length() - 1
VMEM_LIMIT = 56 * 1024 * 1024


def _cparams(sem):
    return pltpu.CompilerParams(dimension_semantics=sem, vmem_limit_bytes=VMEM_LIMIT)


def _pick(n, cands):
    for c in cands:
        if n % c == 0:
            return c
    raise ValueError(f"no tile in {cands} divides {n}")


def _rms(x, g):
    return x * lax.rsqrt(jnp.mean(x * x, axis=-1, keepdims=True) + EPS) * g


def _dot(a, b):
    return jnp.dot(a, b, preferred_element_type=F32)


def _dot_nt(a, b):
    return lax.dot_general(a, b, (((1,), (1,)), ((), ())), preferred_element_type=F32)


def _rope_chunk(y, tab, shift):
    c = tab[:, 0:LANES]
    s1 = tab[:, LANES:2 * LANES]
    s2 = tab[:, 2 * LANES:3 * LANES]
    return (y * c + pltpu.roll(y, LANES - shift, 1) * s1 + pltpu.roll(y, shift, 1) * s2)


def _split3(x):
    a = x.astype(BF16)
    r = x - a.astype(F32)
    b = r.astype(BF16)
    c = (r - b.astype(F32)).astype(BF16)
    return a, b, c


def _even_proj_kernel(h_ref, g_ref, w_ref, sb_ref, lat_ref):
    hn = _rms(h_ref[...], g_ref[...]).astype(BF16)
    y = _dot(hn, w_ref[...])
    n_sb = sb_ref.shape[-1]
    sb_ref[:, :SB_W] = (y[:, :SB_W] * LOG2E).astype(BF16)
    sb_ref[:, SB_W:] = y[:, SB_W:n_sb].astype(BF16)
    lat_ref[...] = y[:, n_sb:].astype(BF16)


def _even_proj(h, g, w):
    t, d = h.shape
    n = w.shape[1]
    n_sb = 3 * SB_W
    tm = _pick(t, (512, 384, 256, 128))
    return pl.pallas_call(
        _even_proj_kernel,
        grid=(t // tm,),
        in_specs=[pl.BlockSpec((tm, d), lambda i: (i, 0)),
                  pl.BlockSpec((1, d), lambda i: (0, 0)),
                  pl.BlockSpec((d, n), lambda i: (0, 0))],
        out_specs=[pl.BlockSpec((tm, n_sb), lambda i: (i, 0)),
                   pl.BlockSpec((tm, n - n_sb), lambda i: (i, 0))],
        out_shape=[jax.ShapeDtypeStruct((t, n_sb), BF16),
                   jax.ShapeDtypeStruct((t, n - n_sb), BF16)],
        compiler_params=_cparams(("parallel",)),
        name="even_proj",
    )(h, g, w)


def _mla_up_kernel(lat_ref, gq_ref, gkv_ref, wq_ref, wk_ref, wv_ref, tq_ref, tk_ref,
                   q_ref, k_ref, v_ref):
    lat = lat_ref[0]
    nq = _rms(lat[:, :MLA_Q_RANK].astype(F32), gq_ref[...]).astype(BF16)
    nkv = _rms(lat[:, MLA_Q_RANK:MLA_Q_RANK + MLA_KV_RANK].astype(F32), gkv_ref[...]).astype(BF16)
    yq = _dot(nq, wq_ref[...])
    xk = jnp.concatenate([nkv, lat[:, MLA_Q_RANK + MLA_KV_RANK:]], axis=1)
    yk = _dot(xk, wk_ref[...])
    v_ref[0] = _dot(nkv, wv_ref[...]).astype(BF16)
    tabq = tq_ref[...]
    tabk = tk_ref[...]
    for hd in range(MLA_HEADS):
        sl = slice(hd * LANES, (hd + 1) * LANES)
        q_ref[0, :, sl] = _rope_chunk(yq[:, sl], tabq, MLA_ROPE // 2).astype(BF16)
        k_ref[0, :, sl] = _rope_chunk(yk[:, sl], tabk, MLA_ROPE // 2).astype(BF16)


def _mla_up(lat, gq, gkv, wq, wk, wv, tabq, tabk):
    b, lp, nl = lat.shape
    tm = _pick(lp, (384, 128))
    nqk = MLA_HEADS * LANES
    nv = MLA_HEADS * MLA_V
    full = lambda shp: pl.BlockSpec(shp, lambda bi, i: (0,) * len(shp))
    return pl.pallas_call(
        _mla_up_kernel,
        grid=(b, lp // tm),
        in_specs=[pl.BlockSpec((1, tm, nl), lambda bi, i: (bi, i, 0)),
                  full(gq.shape), full(gkv.shape), full(wq.shape), full(wk.shape), full(wv.shape),
                  pl.BlockSpec((tm, 3 * LANES), lambda bi, i: (i, 0)),
                  pl.BlockSpec((tm, 3 * LANES), lambda bi, i: (i, 0))],
        out_specs=[pl.BlockSpec((1, tm, nqk), lambda bi, i: (bi, i, 0)),
                   pl.BlockSpec((1, tm, nqk), lambda bi, i: (bi, i, 0)),
                   pl.BlockSpec((1, tm, nv), lambda bi, i: (bi, i, 0))],
        out_shape=[jax.ShapeDtypeStruct((b, lp, nqk), BF16),
                   jax.ShapeDtypeStruct((b, lp, nqk), BF16),
                   jax.ShapeDtypeStruct((b, lp, nv), BF16)],
        compiler_params=_cparams(("parallel", "parallel")),
        name="mla_up",
    )(lat, gq, gkv, wq, wk, wv, tabq, tabk)


N_ODD_MAIN = 3 * DIFF_W + 3 * FOX_W
F_ROWS = 16


def _odd_proj_kernel(h_ref, g_ref, w_ref, wft_ref, bcol_ref, brow_ref, tabq_ref, tabk_ref,
                     main_ref, fcol_ref, frow_ref):
    hn = _rms(h_ref[0], g_ref[...]).astype(BF16)
    y = _dot(hn, w_ref[...])
    n_rope = DIFF_W // LANES
    for c in range(2 * n_rope):
        sl = slice(c * LANES, (c + 1) * LANES)
        tab = tabq_ref[...] if c < n_rope else tabk_ref[...]
        main_ref[0, :, sl] = _rope_chunk(y[:, sl], tab, ROT_DIM // 2).astype(BF16)
    fq = slice(3 * DIFF_W, 3 * DIFF_W + FOX_W)
    main_ref[0, :, 2 * DIFF_W:3 * DIFF_W] = y[:, 2 * DIFF_W:3 * DIFF_W].astype(BF16)
    main_ref[0, :, fq] = (y[:, fq] * LOG2E).astype(BF16)
    main_ref[0, :, 3 * DIFF_W + FOX_W:] = y[:, 3 * DIFF_W + FOX_W:N_ODD_MAIN].astype(BF16)
    fcol_ref[0] = y[:, N_ODD_MAIN:] + bcol_ref[...]
    frow_ref[0] = _dot_nt(wft_ref[...], hn) + brow_ref[...]


def _odd_proj(h3, g, w, wft, bcol, brow, tabq, tabk):
    b, lp, d = h3.shape
    tm = _pick(lp, (384, 128))
    full = lambda shp: pl.BlockSpec(shp, lambda bi, i: (0,) * len(shp))
    return pl.pallas_call(
        _odd_proj_kernel,
        grid=(b, lp // tm),
        in_specs=[pl.BlockSpec((1, tm, d), lambda bi, i: (bi, i, 0)),
                  full(g.shape), full(w.shape), full(wft.shape), full(bcol.shape), full(brow.shape),
                  pl.BlockSpec((tm, 3 * LANES), lambda bi, i: (i, 0)),
                  pl.BlockSpec((tm, 3 * LANES), lambda bi, i: (i, 0))],
        out_specs=[pl.BlockSpec((1, tm, N_ODD_MAIN), lambda bi, i: (bi, i, 0)),
                   pl.BlockSpec((1, tm, LANES), lambda bi, i: (bi, i, 0)),
                   pl.BlockSpec((1, F_ROWS, tm), lambda bi, i: (bi, 0, i))],
        out_shape=[jax.ShapeDtypeStruct((b, lp, N_ODD_MAIN), BF16),
                   jax.ShapeDtypeStruct((b, lp, LANES), F32),
                   jax.ShapeDtypeStruct((b, F_ROWS, lp), F32)],
        compiler_params=_cparams(("parallel", "parallel")),
        name="odd_proj",
    )(h3, g, w, wft, bcol, brow, tabq, tabk)


def _out_proj_kernel(h_ref, a_ref, b_ref, wa_ref, wb_ref, o_ref):
    o_ref[...] = h_ref[...] + _dot(a_ref[...], wa_ref[...]) + _dot(b_ref[...], wb_ref[...])


def _out_proj(h, mix_a, mix_b, wa, wb):
    t, d = h.shape
    tm = _pick(t, (512, 384, 256, 128))
    na, nb = mix_a.shape[1], mix_b.shape[1]
    return pl.pallas_call(
        _out_proj_kernel,
        grid=(t // tm,),
        in_specs=[pl.BlockSpec((tm, d), lambda i: (i, 0)),
                  pl.BlockSpec((tm, na), lambda i: (i, 0)),
                  pl.BlockSpec((tm, nb), lambda i: (i, 0)),
                  pl.BlockSpec((na, d), lambda i: (0, 0)),
                  pl.BlockSpec((nb, d), lambda i: (0, 0))],
        out_specs=pl.BlockSpec((tm, d), lambda i: (i, 0)),
        out_shape=jax.ShapeDtypeStruct((t, d), F32),
        compiler_params=_cparams(("parallel",)),
        name="out_proj",
    )(h, mix_a, mix_b, wa, wb)


def _log_sigmoid(x):
    return jnp.minimum(x, 0.0) - jnp.log(1.0 + jnp.exp(-jnp.abs(x)))


def _fox_cumsum_kernel(fcol_ref, frow_ref, ccol_ref, crow_ref):
    lp = fcol_ref.shape[1]
    blk = LANES
    r = lax.broadcasted_iota(jnp.int32, (blk, blk), 0)
    c = lax.broadcasted_iota(jnp.int32, (blk, blk), 1)
    tri_l = (c <= r).astype(BF16)
    tri_u = (r <= c).astype(BF16)
    row_id = lax.broadcasted_iota(jnp.int32, (blk, LANES), 0)
    col_id = lax.broadcasted_iota(jnp.int32, (F_ROWS, blk), 1)
    carry_c = jnp.zeros((1, LANES), F32)
    carry_r = jnp.zeros((F_ROWS, 1), F32)
    for j in range(lp // blk):
        sl = slice(j * blk, (j + 1) * blk)
        ls = _log_sigmoid(fcol_ref[0, sl, :])
        ls = jnp.where(row_id + j * blk >= FIRST, ls, 0.0)
        a, b, c3 = _split3(ls)
        cs = _dot(tri_l, a) + _dot(tri_l, b) + _dot(tri_l, c3) + carry_c
        ccol_ref[0, sl, :] = cs * LOG2E
        carry_c = cs[blk - 1:blk, :]
        lr = _log_sigmoid(frow_ref[0, :, sl])
        lr = jnp.where(col_id + j * blk >= FIRST, lr, 0.0)
        a, b, c3 = _split3(lr)
        cr = _dot(a, tri_u) + _dot(b, tri_u) + _dot(c3, tri_u) + carry_r
        crow_ref[0, :, sl] = cr * LOG2E
        carry_r = cr[:, blk - 1:blk]


def _fox_cumsum(fcol, frow):
    b, lp, _ = fcol.shape
    return pl.pallas_call(
        _fox_cumsum_kernel,
        grid=(b,),
        in_specs=[pl.BlockSpec((1, lp, LANES), lambda bi: (bi, 0, 0)),
                  pl.BlockSpec((1, F_ROWS, lp), lambda bi: (bi, 0, 0))],
        out_specs=[pl.BlockSpec((1, lp, LANES), lambda bi: (bi, 0, 0)),
                   pl.BlockSpec((1, F_ROWS, lp), lambda bi: (bi, 0, 0))],
        out_shape=[jax.ShapeDtypeStruct((b, lp, LANES), F32),
                   jax.ShapeDtypeStruct((b, F_ROWS, lp), F32)],
        compiler_params=_cparams(("parallel",)),
        name="fox_cumsum",
    )(fcol, frow)


def _rows(i, n):
    if isinstance(i, int):
        return pl.ds(i * n, n)
    return pl.ds(pl.multiple_of(i * n, n), n)


def _keys(k0, w):
    if isinstance(k0, int):
        return pl.ds(k0, w)
    return pl.ds(pl.multiple_of(k0, TQ), w)


def _visible(kind, ks, qs):
    if kind == "chunk":
        return (ks >> CHUNK_SHIFT) <= (qs >> CHUNK_SHIFT)
    if kind == "causal":
        return ks <= qs
    return ks < qs


def _first_window_mask(kind, q0, w, lim):
    r = lax.broadcasted_iota(jnp.int32, (2 * TQ, w), 0)
    ks = lax.broadcasted_iota(jnp.int32, (2 * TQ, w), 1)
    return _visible(kind, ks, q0 + (r & (TQ - 1))) & (ks >= FIRST) & (ks < lim)


def _diag_window_mask(kind, w):
    r = lax.broadcasted_iota(jnp.int32, (2 * TQ, w), 0)
    ks = lax.broadcasted_iota(jnp.int32, (2 * TQ, w), 1) + (TQ - w)
    return _visible(kind, ks, r & (TQ - 1))


def _span_mask(w, lim):
    ks = lax.broadcasted_iota(jnp.int32, (1, w), 1)
    return (ks >= FIRST) & (ks < lim)


def _lane_halves():
    lane = lax.broadcasted_iota(jnp.int32, (TQ, LANES), 1)
    return lane < (LANES // 2)


def _stack_masked(q):
    lo = _lane_halves()
    z = jnp.zeros_like(q)
    return jnp.concatenate([jnp.where(lo, q, z), jnp.where(lo, z, q)], axis=0)


def _softmax_sweep(kind, w, prep, scores_fn, v_fn, dbias_ref):
    ones = jnp.ones((w, LANES), BF16)

    def step(carry, s, v):
        m, acc = carry
        m_new = jnp.maximum(m, jnp.max(s, axis=-1, keepdims=True))
        alpha = jnp.exp2(m - m_new)
        p = jnp.exp2(s - m_new)
        acc = alpha * acc + _dot(p.astype(BF16), jnp.concatenate([v, ones], axis=1))
        return m_new, acc

    def run(qi, mwin):
        g = w // TQ
        ctx = prep(qi)
        q0 = qi * TQ
        pre = qi + 1 - mwin * g
        carry = (jnp.full((2 * TQ, 1), NEG_INF, F32), jnp.zeros((2 * TQ, 2 * LANES), F32))
        s = scores_fn(ctx, 0)
        if mwin == 0:
            s = jnp.where(_first_window_mask(kind, q0, w, pre * TQ), s, NEG_INF)
            return step(carry, s, v_fn(0))[1]
        carry = step(carry, s + jnp.where(_span_mask(w, pre * TQ), 0.0, NEG_INF), v_fn(0))
        for t in range(1, mwin):
            k0 = (pre + (t - 1) * g) * TQ
            carry = step(carry, scores_fn(ctx, k0), v_fn(k0))
        k0 = q0 + TQ - w
        return step(carry, scores_fn(ctx, k0) + dbias_ref[...], v_fn(k0))[1]

    return run


def _for_each_query_block(nq, w, run, emit):
    g = w // TQ
    one = jnp.sign(pl.program_id(0) + 1)
    for mwin in range(-(-nq // g)):
        lo, hi = mwin * g, min((mwin + 1) * g, nq)

        def body(it, _, mwin=mwin, lo=lo):
            qi = lo + Q_GROUP * it
            results = [run(qi + j, mwin) for j in range(Q_GROUP)]
            for j, res in enumerate(results):
                emit(qi + j, res)
            return 0

        n_groups = (hi - lo) // Q_GROUP
        if n_groups:
            lax.fori_loop(0, n_groups * one, body, 0)
        for qi in range(lo + n_groups * Q_GROUP, hi):
            emit(qi, run(qi, mwin))


def _key_window(lp):
    return min(KEY_WINDOW, lp)


def _merge_heads(top, bottom):
    return jnp.where(_lane_halves(), top, bottom)


def _attn_call(kernel, name, b, lp, n_blocks, in_specs, args, n_masks=1):
    w = _key_window(lp)
    return pl.pallas_call(
        kernel,
        grid=(b, n_blocks),
        in_specs=in_specs,
        out_specs=pl.BlockSpec((1, lp, LANES), lambda bi, hp: (bi, 0, hp)),
        out_shape=jax.ShapeDtypeStruct((b, lp, n_blocks * LANES), BF16),
        scratch_shapes=[pltpu.VMEM((2 * TQ, w), F32)] * n_masks,
        compiler_params=_cparams(("parallel", "parallel")),
        name=name,
    )(*args)


def _normalised(acc):
    return acc[:, :LANES] / acc[:, LANES:]


def _col_spec(lp, off):
    return pl.BlockSpec((1, lp, LANES), lambda bi, hp: (bi, 0, off + hp))


def _mla_attn_kernel(q_ref, k_ref, v_ref, o_ref, dbias_ref):
    lp = q_ref.shape[1]
    w = _key_window(lp)
    dbias_ref[...] = jnp.where(_diag_window_mask("chunk", w), 0.0, NEG_INF)

    def prep(qi):
        return q_ref[0, _rows(qi, TQ), 0:LANES], q_ref[0, _rows(qi, TQ), LANES:2 * LANES]

    def scores(ctx, k0):
        qa, qb = ctx
        ka = k_ref[0, _keys(k0, w), 0:LANES]
        kb = k_ref[0, _keys(k0, w), LANES:2 * LANES]
        return jnp.concatenate([_dot_nt(qa, ka), _dot_nt(qb, kb)], axis=0)

    run = _softmax_sweep("chunk", w, prep, scores, lambda k0: v_ref[0, _keys(k0, w), :], dbias_ref)

    def emit(qi, acc):
        o = _normalised(acc)
        o_ref[0, _rows(qi, TQ), :] = _merge_heads(o[:TQ], o[TQ:]).astype(o_ref.dtype)

    _for_each_query_block(lp // TQ, w, run, emit)


def _mla_attn(q, k, v):
    b, lp, _ = q.shape
    wide = lambda: pl.BlockSpec((1, lp, 2 * LANES), lambda bi, hp: (bi, 0, hp))
    return _attn_call(_mla_attn_kernel, "mla_attn", b, lp, MLA_HEADS // 2,
                      [wide(), wide(), _col_spec(lp, 0)], (q, k, v))


def _diff_attn_kernel(lam_init, x_q, x_k, x_v, lam_ref, gsub_ref, o_ref, dbias_ref):
    lp = x_q.shape[1]
    w = _key_window(lp)
    dbias_ref[...] = jnp.where(_diag_window_mask("chunk", w), 0.0, NEG_INF)
    lv = lam_ref[...]
    lam = (jnp.exp(jnp.sum(lv[0:1] * lv[1:2], axis=-1, keepdims=True))
           - jnp.exp(jnp.sum(lv[2:3] * lv[3:4], axis=-1, keepdims=True)) + lam_init)
    gsub = gsub_ref[...]

    def prep(qi):
        return _stack_masked(x_q[0, _rows(qi, TQ), :])

    def scores(qs, k0):
        return _dot_nt(qs, x_k[0, _keys(k0, w), :])

    run = _softmax_sweep("chunk", w, prep, scores, lambda k0: x_v[0, _keys(k0, w), :], dbias_ref)

    def emit(qi, acc):
        o = _normalised(acc)
        o = o[:TQ] - lam * o[TQ:]
        o_ref[0, _rows(qi, TQ), :] = (_rms(o, gsub) * (1.0 - lam_init)).astype(o_ref.dtype)

    _for_each_query_block(lp // TQ, w, run, emit)


def _diff_attn(main, lam_vecs, gsub, lam_init):
    b, lp, _ = main.shape
    nb = DIFF_W // LANES
    full = lambda shp: pl.BlockSpec(shp, lambda bi, hp: (0,) * len(shp))
    return _attn_call(functools.partial(_diff_attn_kernel, lam_init), "diff_attn", b, lp, DIFF_HEADS,
                      [_col_spec(lp, 0), _col_spec(lp, nb), _col_spec(lp, 2 * nb),
                       full(lam_vecs.shape), full(gsub.shape)],
                      (main, main, main, lam_vecs, gsub))


def _fox_attn_kernel(x_q, x_k, x_v, ccol_ref, crow_ref, o_ref, dbias_ref):
    lp = x_q.shape[1]
    w = _key_window(lp)
    dbias_ref[...] = jnp.where(_diag_window_mask("causal", w), 0.0, NEG_INF)
    hp = pl.program_id(1)
    lane = lax.broadcasted_iota(jnp.int32, (TQ, LANES), 1)

    def prep(qi):
        cc = ccol_ref[0, _rows(qi, TQ), :]
        cq0 = jnp.sum(jnp.where(lane == 2 * hp, cc, 0.0), axis=-1, keepdims=True)
        cq1 = jnp.sum(jnp.where(lane == 2 * hp + 1, cc, 0.0), axis=-1, keepdims=True)
        return _stack_masked(x_q[0, _rows(qi, TQ), :]), cq0, cq1

    def scores(ctx, k0):
        qs, cq0, cq1 = ctx
        s = _dot_nt(qs, x_k[0, _keys(k0, w), :])
        ck = crow_ref[0, 0, :, _keys(k0, w)]
        return s + jnp.concatenate([cq0 - ck[0:1, :], cq1 - ck[1:2, :]], axis=0)

    run = _softmax_sweep("causal", w, prep, scores, lambda k0: x_v[0, _keys(k0, w), :], dbias_ref)

    def emit(qi, acc):
        o = _normalised(acc)
        o_ref[0, _rows(qi, TQ), :] = _merge_heads(o[:TQ], o[TQ:]).astype(o_ref.dtype)

    _for_each_query_block(lp // TQ, w, run, emit)


def _fox_attn(main, ccol, crow4):
    b, lp, _ = main.shape
    off = 3 * DIFF_W // LANES
    nb = FOX_W // LANES
    return _attn_call(_fox_attn_kernel, "fox_attn", b, lp, nb,
                      [_col_spec(lp, off), _col_spec(lp, off + nb), _col_spec(lp, off + 2 * nb),
                       pl.BlockSpec((1, lp, LANES), lambda bi, hp: (bi, 0, 0)),
                       pl.BlockSpec((1, 1, 2, lp), lambda bi, hp: (bi, hp, 0, 0))],
                      (main, main, main, ccol, crow4))


def _sb_attn_kernel(x_q, x_k, x_v, o_ref, dkeep_ref, dbias_ref):
    lp = x_q.shape[1]
    w = _key_window(lp)
    g = w // TQ
    cs = SUFFIX_CHUNK if w % SUFFIX_CHUNK == 0 else TQ
    kr = lax.broadcasted_iota(jnp.int32, (cs, cs), 0)
    kc = lax.broadcasted_iota(jnp.int32, (cs, cs), 1)
    later = jnp.where(kr > kc, 1.0, 0.0).astype(BF16)
    dvis = _diag_window_mask("strict", w)
    dkeep_ref[...] = jnp.where(dvis, 1.0, 0.0)
    dbias_ref[...] = jnp.where(dvis, 0.0, NEG_INF)

    def step(carry, qs, k0, keep=None, bias=None, vis=None):
        rsum, acc = carry
        z = _dot_nt(qs, x_k[0, _keys(k0, w), :])
        cost = jnp.maximum(z, 0.0) + jnp.log2(1.0 + jnp.exp2(-jnp.abs(z)))
        if vis is not None:
            cost = jnp.where(vis, cost, 0.0)
        if keep is not None:
            cost = cost * keep
        cb = cost.astype(BF16)
        parts = []
        for c in reversed(range(w // cs)):
            sl = slice(c * cs, (c + 1) * cs)
            inner = _dot(cb[:, sl], later)
            parts.append(inner + rsum)
            rsum = rsum + inner[:, 0:1] + cb[:, c * cs:c * cs + 1].astype(F32)
        arg = z - cost - jnp.concatenate(parts[::-1], axis=1)
        if bias is not None:
            arg = arg + bias
        wgt = jnp.exp2(arg)
        if vis is not None:
            wgt = jnp.where(vis, wgt, 0.0)
        acc = acc + _dot(wgt.astype(BF16), x_v[0, _keys(k0, w), :])
        return rsum, acc

    def run(qi, mwin):
        q0 = qi * TQ
        qs = _stack_masked(x_q[0, _rows(qi, TQ), :])
        pre = qi + 1 - mwin * g
        carry = (jnp.zeros((2 * TQ, 1), F32), jnp.zeros((2 * TQ, LANES), F32))
        if mwin == 0:
            return step(carry, qs, 0, vis=_first_window_mask("strict", q0, w, pre * TQ))[1]
        carry = step(carry, qs, q0 + TQ - w, keep=dkeep_ref[...], bias=dbias_ref[...])
        for t in range(mwin - 1, 0, -1):
            carry = step(carry, qs, (pre + (t - 1) * g) * TQ)
        span = _span_mask(w, pre * TQ)
        return step(carry, qs, 0, keep=jnp.where(span, 1.0, 0.0), bias=jnp.where(span, 0.0, NEG_INF))[1]

    def emit(qi, acc):
        o_ref[0, _rows(qi, TQ), :] = _merge_heads(acc[:TQ], acc[TQ:]).astype(o_ref.dtype)

    _for_each_query_block(lp // TQ, w, run, emit)


def _sb_attn(sb):
    b, lp, _ = sb.shape
    nb = SB_W // LANES
    return _attn_call(_sb_attn_kernel, "sb_attn", b, lp, nb,
                      [_col_spec(lp, 0), _col_spec(lp, nb), _col_spec(lp, 2 * nb)],
                      (sb, sb, sb), n_masks=2)


def _silu(x):
    return x / (1.0 + jnp.exp(-x))


def _ffn_kernel(h_ref, g_ref, wg_ref, wu_ref, wd_ref, o_ref, hn_ref, acc_ref):
    f = pl.program_id(1)

    @pl.when(f == 0)
    def _():
        hn_ref[...] = _rms(h_ref[...], g_ref[...]).astype(BF16)
        acc_ref[...] = jnp.zeros_like(acc_ref)

    hn = hn_ref[...]
    a = _silu(_dot(hn, wg_ref[...])) * _dot(hn, wu_ref[...])
    acc_ref[...] += _dot(a.astype(BF16), wd_ref[...])

    @pl.when(f == pl.num_programs(1) - 1)
    def _():
        o_ref[...] = h_ref[...] + acc_ref[...]


def _ffn(h, g, wg, wu, wd):
    t, d = h.shape
    ff = wg.shape[1]
    tm = _pick(t, (512, 384, 256, 128))
    tf = _pick(ff, (1408, 512, 256, 128))
    return pl.pallas_call(
        _ffn_kernel,
        grid=(t // tm, ff // tf),
        in_specs=[pl.BlockSpec((tm, d), lambda i, f: (i, 0)),
                  pl.BlockSpec((1, d), lambda i, f: (0, 0)),
                  pl.BlockSpec((d, tf), lambda i, f: (0, f)),
                  pl.BlockSpec((d, tf), lambda i, f: (0, f)),
                  pl.BlockSpec((tf, d), lambda i, f: (f, 0))],
        out_specs=pl.BlockSpec((tm, d), lambda i, f: (i, 0)),
        out_shape=jax.ShapeDtypeStruct((t, d), F32),
        scratch_shapes=[pltpu.VMEM((tm, d), BF16), pltpu.VMEM((tm, d), F32)],
        compiler_params=_cparams(("parallel", "arbitrary")),
        name="ffn",
    )(h, g, wg, wu, wd)


def _router_gates(hn, wr_ref):
    a, b, c = _split3(hn)
    wa, wb, wc = wr_ref[0], wr_ref[1], wr_ref[2]
    logits = (_dot(a, wa) + _dot(a, wb) + _dot(b, wa) + _dot(a, wc) + _dot(b, wb) + _dot(c, wa))
    lane = lax.broadcasted_iota(jnp.int32, logits.shape, 1).astype(F32)
    logits = jnp.where(lane < N_EXPERTS, logits, -jnp.inf)
    v1 = jnp.max(logits, axis=-1, keepdims=True)
    i1 = jnp.min(jnp.where(logits == v1, lane, float(LANES)), axis=-1, keepdims=True)
    rest = jnp.where(lane == i1, -jnp.inf, logits)
    v2 = jnp.max(rest, axis=-1, keepdims=True)
    i2 = jnp.min(jnp.where(rest == v2, lane, float(LANES)), axis=-1, keepdims=True)
    e2 = jnp.exp(v2 - v1)
    den = 1.0 + e2
    gate = jnp.where(lane == i1, 1.0 / den, 0.0) + jnp.where(lane == i2, e2 / den, 0.0)
    sel = jnp.where((lane == i1) | (lane == i2), 1.0, 0.0)
    return gate, sel


MOE_TM = 512
MOE_CHUNK = 128
MOE_ALIGN = 16
MOE_ROWS = 1024
SEL_LANE = N_EXPERTS
RANK_LANE = 2 * N_EXPERTS
T_ROWS = 16


def _route_kernel(h_ref, g_ref, wr_ref, hn_ref, route_ref, selt_ref, rankt_ref, cnt_ref):
    tm = h_ref.shape[0]
    hn = _rms(h_ref[...], g_ref[...])
    hn_ref[...] = hn.astype(BF16)
    gate, sel = _router_gates(hn, wr_ref)
    selb = sel.astype(BF16)
    r = lax.broadcasted_iota(jnp.int32, (tm, tm), 0)
    c = lax.broadcasted_iota(jnp.int32, (tm, tm), 1)
    rank = _dot(jnp.where(c < r, 1.0, 0.0).astype(BF16), selb)
    route_ref[...] = gate + pltpu.roll(sel, SEL_LANE, 1) + pltpu.roll(rank, RANK_LANE, 1)
    er = lax.broadcasted_iota(jnp.int32, (T_ROWS, LANES), 0)
    el = lax.broadcasted_iota(jnp.int32, (T_ROWS, LANES), 1)
    pick = jnp.where((er == el) & (er < N_EXPERTS), 1.0, 0.0).astype(BF16)
    selt = _dot_nt(pick, selb)
    selt_ref[0] = selt
    rankt_ref[0] = _dot(selt.astype(BF16), jnp.where(r < c, 1.0, 0.0).astype(BF16))
    cnt_ref[0] = jnp.sum(sel, axis=0, keepdims=True)


def _moe_route(h, g, wr3):
    t, d = h.shape
    tm = _pick(t, (MOE_TM, 384, 256, 128))
    nt = t // tm
    return pl.pallas_call(
        _route_kernel,
        grid=(nt,),
        in_specs=[pl.BlockSpec((tm, d), lambda i: (i, 0)),
                  pl.BlockSpec((1, d), lambda i: (0, 0)),
                  pl.BlockSpec((3, d, LANES), lambda i: (0, 0, 0))],
        out_specs=[pl.BlockSpec((tm, d), lambda i: (i, 0)),
                   pl.BlockSpec((tm, LANES), lambda i: (i, 0)),
                   pl.BlockSpec((1, T_ROWS, tm), lambda i: (i, 0, 0)),
                   pl.BlockSpec((1, T_ROWS, tm), lambda i: (i, 0, 0)),
                   pl.BlockSpec((1, 1, LANES), lambda i: (i, 0, 0))],
        out_shape=[jax.ShapeDtypeStruct((t, d), BF16),
                   jax.ShapeDtypeStruct((t, LANES), F32),
                   jax.ShapeDtypeStruct((nt, T_ROWS, tm), F32),
                   jax.ShapeDtypeStruct((nt, T_ROWS, tm), F32),
                   jax.ShapeDtypeStruct((nt, 1, LANES), F32)],
        compiler_params=_cparams(("parallel",)),
        name="moe_route",
    )(h, g, wr3)


def _moe_plan(cnt, n_row_tiles):
    cnt_al = (cnt + MOE_ALIGN - 1) // MOE_ALIGN * MOE_ALIGN
    tot = jnp.sum(cnt_al, axis=0)
    cap = (tot + MOE_CHUNK + MOE_ROWS - 1) // MOE_ROWS * MOE_ROWS
    base = jnp.cumsum(cap) - cap
    offs = base[None, :] + jnp.cumsum(cnt_al, axis=0) - cnt_al
    starts = jnp.arange(n_row_tiles, dtype=jnp.int32) * MOE_ROWS
    tile_e = jnp.clip(jnp.sum(starts[:, None] >= base[None, :], axis=1) - 1, 0, N_EXPERTS - 1)
    tile_ok = starts < (base + tot)[tile_e]
    return (offs.reshape(-1).astype(jnp.int32), tile_e.astype(jnp.int32), tile_ok.astype(jnp.int32))


def _chunk_copies(n_chunks):
    return [(e, c) for e in range(N_EXPERTS) for c in range(n_chunks)]


def _dispatch_kernel(offs_ref, cnts_ref, hn_ref, selt_ref, rankt_ref, xs_in_ref, xs_ref, stage, sems):
    del xs_in_ref
    i = pl.program_id(0)
    tm = hn_ref.shape[0]
    hn = hn_ref[...]
    ridx = lax.broadcasted_iota(jnp.int32, (MOE_CHUNK, tm), 0).astype(F32)

    def copy(e, c):
        slot = e * (tm // MOE_CHUNK) + c
        row0 = pl.multiple_of(offs_ref[i * N_EXPERTS + e] + c * MOE_CHUNK, MOE_ALIGN)
        return pltpu.make_async_copy(stage.at[slot], xs_ref.at[pl.ds(row0, MOE_CHUNK)], sems.at[slot])

    for e, c in _chunk_copies(tm // MOE_CHUNK):
        @pl.when(c * MOE_CHUNK < cnts_ref[i * N_EXPERTS + e])
        def _(e=e, c=c):
            hit = (rankt_ref[0, e:e + 1, :] - float(c * MOE_CHUNK) == ridx) & (selt_ref[0, e:e + 1, :] > 0.5)
            rows = _dot(jnp.where(hit, 1.0, 0.0).astype(BF16), hn)
            stage[e * (tm // MOE_CHUNK) + c] = rows.astype(BF16)
            copy(e, c).start()

    for e, c in _chunk_copies(tm // MOE_CHUNK):
        @pl.when(c * MOE_CHUNK < cnts_ref[i * N_EXPERTS + e])
        def _(e=e, c=c):
            copy(e, c).wait()


def _moe_dispatch(offs, cnts, hn, selt, rankt, n_rows):
    t, d = hn.shape
    nt, _, tm = selt.shape
    n_slots = N_EXPERTS * (tm // MOE_CHUNK)
    xs0 = jnp.zeros((n_rows, d), BF16)
    grid_spec = pltpu.PrefetchScalarGridSpec(
        num_scalar_prefetch=2,
        grid=(nt,),
        in_specs=[pl.BlockSpec((tm, d), lambda i, *_: (i, 0)),
                  pl.BlockSpec((1, T_ROWS, tm), lambda i, *_: (i, 0, 0)),
                  pl.BlockSpec((1, T_ROWS, tm), lambda i, *_: (i, 0, 0)),
                  pl.BlockSpec(memory_space=pl.ANY)],
        out_specs=pl.BlockSpec(memory_space=pl.ANY),
        scratch_shapes=[pltpu.VMEM((n_slots, MOE_CHUNK, d), BF16), pltpu.SemaphoreType.DMA((n_slots,))],
    )
    return pl.pallas_call(
        _dispatch_kernel,
        grid_spec=grid_spec,
        out_shape=jax.ShapeDtypeStruct((n_rows, d), BF16),
        input_output_aliases={5: 0},
        compiler_params=_cparams(("arbitrary",)),
        name="moe_dispatch",
    )(offs, cnts, hn, selt, rankt, xs0)


def _expert_kernel(te_ref, ok_ref, xs_ref, wg_ref, wu_ref, wd_ref, ys_ref, acc_ref):
    s = pl.program_id(0)
    f = pl.program_id(1)
    ok = ok_ref[s] > 0

    @pl.when(ok & (f == 0))
    def _():
        acc_ref[...] = jnp.zeros_like(acc_ref)

    @pl.when(ok)
    def _():
        x = xs_ref[...]
        a = _silu(_dot(x, wg_ref[0])) * _dot(x, wu_ref[0])
        acc_ref[...] += _dot(a.astype(BF16), wd_ref[0])

    last = f == pl.num_programs(1) - 1

    @pl.when(ok & last)
    def _():
        ys_ref[...] = acc_ref[...].astype(ys_ref.dtype)

    @pl.when(jnp.logical_not(ok) & last)
    def _():
        ys_ref[...] = jnp.zeros_like(ys_ref)


def _moe_experts(tile_e, tile_ok, xs, wg, wu, wd):
    n_rows, d = xs.shape
    ff = wg.shape[2]
    tf = _pick(ff, (512, 256, 128))
    nf = ff // tf

    def f_idx(s, f, ok_ref):
        return jnp.where(ok_ref[s] > 0, f, nf - 1)

    grid_spec = pltpu.PrefetchScalarGridSpec(
        num_scalar_prefetch=2,
        grid=(n_rows // MOE_ROWS, nf),
        in_specs=[pl.BlockSpec((MOE_ROWS, d), lambda s, f, te, ok: (s, 0)),
                  pl.BlockSpec((1, d, tf), lambda s, f, te, ok: (te[s], 0, f_idx(s, f, ok))),
                  pl.BlockSpec((1, d, tf), lambda s, f, te, ok: (te[s], 0, f_idx(s, f, ok))),
                  pl.BlockSpec((1, tf, d), lambda s, f, te, ok: (te[s], f_idx(s, f, ok), 0))],
        out_specs=pl.BlockSpec((MOE_ROWS, d), lambda s, f, te, ok: (s, 0)),
        scratch_shapes=[pltpu.VMEM((MOE_ROWS, d), F32)],
    )
    return pl.pallas_call(
        _expert_kernel,
        grid_spec=grid_spec,
        out_shape=jax.ShapeDtypeStruct((n_rows, d), BF16),
        compiler_params=_cparams(("parallel", "arbitrary")),
        name="moe_experts",
    )(tile_e, tile_ok, xs, wg, wu, wd)


def _combine_kernel(offs_ref, cnts_ref, h_ref, route_ref, ys_ref, o_ref, stage, sems):
    i = pl.program_id(0)
    tm = h_ref.shape[0]
    n_chunks = tm // MOE_CHUNK
    ridx = lax.broadcasted_iota(jnp.int32, (tm, MOE_CHUNK), 1).astype(F32)

    def copy(e, c):
        slot = e * n_chunks + c
        row0 = pl.multiple_of(offs_ref[i * N_EXPERTS + e] + c * MOE_CHUNK, MOE_ALIGN)
        return pltpu.make_async_copy(ys_ref.at[pl.ds(row0, MOE_CHUNK)], stage.at[slot], sems.at[slot])

    for e, c in _chunk_copies(n_chunks):
        @pl.when(c * MOE_CHUNK < cnts_ref[i * N_EXPERTS + e])
        def _(e=e, c=c):
            copy(e, c).start()

    o_ref[...] = h_ref[...]
    for e, c in _chunk_copies(n_chunks):
        @pl.when(c * MOE_CHUNK < cnts_ref[i * N_EXPERTS + e])
        def _(e=e, c=c):
            copy(e, c).wait()
            gate = route_ref[:, e:e + 1]
            sel = route_ref[:, SEL_LANE + e:SEL_LANE + e + 1]
            rank = route_ref[:, RANK_LANE + e:RANK_LANE + e + 1]
            hit = (rank - float(c * MOE_CHUNK) == ridx) & (sel > 0.5)
            back = _dot(jnp.where(hit, 1.0, 0.0).astype(BF16), stage[e * n_chunks + c])
            o_ref[...] += gate * back


def _moe_combine(offs, cnts, h, route, ys):
    t, d = h.shape
    tm = _pick(t, (MOE_TM, 384, 256, 128))
    n_slots = N_EXPERTS * (tm // MOE_CHUNK)
    grid_spec = pltpu.PrefetchScalarGridSpec(
        num_scalar_prefetch=2,
        grid=(t // tm,),
        in_specs=[pl.BlockSpec((tm, d), lambda i, *_: (i, 0)),
                  pl.BlockSpec((tm, LANES), lambda i, *_: (i, 0)),
                  pl.BlockSpec(memory_space=pl.ANY)],
        out_specs=pl.BlockSpec((tm, d), lambda i, *_: (i, 0)),
        scratch_shapes=[pltpu.VMEM((n_slots, MOE_CHUNK, d), BF16), pltpu.SemaphoreType.DMA((n_slots,))],
    )
    return pl.pallas_call(
        _combine_kernel,
        grid_spec=grid_spec,
        out_shape=jax.ShapeDtypeStruct((t, d), F32),
        compiler_params=_cparams(("arbitrary",)),
        name="moe_combine",
    )(offs, cnts, h, route, ys)


def _moe(h, g, wr3, wg, wu, wd):
    t, d = h.shape
    hn, route, selt, rankt, cnt = _moe_route(h, g, wr3)
    nt = cnt.shape[0]
    cnt = cnt[:, 0, :N_EXPERTS].astype(jnp.int32)
    worst = TOP_K * t + nt * N_EXPERTS * (MOE_ALIGN - 1) + N_EXPERTS * (MOE_CHUNK + MOE_ROWS - 1)
    n_row_tiles = -(-worst // MOE_ROWS)
    offs, tile_e, tile_ok = _moe_plan(cnt, n_row_tiles)
    cnts = cnt.reshape(-1)
    xs = _moe_dispatch(offs, cnts, hn, selt, rankt, n_row_tiles * MOE_ROWS)
    ys = _moe_experts(tile_e, tile_ok, xs, wg, wu, wd)
    return _moe_combine(offs, cnts, h, route, ys)


def _final_norm_kernel(h_ref, g_ref, o_ref):
    o_ref[0] = _rms(h_ref[0], g_ref[...])


def _final_norm(h3, g, seq):
    b, lp, d = h3.shape
    tm = PAD
    return pl.pallas_call(
        _final_norm_kernel,
        grid=(b, seq // tm),
        in_specs=[pl.BlockSpec((1, tm, d), lambda bi, i: (bi, i + PAD // tm, 0)),
                  pl.BlockSpec((1, d), lambda bi, i: (0, 0))],
        out_specs=pl.BlockSpec((1, tm, d), lambda bi, i: (bi, i, 0)),
        out_shape=jax.ShapeDtypeStruct((b, seq, d), F32),
        compiler_params=_cparams(("parallel", "parallel")),
        name="final_norm",
    )(h3, g)


def _rope_table(lp, rot_dim, theta, width, offset, scale):
    half = rot_dim // 2
    pos = (jnp.arange(lp, dtype=jnp.int32) - FIRST).astype(F32)
    inv = theta ** (-jnp.arange(half, dtype=F32) * 2.0 / rot_dim)
    ang = pos[:, None] * inv[None, :]
    cos, sin = jnp.cos(ang), jnp.sin(ang)
    ones = jnp.ones((lp, 1), F32)
    zeros = jnp.zeros((lp, 1), F32)

    def unit(first, second, fill):
        parts = [jnp.tile(fill, (1, offset)), first, second,
                 jnp.tile(fill, (1, width - offset - rot_dim))]
        return jnp.tile(jnp.concatenate(parts, axis=1), (1, LANES // width))

    c = unit(cos, cos, ones)
    s1 = unit(-sin, jnp.zeros_like(sin), zeros)
    s2 = unit(jnp.zeros_like(sin), sin, zeros)
    return jnp.concatenate([c, s1, s2], axis=1) * scale


def _pad_heads(w, heads, used, width):
    k = w.shape[0]
    w = w.reshape(k, heads, used)
    return jnp.pad(w, ((0, 0), (0, 0), (0, width - used))).reshape(k, heads * width)


def kernel(x, meta_tokens, ln_mix_e, w_in_e, ln_mla_q_e, w_mla_uq_e, ln_mla_kv_e, w_mla_ukv_e,
           w_out_e, ln_ffn_e, w_ffn_gate_e, w_ffn_up_e, w_ffn_down_e, ln_mix_o, w_in_o, b_forget_o,
           diff_lq1_o, diff_lk1_o, diff_lq2_o, diff_lk2_o, diff_subln_o, w_out_o, ln_ffn_o,
           w_router_o, w_moe_gate_o, w_moe_up_o, w_moe_down_o, ln_final):
    b, seq, d = x.shape
    assert d == D_MODEL and seq % TQ == 0
    lp = PAD + seq
    t = b * lp
    depth = 2

    meta = jnp.broadcast_to(meta_tokens.astype(x.dtype)[None], (b, N_META, d))
    h = jnp.concatenate([jnp.zeros((b, FIRST, d), x.dtype), meta, x], axis=1).reshape(t, d)

    row = lambda v: v.reshape(1, -1).astype(F32)
    sb_scale = HEAD_DIM ** -0.5
    tab_mla_q = _rope_table(lp, MLA_ROPE, MLA_THETA, LANES, MLA_NOPE,
                            (MLA_NOPE + MLA_ROPE) ** -0.5 * LOG2E)
    tab_mla_k = _rope_table(lp, MLA_ROPE, MLA_THETA, LANES, MLA_NOPE, 1.0)
    tab_diff_q = _rope_table(lp, ROT_DIM, ROPE_THETA, DIFF_DIM, 0, LOG2E)
    tab_diff_k = _rope_table(lp, ROT_DIM, ROPE_THETA, DIFF_DIM, 0, 1.0)

    for i in range(depth):
        j = i // 2
        if i % 2 == 0:
            w_in = w_in_e[j]
            lat_w = MLA_Q_RANK + MLA_KV_RANK + MLA_ROPE
            w_cat = jnp.concatenate(
                [w_in[:, :SB_W] * sb_scale, w_in[:, SB_W:3 * SB_W + lat_w],
                 jnp.zeros((d, 4 * LANES - lat_w), F32)], axis=1).astype(BF16)
            sb, lat = _even_proj(h, row(ln_mix_e[j]), w_cat)
            sb = sb.reshape(b, lp, -1)
            lat = lat.reshape(b, lp, -1)

            wq = _pad_heads(w_mla_uq_e[j], MLA_HEADS, MLA_NOPE + MLA_ROPE, LANES).astype(BF16)
            ukv = w_mla_ukv_e[j].reshape(MLA_KV_RANK, MLA_HEADS, MLA_NOPE + MLA_V)
            wk_nope = _pad_heads(ukv[:, :, :MLA_NOPE].reshape(MLA_KV_RANK, -1), MLA_HEADS, MLA_NOPE, LANES)
            place = jnp.pad(jnp.eye(MLA_ROPE, dtype=F32), ((0, 0), (MLA_NOPE, LANES - MLA_NOPE - MLA_ROPE)))
            wk_rope = jnp.tile(place, (1, MLA_HEADS))
            wk = jnp.concatenate(
                [wk_nope, wk_rope, jnp.zeros((LANES - MLA_ROPE, MLA_HEADS * LANES), F32)], axis=0).astype(BF16)
            wv = ukv[:, :, MLA_NOPE:].reshape(MLA_KV_RANK, -1).astype(BF16)
            mq, mk, mv = _mla_up(lat, row(ln_mla_q_e[j]), row(ln_mla_kv_e[j]), wq, wk, wv,
                                 tab_mla_q, tab_mla_k)

            a_out = _sb_attn(sb).reshape(t, -1)
            b_out = _mla_attn(mq, mk, mv).reshape(t, -1)
            w_out = w_out_e[j].astype(BF16)
            h = _out_proj(h, a_out, b_out, w_out[:SB_W], w_out[SB_W:])
            h = _ffn(h, row(ln_ffn_e[j]), w_ffn_gate_e[j].astype(BF16), w_ffn_up_e[j].astype(BF16),
                     w_ffn_down_e[j].astype(BF16))
        else:
            lam_init = 0.8 - 0.6 * math.exp(-0.3 * i)
            w_in = w_in_o[j]
            wf = w_in[:, N_ODD_MAIN:]
            w_cat = jnp.concatenate(
                [w_in[:, :DIFF_W] * sb_scale, w_in[:, DIFF_W:3 * DIFF_W],
                 w_in[:, 3 * DIFF_W:3 * DIFF_W + FOX_W] * sb_scale, w_in[:, 3 * DIFF_W + FOX_W:N_ODD_MAIN],
                 wf, jnp.zeros((d, LANES - FOX_HEADS), F32)], axis=1).astype(BF16)
            wft = jnp.pad(wf.T, ((0, F_ROWS - FOX_HEADS), (0, 0))).astype(BF16)
            bf = b_forget_o[j].astype(F32)
            bcol = jnp.pad(bf, (0, LANES - FOX_HEADS)).reshape(1, LANES)
            brow = jnp.pad(bf, (0, F_ROWS - FOX_HEADS)).reshape(F_ROWS, 1)
            main, fcol, frow = _odd_proj(h.reshape(b, lp, d), row(ln_mix_o[j]), w_cat, wft, bcol, brow,
                                         tab_diff_q, tab_diff_k)
            ccol, crow = _fox_cumsum(fcol, frow)
            crow4 = crow[:, :FOX_HEADS].reshape(b, FOX_HEADS // 2, 2, lp)
            lam_vecs = jnp.stack([diff_lq1_o[j], diff_lk1_o[j], diff_lq2_o[j], diff_lk2_o[j]]).astype(F32)
            c_out = _diff_attn(main, lam_vecs, row(diff_subln_o[j]), lam_init).reshape(t, -1)
            d_out = _fox_attn(main, ccol, crow4).reshape(t, -1)
            w_out = w_out_o[j].astype(BF16)
            h = _out_proj(h, c_out, d_out, w_out[:DIFF_W], w_out[DIFF_W:])
            wr = jnp.pad(w_router_o[j].astype(F32), ((0, 0), (0, LANES - N_EXPERTS)))
            wr3 = jnp.stack(_split3(wr))
            h = _moe(h, row(ln_ffn_o[j]), wr3, w_moe_gate_o[j].astype(BF16), w_moe_up_o[j].astype(BF16),
                     w_moe_down_o[j].astype(BF16))

    return _final_norm(h.reshape(b, lp, d), row(ln_final), seq)
```

```python
import functools
import math

import jax
import jax.numpy as jnp
from jax import lax
from jax.experimental import pallas as pl
from jax.experimental.pallas import tpu as pltpu

F32 = jnp.float32
BF16 = jnp.bfloat16

D_MODEL = 1024
CHUNK = 64
N_META = 16
EPS = 1e-6
LOG2E = 1.4426950408889634
NEG_INF = -1e30

HEAD_DIM = 64
SB_HEADS = 8
MLA_HEADS = 8
MLA_NOPE = 64
MLA_ROPE = 32
MLA_V = 64
MLA_Q_RANK = 256
MLA_KV_RANK = 128
MLA_THETA = 10000.0
DIFF_HEADS = 4
DIFF_DIM = 64
FOX_HEADS = 8
ROPE_THETA = 500000.0
ROT_DIM = HEAD_DIM // 4
N_EXPERTS = 8
TOP_K = 2

SB_W = SB_HEADS * HEAD_DIM
DIFF_W = DIFF_HEADS * 2 * DIFF_DIM
FOX_W = FOX_HEADS * HEAD_DIM

LANES = 128
PAD = 128
FIRST = PAD - N_META
TQ = 128
KEY_WINDOW = 512
Q_GROUP = 4
SB_Q_GROUP = 2
F32_EXP2_UNDERFLOW = 160.0
SUFFIX_CHUNK = 256
CHUNK_SHIFT = CHUNK.bit_length() - 1
VMEM_LIMIT = 56 * 1024 * 1024


def _cparams(sem):
    return pltpu.CompilerParams(dimension_semantics=sem, vmem_limit_bytes=VMEM_LIMIT)


def _pick(n, cands):
    for c in cands:
        if n % c == 0:
            return c
    raise ValueError(f"no tile in {cands} divides {n}")


def _rms(x, g):
    return x * lax.rsqrt(jnp.mean(x * x, axis=-1, keepdims=True) + EPS) * g


def _dot(a, b):
    return jnp.dot(a, b, preferred_element_type=F32)


def _dot_nt(a, b):
    return lax.dot_general(a, b, (((1,), (1,)), ((), ())), preferred_element_type=F32)


def _rope_chunk(y, tab, shift):
    c = tab[:, 0:LANES]
    s1 = tab[:, LANES:2 * LANES]
    s2 = tab[:, 2 * LANES:3 * LANES]
    return (y * c + pltpu.roll(y, LANES - shift, 1) * s1 + pltpu.roll(y, shift, 1) * s2)


def _split3(x):
    a = x.astype(BF16)
    r = x - a.astype(F32)
    b = r.astype(BF16)
    c = (r - b.astype(F32)).astype(BF16)
    return a, b, c


def _even_proj_kernel(h_ref, g_ref, w_ref, sb_ref, lat_ref):
    hn = _rms(h_ref[...], g_ref[...]).astype(BF16)
    y = _dot(hn, w_ref[...])
    n_sb = sb_ref.shape[-1]
    sb_ref[:, :SB_W] = (y[:, :SB_W] * LOG2E).astype(BF16)
    sb_ref[:, SB_W:] = y[:, SB_W:n_sb].astype(BF16)
    lat_ref[...] = y[:, n_sb:].astype(BF16)


def _even_proj(h, g, w):
    t, d = h.shape
    n = w.shape[1]
    n_sb = 3 * SB_W
    tm = _pick(t, (512, 384, 256, 128))
    return pl.pallas_call(
        _even_proj_kernel,
        grid=(t // tm,),
        in_specs=[pl.BlockSpec((tm, d), lambda i: (i, 0)),
                  pl.BlockSpec((1, d), lambda i: (0, 0)),
                  pl.BlockSpec((d, n), lambda i: (0, 0))],
        out_specs=[pl.BlockSpec((tm, n_sb), lambda i: (i, 0)),
                   pl.BlockSpec((tm, n - n_sb), lambda i: (i, 0))],
        out_shape=[jax.ShapeDtypeStruct((t, n_sb), BF16),
                   jax.ShapeDtypeStruct((t, n - n_sb), BF16)],
        compiler_params=_cparams(("parallel",)),
        name="even_proj",
    )(h, g, w)


def _mla_up_kernel(lat_ref, gq_ref, gkv_ref, wq_ref, wk_ref, wv_ref, tq_ref, tk_ref,
                   q_ref, k_ref, v_ref):
    lat = lat_ref[0]
    nq = _rms(lat[:, :MLA_Q_RANK].astype(F32), gq_ref[...]).astype(BF16)
    nkv = _rms(lat[:, MLA_Q_RANK:MLA_Q_RANK + MLA_KV_RANK].astype(F32), gkv_ref[...]).astype(BF16)
    yq = _dot(nq, wq_ref[...])
    xk = jnp.concatenate([nkv, lat[:, MLA_Q_RANK + MLA_KV_RANK:]], axis=1)
    yk = _dot(xk, wk_ref[...])
    v_ref[0] = _dot(nkv, wv_ref[...]).astype(BF16)
    tabq = tq_ref[...]
    tabk = tk_ref[...]
    for hd in range(MLA_HEADS):
        sl = slice(hd * LANES, (hd + 1) * LANES)
        q_ref[0, :, sl] = _rope_chunk(yq[:, sl], tabq, MLA_ROPE // 2).astype(BF16)
        k_ref[0, :, sl] = _rope_chunk(yk[:, sl], tabk, MLA_ROPE // 2).astype(BF16)


def _mla_up(lat, gq, gkv, wq, wk, wv, tabq, tabk):
    b, lp, nl = lat.shape
    tm = _pick(lp, (384, 128))
    nqk = MLA_HEADS * LANES
    nv = MLA_HEADS * MLA_V
    full = lambda shp: pl.BlockSpec(shp, lambda bi, i: (0,) * len(shp))
    return pl.pallas_call(
        _mla_up_kernel,
        grid=(b, lp // tm),
        in_specs=[pl.BlockSpec((1, tm, nl), lambda bi, i: (bi, i, 0)),
                  full(gq.shape), full(gkv.shape), full(wq.shape), full(wk.shape), full(wv.shape),
                  pl.BlockSpec((tm, 3 * LANES), lambda bi, i: (i, 0)),
                  pl.BlockSpec((tm, 3 * LANES), lambda bi, i: (i, 0))],
        out_specs=[pl.BlockSpec((1, tm, nqk), lambda bi, i: (bi, i, 0)),
                   pl.BlockSpec((1, tm, nqk), lambda bi, i: (bi, i, 0)),
                   pl.BlockSpec((1, tm, nv), lambda bi, i: (bi, i, 0))],
        out_shape=[jax.ShapeDtypeStruct((b, lp, nqk), BF16),
                   jax.ShapeDtypeStruct((b, lp, nqk), BF16),
                   jax.ShapeDtypeStruct((b, lp, nv), BF16)],
        compiler_params=_cparams(("parallel", "parallel")),
        name="mla_up",
    )(lat, gq, gkv, wq, wk, wv, tabq, tabk)


N_ODD_MAIN = 3 * DIFF_W + 3 * FOX_W
F_ROWS = 16


def _odd_proj_kernel(h_ref, g_ref, w_ref, wft_ref, bcol_ref, brow_ref, tabq_ref, tabk_ref,
                     main_ref, fcol_ref, frow_ref):
    hn = _rms(h_ref[0], g_ref[...]).astype(BF16)
    y = _dot(hn, w_ref[...])
    n_rope = DIFF_W // LANES
    for c in range(2 * n_rope):
        sl = slice(c * LANES, (c + 1) * LANES)
        tab = tabq_ref[...] if c < n_rope else tabk_ref[...]
        main_ref[0, :, sl] = _rope_chunk(y[:, sl], tab, ROT_DIM // 2).astype(BF16)
    fq = slice(3 * DIFF_W, 3 * DIFF_W + FOX_W)
    main_ref[0, :, 2 * DIFF_W:3 * DIFF_W] = y[:, 2 * DIFF_W:3 * DIFF_W].astype(BF16)
    main_ref[0, :, fq] = (y[:, fq] * LOG2E).astype(BF16)
    main_ref[0, :, 3 * DIFF_W + FOX_W:] = y[:, 3 * DIFF_W + FOX_W:N_ODD_MAIN].astype(BF16)
    fcol_ref[0] = y[:, N_ODD_MAIN:] + bcol_ref[...]
    frow_ref[0] = _dot_nt(wft_ref[...], hn) + brow_ref[...]


def _odd_proj(h3, g, w, wft, bcol, brow, tabq, tabk):
    b, lp, d = h3.shape
    tm = _pick(lp, (384, 128))
    full = lambda shp: pl.BlockSpec(shp, lambda bi, i: (0,) * len(shp))
    return pl.pallas_call(
        _odd_proj_kernel,
        grid=(b, lp // tm),
        in_specs=[pl.BlockSpec((1, tm, d), lambda bi, i: (bi, i, 0)),
                  full(g.shape), full(w.shape), full(wft.shape), full(bcol.shape), full(brow.shape),
                  pl.BlockSpec((tm, 3 * LANES), lambda bi, i: (i, 0)),
                  pl.BlockSpec((tm, 3 * LANES), lambda bi, i: (i, 0))],
        out_specs=[pl.BlockSpec((1, tm, N_ODD_MAIN), lambda bi, i: (bi, i, 0)),
                   pl.BlockSpec((1, tm, LANES), lambda bi, i: (bi, i, 0)),
                   pl.BlockSpec((1, F_ROWS, tm), lambda bi, i: (bi, 0, i))],
        out_shape=[jax.ShapeDtypeStruct((b, lp, N_ODD_MAIN), BF16),
                   jax.ShapeDtypeStruct((b, lp, LANES), F32),
                   jax.ShapeDtypeStruct((b, F_ROWS, lp), F32)],
        compiler_params=_cparams(("parallel", "parallel")),
        name="odd_proj",
    )(h3, g, w, wft, bcol, brow, tabq, tabk)


def _out_proj_kernel(h_ref, a_ref, b_ref, wa_ref, wb_ref, o_ref):
    o_ref[...] = h_ref[...] + _dot(a_ref[...], wa_ref[...]) + _dot(b_ref[...], wb_ref[...])


def _out_proj(h, mix_a, mix_b, wa, wb):
    t, d = h.shape
    tm = _pick(t, (512, 384, 256, 128))
    na, nb = mix_a.shape[1], mix_b.shape[1]
    return pl.pallas_call(
        _out_proj_kernel,
        grid=(t // tm,),
        in_specs=[pl.BlockSpec((tm, d), lambda i: (i, 0)),
                  pl.BlockSpec((tm, na), lambda i: (i, 0)),
                  pl.BlockSpec((tm, nb), lambda i: (i, 0)),
                  pl.BlockSpec((na, d), lambda i: (0, 0)),
                  pl.BlockSpec((nb, d), lambda i: (0, 0))],
        out_specs=pl.BlockSpec((tm, d), lambda i: (i, 0)),
        out_shape=jax.ShapeDtypeStruct((t, d), F32),
        compiler_params=_cparams(("parallel",)),
        name="out_proj",
    )(h, mix_a, mix_b, wa, wb)


def _log_sigmoid(x):
    return jnp.minimum(x, 0.0) - jnp.log(1.0 + jnp.exp(-jnp.abs(x)))


def _fox_cumsum_kernel(fcol_ref, frow_ref, ccol_ref, crow_ref):
    lp = fcol_ref.shape[1]
    blk = LANES
    r = lax.broadcasted_iota(jnp.int32, (blk, blk), 0)
    c = lax.broadcasted_iota(jnp.int32, (blk, blk), 1)
    tri_l = (c <= r).astype(BF16)
    tri_u = (r <= c).astype(BF16)
    row_id = lax.broadcasted_iota(jnp.int32, (blk, LANES), 0)
    col_id = lax.broadcasted_iota(jnp.int32, (F_ROWS, blk), 1)
    carry_c = jnp.zeros((1, LANES), F32)
    carry_r = jnp.zeros((F_ROWS, 1), F32)
    for j in range(lp // blk):
        sl = slice(j * blk, (j + 1) * blk)
        ls = _log_sigmoid(fcol_ref[0, sl, :])
        ls = jnp.where(row_id + j * blk >= FIRST, ls, 0.0)
        a, b, c3 = _split3(ls)
        cs = _dot(tri_l, a) + _dot(tri_l, b) + _dot(tri_l, c3) + carry_c
        ccol_ref[0, sl, :] = cs * LOG2E
        carry_c = cs[blk - 1:blk, :]
        lr = _log_sigmoid(frow_ref[0, :, sl])
        lr = jnp.where(col_id + j * blk >= FIRST, lr, 0.0)
        a, b, c3 = _split3(lr)
        cr = _dot(a, tri_u) + _dot(b, tri_u) + _dot(c3, tri_u) + carry_r
        crow_ref[0, :, sl] = cr * LOG2E
        carry_r = cr[:, blk - 1:blk]


def _fox_cumsum(fcol, frow):
    b, lp, _ = fcol.shape
    return pl.pallas_call(
        _fox_cumsum_kernel,
        grid=(b,),
        in_specs=[pl.BlockSpec((1, lp, LANES), lambda bi: (bi, 0, 0)),
                  pl.BlockSpec((1, F_ROWS, lp), lambda bi: (bi, 0, 0))],
        out_specs=[pl.BlockSpec((1, lp, LANES), lambda bi: (bi, 0, 0)),
                   pl.BlockSpec((1, F_ROWS, lp), lambda bi: (bi, 0, 0))],
        out_shape=[jax.ShapeDtypeStruct((b, lp, LANES), F32),
                   jax.ShapeDtypeStruct((b, F_ROWS, lp), F32)],
        compiler_params=_cparams(("parallel",)),
        name="fox_cumsum",
    )(fcol, frow)


def _rows(i, n):
    if isinstance(i, int):
        return pl.ds(i * n, n)
    return pl.ds(pl.multiple_of(i * n, n), n)


def _keys(k0, w):
    if isinstance(k0, int):
        return pl.ds(k0, w)
    return pl.ds(pl.multiple_of(k0, TQ), w)


def _visible(kind, ks, qs):
    if kind == "chunk":
        return (ks >> CHUNK_SHIFT) <= (qs >> CHUNK_SHIFT)
    if kind == "causal":
        return ks <= qs
    return ks < qs


def _first_window_mask(kind, q0, w, lim):
    r = lax.broadcasted_iota(jnp.int32, (2 * TQ, w), 0)
    ks = lax.broadcasted_iota(jnp.int32, (2 * TQ, w), 1)
    return _visible(kind, ks, q0 + (r & (TQ - 1))) & (ks >= FIRST) & (ks < lim)


def _diag_window_mask(kind, w):
    r = lax.broadcasted_iota(jnp.int32, (2 * TQ, w), 0)
    ks = lax.broadcasted_iota(jnp.int32, (2 * TQ, w), 1) + (TQ - w)
    return _visible(kind, ks, r & (TQ - 1))


def _span_mask(w, lim):
    ks = lax.broadcasted_iota(jnp.int32, (1, w), 1)
    return (ks >= FIRST) & (ks < lim)


def _lane_halves():
    lane = lax.broadcasted_iota(jnp.int32, (TQ, LANES), 1)
    return lane < (LANES // 2)


def _stack_masked(q):
    lo = _lane_halves()
    z = jnp.zeros_like(q)
    return jnp.concatenate([jnp.where(lo, q, z), jnp.where(lo, z, q)], axis=0)


def _softmax_sweep(kind, w, prep, scores_fn, v_fn, dbias_ref):
    ones = jnp.ones((w, LANES), BF16)

    def step(carry, s, v):
        m, acc = carry
        m_new = jnp.maximum(m, jnp.max(s, axis=-1, keepdims=True))
        alpha = jnp.exp2(m - m_new)
        p = jnp.exp2(s - m_new)
        acc = alpha * acc + _dot(p.astype(BF16), jnp.concatenate([v, ones], axis=1))
        return m_new, acc

    def run(qi, mwin):
        g = w // TQ
        ctx = prep(qi)
        q0 = qi * TQ
        pre = qi + 1 - mwin * g
        carry = (jnp.full((2 * TQ, 1), NEG_INF, F32), jnp.zeros((2 * TQ, 2 * LANES), F32))
        s = scores_fn(ctx, 0)
        if mwin == 0:
            s = jnp.where(_first_window_mask(kind, q0, w, pre * TQ), s, NEG_INF)
            return step(carry, s, v_fn(0))[1]
        carry = step(carry, s + jnp.where(_span_mask(w, pre * TQ), 0.0, NEG_INF), v_fn(0))
        for t in range(1, mwin):
            k0 = (pre + (t - 1) * g) * TQ
            carry = step(carry, scores_fn(ctx, k0), v_fn(k0))
        k0 = q0 + TQ - w
        return step(carry, scores_fn(ctx, k0) + dbias_ref[...], v_fn(k0))[1]

    return run


def _for_each_query_group(nq, w, q_group, group_fn):
    g = w // TQ
    one = jnp.sign(pl.program_id(0) + 1)
    for mwin in range(-(-nq // g)):
        lo, hi = mwin * g, min((mwin + 1) * g, nq)

        def body(it, _, mwin=mwin, lo=lo):
            qi = lo + q_group * it
            group_fn([qi + j for j in range(q_group)], mwin)
            return 0

        n_groups = (hi - lo) // q_group
        if n_groups:
            lax.fori_loop(0, n_groups * one, body, 0)
        for qi in range(lo + n_groups * q_group, hi):
            group_fn([qi], mwin)


def _for_each_query_block(nq, w, run, emit):
    def group_fn(qis, mwin):
        results = [run(qi, mwin) for qi in qis]
        for qi, res in zip(qis, results):
            emit(qi, res)

    _for_each_query_group(nq, w, Q_GROUP, group_fn)


def _key_window(lp):
    return min(KEY_WINDOW, lp)


def _merge_heads(top, bottom):
    return jnp.where(_lane_halves(), top, bottom)


def _attn_call(kernel, name, b, lp, n_blocks, in_specs, args, n_masks=1):
    w = _key_window(lp)
    return pl.pallas_call(
        kernel,
        grid=(b, n_blocks),
        in_specs=in_specs,
        out_specs=pl.BlockSpec((1, lp, LANES), lambda bi, hp: (bi, 0, hp)),
        out_shape=jax.ShapeDtypeStruct((b, lp, n_blocks * LANES), BF16),
        scratch_shapes=[pltpu.VMEM((2 * TQ, w), F32)] * n_masks,
        compiler_params=_cparams(("parallel", "parallel")),
        name=name,
    )(*args)


def _normalised(acc):
    return acc[:, :LANES] / acc[:, LANES:]


def _col_spec(lp, off):
    return pl.BlockSpec((1, lp, LANES), lambda bi, hp: (bi, 0, off + hp))


def _mla_attn_kernel(q_ref, k_ref, v_ref, o_ref, dbias_ref):
    lp = q_ref.shape[1]
    w = _key_window(lp)
    dbias_ref[...] = jnp.where(_diag_window_mask("chunk", w), 0.0, NEG_INF)

    def prep(qi):
        return q_ref[0, _rows(qi, TQ), 0:LANES], q_ref[0, _rows(qi, TQ), LANES:2 * LANES]

    def scores(ctx, k0):
        qa, qb = ctx
        ka = k_ref[0, _keys(k0, w), 0:LANES]
        kb = k_ref[0, _keys(k0, w), LANES:2 * LANES]
        return jnp.concatenate([_dot_nt(qa, ka), _dot_nt(qb, kb)], axis=0)

    run = _softmax_sweep("chunk", w, prep, scores, lambda k0: v_ref[0, _keys(k0, w), :], dbias_ref)

    def emit(qi, acc):
        o = _normalised(acc)
        o_ref[0, _rows(qi, TQ), :] = _merge_heads(o[:TQ], o[TQ:]).astype(o_ref.dtype)

    _for_each_query_block(lp // TQ, w, run, emit)


def _mla_attn(q, k, v):
    b, lp, _ = q.shape
    wide = lambda: pl.BlockSpec((1, lp, 2 * LANES), lambda bi, hp: (bi, 0, hp))
    return _attn_call(_mla_attn_kernel, "mla_attn", b, lp, MLA_HEADS // 2,
                      [wide(), wide(), _col_spec(lp, 0)], (q, k, v))


def _diff_attn_kernel(lam_init, x_q, x_k, x_v, lam_ref, gsub_ref, o_ref, dbias_ref):
    lp = x_q.shape[1]
    w = _key_window(lp)
    dbias_ref[...] = jnp.where(_diag_window_mask("chunk", w), 0.0, NEG_INF)
    lv = lam_ref[...]
    lam = (jnp.exp(jnp.sum(lv[0:1] * lv[1:2], axis=-1, keepdims=True))
           - jnp.exp(jnp.sum(lv[2:3] * lv[3:4], axis=-1, keepdims=True)) + lam_init)
    gsub = gsub_ref[...]

    def prep(qi):
        return _stack_masked(x_q[0, _rows(qi, TQ), :])

    def scores(qs, k0):
        return _dot_nt(qs, x_k[0, _keys(k0, w), :])

    run = _softmax_sweep("chunk", w, prep, scores, lambda k0: x_v[0, _keys(k0, w), :], dbias_ref)

    def emit(qi, acc):
        o = _normalised(acc)
        o = o[:TQ] - lam * o[TQ:]
        o_ref[0, _rows(qi, TQ), :] = (_rms(o, gsub) * (1.0 - lam_init)).astype(o_ref.dtype)

    _for_each_query_block(lp // TQ, w, run, emit)


def _diff_attn(main, lam_vecs, gsub, lam_init):
    b, lp, _ = main.shape
    nb = DIFF_W // LANES
    full = lambda shp: pl.BlockSpec(shp, lambda bi, hp: (0,) * len(shp))
    return _attn_call(functools.partial(_diff_attn_kernel, lam_init), "diff_attn", b, lp, DIFF_HEADS,
                      [_col_spec(lp, 0), _col_spec(lp, nb), _col_spec(lp, 2 * nb),
                       full(lam_vecs.shape), full(gsub.shape)],
                      (main, main, main, lam_vecs, gsub))


def _fox_attn_kernel(x_q, x_k, x_v, ccol_ref, crow_ref, o_ref, dbias_ref):
    lp = x_q.shape[1]
    w = _key_window(lp)
    dbias_ref[...] = jnp.where(_diag_window_mask("causal", w), 0.0, NEG_INF)
    hp = pl.program_id(1)
    lane = lax.broadcasted_iota(jnp.int32, (TQ, LANES), 1)

    def prep(qi):
        cc = ccol_ref[0, _rows(qi, TQ), :]
        cq0 = jnp.sum(jnp.where(lane == 2 * hp, cc, 0.0), axis=-1, keepdims=True)
        cq1 = jnp.sum(jnp.where(lane == 2 * hp + 1, cc, 0.0), axis=-1, keepdims=True)
        return _stack_masked(x_q[0, _rows(qi, TQ), :]), cq0, cq1

    def scores(ctx, k0):
        qs, cq0, cq1 = ctx
        s = _dot_nt(qs, x_k[0, _keys(k0, w), :])
        ck = crow_ref[0, 0, :, _keys(k0, w)]
        return s + jnp.concatenate([cq0 - ck[0:1, :], cq1 - ck[1:2, :]], axis=0)

    run = _softmax_sweep("causal", w, prep, scores, lambda k0: x_v[0, _keys(k0, w), :], dbias_ref)

    def emit(qi, acc):
        o = _normalised(acc)
        o_ref[0, _rows(qi, TQ), :] = _merge_heads(o[:TQ], o[TQ:]).astype(o_ref.dtype)

    _for_each_query_block(lp // TQ, w, run, emit)


def _fox_attn(main, ccol, crow4):
    b, lp, _ = main.shape
    off = 3 * DIFF_W // LANES
    nb = FOX_W // LANES
    return _attn_call(_fox_attn_kernel, "fox_attn", b, lp, nb,
                      [_col_spec(lp, off), _col_spec(lp, off + nb), _col_spec(lp, off + 2 * nb),
                       pl.BlockSpec((1, lp, LANES), lambda bi, hp: (bi, 0, 0)),
                       pl.BlockSpec((1, 1, 2, lp), lambda bi, hp: (bi, hp, 0, 0))],
                      (main, main, main, ccol, crow4))


def _sb_attn_kernel(x_q, x_k, x_v, o_ref, dkeep_ref, dbias_ref):
    lp = x_q.shape[1]
    w = _key_window(lp)
    g = w // TQ
    cs = SUFFIX_CHUNK if w % SUFFIX_CHUNK == 0 else TQ
    kr = lax.broadcasted_iota(jnp.int32, (cs, cs), 0)
    kc = lax.broadcasted_iota(jnp.int32, (cs, cs), 1)
    later = jnp.where(kr > kc, 1.0, 0.0).astype(BF16)
    dvis = _diag_window_mask("strict", w)
    dkeep_ref[...] = jnp.where(dvis, 1.0, 0.0)
    dbias_ref[...] = jnp.where(dvis, 0.0, NEG_INF)

    def step(carry, qs, k0, keep=None, bias=None, vis=None):
        rsum, acc = carry
        z = _dot_nt(qs, x_k[0, _keys(k0, w), :])
        cost = jnp.maximum(z, 0.0) + jnp.log2(1.0 + jnp.exp2(-jnp.abs(z)))
        if vis is not None:
            cost = jnp.where(vis, cost, 0.0)
        if keep is not None:
            cost = cost * keep
        cb = cost.astype(BF16)
        parts = []
        for c in reversed(range(w // cs)):
            sl = slice(c * cs, (c + 1) * cs)
            inner = _dot(cb[:, sl], later)
            parts.append(inner + rsum)
            rsum = rsum + inner[:, 0:1] + cb[:, c * cs:c * cs + 1].astype(F32)
        arg = z - cost - jnp.concatenate(parts[::-1], axis=1)
        if bias is not None:
            arg = arg + bias
        wgt = jnp.exp2(arg)
        if vis is not None:
            wgt = jnp.where(vis, wgt, 0.0)
        acc = acc + _dot(wgt.astype(BF16), x_v[0, _keys(k0, w), :])
        return rsum, acc

    def knorm(j, best):
        kk = x_k[0, _rows(j, TQ), :].astype(F32)
        return jnp.maximum(best, jnp.sum(kk * kk, axis=-1, keepdims=True))

    kn2 = jnp.max(lax.fori_loop(0, lp // TQ, knorm, jnp.zeros((TQ, 1), F32)), axis=0, keepdims=True)

    def emit(qi, acc):
        o_ref[0, _rows(qi, TQ), :] = _merge_heads(acc[:TQ], acc[TQ:]).astype(o_ref.dtype)

    def group_fn(qis, mwin):
        qss = [_stack_masked(x_q[0, _rows(qi, TQ), :]) for qi in qis]
        pres = [qi + 1 - mwin * g for qi in qis]
        zero = (jnp.zeros((2 * TQ, 1), F32), jnp.zeros((2 * TQ, LANES), F32))
        if mwin == 0:
            for qi, qs, pre in zip(qis, qss, pres):
                emit(qi, step(zero, qs, 0, vis=_first_window_mask("strict", qi * TQ, w, pre * TQ))[1])
            return
        carries = [step(zero, qs, qi * TQ + TQ - w, keep=dkeep_ref[...], bias=dbias_ref[...])
                   for qi, qs in zip(qis, qss)]

        def rest(cs):
            out = []
            for qs, pre, carry in zip(qss, pres, cs):
                for t in range(mwin - 1, 0, -1):
                    carry = step(carry, qs, (pre + (t - 1) * g) * TQ)
                span = _span_mask(w, pre * TQ)
                out.append(step(carry, qs, 0, keep=jnp.where(span, 1.0, 0.0),
                                bias=jnp.where(span, 0.0, NEG_INF)))
            return tuple(out)

        slack = [jnp.max(jnp.sqrt(jnp.sum(jnp.square(qs.astype(F32)), axis=-1, keepdims=True) * kn2)
                         - carry[0]) for qs, carry in zip(qss, carries)]
        live = functools.reduce(jnp.maximum, slack) > -F32_EXP2_UNDERFLOW
        carries = lax.cond(live, rest, lambda cs: cs, tuple(carries))
        for qi, carry in zip(qis, carries):
            emit(qi, carry[1])

    _for_each_query_group(lp // TQ, w, SB_Q_GROUP, group_fn)


def _sb_attn(sb):
    b, lp, _ = sb.shape
    nb = SB_W // LANES
    return _attn_call(_sb_attn_kernel, "sb_attn", b, lp, nb,
                      [_col_spec(lp, 0), _col_spec(lp, nb), _col_spec(lp, 2 * nb)],
                      (sb, sb, sb), n_masks=2)


def _silu(x):
    return x / (1.0 + jnp.exp(-x))


def _ffn_kernel(h_ref, g_ref, wg_ref, wu_ref, wd_ref, o_ref, hn_ref, acc_ref):
    f = pl.program_id(1)

    @pl.when(f == 0)
    def _():
        hn_ref[...] = _rms(h_ref[...], g_ref[...]).astype(BF16)
        acc_ref[...] = jnp.zeros_like(acc_ref)

    hn = hn_ref[...]
    a = _silu(_dot(hn, wg_ref[...])) * _dot(hn, wu_ref[...])
    acc_ref[...] += _dot(a.astype(BF16), wd_ref[...])

    @pl.when(f == pl.num_programs(1) - 1)
    def _():
        o_ref[...] = h_ref[...] + acc_ref[...]


def _ffn(h, g, wg, wu, wd):
    t, d = h.shape
    ff = wg.shape[1]
    tm = _pick(t, (512, 384, 256, 128))
    tf = _pick(ff, (1408, 512, 256, 128))
    return pl.pallas_call(
        _ffn_kernel,
        grid=(t // tm, ff // tf),
        in_specs=[pl.BlockSpec((tm, d), lambda i, f: (i, 0)),
                  pl.BlockSpec((1, d), lambda i, f: (0, 0)),
                  pl.BlockSpec((d, tf), lambda i, f: (0, f)),
                  pl.BlockSpec((d, tf), lambda i, f: (0, f)),
                  pl.BlockSpec((tf, d), lambda i, f: (f, 0))],
        out_specs=pl.BlockSpec((tm, d), lambda i, f: (i, 0)),
        out_shape=jax.ShapeDtypeStruct((t, d), F32),
        scratch_shapes=[pltpu.VMEM((tm, d), BF16), pltpu.VMEM((tm, d), F32)],
        compiler_params=_cparams(("parallel", "arbitrary")),
        name="ffn",
    )(h, g, wg, wu, wd)


def _router_gates(hn, wr_ref):
    a, b, c = _split3(hn)
    wa, wb, wc = wr_ref[0], wr_ref[1], wr_ref[2]
    logits = (_dot(a, wa) + _dot(a, wb) + _dot(b, wa) + _dot(a, wc) + _dot(b, wb) + _dot(c, wa))
    lane = lax.broadcasted_iota(jnp.int32, logits.shape, 1).astype(F32)
    logits = jnp.where(lane < N_EXPERTS, logits, -jnp.inf)
    v1 = jnp.max(logits, axis=-1, keepdims=True)
    i1 = jnp.min(jnp.where(logits == v1, lane, float(LANES)), axis=-1, keepdims=True)
    rest = jnp.where(lane == i1, -jnp.inf, logits)
    v2 = jnp.max(rest, axis=-1, keepdims=True)
    i2 = jnp.min(jnp.where(rest == v2, lane, float(LANES)), axis=-1, keepdims=True)
    e2 = jnp.exp(v2 - v1)
    den = 1.0 + e2
    gate = jnp.where(lane == i1, 1.0 / den, 0.0) + jnp.where(lane == i2, e2 / den, 0.0)
    sel = jnp.where((lane == i1) | (lane == i2), 1.0, 0.0)
    return gate, sel


MOE_TM = 512
MOE_CHUNK = 128
MOE_ALIGN = 16
MOE_ROWS = 1024
SEL_LANE = N_EXPERTS
RANK_LANE = 2 * N_EXPERTS
T_ROWS = 16


def _route_kernel(h_ref, g_ref, wr_ref, hn_ref, route_ref, selt_ref, rankt_ref, cnt_ref):
    tm = h_ref.shape[0]
    hn = _rms(h_ref[...], g_ref[...])
    hn_ref[...] = hn.astype(BF16)
    gate, sel = _router_gates(hn, wr_ref)
    selb = sel.astype(BF16)
    r = lax.broadcasted_iota(jnp.int32, (tm, tm), 0)
    c = lax.broadcasted_iota(jnp.int32, (tm, tm), 1)
    rank = _dot(jnp.where(c < r, 1.0, 0.0).astype(BF16), selb)
    route_ref[...] = gate + pltpu.roll(sel, SEL_LANE, 1) + pltpu.roll(rank, RANK_LANE, 1)
    er = lax.broadcasted_iota(jnp.int32, (T_ROWS, LANES), 0)
    el = lax.broadcasted_iota(jnp.int32, (T_ROWS, LANES), 1)
    pick = jnp.where((er == el) & (er < N_EXPERTS), 1.0, 0.0).astype(BF16)
    selt = _dot_nt(pick, selb)
    selt_ref[0] = selt
    rankt_ref[0] = _dot(selt.astype(BF16), jnp.where(r < c, 1.0, 0.0).astype(BF16))
    cnt_ref[0] = jnp.sum(sel, axis=0, keepdims=True)


def _moe_route(h, g, wr3):
    t, d = h.shape
    tm = _pick(t, (MOE_TM, 384, 256, 128))
    nt = t // tm
    return pl.pallas_call(
        _route_kernel,
        grid=(nt,),
        in_specs=[pl.BlockSpec((tm, d), lambda i: (i, 0)),
                  pl.BlockSpec((1, d), lambda i: (0, 0)),
                  pl.BlockSpec((3, d, LANES), lambda i: (0, 0, 0))],
        out_specs=[pl.BlockSpec((tm, d), lambda i: (i, 0)),
                   pl.BlockSpec((tm, LANES), lambda i: (i, 0)),
                   pl.BlockSpec((1, T_ROWS, tm), lambda i: (i, 0, 0)),
                   pl.BlockSpec((1, T_ROWS, tm), lambda i: (i, 0, 0)),
                   pl.BlockSpec((1, 1, LANES), lambda i: (i, 0, 0))],
        out_shape=[jax.ShapeDtypeStruct((t, d), BF16),
                   jax.ShapeDtypeStruct((t, LANES), F32),
                   jax.ShapeDtypeStruct((nt, T_ROWS, tm), F32),
                   jax.ShapeDtypeStruct((nt, T_ROWS, tm), F32),
                   jax.ShapeDtypeStruct((nt, 1, LANES), F32)],
        compiler_params=_cparams(("parallel",)),
        name="moe_route",
    )(h, g, wr3)


def _moe_plan(cnt, n_row_tiles):
    cnt_al = (cnt + MOE_ALIGN - 1) // MOE_ALIGN * MOE_ALIGN
    tot = jnp.sum(cnt_al, axis=0)
    cap = (tot + MOE_CHUNK + MOE_ROWS - 1) // MOE_ROWS * MOE_ROWS
    base = jnp.cumsum(cap) - cap
    offs = base[None, :] + jnp.cumsum(cnt_al, axis=0) - cnt_al
    starts = jnp.arange(n_row_tiles, dtype=jnp.int32) * MOE_ROWS
    tile_e = jnp.clip(jnp.sum(starts[:, None] >= base[None, :], axis=1) - 1, 0, N_EXPERTS - 1)
    tile_ok = starts < (base + tot)[tile_e]
    return (offs.reshape(-1).astype(jnp.int32), tile_e.astype(jnp.int32), tile_ok.astype(jnp.int32))


def _chunk_copies(n_chunks):
    return [(e, c) for e in range(N_EXPERTS) for c in range(n_chunks)]


def _dispatch_kernel(offs_ref, cnts_ref, hn_ref, selt_ref, rankt_ref, xs_in_ref, xs_ref, stage, sems):
    del xs_in_ref
    i = pl.program_id(0)
    tm = hn_ref.shape[0]
    hn = hn_ref[...]
    ridx = lax.broadcasted_iota(jnp.int32, (MOE_CHUNK, tm), 0).astype(F32)

    def copy(e, c):
        slot = e * (tm // MOE_CHUNK) + c
        row0 = pl.multiple_of(offs_ref[i * N_EXPERTS + e] + c * MOE_CHUNK, MOE_ALIGN)
        return pltpu.make_async_copy(stage.at[slot], xs_ref.at[pl.ds(row0, MOE_CHUNK)], sems.at[slot])

    for e, c in _chunk_copies(tm // MOE_CHUNK):
        @pl.when(c * MOE_CHUNK < cnts_ref[i * N_EXPERTS + e])
        def _(e=e, c=c):
            hit = (rankt_ref[0, e:e + 1, :] - float(c * MOE_CHUNK) == ridx) & (selt_ref[0, e:e + 1, :] > 0.5)
            rows = _dot(jnp.where(hit, 1.0, 0.0).astype(BF16), hn)
            stage[e * (tm // MOE_CHUNK) + c] = rows.astype(BF16)
            copy(e, c).start()

    for e, c in _chunk_copies(tm // MOE_CHUNK):
        @pl.when(c * MOE_CHUNK < cnts_ref[i * N_EXPERTS + e])
        def _(e=e, c=c):
            copy(e, c).wait()


def _moe_dispatch(offs, cnts, hn, selt, rankt, n_rows):
    t, d = hn.shape
    nt, _, tm = selt.shape
    n_slots = N_EXPERTS * (tm // MOE_CHUNK)
    xs0 = jnp.zeros((n_rows, d), BF16)
    grid_spec = pltpu.PrefetchScalarGridSpec(
        num_scalar_prefetch=2,
        grid=(nt,),
        in_specs=[pl.BlockSpec((tm, d), lambda i, *_: (i, 0)),
                  pl.BlockSpec((1, T_ROWS, tm), lambda i, *_: (i, 0, 0)),
                  pl.BlockSpec((1, T_ROWS, tm), lambda i, *_: (i, 0, 0)),
                  pl.BlockSpec(memory_space=pl.ANY)],
        out_specs=pl.BlockSpec(memory_space=pl.ANY),
        scratch_shapes=[pltpu.VMEM((n_slots, MOE_CHUNK, d), BF16), pltpu.SemaphoreType.DMA((n_slots,))],
    )
    return pl.pallas_call(
        _dispatch_kernel,
        grid_spec=grid_spec,
        out_shape=jax.ShapeDtypeStruct((n_rows, d), BF16),
        input_output_aliases={5: 0},
        compiler_params=_cparams(("arbitrary",)),
        name="moe_dispatch",
    )(offs, cnts, hn, selt, rankt, xs0)


def _expert_kernel(te_ref, ok_ref, xs_ref, wg_ref, wu_ref, wd_ref, ys_ref, acc_ref):
    s = pl.program_id(0)
    f = pl.program_id(1)
    ok = ok_ref[s] > 0

    @pl.when(ok & (f == 0))
    def _():
        acc_ref[...] = jnp.zeros_like(acc_ref)

    @pl.when(ok)
    def _():
        x = xs_ref[...]
        a = _silu(_dot(x, wg_ref[0])) * _dot(x, wu_ref[0])
        acc_ref[...] += _dot(a.astype(BF16), wd_ref[0])

    last = f == pl.num_programs(1) - 1

    @pl.when(ok & last)
    def _():
        ys_ref[...] = acc_ref[...].astype(ys_ref.dtype)

    @pl.when(jnp.logical_not(ok) & last)
    def _():
        ys_ref[...] = jnp.zeros_like(ys_ref)


def _moe_experts(tile_e, tile_ok, xs, wg, wu, wd):
    n_rows, d = xs.shape
    ff = wg.shape[2]
    tf = _pick(ff, (512, 256, 128))
    nf = ff // tf

    def f_idx(s, f, ok_ref):
        return jnp.where(ok_ref[s] > 0, f, nf - 1)

    grid_spec = pltpu.PrefetchScalarGridSpec(
        num_scalar_prefetch=2,
        grid=(n_rows // MOE_ROWS, nf),
        in_specs=[pl.BlockSpec((MOE_ROWS, d), lambda s, f, te, ok: (s, 0)),
                  pl.BlockSpec((1, d, tf), lambda s, f, te, ok: (te[s], 0, f_idx(s, f, ok))),
                  pl.BlockSpec((1, d, tf), lambda s, f, te, ok: (te[s], 0, f_idx(s, f, ok))),
                  pl.BlockSpec((1, tf, d), lambda s, f, te, ok: (te[s], f_idx(s, f, ok), 0))],
        out_specs=pl.BlockSpec((MOE_ROWS, d), lambda s, f, te, ok: (s, 0)),
        scratch_shapes=[pltpu.VMEM((MOE_ROWS, d), F32)],
    )
    return pl.pallas_call(
        _expert_kernel,
        grid_spec=grid_spec,
        out_shape=jax.ShapeDtypeStruct((n_rows, d), BF16),
        compiler_params=_cparams(("parallel", "arbitrary")),
        name="moe_experts",
    )(tile_e, tile_ok, xs, wg, wu, wd)


def _combine_kernel(offs_ref, cnts_ref, h_ref, route_ref, ys_ref, o_ref, stage, sems):
    i = pl.program_id(0)
    tm = h_ref.shape[0]
    n_chunks = tm // MOE_CHUNK
    ridx = lax.broadcasted_iota(jnp.int32, (tm, MOE_CHUNK), 1).astype(F32)

    def copy(e, c):
        slot = e * n_chunks + c
        row0 = pl.multiple_of(offs_ref[i * N_EXPERTS + e] + c * MOE_CHUNK, MOE_ALIGN)
        return pltpu.make_async_copy(ys_ref.at[pl.ds(row0, MOE_CHUNK)], stage.at[slot], sems.at[slot])

    for e, c in _chunk_copies(n_chunks):
        @pl.when(c * MOE_CHUNK < cnts_ref[i * N_EXPERTS + e])
        def _(e=e, c=c):
            copy(e, c).start()

    o_ref[...] = h_ref[...]
    for e, c in _chunk_copies(n_chunks):
        @pl.when(c * MOE_CHUNK < cnts_ref[i * N_EXPERTS + e])
        def _(e=e, c=c):
            copy(e, c).wait()
            gate = route_ref[:, e:e + 1]
            sel = route_ref[:, SEL_LANE + e:SEL_LANE + e + 1]
            rank = route_ref[:, RANK_LANE + e:RANK_LANE + e + 1]
            hit = (rank - float(c * MOE_CHUNK) == ridx) & (sel > 0.5)
            back = _dot(jnp.where(hit, 1.0, 0.0).astype(BF16), stage[e * n_chunks + c])
            o_ref[...] += gate * back


def _moe_combine(offs, cnts, h, route, ys):
    t, d = h.shape
    tm = _pick(t, (MOE_TM, 384, 256, 128))
    n_slots = N_EXPERTS * (tm // MOE_CHUNK)
    grid_spec = pltpu.PrefetchScalarGridSpec(
        num_scalar_prefetch=2,
        grid=(t // tm,),
        in_specs=[pl.BlockSpec((tm, d), lambda i, *_: (i, 0)),
                  pl.BlockSpec((tm, LANES), lambda i, *_: (i, 0)),
                  pl.BlockSpec(memory_space=pl.ANY)],
        out_specs=pl.BlockSpec((tm, d), lambda i, *_: (i, 0)),
        scratch_shapes=[pltpu.VMEM((n_slots, MOE_CHUNK, d), BF16), pltpu.SemaphoreType.DMA((n_slots,))],
    )
    return pl.pallas_call(
        _combine_kernel,
        grid_spec=grid_spec,
        out_shape=jax.ShapeDtypeStruct((t, d), F32),
        compiler_params=_cparams(("arbitrary",)),
        name="moe_combine",
    )(offs, cnts, h, route, ys)


def _moe(h, g, wr3, wg, wu, wd):
    t, d = h.shape
    hn, route, selt, rankt, cnt = _moe_route(h, g, wr3)
    nt = cnt.shape[0]
    cnt = cnt[:, 0, :N_EXPERTS].astype(jnp.int32)
    worst = TOP_K * t + nt * N_EXPERTS * (MOE_ALIGN - 1) + N_EXPERTS * (MOE_CHUNK + MOE_ROWS - 1)
    n_row_tiles = -(-worst // MOE_ROWS)
    offs, tile_e, tile_ok = _moe_plan(cnt, n_row_tiles)
    cnts = cnt.reshape(-1)
    xs = _moe_dispatch(offs, cnts, hn, selt, rankt, n_row_tiles * MOE_ROWS)
    ys = _moe_experts(tile_e, tile_ok, xs, wg, wu, wd)
    return _moe_combine(offs, cnts, h, route, ys)


def _final_norm_kernel(h_ref, g_ref, o_ref):
    o_ref[0] = _rms(h_ref[0], g_ref[...])


def _final_norm(h3, g, seq):
    b, lp, d = h3.shape
    tm = PAD
    return pl.pallas_call(
        _final_norm_kernel,
        grid=(b, seq // tm),
        in_specs=[pl.BlockSpec((1, tm, d), lambda bi, i: (bi, i + PAD // tm, 0)),
                  pl.BlockSpec((1, d), lambda bi, i: (0, 0))],
        out_specs=pl.BlockSpec((1, tm, d), lambda bi, i: (bi, i, 0)),
        out_shape=jax.ShapeDtypeStruct((b, seq, d), F32),
        compiler_params=_cparams(("parallel", "parallel")),
        name="final_norm",
    )(h3, g)


def _rope_table(lp, rot_dim, theta, width, offset, scale):
    half = rot_dim // 2
    pos = (jnp.arange(lp, dtype=jnp.int32) - FIRST).astype(F32)
    inv = theta ** (-jnp.arange(half, dtype=F32) * 2.0 / rot_dim)
    ang = pos[:, None] * inv[None, :]
    cos, sin = jnp.cos(ang), jnp.sin(ang)
    ones = jnp.ones((lp, 1), F32)
    zeros = jnp.zeros((lp, 1), F32)

    def unit(first, second, fill):
        parts = [jnp.tile(fill, (1, offset)), first, second,
                 jnp.tile(fill, (1, width - offset - rot_dim))]
        return jnp.tile(jnp.concatenate(parts, axis=1), (1, LANES // width))

    c = unit(cos, cos, ones)
    s1 = unit(-sin, jnp.zeros_like(sin), zeros)
    s2 = unit(jnp.zeros_like(sin), sin, zeros)
    return jnp.concatenate([c, s1, s2], axis=1) * scale


def _pad_heads(w, heads, used, width):
    k = w.shape[0]
    w = w.reshape(k, heads, used)
    return jnp.pad(w, ((0, 0), (0, 0), (0, width - used))).reshape(k, heads * width)


def kernel(x, meta_tokens, ln_mix_e, w_in_e, ln_mla_q_e, w_mla_uq_e, ln_mla_kv_e, w_mla_ukv_e,
           w_out_e, ln_ffn_e, w_ffn_gate_e, w_ffn_up_e, w_ffn_down_e, ln_mix_o, w_in_o, b_forget_o,
           diff_lq1_o, diff_lk1_o, diff_lq2_o, diff_lk2_o, diff_subln_o, w_out_o, ln_ffn_o,
           w_router_o, w_moe_gate_o, w_moe_up_o, w_moe_down_o, ln_final):
    b, seq, d = x.shape
    assert d == D_MODEL and seq % TQ == 0
    lp = PAD + seq
    t = b * lp
    depth = 2

    meta = jnp.broadcast_to(meta_tokens.astype(x.dtype)[None], (b, N_META, d))
    h = jnp.concatenate([jnp.zeros((b, FIRST, d), x.dtype), meta, x], axis=1).reshape(t, d)

    row = lambda v: v.reshape(1, -1).astype(F32)
    sb_scale = HEAD_DIM ** -0.5
    tab_mla_q = _rope_table(lp, MLA_ROPE, MLA_THETA, LANES, MLA_NOPE,
                            (MLA_NOPE + MLA_ROPE) ** -0.5 * LOG2E)
    tab_mla_k = _rope_table(lp, MLA_ROPE, MLA_THETA, LANES, MLA_NOPE, 1.0)
    tab_diff_q = _rope_table(lp, ROT_DIM, ROPE_THETA, DIFF_DIM, 0, LOG2E)
    tab_diff_k = _rope_table(lp, ROT_DIM, ROPE_THETA, DIFF_DIM, 0, 1.0)

    for i in range(depth):
        j = i // 2
        if i % 2 == 0:
            w_in = w_in_e[j]
            lat_w = MLA_Q_RANK + MLA_KV_RANK + MLA_ROPE
            w_cat = jnp.concatenate(
                [w_in[:, :SB_W] * sb_scale, w_in[:, SB_W:3 * SB_W + lat_w],
                 jnp.zeros((d, 4 * LANES - lat_w), F32)], axis=1).astype(BF16)
            sb, lat = _even_proj(h, row(ln_mix_e[j]), w_cat)
            sb = sb.reshape(b, lp, -1)
            lat = lat.reshape(b, lp, -1)

            wq = _pad_heads(w_mla_uq_e[j], MLA_HEADS, MLA_NOPE + MLA_ROPE, LANES).astype(BF16)
            ukv = w_mla_ukv_e[j].reshape(MLA_KV_RANK, MLA_HEADS, MLA_NOPE + MLA_V)
            wk_nope = _pad_heads(ukv[:, :, :MLA_NOPE].reshape(MLA_KV_RANK, -1), MLA_HEADS, MLA_NOPE, LANES)
            place = jnp.pad(jnp.eye(MLA_ROPE, dtype=F32), ((0, 0), (MLA_NOPE, LANES - MLA_NOPE - MLA_ROPE)))
            wk_rope = jnp.tile(place, (1, MLA_HEADS))
            wk = jnp.concatenate(
                [wk_nope, wk_rope, jnp.zeros((LANES - MLA_ROPE, MLA_HEADS * LANES), F32)], axis=0).astype(BF16)
            wv = ukv[:, :, MLA_NOPE:].reshape(MLA_KV_RANK, -1).astype(BF16)
            mq, mk, mv = _mla_up(lat, row(ln_mla_q_e[j]), row(ln_mla_kv_e[j]), wq, wk, wv,
                                 tab_mla_q, tab_mla_k)

            a_out = _sb_attn(sb).reshape(t, -1)
            b_out = _mla_attn(mq, mk, mv).reshape(t, -1)
            w_out = w_out_e[j].astype(BF16)
            h = _out_proj(h, a_out, b_out, w_out[:SB_W], w_out[SB_W:])
            h = _ffn(h, row(ln_ffn_e[j]), w_ffn_gate_e[j].astype(BF16), w_ffn_up_e[j].astype(BF16),
                     w_ffn_down_e[j].astype(BF16))
        else:
            lam_init = 0.8 - 0.6 * math.exp(-0.3 * i)
            w_in = w_in_o[j]
            wf = w_in[:, N_ODD_MAIN:]
            w_cat = jnp.concatenate(
                [w_in[:, :DIFF_W] * sb_scale, w_in[:, DIFF_W:3 * DIFF_W],
                 w_in[:, 3 * DIFF_W:3 * DIFF_W + FOX_W] * sb_scale, w_in[:, 3 * DIFF_W + FOX_W:N_ODD_MAIN],
                 wf, jnp.zeros((d, LANES - FOX_HEADS), F32)], axis=1).astype(BF16)
            wft = jnp.pad(wf.T, ((0, F_ROWS - FOX_HEADS), (0, 0))).astype(BF16)
            bf = b_forget_o[j].astype(F32)
            bcol = jnp.pad(bf, (0, LANES - FOX_HEADS)).reshape(1, LANES)
            brow = jnp.pad(bf, (0, F_ROWS - FOX_HEADS)).reshape(F_ROWS, 1)
            main, fcol, frow = _odd_proj(h.reshape(b, lp, d), row(ln_mix_o[j]), w_cat, wft, bcol, brow,
                                         tab_diff_q, tab_diff_k)
            ccol, crow = _fox_cumsum(fcol, frow)
            crow4 = crow[:, :FOX_HEADS].reshape(b, FOX_HEADS // 2, 2, lp)
            lam_vecs = jnp.stack([diff_lq1_o[j], diff_lk1_o[j], diff_lq2_o[j], diff_lk2_o[j]]).astype(F32)
            c_out = _diff_attn(main, lam_vecs, row(diff_subln_o[j]), lam_init).reshape(t, -1)
            d_out = _fox_attn(main, ccol, crow4).reshape(t, -1)
            w_out = w_out_o[j].astype(BF16)
            h = _out_proj(h, c_out, d_out, w_out[:DIFF_W], w_out[DIFF_W:])
            wr = jnp.pad(w_router_o[j].astype(F32), ((0, 0), (0, LANES - N_EXPERTS)))
            wr3 = jnp.stack(_split3(wr))
            h = _moe(h, row(ln_ffn_o[j]), wr3, w_moe_gate_o[j].astype(BF16), w_moe_up_o[j].astype(BF16),
                     w_moe_down_o[j].astype(BF16))

    return _final_norm(h.reshape(b, lp, d), row(ln_final), seq)
```

```python
import functools
import math

import jax
import jax.numpy as jnp
from jax import lax
from jax.experimental import pallas as pl
from jax.experimental.pallas import tpu as pltpu

F32 = jnp.float32
BF16 = jnp.bfloat16

D_MODEL = 1024
CHUNK = 64
N_META = 16
EPS = 1e-6
LOG2E = 1.4426950408889634
NEG_INF = -1e30

HEAD_DIM = 64
SB_HEADS = 8
MLA_HEADS = 8
MLA_NOPE = 64
MLA_ROPE = 32
MLA_V = 64
MLA_Q_RANK = 256
MLA_KV_RANK = 128
MLA_THETA = 10000.0
DIFF_HEADS = 4
DIFF_DIM = 64
FOX_HEADS = 8
ROPE_THETA = 500000.0
ROT_DIM = HEAD_DIM // 4
N_EXPERTS = 8
TOP_K = 2

SB_W = SB_HEADS * HEAD_DIM
DIFF_W = DIFF_HEADS * 2 * DIFF_DIM
FOX_W = FOX_HEADS * HEAD_DIM

LANES = 128
PAD = 128
FIRST = PAD - N_META
TQ = 128
KEY_WINDOW = 512
Q_GROUP = 4
SB_Q_GROUP = 2
F32_EXP2_UNDERFLOW = 160.0
SUFFIX_CHUNK = 256
CHUNK_SHIFT = CHUNK.bit_length() - 1
VMEM_LIMIT = 56 * 1024 * 1024


def _cparams(sem):
    return pltpu.CompilerParams(dimension_semantics=sem, vmem_limit_bytes=VMEM_LIMIT)


def _pick(n, cands):
    for c in cands:
        if n % c == 0:
            return c
    raise ValueError(f"no tile in {cands} divides {n}")


def _rms(x, g):
    return x * lax.rsqrt(jnp.mean(x * x, axis=-1, keepdims=True) + EPS) * g


def _dot(a, b):
    return jnp.dot(a, b, preferred_element_type=F32)


def _dot_nt(a, b):
    return lax.dot_general(a, b, (((1,), (1,)), ((), ())), preferred_element_type=F32)


def _rope_chunk(y, tab, shift):
    c = tab[:, 0:LANES]
    s1 = tab[:, LANES:2 * LANES]
    s2 = tab[:, 2 * LANES:3 * LANES]
    return (y * c + pltpu.roll(y, LANES - shift, 1) * s1 + pltpu.roll(y, shift, 1) * s2)


def _split3(x):
    a = x.astype(BF16)
    r = x - a.astype(F32)
    b = r.astype(BF16)
    c = (r - b.astype(F32)).astype(BF16)
    return a, b, c


def _even_proj_kernel(h_ref, g_ref, w_ref, sb_ref, lat_ref):
    hn = _rms(h_ref[...], g_ref[...]).astype(BF16)
    y = _dot(hn, w_ref[...])
    n_sb = sb_ref.shape[-1]
    sb_ref[:, :SB_W] = (y[:, :SB_W] * LOG2E).astype(BF16)
    sb_ref[:, SB_W:] = y[:, SB_W:n_sb].astype(BF16)
    lat_ref[...] = y[:, n_sb:].astype(BF16)


def _even_proj(h, g, w):
    t, d = h.shape
    n = w.shape[1]
    n_sb = 3 * SB_W
    tm = _pick(t, (512, 384, 256, 128))
    return pl.pallas_call(
        _even_proj_kernel,
        grid=(t // tm,),
        in_specs=[pl.BlockSpec((tm, d), lambda i: (i, 0)),
                  pl.BlockSpec((1, d), lambda i: (0, 0)),
                  pl.BlockSpec((d, n), lambda i: (0, 0))],
        out_specs=[pl.BlockSpec((tm, n_sb), lambda i: (i, 0)),
                   pl.BlockSpec((tm, n - n_sb), lambda i: (i, 0))],
        out_shape=[jax.ShapeDtypeStruct((t, n_sb), BF16),
                   jax.ShapeDtypeStruct((t, n - n_sb), BF16)],
        compiler_params=_cparams(("parallel",)),
        name="even_proj",
    )(h, g, w)


def _mla_up_kernel(lat_ref, gq_ref, gkv_ref, wq_ref, wk_ref, wv_ref, tq_ref, tk_ref,
                   q_ref, k_ref, v_ref):
    lat = lat_ref[0]
    nq = _rms(lat[:, :MLA_Q_RANK].astype(F32), gq_ref[...]).astype(BF16)
    nkv = _rms(lat[:, MLA_Q_RANK:MLA_Q_RANK + MLA_KV_RANK].astype(F32), gkv_ref[...]).astype(BF16)
    yq = _dot(nq, wq_ref[...])
    xk = jnp.concatenate([nkv, lat[:, MLA_Q_RANK + MLA_KV_RANK:]], axis=1)
    yk = _dot(xk, wk_ref[...])
    v_ref[0] = _dot(nkv, wv_ref[...]).astype(BF16)
    tabq = tq_ref[...]
    tabk = tk_ref[...]
    for hd in range(MLA_HEADS):
        sl = slice(hd * LANES, (hd + 1) * LANES)
        q_ref[0, :, sl] = _rope_chunk(yq[:, sl], tabq, MLA_ROPE // 2).astype(BF16)
        k_ref[0, :, sl] = _rope_chunk(yk[:, sl], tabk, MLA_ROPE // 2).astype(BF16)


def _mla_up(lat, gq, gkv, wq, wk, wv, tabq, tabk):
    b, lp, nl = lat.shape
    tm = _pick(lp, (384, 128))
    nqk = MLA_HEADS * LANES
    nv = MLA_HEADS * MLA_V
    full = lambda shp: pl.BlockSpec(shp, lambda bi, i: (0,) * len(shp))
    return pl.pallas_call(
        _mla_up_kernel,
        grid=(b, lp // tm),
        in_specs=[pl.BlockSpec((1, tm, nl), lambda bi, i: (bi, i, 0)),
                  full(gq.shape), full(gkv.shape), full(wq.shape), full(wk.shape), full(wv.shape),
                  pl.BlockSpec((tm, 3 * LANES), lambda bi, i: (i, 0)),
                  pl.BlockSpec((tm, 3 * LANES), lambda bi, i: (i, 0))],
        out_specs=[pl.BlockSpec((1, tm, nqk), lambda bi, i: (bi, i, 0)),
                   pl.BlockSpec((1, tm, nqk), lambda bi, i: (bi, i, 0)),
                   pl.BlockSpec((1, tm, nv), lambda bi, i: (bi, i, 0))],
        out_shape=[jax.ShapeDtypeStruct((b, lp, nqk), BF16),
                   jax.ShapeDtypeStruct((b, lp, nqk), BF16),
                   jax.ShapeDtypeStruct((b, lp, nv), BF16)],
        compiler_params=_cparams(("parallel", "parallel")),
        name="mla_up",
    )(lat, gq, gkv, wq, wk, wv, tabq, tabk)


N_ODD_MAIN = 3 * DIFF_W + 3 * FOX_W
F_ROWS = 16


def _odd_proj_kernel(h_ref, g_ref, w_ref, wft_ref, bcol_ref, brow_ref, tabq_ref, tabk_ref,
                     main_ref, fcol_ref, frow_ref):
    hn = _rms(h_ref[0], g_ref[...]).astype(BF16)
    y = _dot(hn, w_ref[...])
    n_rope = DIFF_W // LANES
    for c in range(2 * n_rope):
        sl = slice(c * LANES, (c + 1) * LANES)
        tab = tabq_ref[...] if c < n_rope else tabk_ref[...]
        main_ref[0, :, sl] = _rope_chunk(y[:, sl], tab, ROT_DIM // 2).astype(BF16)
    fq = slice(3 * DIFF_W, 3 * DIFF_W + FOX_W)
    main_ref[0, :, 2 * DIFF_W:3 * DIFF_W] = y[:, 2 * DIFF_W:3 * DIFF_W].astype(BF16)
    main_ref[0, :, fq] = (y[:, fq] * LOG2E).astype(BF16)
    main_ref[0, :, 3 * DIFF_W + FOX_W:] = y[:, 3 * DIFF_W + FOX_W:N_ODD_MAIN].astype(BF16)
    fcol_ref[0] = y[:, N_ODD_MAIN:] + bcol_ref[...]
    frow_ref[0] = _dot_nt(wft_ref[...], hn) + brow_ref[...]


def _odd_proj(h3, g, w, wft, bcol, brow, tabq, tabk):
    b, lp, d = h3.shape
    tm = _pick(lp, (384, 128))
    full = lambda shp: pl.BlockSpec(shp, lambda bi, i: (0,) * len(shp))
    return pl.pallas_call(
        _odd_proj_kernel,
        grid=(b, lp // tm),
        in_specs=[pl.BlockSpec((1, tm, d), lambda bi, i: (bi, i, 0)),
                  full(g.shape), full(w.shape), full(wft.shape), full(bcol.shape), full(brow.shape),
                  pl.BlockSpec((tm, 3 * LANES), lambda bi, i: (i, 0)),
                  pl.BlockSpec((tm, 3 * LANES), lambda bi, i: (i, 0))],
        out_specs=[pl.BlockSpec((1, tm, N_ODD_MAIN), lambda bi, i: (bi, i, 0)),
                   pl.BlockSpec((1, tm, LANES), lambda bi, i: (bi, i, 0)),
                   pl.BlockSpec((1, F_ROWS, tm), lambda bi, i: (bi, 0, i))],
        out_shape=[jax.ShapeDtypeStruct((b, lp, N_ODD_MAIN), BF16),
                   jax.ShapeDtypeStruct((b, lp, LANES), F32),
                   jax.ShapeDtypeStruct((b, F_ROWS, lp), F32)],
        compiler_params=_cparams(("parallel", "parallel")),
        name="odd_proj",
    )(h3, g, w, wft, bcol, brow, tabq, tabk)


def _log_sigmoid(x):
    return jnp.minimum(x, 0.0) - jnp.log(1.0 + jnp.exp(-jnp.abs(x)))


def _fox_cumsum_kernel(fcol_ref, frow_ref, ccol_ref, crow_ref):
    lp = fcol_ref.shape[1]
    blk = LANES
    r = lax.broadcasted_iota(jnp.int32, (blk, blk), 0)
    c = lax.broadcasted_iota(jnp.int32, (blk, blk), 1)
    tri_l = (c <= r).astype(BF16)
    tri_u = (r <= c).astype(BF16)
    row_id = lax.broadcasted_iota(jnp.int32, (blk, LANES), 0)
    col_id = lax.broadcasted_iota(jnp.int32, (F_ROWS, blk), 1)
    carry_c = jnp.zeros((1, LANES), F32)
    carry_r = jnp.zeros((F_ROWS, 1), F32)
    for j in range(lp // blk):
        sl = slice(j * blk, (j + 1) * blk)
        ls = _log_sigmoid(fcol_ref[0, sl, :])
        ls = jnp.where(row_id + j * blk >= FIRST, ls, 0.0)
        a, b, c3 = _split3(ls)
        cs = _dot(tri_l, a) + _dot(tri_l, b) + _dot(tri_l, c3) + carry_c
        ccol_ref[0, sl, :] = cs * LOG2E
        carry_c = cs[blk - 1:blk, :]
        lr = _log_sigmoid(frow_ref[0, :, sl])
        lr = jnp.where(col_id + j * blk >= FIRST, lr, 0.0)
        a, b, c3 = _split3(lr)
        cr = _dot(a, tri_u) + _dot(b, tri_u) + _dot(c3, tri_u) + carry_r
        crow_ref[0, :, sl] = cr * LOG2E
        carry_r = cr[:, blk - 1:blk]


def _fox_cumsum(fcol, frow):
    b, lp, _ = fcol.shape
    return pl.pallas_call(
        _fox_cumsum_kernel,
        grid=(b,),
        in_specs=[pl.BlockSpec((1, lp, LANES), lambda bi: (bi, 0, 0)),
                  pl.BlockSpec((1, F_ROWS, lp), lambda bi: (bi, 0, 0))],
        out_specs=[pl.BlockSpec((1, lp, LANES), lambda bi: (bi, 0, 0)),
                   pl.BlockSpec((1, F_ROWS, lp), lambda bi: (bi, 0, 0))],
        out_shape=[jax.ShapeDtypeStruct((b, lp, LANES), F32),
                   jax.ShapeDtypeStruct((b, F_ROWS, lp), F32)],
        compiler_params=_cparams(("parallel",)),
        name="fox_cumsum",
    )(fcol, frow)


def _rows(i, n):
    if isinstance(i, int):
        return pl.ds(i * n, n)
    return pl.ds(pl.multiple_of(i * n, n), n)


def _keys(k0, w):
    if isinstance(k0, int):
        return pl.ds(k0, w)
    return pl.ds(pl.multiple_of(k0, TQ), w)


def _visible(kind, ks, qs):
    if kind == "chunk":
        return (ks >> CHUNK_SHIFT) <= (qs >> CHUNK_SHIFT)
    if kind == "causal":
        return ks <= qs
    return ks < qs


def _first_window_mask(kind, q0, w, lim):
    r = lax.broadcasted_iota(jnp.int32, (2 * TQ, w), 0)
    ks = lax.broadcasted_iota(jnp.int32, (2 * TQ, w), 1)
    return _visible(kind, ks, q0 + (r & (TQ - 1))) & (ks >= FIRST) & (ks < lim)


def _diag_window_mask(kind, w):
    r = lax.broadcasted_iota(jnp.int32, (2 * TQ, w), 0)
    ks = lax.broadcasted_iota(jnp.int32, (2 * TQ, w), 1) + (TQ - w)
    return _visible(kind, ks, r & (TQ - 1))


def _span_mask(w, lim):
    ks = lax.broadcasted_iota(jnp.int32, (1, w), 1)
    return (ks >= FIRST) & (ks < lim)


def _lane_halves():
    lane = lax.broadcasted_iota(jnp.int32, (TQ, LANES), 1)
    return lane < (LANES // 2)


def _stack_masked(q):
    lo = _lane_halves()
    z = jnp.zeros_like(q)
    return jnp.concatenate([jnp.where(lo, q, z), jnp.where(lo, z, q)], axis=0)


def _softmax_sweep(kind, w, prep, scores_fn, v_fn, dbias_ref):
    ones = jnp.ones((w, LANES), BF16)

    def step(carry, s, v):
        m, acc = carry
        m_new = jnp.maximum(m, jnp.max(s, axis=-1, keepdims=True))
        alpha = jnp.exp2(m - m_new)
        p = jnp.exp2(s - m_new)
        acc = alpha * acc + _dot(p.astype(BF16), jnp.concatenate([v, ones], axis=1))
        return m_new, acc

    def run(qi, mwin):
        g = w // TQ
        ctx = prep(qi)
        q0 = qi * TQ
        pre = qi + 1 - mwin * g
        carry = (jnp.full((2 * TQ, 1), NEG_INF, F32), jnp.zeros((2 * TQ, 2 * LANES), F32))
        s = scores_fn(ctx, 0)
        if mwin == 0:
            s = jnp.where(_first_window_mask(kind, q0, w, pre * TQ), s, NEG_INF)
            return step(carry, s, v_fn(0))[1]
        carry = step(carry, s + jnp.where(_span_mask(w, pre * TQ), 0.0, NEG_INF), v_fn(0))
        for t in range(1, mwin):
            k0 = (pre + (t - 1) * g) * TQ
            carry = step(carry, scores_fn(ctx, k0), v_fn(k0))
        k0 = q0 + TQ - w
        return step(carry, scores_fn(ctx, k0) + dbias_ref[...], v_fn(k0))[1]

    return run


def _for_each_query_group(nq, w, q_group, group_fn):
    g = w // TQ
    one = jnp.sign(pl.program_id(0) + 1)
    for mwin in range(-(-nq // g)):
        lo, hi = mwin * g, min((mwin + 1) * g, nq)

        def body(it, _, mwin=mwin, lo=lo):
            qi = lo + q_group * it
            group_fn([qi + j for j in range(q_group)], mwin)
            return 0

        n_groups = (hi - lo) // q_group
        if n_groups:
            lax.fori_loop(0, n_groups * one, body, 0)
        for qi in range(lo + n_groups * q_group, hi):
            group_fn([qi], mwin)


def _for_each_query_block(nq, w, run, emit):
    def group_fn(qis, mwin):
        results = [run(qi, mwin) for qi in qis]
        for qi, res in zip(qis, results):
            emit(qi, res)

    _for_each_query_group(nq, w, Q_GROUP, group_fn)


def _key_window(lp):
    return min(KEY_WINDOW, lp)


def _merge_heads(top, bottom):
    return jnp.where(_lane_halves(), top, bottom)


def _attn_call(kernel, name, b, lp, n_blocks, in_specs, args, n_masks=1):
    w = _key_window(lp)
    return pl.pallas_call(
        kernel,
        grid=(b, n_blocks),
        in_specs=in_specs,
        out_specs=pl.BlockSpec((1, lp, LANES), lambda bi, hp: (bi, 0, hp)),
        out_shape=jax.ShapeDtypeStruct((b, lp, n_blocks * LANES), BF16),
        scratch_shapes=[pltpu.VMEM((2 * TQ, w), F32)] * n_masks,
        compiler_params=_cparams(("parallel", "parallel")),
        name=name,
    )(*args)


def _normalised(acc):
    return acc[:, :LANES] / acc[:, LANES:]


def _col_spec(lp, off):
    return pl.BlockSpec((1, lp, LANES), lambda bi, hp: (bi, 0, off + hp))


def _mla_attn_kernel(q_ref, k_ref, v_ref, o_ref, dbias_ref):
    lp = q_ref.shape[1]
    w = _key_window(lp)
    dbias_ref[...] = jnp.where(_diag_window_mask("chunk", w), 0.0, NEG_INF)

    def prep(qi):
        return q_ref[0, _rows(qi, TQ), 0:LANES], q_ref[0, _rows(qi, TQ), LANES:2 * LANES]

    def scores(ctx, k0):
        qa, qb = ctx
        ka = k_ref[0, _keys(k0, w), 0:LANES]
        kb = k_ref[0, _keys(k0, w), LANES:2 * LANES]
        return jnp.concatenate([_dot_nt(qa, ka), _dot_nt(qb, kb)], axis=0)

    run = _softmax_sweep("chunk", w, prep, scores, lambda k0: v_ref[0, _keys(k0, w), :], dbias_ref)

    def emit(qi, acc):
        o = _normalised(acc)
        o_ref[0, _rows(qi, TQ), :] = _merge_heads(o[:TQ], o[TQ:]).astype(o_ref.dtype)

    _for_each_query_block(lp // TQ, w, run, emit)


def _mla_attn(q, k, v):
    b, lp, _ = q.shape
    wide = lambda: pl.BlockSpec((1, lp, 2 * LANES), lambda bi, hp: (bi, 0, hp))
    return _attn_call(_mla_attn_kernel, "mla_attn", b, lp, MLA_HEADS // 2,
                      [wide(), wide(), _col_spec(lp, 0)], (q, k, v))


def _diff_attn_kernel(lam_init, x_q, x_k, x_v, lam_ref, gsub_ref, o_ref, dbias_ref):
    lp = x_q.shape[1]
    w = _key_window(lp)
    dbias_ref[...] = jnp.where(_diag_window_mask("chunk", w), 0.0, NEG_INF)
    lv = lam_ref[...]
    lam = (jnp.exp(jnp.sum(lv[0:1] * lv[1:2], axis=-1, keepdims=True))
           - jnp.exp(jnp.sum(lv[2:3] * lv[3:4], axis=-1, keepdims=True)) + lam_init)
    gsub = gsub_ref[...]

    def prep(qi):
        return _stack_masked(x_q[0, _rows(qi, TQ), :])

    def scores(qs, k0):
        return _dot_nt(qs, x_k[0, _keys(k0, w), :])

    run = _softmax_sweep("chunk", w, prep, scores, lambda k0: x_v[0, _keys(k0, w), :], dbias_ref)

    def emit(qi, acc):
        o = _normalised(acc)
        o = o[:TQ] - lam * o[TQ:]
        o_ref[0, _rows(qi, TQ), :] = (_rms(o, gsub) * (1.0 - lam_init)).astype(o_ref.dtype)

    _for_each_query_block(lp // TQ, w, run, emit)


def _diff_attn(main, lam_vecs, gsub, lam_init):
    b, lp, _ = main.shape
    nb = DIFF_W // LANES
    full = lambda shp: pl.BlockSpec(shp, lambda bi, hp: (0,) * len(shp))
    return _attn_call(functools.partial(_diff_attn_kernel, lam_init), "diff_attn", b, lp, DIFF_HEADS,
                      [_col_spec(lp, 0), _col_spec(lp, nb), _col_spec(lp, 2 * nb),
                       full(lam_vecs.shape), full(gsub.shape)],
                      (main, main, main, lam_vecs, gsub))


def _fox_attn_kernel(x_q, x_k, x_v, ccol_ref, crow_ref, o_ref, dbias_ref):
    lp = x_q.shape[1]
    w = _key_window(lp)
    dbias_ref[...] = jnp.where(_diag_window_mask("causal", w), 0.0, NEG_INF)
    hp = pl.program_id(1)
    lane = lax.broadcasted_iota(jnp.int32, (TQ, LANES), 1)

    def prep(qi):
        cc = ccol_ref[0, _rows(qi, TQ), :]
        cq0 = jnp.sum(jnp.where(lane == 2 * hp, cc, 0.0), axis=-1, keepdims=True)
        cq1 = jnp.sum(jnp.where(lane == 2 * hp + 1, cc, 0.0), axis=-1, keepdims=True)
        return _stack_masked(x_q[0, _rows(qi, TQ), :]), cq0, cq1

    def scores(ctx, k0):
        qs, cq0, cq1 = ctx
        s = _dot_nt(qs, x_k[0, _keys(k0, w), :])
        ck = crow_ref[0, 0, :, _keys(k0, w)]
        return s + jnp.concatenate([cq0 - ck[0:1, :], cq1 - ck[1:2, :]], axis=0)

    run = _softmax_sweep("causal", w, prep, scores, lambda k0: x_v[0, _keys(k0, w), :], dbias_ref)

    def emit(qi, acc):
        o = _normalised(acc)
        o_ref[0, _rows(qi, TQ), :] = _merge_heads(o[:TQ], o[TQ:]).astype(o_ref.dtype)

    _for_each_query_block(lp // TQ, w, run, emit)


def _fox_attn(main, ccol, crow4):
    b, lp, _ = main.shape
    off = 3 * DIFF_W // LANES
    nb = FOX_W // LANES
    return _attn_call(_fox_attn_kernel, "fox_attn", b, lp, nb,
                      [_col_spec(lp, off), _col_spec(lp, off + nb), _col_spec(lp, off + 2 * nb),
                       pl.BlockSpec((1, lp, LANES), lambda bi, hp: (bi, 0, 0)),
                       pl.BlockSpec((1, 1, 2, lp), lambda bi, hp: (bi, hp, 0, 0))],
                      (main, main, main, ccol, crow4))


def _sb_attn_kernel(x_q, x_k, x_v, o_ref, dkeep_ref, dbias_ref):
    lp = x_q.shape[1]
    w = _key_window(lp)
    g = w // TQ
    cs = SUFFIX_CHUNK if w % SUFFIX_CHUNK == 0 else TQ
    kr = lax.broadcasted_iota(jnp.int32, (cs, cs), 0)
    kc = lax.broadcasted_iota(jnp.int32, (cs, cs), 1)
    later = jnp.where(kr > kc, 1.0, 0.0).astype(BF16)
    dvis = _diag_window_mask("strict", w)
    dkeep_ref[...] = jnp.where(dvis, 1.0, 0.0)
    dbias_ref[...] = jnp.where(dvis, 0.0, NEG_INF)

    def step(carry, qs, k0, keep=None, bias=None, vis=None):
        rsum, acc = carry
        z = _dot_nt(qs, x_k[0, _keys(k0, w), :])
        cost = jnp.maximum(z, 0.0) + jnp.log2(1.0 + jnp.exp2(-jnp.abs(z)))
        if vis is not None:
            cost = jnp.where(vis, cost, 0.0)
        if keep is not None:
            cost = cost * keep
        cb = cost.astype(BF16)
        parts = []
        for c in reversed(range(w // cs)):
            sl = slice(c * cs, (c + 1) * cs)
            inner = _dot(cb[:, sl], later)
            parts.append(inner + rsum)
            rsum = rsum + inner[:, 0:1] + cb[:, c * cs:c * cs + 1].astype(F32)
        arg = z - cost - jnp.concatenate(parts[::-1], axis=1)
        if bias is not None:
            arg = arg + bias
        wgt = jnp.exp2(arg)
        if vis is not None:
            wgt = jnp.where(vis, wgt, 0.0)
        acc = acc + _dot(wgt.astype(BF16), x_v[0, _keys(k0, w), :])
        return rsum, acc

    def knorm(j, best):
        kk = x_k[0, _rows(j, TQ), :].astype(F32)
        return jnp.maximum(best, jnp.sum(kk * kk, axis=-1, keepdims=True))

    kn2 = jnp.max(lax.fori_loop(0, lp // TQ, knorm, jnp.zeros((TQ, 1), F32)), axis=0, keepdims=True)

    def emit(qi, acc):
        o_ref[0, _rows(qi, TQ), :] = _merge_heads(acc[:TQ], acc[TQ:]).astype(o_ref.dtype)

    def group_fn(qis, mwin):
        qss = [_stack_masked(x_q[0, _rows(qi, TQ), :]) for qi in qis]
        pres = [qi + 1 - mwin * g for qi in qis]
        zero = (jnp.zeros((2 * TQ, 1), F32), jnp.zeros((2 * TQ, LANES), F32))
        if mwin == 0:
            for qi, qs, pre in zip(qis, qss, pres):
                emit(qi, step(zero, qs, 0, vis=_first_window_mask("strict", qi * TQ, w, pre * TQ))[1])
            return
        carries = [step(zero, qs, qi * TQ + TQ - w, keep=dkeep_ref[...], bias=dbias_ref[...])
                   for qi, qs in zip(qis, qss)]

        def rest(cs):
            out = []
            for qs, pre, carry in zip(qss, pres, cs):
                for t in range(mwin - 1, 0, -1):
                    carry = step(carry, qs, (pre + (t - 1) * g) * TQ)
                span = _span_mask(w, pre * TQ)
                out.append(step(carry, qs, 0, keep=jnp.where(span, 1.0, 0.0),
                                bias=jnp.where(span, 0.0, NEG_INF)))
            return tuple(out)

        slack = [jnp.max(jnp.sqrt(jnp.sum(jnp.square(qs.astype(F32)), axis=-1, keepdims=True) * kn2)
                         - carry[0]) for qs, carry in zip(qss, carries)]
        live = functools.reduce(jnp.maximum, slack) > -F32_EXP2_UNDERFLOW
        carries = lax.cond(live, rest, lambda cs: cs, tuple(carries))
        for qi, carry in zip(qis, carries):
            emit(qi, carry[1])

    _for_each_query_group(lp // TQ, w, SB_Q_GROUP, group_fn)


def _sb_attn(sb):
    b, lp, _ = sb.shape
    nb = SB_W // LANES
    return _attn_call(_sb_attn_kernel, "sb_attn", b, lp, nb,
                      [_col_spec(lp, 0), _col_spec(lp, nb), _col_spec(lp, 2 * nb)],
                      (sb, sb, sb), n_masks=2)


def _silu(x):
    return x / (1.0 + jnp.exp(-x))


def _mixer_residual(h_ref, a_ref, b_ref, wa_ref, wb_ref):
    return h_ref[...] + _dot(a_ref[...], wa_ref[...]) + _dot(b_ref[...], wb_ref[...])


def _ffn_kernel(h_ref, a_ref, b_ref, wa_ref, wb_ref, g_ref, wg_ref, wu_ref, wd_ref, o_ref, hn_ref, acc_ref):
    f = pl.program_id(1)

    @pl.when(f == 0)
    def _():
        hmid = _mixer_residual(h_ref, a_ref, b_ref, wa_ref, wb_ref)
        hn_ref[...] = _rms(hmid, g_ref[...]).astype(BF16)
        acc_ref[...] = hmid

    hn = hn_ref[...]
    a = _silu(_dot(hn, wg_ref[...])) * _dot(hn, wu_ref[...])
    acc_ref[...] += _dot(a.astype(BF16), wd_ref[...])

    @pl.when(f == pl.num_programs(1) - 1)
    def _():
        o_ref[...] = acc_ref[...]


def _ffn(h, mix_a, mix_b, wa, wb, g, wg, wu, wd):
    t, d = h.shape
    ff = wg.shape[1]
    na, nb = mix_a.shape[1], mix_b.shape[1]
    tm = _pick(t, (512, 384, 256, 128))
    tf = _pick(ff, (1408, 512, 256, 128))
    return pl.pallas_call(
        _ffn_kernel,
        grid=(t // tm, ff // tf),
        in_specs=[pl.BlockSpec((tm, d), lambda i, f: (i, 0)),
                  pl.BlockSpec((tm, na), lambda i, f: (i, 0)),
                  pl.BlockSpec((tm, nb), lambda i, f: (i, 0)),
                  pl.BlockSpec((na, d), lambda i, f: (0, 0)),
                  pl.BlockSpec((nb, d), lambda i, f: (0, 0)),
                  pl.BlockSpec((1, d), lambda i, f: (0, 0)),
                  pl.BlockSpec((d, tf), lambda i, f: (0, f)),
                  pl.BlockSpec((d, tf), lambda i, f: (0, f)),
                  pl.BlockSpec((tf, d), lambda i, f: (f, 0))],
        out_specs=pl.BlockSpec((tm, d), lambda i, f: (i, 0)),
        out_shape=jax.ShapeDtypeStruct((t, d), F32),
        scratch_shapes=[pltpu.VMEM((tm, d), BF16), pltpu.VMEM((tm, d), F32)],
        compiler_params=_cparams(("parallel", "arbitrary")),
        name="ffn",
    )(h, mix_a, mix_b, wa, wb, g, wg, wu, wd)


def _router_gates(hn, wr_ref):
    a, b, c = _split3(hn)
    wa, wb, wc = wr_ref[0], wr_ref[1], wr_ref[2]
    logits = (_dot(a, wa) + _dot(a, wb) + _dot(b, wa) + _dot(a, wc) + _dot(b, wb) + _dot(c, wa))
    lane = lax.broadcasted_iota(jnp.int32, logits.shape, 1).astype(F32)
    logits = jnp.where(lane < N_EXPERTS, logits, -jnp.inf)
    v1 = jnp.max(logits, axis=-1, keepdims=True)
    i1 = jnp.min(jnp.where(logits == v1, lane, float(LANES)), axis=-1, keepdims=True)
    rest = jnp.where(lane == i1, -jnp.inf, logits)
    v2 = jnp.max(rest, axis=-1, keepdims=True)
    i2 = jnp.min(jnp.where(rest == v2, lane, float(LANES)), axis=-1, keepdims=True)
    e2 = jnp.exp(v2 - v1)
    den = 1.0 + e2
    gate = jnp.where(lane == i1, 1.0 / den, 0.0) + jnp.where(lane == i2, e2 / den, 0.0)
    sel = jnp.where((lane == i1) | (lane == i2), 1.0, 0.0)
    return gate, sel


MOE_TM = 512
MOE_CHUNK = 256
MOE_ALIGN = 16
MOE_ROWS = 1024
SEL_LANE = N_EXPERTS
RANK_LANE = 2 * N_EXPERTS
T_ROWS = 16


def _route_kernel(h_ref, a_ref, b_ref, wa_ref, wb_ref, g_ref, wr_ref,
                  hmid_ref, hn_ref, route_ref, selt_ref, rankt_ref, cnt_ref):
    tm = h_ref.shape[0]
    hmid = _mixer_residual(h_ref, a_ref, b_ref, wa_ref, wb_ref)
    hmid_ref[...] = hmid
    hn = _rms(hmid, g_ref[...])
    hn_ref[...] = hn.astype(BF16)
    gate, sel = _router_gates(hn, wr_ref)
    selb = sel.astype(BF16)
    r = lax.broadcasted_iota(jnp.int32, (tm, tm), 0)
    c = lax.broadcasted_iota(jnp.int32, (tm, tm), 1)
    rank = _dot(jnp.where(c < r, 1.0, 0.0).astype(BF16), selb)
    route_ref[...] = gate + pltpu.roll(sel, SEL_LANE, 1) + pltpu.roll(rank, RANK_LANE, 1)
    er = lax.broadcasted_iota(jnp.int32, (T_ROWS, LANES), 0)
    el = lax.broadcasted_iota(jnp.int32, (T_ROWS, LANES), 1)
    pick = jnp.where((er == el) & (er < N_EXPERTS), 1.0, 0.0).astype(BF16)
    selt = _dot_nt(pick, selb)
    selt_ref[0] = selt
    rankt_ref[0] = _dot(selt.astype(BF16), jnp.where(r < c, 1.0, 0.0).astype(BF16))
    cnt_ref[0] = jnp.sum(sel, axis=0, keepdims=True)


def _moe_route(h, mix_a, mix_b, wa, wb, g, wr3):
    t, d = h.shape
    na, nb = mix_a.shape[1], mix_b.shape[1]
    tm = _pick(t, (MOE_TM, 384, 256, 128))
    nt = t // tm
    return pl.pallas_call(
        _route_kernel,
        grid=(nt,),
        in_specs=[pl.BlockSpec((tm, d), lambda i: (i, 0)),
                  pl.BlockSpec((tm, na), lambda i: (i, 0)),
                  pl.BlockSpec((tm, nb), lambda i: (i, 0)),
                  pl.BlockSpec((na, d), lambda i: (0, 0)),
                  pl.BlockSpec((nb, d), lambda i: (0, 0)),
                  pl.BlockSpec((1, d), lambda i: (0, 0)),
                  pl.BlockSpec((3, d, LANES), lambda i: (0, 0, 0))],
        out_specs=[pl.BlockSpec((tm, d), lambda i: (i, 0)),
                   pl.BlockSpec((tm, d), lambda i: (i, 0)),
                   pl.BlockSpec((tm, LANES), lambda i: (i, 0)),
                   pl.BlockSpec((1, T_ROWS, tm), lambda i: (i, 0, 0)),
                   pl.BlockSpec((1, T_ROWS, tm), lambda i: (i, 0, 0)),
                   pl.BlockSpec((1, 1, LANES), lambda i: (i, 0, 0))],
        out_shape=[jax.ShapeDtypeStruct((t, d), F32),
                   jax.ShapeDtypeStruct((t, d), BF16),
                   jax.ShapeDtypeStruct((t, LANES), F32),
                   jax.ShapeDtypeStruct((nt, T_ROWS, tm), F32),
                   jax.ShapeDtypeStruct((nt, T_ROWS, tm), F32),
                   jax.ShapeDtypeStruct((nt, 1, LANES), F32)],
        compiler_params=_cparams(("parallel",)),
        name="moe_route",
    )(h, mix_a, mix_b, wa, wb, g, wr3)


def _moe_plan(cnt, n_row_tiles):
    cnt_al = (cnt + MOE_ALIGN - 1) // MOE_ALIGN * MOE_ALIGN
    tot = jnp.sum(cnt_al, axis=0)
    cap = (tot + MOE_CHUNK + MOE_ROWS - 1) // MOE_ROWS * MOE_ROWS
    base = jnp.cumsum(cap) - cap
    offs = base[None, :] + jnp.cumsum(cnt_al, axis=0) - cnt_al
    starts = jnp.arange(n_row_tiles, dtype=jnp.int32) * MOE_ROWS
    tile_e = jnp.clip(jnp.sum(starts[:, None] >= base[None, :], axis=1) - 1, 0, N_EXPERTS - 1)
    tile_ok = starts < (base + tot)[tile_e]
    return (offs.reshape(-1).astype(jnp.int32), tile_e.astype(jnp.int32), tile_ok.astype(jnp.int32))


def _chunk_copies(n_chunks):
    return [(e, c) for e in range(N_EXPERTS) for c in range(n_chunks)]


def _dispatch_kernel(offs_ref, cnts_ref, hn_ref, selt_ref, rankt_ref, xs_in_ref, xs_ref, stage, sems):
    del xs_in_ref
    i = pl.program_id(0)
    tm = hn_ref.shape[0]
    hn = hn_ref[...]
    ridx = lax.broadcasted_iota(jnp.int32, (MOE_CHUNK, tm), 0).astype(F32)

    def copy(e, c):
        slot = e * (pl.cdiv(tm, MOE_CHUNK)) + c
        row0 = pl.multiple_of(offs_ref[i * N_EXPERTS + e] + c * MOE_CHUNK, MOE_ALIGN)
        return pltpu.make_async_copy(stage.at[slot], xs_ref.at[pl.ds(row0, MOE_CHUNK)], sems.at[slot])

    for e, c in _chunk_copies(pl.cdiv(tm, MOE_CHUNK)):
        @pl.when(c * MOE_CHUNK < cnts_ref[i * N_EXPERTS + e])
        def _(e=e, c=c):
            hit = (rankt_ref[0, e:e + 1, :] - float(c * MOE_CHUNK) == ridx) & (selt_ref[0, e:e + 1, :] > 0.5)
            rows = _dot(jnp.where(hit, 1.0, 0.0).astype(BF16), hn)
            stage[e * (pl.cdiv(tm, MOE_CHUNK)) + c] = rows.astype(BF16)
            copy(e, c).start()

    for e, c in _chunk_copies(pl.cdiv(tm, MOE_CHUNK)):
        @pl.when(c * MOE_CHUNK < cnts_ref[i * N_EXPERTS + e])
        def _(e=e, c=c):
            copy(e, c).wait()


def _moe_dispatch(offs, cnts, hn, selt, rankt, n_rows):
    t, d = hn.shape
    nt, _, tm = selt.shape
    n_slots = N_EXPERTS * (pl.cdiv(tm, MOE_CHUNK))
    xs0 = jnp.zeros((n_rows, d), BF16)
    grid_spec = pltpu.PrefetchScalarGridSpec(
        num_scalar_prefetch=2,
        grid=(nt,),
        in_specs=[pl.BlockSpec((tm, d), lambda i, *_: (i, 0)),
                  pl.BlockSpec((1, T_ROWS, tm), lambda i, *_: (i, 0, 0)),
                  pl.BlockSpec((1, T_ROWS, tm), lambda i, *_: (i, 0, 0)),
                  pl.BlockSpec(memory_space=pl.ANY)],
        out_specs=pl.BlockSpec(memory_space=pl.ANY),
        scratch_shapes=[pltpu.VMEM((n_slots, MOE_CHUNK, d), BF16), pltpu.SemaphoreType.DMA((n_slots,))],
    )
    return pl.pallas_call(
        _dispatch_kernel,
        grid_spec=grid_spec,
        out_shape=jax.ShapeDtypeStruct((n_rows, d), BF16),
        input_output_aliases={5: 0},
        compiler_params=_cparams(("arbitrary",)),
        name="moe_dispatch",
    )(offs, cnts, hn, selt, rankt, xs0)


def _expert_kernel(te_ref, ok_ref, xs_ref, wg_ref, wu_ref, wd_ref, ys_ref, acc_ref):
    s = pl.program_id(0)
    f = pl.program_id(1)
    ok = ok_ref[s] > 0

    @pl.when(ok & (f == 0))
    def _():
        acc_ref[...] = jnp.zeros_like(acc_ref)

    @pl.when(ok)
    def _():
        x = xs_ref[...]
        a = _silu(_dot(x, wg_ref[0])) * _dot(x, wu_ref[0])
        acc_ref[...] += _dot(a.astype(BF16), wd_ref[0])

    last = f == pl.num_programs(1) - 1

    @pl.when(ok & last)
    def _():
        ys_ref[...] = acc_ref[...].astype(ys_ref.dtype)

    @pl.when(jnp.logical_not(ok) & last)
    def _():
        ys_ref[...] = jnp.zeros_like(ys_ref)


def _moe_experts(tile_e, tile_ok, xs, wg, wu, wd):
    n_rows, d = xs.shape
    ff = wg.shape[2]
    tf = _pick(ff, (512, 256, 128))
    nf = ff // tf

    def f_idx(s, f, ok_ref):
        return jnp.where(ok_ref[s] > 0, f, nf - 1)

    grid_spec = pltpu.PrefetchScalarGridSpec(
        num_scalar_prefetch=2,
        grid=(n_rows // MOE_ROWS, nf),
        in_specs=[pl.BlockSpec((MOE_ROWS, d), lambda s, f, te, ok: (s, 0)),
                  pl.BlockSpec((1, d, tf), lambda s, f, te, ok: (te[s], 0, f_idx(s, f, ok))),
                  pl.BlockSpec((1, d, tf), lambda s, f, te, ok: (te[s], 0, f_idx(s, f, ok))),
                  pl.BlockSpec((1, tf, d), lambda s, f, te, ok: (te[s], f_idx(s, f, ok), 0))],
        out_specs=pl.BlockSpec((MOE_ROWS, d), lambda s, f, te, ok: (s, 0)),
        scratch_shapes=[pltpu.VMEM((MOE_ROWS, d), F32)],
    )
    return pl.pallas_call(
        _expert_kernel,
        grid_spec=grid_spec,
        out_shape=jax.ShapeDtypeStruct((n_rows, d), BF16),
        compiler_params=_cparams(("parallel", "arbitrary")),
        name="moe_experts",
    )(tile_e, tile_ok, xs, wg, wu, wd)


def _combine_kernel(offs_ref, cnts_ref, h_ref, route_ref, ys_ref, o_ref, stage, sems):
    i = pl.program_id(0)
    tm = h_ref.shape[0]
    n_chunks = pl.cdiv(tm, MOE_CHUNK)
    ridx = lax.broadcasted_iota(jnp.int32, (tm, MOE_CHUNK), 1).astype(F32)

    def copy(e, c):
        slot = e * n_chunks + c
        row0 = pl.multiple_of(offs_ref[i * N_EXPERTS + e] + c * MOE_CHUNK, MOE_ALIGN)
        return pltpu.make_async_copy(ys_ref.at[pl.ds(row0, MOE_CHUNK)], stage.at[slot], sems.at[slot])

    for e, c in _chunk_copies(n_chunks):
        @pl.when(c * MOE_CHUNK < cnts_ref[i * N_EXPERTS + e])
        def _(e=e, c=c):
            copy(e, c).start()

    o_ref[...] = h_ref[...]
    for e, c in _chunk_copies(n_chunks):
        @pl.when(c * MOE_CHUNK < cnts_ref[i * N_EXPERTS + e])
        def _(e=e, c=c):
            copy(e, c).wait()
            gate = route_ref[:, e:e + 1]
            sel = route_ref[:, SEL_LANE + e:SEL_LANE + e + 1]
            rank = route_ref[:, RANK_LANE + e:RANK_LANE + e + 1]
            hit = (rank - float(c * MOE_CHUNK) == ridx) & (sel > 0.5)
            back = _dot(jnp.where(hit, 1.0, 0.0).astype(BF16), stage[e * n_chunks + c])
            o_ref[...] += gate * back


def _moe_combine(offs, cnts, h, route, ys):
    t, d = h.shape
    tm = _pick(t, (MOE_TM, 384, 256, 128))
    n_slots = N_EXPERTS * (pl.cdiv(tm, MOE_CHUNK))
    grid_spec = pltpu.PrefetchScalarGridSpec(
        num_scalar_prefetch=2,
        grid=(t // tm,),
        in_specs=[pl.BlockSpec((tm, d), lambda i, *_: (i, 0)),
                  pl.BlockSpec((tm, LANES), lambda i, *_: (i, 0)),
                  pl.BlockSpec(memory_space=pl.ANY)],
        out_specs=pl.BlockSpec((tm, d), lambda i, *_: (i, 0)),
        scratch_shapes=[pltpu.VMEM((n_slots, MOE_CHUNK, d), BF16), pltpu.SemaphoreType.DMA((n_slots,))],
    )
    return pl.pallas_call(
        _combine_kernel,
        grid_spec=grid_spec,
        out_shape=jax.ShapeDtypeStruct((t, d), F32),
        compiler_params=_cparams(("arbitrary",)),
        name="moe_combine",
    )(offs, cnts, h, route, ys)


def _moe(h, mix_a, mix_b, wa, wb, g, wr3, wg, wu, wd):
    t, d = h.shape
    h, hn, route, selt, rankt, cnt = _moe_route(h, mix_a, mix_b, wa, wb, g, wr3)
    nt = cnt.shape[0]
    cnt = cnt[:, 0, :N_EXPERTS].astype(jnp.int32)
    worst = TOP_K * t + nt * N_EXPERTS * (MOE_ALIGN - 1) + N_EXPERTS * (MOE_CHUNK + MOE_ROWS - 1)
    n_row_tiles = -(-worst // MOE_ROWS)
    offs, tile_e, tile_ok = _moe_plan(cnt, n_row_tiles)
    cnts = cnt.reshape(-1)
    xs = _moe_dispatch(offs, cnts, hn, selt, rankt, n_row_tiles * MOE_ROWS)
    ys = _moe_experts(tile_e, tile_ok, xs, wg, wu, wd)
    return _moe_combine(offs, cnts, h, route, ys)


def _final_norm_kernel(h_ref, g_ref, o_ref):
    o_ref[0] = _rms(h_ref[0], g_ref[...])


def _final_norm(h3, g, seq):
    b, lp, d = h3.shape
    tm = PAD
    return pl.pallas_call(
        _final_norm_kernel,
        grid=(b, seq // tm),
        in_specs=[pl.BlockSpec((1, tm, d), lambda bi, i: (bi, i + PAD // tm, 0)),
                  pl.BlockSpec((1, d), lambda bi, i: (0, 0))],
        out_specs=pl.BlockSpec((1, tm, d), lambda bi, i: (bi, i, 0)),
        out_shape=jax.ShapeDtypeStruct((b, seq, d), F32),
        compiler_params=_cparams(("parallel", "parallel")),
        name="final_norm",
    )(h3, g)


def _rope_table(lp, rot_dim, theta, width, offset, scale):
    half = rot_dim // 2
    pos = (jnp.arange(lp, dtype=jnp.int32) - FIRST).astype(F32)
    inv = theta ** (-jnp.arange(half, dtype=F32) * 2.0 / rot_dim)
    ang = pos[:, None] * inv[None, :]
    cos, sin = jnp.cos(ang), jnp.sin(ang)
    ones = jnp.ones((lp, 1), F32)
    zeros = jnp.zeros((lp, 1), F32)

    def unit(first, second, fill):
        parts = [jnp.tile(fill, (1, offset)), first, second,
                 jnp.tile(fill, (1, width - offset - rot_dim))]
        return jnp.tile(jnp.concatenate(parts, axis=1), (1, LANES // width))

    c = unit(cos, cos, ones)
    s1 = unit(-sin, jnp.zeros_like(sin), zeros)
    s2 = unit(jnp.zeros_like(sin), sin, zeros)
    return jnp.concatenate([c, s1, s2], axis=1) * scale


def _pad_heads(w, heads, used, width):
    k = w.shape[0]
    w = w.reshape(k, heads, used)
    return jnp.pad(w, ((0, 0), (0, 0), (0, width - used))).reshape(k, heads * width)


def kernel(x, meta_tokens, ln_mix_e, w_in_e, ln_mla_q_e, w_mla_uq_e, ln_mla_kv_e, w_mla_ukv_e,
           w_out_e, ln_ffn_e, w_ffn_gate_e, w_ffn_up_e, w_ffn_down_e, ln_mix_o, w_in_o, b_forget_o,
           diff_lq1_o, diff_lk1_o, diff_lq2_o, diff_lk2_o, diff_subln_o, w_out_o, ln_ffn_o,
           w_router_o, w_moe_gate_o, w_moe_up_o, w_moe_down_o, ln_final):
    b, seq, d = x.shape
    assert d == D_MODEL and seq % TQ == 0
    lp = PAD + seq
    t = b * lp
    depth = 2

    meta = jnp.broadcast_to(meta_tokens.astype(x.dtype)[None], (b, N_META, d))
    h = jnp.concatenate([jnp.zeros((b, FIRST, d), x.dtype), meta, x], axis=1).reshape(t, d)

    row = lambda v: v.reshape(1, -1).astype(F32)
    sb_scale = HEAD_DIM ** -0.5
    tab_mla_q = _rope_table(lp, MLA_ROPE, MLA_THETA, LANES, MLA_NOPE,
                            (MLA_NOPE + MLA_ROPE) ** -0.5 * LOG2E)
    tab_mla_k = _rope_table(lp, MLA_ROPE, MLA_THETA, LANES, MLA_NOPE, 1.0)
    tab_diff_q = _rope_table(lp, ROT_DIM, ROPE_THETA, DIFF_DIM, 0, LOG2E)
    tab_diff_k = _rope_table(lp, ROT_DIM, ROPE_THETA, DIFF_DIM, 0, 1.0)

    for i in range(depth):
        j = i // 2
        if i % 2 == 0:
            w_in = w_in_e[j]
            lat_w = MLA_Q_RANK + MLA_KV_RANK + MLA_ROPE
            w_cat = jnp.concatenate(
                [w_in[:, :SB_W] * sb_scale, w_in[:, SB_W:3 * SB_W + lat_w],
                 jnp.zeros((d, 4 * LANES - lat_w), F32)], axis=1).astype(BF16)
            sb, lat = _even_proj(h, row(ln_mix_e[j]), w_cat)
            sb = sb.reshape(b, lp, -1)
            lat = lat.reshape(b, lp, -1)

            wq = _pad_heads(w_mla_uq_e[j], MLA_HEADS, MLA_NOPE + MLA_ROPE, LANES).astype(BF16)
            ukv = w_mla_ukv_e[j].reshape(MLA_KV_RANK, MLA_HEADS, MLA_NOPE + MLA_V)
            wk_nope = _pad_heads(ukv[:, :, :MLA_NOPE].reshape(MLA_KV_RANK, -1), MLA_HEADS, MLA_NOPE, LANES)
            place = jnp.pad(jnp.eye(MLA_ROPE, dtype=F32), ((0, 0), (MLA_NOPE, LANES - MLA_NOPE - MLA_ROPE)))
            wk_rope = jnp.tile(place, (1, MLA_HEADS))
            wk = jnp.concatenate(
                [wk_nope, wk_rope, jnp.zeros((LANES - MLA_ROPE, MLA_HEADS * LANES), F32)], axis=0).astype(BF16)
            wv = ukv[:, :, MLA_NOPE:].reshape(MLA_KV_RANK, -1).astype(BF16)
            mq, mk, mv = _mla_up(lat, row(ln_mla_q_e[j]), row(ln_mla_kv_e[j]), wq, wk, wv,
                                 tab_mla_q, tab_mla_k)

            a_out = _sb_attn(sb).reshape(t, -1)
            b_out = _mla_attn(mq, mk, mv).reshape(t, -1)
            w_out = w_out_e[j].astype(BF16)
            h = _ffn(h, a_out, b_out, w_out[:SB_W], w_out[SB_W:], row(ln_ffn_e[j]),
                     w_ffn_gate_e[j].astype(BF16), w_ffn_up_e[j].astype(BF16), w_ffn_down_e[j].astype(BF16))
        else:
            lam_init = 0.8 - 0.6 * math.exp(-0.3 * i)
            w_in = w_in_o[j]
            wf = w_in[:, N_ODD_MAIN:]
            w_cat = jnp.concatenate(
                [w_in[:, :DIFF_W] * sb_scale, w_in[:, DIFF_W:3 * DIFF_W],
                 w_in[:, 3 * DIFF_W:3 * DIFF_W + FOX_W] * sb_scale, w_in[:, 3 * DIFF_W + FOX_W:N_ODD_MAIN],
                 wf, jnp.zeros((d, LANES - FOX_HEADS), F32)], axis=1).astype(BF16)
            wft = jnp.pad(wf.T, ((0, F_ROWS - FOX_HEADS), (0, 0))).astype(BF16)
            bf = b_forget_o[j].astype(F32)
            bcol = jnp.pad(bf, (0, LANES - FOX_HEADS)).reshape(1, LANES)
            brow = jnp.pad(bf, (0, F_ROWS - FOX_HEADS)).reshape(F_ROWS, 1)
            main, fcol, frow = _odd_proj(h.reshape(b, lp, d), row(ln_mix_o[j]), w_cat, wft, bcol, brow,
                                         tab_diff_q, tab_diff_k)
            ccol, crow = _fox_cumsum(fcol, frow)
            crow4 = crow[:, :FOX_HEADS].reshape(b, FOX_HEADS // 2, 2, lp)
            lam_vecs = jnp.stack([diff_lq1_o[j], diff_lk1_o[j], diff_lq2_o[j], diff_lk2_o[j]]).astype(F32)
            c_out = _diff_attn(main, lam_vecs, row(diff_subln_o[j]), lam_init).reshape(t, -1)
            d_out = _fox_attn(main, ccol, crow4).reshape(t, -1)
            w_out = w_out_o[j].astype(BF16)
            wr = jnp.pad(w_router_o[j].astype(F32), ((0, 0), (0, LANES - N_EXPERTS)))
            wr3 = jnp.stack(_split3(wr))
            h = _moe(h, c_out, d_out, w_out[:DIFF_W], w_out[DIFF_W:], row(ln_ffn_o[j]), wr3,
                     w_moe_gate_o[j].astype(BF16), w_moe_up_o[j].astype(BF16), w_moe_down_o[j].astype(BF16))

    return _final_norm(h.reshape(b, lp, d), row(ln_final), seq)
```

```python
import functools
import math

import jax
import jax.numpy as jnp
from jax import lax
from jax.experimental import pallas as pl
from jax.experimental.pallas import tpu as pltpu

F32 = jnp.float32
BF16 = jnp.bfloat16

D_MODEL = 1024
CHUNK = 64
N_META = 16
EPS = 1e-6
LOG2E = 1.4426950408889634
NEG_INF = -1e30

HEAD_DIM = 64
SB_HEADS = 8
MLA_HEADS = 8
MLA_NOPE = 64
MLA_ROPE = 32
MLA_V = 64
MLA_Q_RANK = 256
MLA_KV_RANK = 128
MLA_THETA = 10000.0
DIFF_HEADS = 4
DIFF_DIM = 64
FOX_HEADS = 8
ROPE_THETA = 500000.0
ROT_DIM = HEAD_DIM // 4
N_EXPERTS = 8
TOP_K = 2

SB_W = SB_HEADS * HEAD_DIM
DIFF_W = DIFF_HEADS * 2 * DIFF_DIM
FOX_W = FOX_HEADS * HEAD_DIM

LANES = 128
PAD = 128
FIRST = PAD - N_META
TQ = 128
KEY_WINDOW = 512
Q_GROUP = 4
SB_Q_GROUP = 2
F32_EXP2_UNDERFLOW = 160.0
SUFFIX_CHUNK = 256
CHUNK_SHIFT = CHUNK.bit_length() - 1
VMEM_LIMIT = 56 * 1024 * 1024


def _cparams(sem):
    return pltpu.CompilerParams(dimension_semantics=sem, vmem_limit_bytes=VMEM_LIMIT)


def _pick(n, cands):
    for c in cands:
        if n % c == 0:
            return c
    raise ValueError(f"no tile in {cands} divides {n}")


def _rms(x, g):
    return x * lax.rsqrt(jnp.mean(x * x, axis=-1, keepdims=True) + EPS) * g


def _dot(a, b):
    return jnp.dot(a, b, preferred_element_type=F32)


def _dot_nt(a, b):
    return lax.dot_general(a, b, (((1,), (1,)), ((), ())), preferred_element_type=F32)


def _rope_chunk(y, tab, shift):
    c = tab[:, 0:LANES]
    s1 = tab[:, LANES:2 * LANES]
    s2 = tab[:, 2 * LANES:3 * LANES]
    return (y * c + pltpu.roll(y, LANES - shift, 1) * s1 + pltpu.roll(y, shift, 1) * s2)


def _split3(x):
    a = x.astype(BF16)
    r = x - a.astype(F32)
    b = r.astype(BF16)
    c = (r - b.astype(F32)).astype(BF16)
    return a, b, c


def _even_proj_kernel(h_ref, g_ref, w_ref, sb_ref, lat_ref):
    hn = _rms(h_ref[...], g_ref[...]).astype(BF16)
    y = _dot(hn, w_ref[...])
    n_sb = sb_ref.shape[-1]
    sb_ref[:, :SB_W] = (y[:, :SB_W] * LOG2E).astype(BF16)
    sb_ref[:, SB_W:] = y[:, SB_W:n_sb].astype(BF16)
    lat_ref[...] = y[:, n_sb:].astype(BF16)


def _even_proj(h, g, w):
    t, d = h.shape
    n = w.shape[1]
    n_sb = 3 * SB_W
    tm = _pick(t, (512, 384, 256, 128))
    return pl.pallas_call(
        _even_proj_kernel,
        grid=(t // tm,),
        in_specs=[pl.BlockSpec((tm, d), lambda i: (i, 0)),
                  pl.BlockSpec((1, d), lambda i: (0, 0)),
                  pl.BlockSpec((d, n), lambda i: (0, 0))],
        out_specs=[pl.BlockSpec((tm, n_sb), lambda i: (i, 0)),
                   pl.BlockSpec((tm, n - n_sb), lambda i: (i, 0))],
        out_shape=[jax.ShapeDtypeStruct((t, n_sb), BF16),
                   jax.ShapeDtypeStruct((t, n - n_sb), BF16)],
        compiler_params=_cparams(("parallel",)),
        name="even_proj",
    )(h, g, w)


def _mla_up_kernel(lat_ref, gq_ref, gkv_ref, wq_ref, wk_ref, wv_ref, tq_ref, tk_ref,
                   q_ref, k_ref, v_ref):
    lat = lat_ref[0]
    nq = _rms(lat[:, :MLA_Q_RANK].astype(F32), gq_ref[...]).astype(BF16)
    nkv = _rms(lat[:, MLA_Q_RANK:MLA_Q_RANK + MLA_KV_RANK].astype(F32), gkv_ref[...]).astype(BF16)
    yq = _dot(nq, wq_ref[...])
    xk = jnp.concatenate([nkv, lat[:, MLA_Q_RANK + MLA_KV_RANK:]], axis=1)
    yk = _dot(xk, wk_ref[...])
    v_ref[0] = _dot(nkv, wv_ref[...]).astype(BF16)
    tabq = tq_ref[...]
    tabk = tk_ref[...]
    for hd in range(MLA_HEADS):
        sl = slice(hd * LANES, (hd + 1) * LANES)
        q_ref[0, :, sl] = _rope_chunk(yq[:, sl], tabq, MLA_ROPE // 2).astype(BF16)
        k_ref[0, :, sl] = _rope_chunk(yk[:, sl], tabk, MLA_ROPE // 2).astype(BF16)


def _mla_up(lat, gq, gkv, wq, wk, wv, tabq, tabk):
    b, lp, nl = lat.shape
    tm = _pick(lp, (384, 128))
    nqk = MLA_HEADS * LANES
    nv = MLA_HEADS * MLA_V
    full = lambda shp: pl.BlockSpec(shp, lambda bi, i: (0,) * len(shp))
    return pl.pallas_call(
        _mla_up_kernel,
        grid=(b, lp // tm),
        in_specs=[pl.BlockSpec((1, tm, nl), lambda bi, i: (bi, i, 0)),
                  full(gq.shape), full(gkv.shape), full(wq.shape), full(wk.shape), full(wv.shape),
                  pl.BlockSpec((tm, 3 * LANES), lambda bi, i: (i, 0)),
                  pl.BlockSpec((tm, 3 * LANES), lambda bi, i: (i, 0))],
        out_specs=[pl.BlockSpec((1, tm, nqk), lambda bi, i: (bi, i, 0)),
                   pl.BlockSpec((1, tm, nqk), lambda bi, i: (bi, i, 0)),
                   pl.BlockSpec((1, tm, nv), lambda bi, i: (bi, i, 0))],
        out_shape=[jax.ShapeDtypeStruct((b, lp, nqk), BF16),
                   jax.ShapeDtypeStruct((b, lp, nqk), BF16),
                   jax.ShapeDtypeStruct((b, lp, nv), BF16)],
        compiler_params=_cparams(("parallel", "parallel")),
        name="mla_up",
    )(lat, gq, gkv, wq, wk, wv, tabq, tabk)


N_ODD_MAIN = 3 * DIFF_W + 3 * FOX_W
F_ROWS = 16


def _odd_proj_kernel(h_ref, g_ref, w_ref, wft_ref, bcol_ref, brow_ref, tabq_ref, tabk_ref,
                     main_ref, fcol_ref, frow_ref):
    hn = _rms(h_ref[0], g_ref[...]).astype(BF16)
    y = _dot(hn, w_ref[...])
    n_rope = DIFF_W // LANES
    for c in range(2 * n_rope):
        sl = slice(c * LANES, (c + 1) * LANES)
        tab = tabq_ref[...] if c < n_rope else tabk_ref[...]
        main_ref[0, :, sl] = _rope_chunk(y[:, sl], tab, ROT_DIM // 2).astype(BF16)
    fq = slice(3 * DIFF_W, 3 * DIFF_W + FOX_W)
    main_ref[0, :, 2 * DIFF_W:3 * DIFF_W] = y[:, 2 * DIFF_W:3 * DIFF_W].astype(BF16)
    main_ref[0, :, fq] = (y[:, fq] * LOG2E).astype(BF16)
    main_ref[0, :, 3 * DIFF_W + FOX_W:] = y[:, 3 * DIFF_W + FOX_W:N_ODD_MAIN].astype(BF16)
    fcol_ref[0] = y[:, N_ODD_MAIN:] + bcol_ref[...]
    frow_ref[0] = _dot_nt(wft_ref[...], hn) + brow_ref[...]


def _odd_proj(h3, g, w, wft, bcol, brow, tabq, tabk):
    b, lp, d = h3.shape
    tm = _pick(lp, (384, 128))
    full = lambda shp: pl.BlockSpec(shp, lambda bi, i: (0,) * len(shp))
    return pl.pallas_call(
        _odd_proj_kernel,
        grid=(b, lp // tm),
        in_specs=[pl.BlockSpec((1, tm, d), lambda bi, i: (bi, i, 0)),
                  full(g.shape), full(w.shape), full(wft.shape), full(bcol.shape), full(brow.shape),
                  pl.BlockSpec((tm, 3 * LANES), lambda bi, i: (i, 0)),
                  pl.BlockSpec((tm, 3 * LANES), lambda bi, i: (i, 0))],
        out_specs=[pl.BlockSpec((1, tm, N_ODD_MAIN), lambda bi, i: (bi, i, 0)),
                   pl.BlockSpec((1, tm, LANES), lambda bi, i: (bi, i, 0)),
                   pl.BlockSpec((1, F_ROWS, tm), lambda bi, i: (bi, 0, i))],
        out_shape=[jax.ShapeDtypeStruct((b, lp, N_ODD_MAIN), BF16),
                   jax.ShapeDtypeStruct((b, lp, LANES), F32),
                   jax.ShapeDtypeStruct((b, F_ROWS, lp), F32)],
        compiler_params=_cparams(("parallel", "parallel")),
        name="odd_proj",
    )(h3, g, w, wft, bcol, brow, tabq, tabk)


def _log_sigmoid(x):
    return jnp.minimum(x, 0.0) - jnp.log(1.0 + jnp.exp(-jnp.abs(x)))


def _fox_cumsum_kernel(fcol_ref, frow_ref, ccol_ref, crow_ref):
    lp = fcol_ref.shape[1]
    blk = LANES
    r = lax.broadcasted_iota(jnp.int32, (blk, blk), 0)
    c = lax.broadcasted_iota(jnp.int32, (blk, blk), 1)
    tri_l = (c <= r).astype(BF16)
    tri_u = (r <= c).astype(BF16)
    row_id = lax.broadcasted_iota(jnp.int32, (blk, LANES), 0)
    col_id = lax.broadcasted_iota(jnp.int32, (F_ROWS, blk), 1)
    carry_c = jnp.zeros((1, LANES), F32)
    carry_r = jnp.zeros((F_ROWS, 1), F32)
    for j in range(lp // blk):
        sl = slice(j * blk, (j + 1) * blk)
        ls = _log_sigmoid(fcol_ref[0, sl, :])
        ls = jnp.where(row_id + j * blk >= FIRST, ls, 0.0)
        a, b, c3 = _split3(ls)
        cs = _dot(tri_l, a) + _dot(tri_l, b) + _dot(tri_l, c3) + carry_c
        ccol_ref[0, sl, :] = cs * LOG2E
        carry_c = cs[blk - 1:blk, :]
        lr = _log_sigmoid(frow_ref[0, :, sl])
        lr = jnp.where(col_id + j * blk >= FIRST, lr, 0.0)
        a, b, c3 = _split3(lr)
        cr = _dot(a, tri_u) + _dot(b, tri_u) + _dot(c3, tri_u) + carry_r
        crow_ref[0, :, sl] = cr * LOG2E
        carry_r = cr[:, blk - 1:blk]


def _fox_cumsum(fcol, frow):
    b, lp, _ = fcol.shape
    return pl.pallas_call(
        _fox_cumsum_kernel,
        grid=(b,),
        in_specs=[pl.BlockSpec((1, lp, LANES), lambda bi: (bi, 0, 0)),
                  pl.BlockSpec((1, F_ROWS, lp), lambda bi: (bi, 0, 0))],
        out_specs=[pl.BlockSpec((1, lp, LANES), lambda bi: (bi, 0, 0)),
                   pl.BlockSpec((1, F_ROWS, lp), lambda bi: (bi, 0, 0))],
        out_shape=[jax.ShapeDtypeStruct((b, lp, LANES), F32),
                   jax.ShapeDtypeStruct((b, F_ROWS, lp), F32)],
        compiler_params=_cparams(("parallel",)),
        name="fox_cumsum",
    )(fcol, frow)


def _rows(i, n):
    if isinstance(i, int):
        return pl.ds(i * n, n)
    return pl.ds(pl.multiple_of(i * n, n), n)


def _keys(k0, w):
    if isinstance(k0, int):
        return pl.ds(k0, w)
    return pl.ds(pl.multiple_of(k0, TQ), w)


def _visible(kind, ks, qs):
    if kind == "chunk":
        return (ks >> CHUNK_SHIFT) <= (qs >> CHUNK_SHIFT)
    if kind == "causal":
        return ks <= qs
    return ks < qs


def _first_window_mask(kind, q0, w, lim):
    r = lax.broadcasted_iota(jnp.int32, (2 * TQ, w), 0)
    ks = lax.broadcasted_iota(jnp.int32, (2 * TQ, w), 1)
    return _visible(kind, ks, q0 + (r & (TQ - 1))) & (ks >= FIRST) & (ks < lim)


def _diag_window_mask(kind, w):
    r = lax.broadcasted_iota(jnp.int32, (2 * TQ, w), 0)
    ks = lax.broadcasted_iota(jnp.int32, (2 * TQ, w), 1) + (TQ - w)
    return _visible(kind, ks, r & (TQ - 1))


def _span_mask(w, lim):
    ks = lax.broadcasted_iota(jnp.int32, (1, w), 1)
    return (ks >= FIRST) & (ks < lim)


def _lane_halves():
    lane = lax.broadcasted_iota(jnp.int32, (TQ, LANES), 1)
    return lane < (LANES // 2)


def _stack_masked(q):
    lo = _lane_halves()
    z = jnp.zeros_like(q)
    return jnp.concatenate([jnp.where(lo, q, z), jnp.where(lo, z, q)], axis=0)


def _softmax_sweep(kind, w, prep, scores_fn, v_fn, dbias_ref):
    def step(carry, s, v):
        m, acc = carry
        m_new = jnp.maximum(m, jnp.max(s, axis=-1, keepdims=True))
        alpha = jnp.exp2(m - m_new)
        p = jnp.exp2(s - m_new)
        ones = jnp.ones((v.shape[0], LANES), BF16)
        acc = alpha * acc + _dot(p.astype(BF16), jnp.concatenate([v, ones], axis=1))
        return m_new, acc

    def run(qi, mwin):
        g = w // TQ
        ctx = prep(qi)
        q0 = qi * TQ
        pre = qi + 1 - mwin * g
        n0 = w
        carry = (jnp.full((2 * TQ, 1), NEG_INF, F32), jnp.zeros((2 * TQ, 2 * LANES), F32))
        s = scores_fn(ctx, 0, n0)
        if mwin == 0:
            s = jnp.where(_first_window_mask(kind, q0, n0, pre * TQ), s, NEG_INF)
            return step(carry, s, v_fn(0, n0))[1]
        carry = step(carry, s + jnp.where(_span_mask(n0, pre * TQ), 0.0, NEG_INF), v_fn(0, n0))
        for t in range(1, mwin):
            k0 = (pre + (t - 1) * g) * TQ
            carry = step(carry, scores_fn(ctx, k0, w), v_fn(k0, w))
        k0 = q0 + TQ - w
        return step(carry, scores_fn(ctx, k0, w) + dbias_ref[...], v_fn(k0, w))[1]

    return run


def _for_each_query_group(nq, w, q_group, group_fn):
    g = w // TQ
    one = jnp.sign(pl.program_id(0) + 1)
    for mwin in range(-(-nq // g)):
        lo, hi = mwin * g, min((mwin + 1) * g, nq)

        n_groups = (hi - lo) // q_group

        def body(it, _, mwin=mwin, lo=lo):
            qi = lo + q_group * it
            group_fn([qi + j for j in range(q_group)], mwin)
            return 0

        if n_groups:
            lax.fori_loop(0, n_groups * one, body, 0)
        for qi in range(lo + n_groups * q_group, hi):
            group_fn([qi], mwin)


def _for_each_query_block(nq, w, run, emit):
    def group_fn(qis, mwin):
        results = [run(qi, mwin) for qi in qis]
        for qi, res in zip(qis, results):
            emit(qi, res)

    _for_each_query_group(nq, w, Q_GROUP, group_fn)


def _key_window(lp):
    return min(KEY_WINDOW, lp)


def _merge_heads(top, bottom):
    return jnp.where(_lane_halves(), top, bottom)


def _attn_call(kernel, name, b, lp, n_blocks, in_specs, args, n_masks=1):
    w = _key_window(lp)
    return pl.pallas_call(
        kernel,
        grid=(b, n_blocks),
        in_specs=in_specs,
        out_specs=pl.BlockSpec((1, lp, LANES), lambda bi, hp: (bi, 0, hp)),
        out_shape=jax.ShapeDtypeStruct((b, lp, n_blocks * LANES), BF16),
        scratch_shapes=[pltpu.VMEM((2 * TQ, w), F32)] * n_masks,
        compiler_params=_cparams(("parallel", "parallel")),
        name=name,
    )(*args)


def _normalised(acc):
    return acc[:, :LANES] / acc[:, LANES:]


def _col_spec(lp, off):
    return pl.BlockSpec((1, lp, LANES), lambda bi, hp: (bi, 0, off + hp))


def _mla_attn_kernel(q_ref, k_ref, v_ref, o_ref, dbias_ref):
    lp = q_ref.shape[1]
    w = _key_window(lp)
    dbias_ref[...] = jnp.where(_diag_window_mask("chunk", w), 0.0, NEG_INF)

    def prep(qi):
        return q_ref[0, _rows(qi, TQ), 0:LANES], q_ref[0, _rows(qi, TQ), LANES:2 * LANES]

    def scores(ctx, k0, n):
        qa, qb = ctx
        ka = k_ref[0, _keys(k0, n), 0:LANES]
        kb = k_ref[0, _keys(k0, n), LANES:2 * LANES]
        return jnp.concatenate([_dot_nt(qa, ka), _dot_nt(qb, kb)], axis=0)

    run = _softmax_sweep("chunk", w, prep, scores, lambda k0, n: v_ref[0, _keys(k0, n), :], dbias_ref)

    def emit(qi, acc):
        o = _normalised(acc)
        o_ref[0, _rows(qi, TQ), :] = _merge_heads(o[:TQ], o[TQ:]).astype(o_ref.dtype)

    _for_each_query_block(lp // TQ, w, run, emit)


def _mla_attn(q, k, v):
    b, lp, _ = q.shape
    wide = lambda: pl.BlockSpec((1, lp, 2 * LANES), lambda bi, hp: (bi, 0, hp))
    return _attn_call(_mla_attn_kernel, "mla_attn", b, lp, MLA_HEADS // 2,
                      [wide(), wide(), _col_spec(lp, 0)], (q, k, v))


def _diff_attn_kernel(lam_init, x_q, x_k, x_v, lam_ref, gsub_ref, o_ref, dbias_ref):
    lp = x_q.shape[1]
    w = _key_window(lp)
    dbias_ref[...] = jnp.where(_diag_window_mask("chunk", w), 0.0, NEG_INF)
    lv = lam_ref[...]
    lam = (jnp.exp(jnp.sum(lv[0:1] * lv[1:2], axis=-1, keepdims=True))
           - jnp.exp(jnp.sum(lv[2:3] * lv[3:4], axis=-1, keepdims=True)) + lam_init)
    gsub = gsub_ref[...]

    def prep(qi):
        return _stack_masked(x_q[0, _rows(qi, TQ), :])

    def scores(qs, k0, n):
        return _dot_nt(qs, x_k[0, _keys(k0, n), :])

    run = _softmax_sweep("chunk", w, prep, scores, lambda k0, n: x_v[0, _keys(k0, n), :], dbias_ref)

    def emit(qi, acc):
        o = _normalised(acc)
        o = o[:TQ] - lam * o[TQ:]
        o_ref[0, _rows(qi, TQ), :] = (_rms(o, gsub) * (1.0 - lam_init)).astype(o_ref.dtype)

    _for_each_query_block(lp // TQ, w, run, emit)


def _diff_attn(main, lam_vecs, gsub, lam_init):
    b, lp, _ = main.shape
    nb = DIFF_W // LANES
    full = lambda shp: pl.BlockSpec(shp, lambda bi, hp: (0,) * len(shp))
    return _attn_call(functools.partial(_diff_attn_kernel, lam_init), "diff_attn", b, lp, DIFF_HEADS,
                      [_col_spec(lp, 0), _col_spec(lp, nb), _col_spec(lp, 2 * nb),
                       full(lam_vecs.shape), full(gsub.shape)],
                      (main, main, main, lam_vecs, gsub))


def _fox_attn_kernel(x_q, x_k, x_v, ccol_ref, crow_ref, o_ref, dbias_ref):
    lp = x_q.shape[1]
    w = _key_window(lp)
    dbias_ref[...] = jnp.where(_diag_window_mask("causal", w), 0.0, NEG_INF)
    hp = pl.program_id(1)
    lane = lax.broadcasted_iota(jnp.int32, (TQ, LANES), 1)

    def prep(qi):
        cc = ccol_ref[0, _rows(qi, TQ), :]
        cq0 = jnp.sum(jnp.where(lane == 2 * hp, cc, 0.0), axis=-1, keepdims=True)
        cq1 = jnp.sum(jnp.where(lane == 2 * hp + 1, cc, 0.0), axis=-1, keepdims=True)
        return _stack_masked(x_q[0, _rows(qi, TQ), :]), cq0, cq1

    def scores(ctx, k0, n):
        qs, cq0, cq1 = ctx
        s = _dot_nt(qs, x_k[0, _keys(k0, n), :])
        ck = crow_ref[0, 0, :, _keys(k0, n)]
        return s + jnp.concatenate([cq0 - ck[0:1, :], cq1 - ck[1:2, :]], axis=0)

    run = _softmax_sweep("causal", w, prep, scores, lambda k0, n: x_v[0, _keys(k0, n), :], dbias_ref)

    def emit(qi, acc):
        o = _normalised(acc)
        o_ref[0, _rows(qi, TQ), :] = _merge_heads(o[:TQ], o[TQ:]).astype(o_ref.dtype)

    _for_each_query_block(lp // TQ, w, run, emit)


def _fox_attn(main, ccol, crow4):
    b, lp, _ = main.shape
    off = 3 * DIFF_W // LANES
    nb = FOX_W // LANES
    return _attn_call(_fox_attn_kernel, "fox_attn", b, lp, nb,
                      [_col_spec(lp, off), _col_spec(lp, off + nb), _col_spec(lp, off + 2 * nb),
                       pl.BlockSpec((1, lp, LANES), lambda bi, hp: (bi, 0, 0)),
                       pl.BlockSpec((1, 1, 2, lp), lambda bi, hp: (bi, hp, 0, 0))],
                      (main, main, main, ccol, crow4))


def _sb_attn_kernel(x_q, x_k, x_v, o_ref, dkeep_ref, dbias_ref):
    lp = x_q.shape[1]
    w = _key_window(lp)
    g = w // TQ
    cs = SUFFIX_CHUNK if w % SUFFIX_CHUNK == 0 else TQ
    kr = lax.broadcasted_iota(jnp.int32, (cs, cs), 0)
    kc = lax.broadcasted_iota(jnp.int32, (cs, cs), 1)
    later = jnp.where(kr > kc, 1.0, 0.0).astype(BF16)
    dvis = _diag_window_mask("strict", w)
    dkeep_ref[...] = jnp.where(dvis, 1.0, 0.0)
    dbias_ref[...] = jnp.where(dvis, 0.0, NEG_INF)

    def step(carry, qs, k0, keep=None, bias=None, vis=None):
        rsum, acc = carry
        z = _dot_nt(qs, x_k[0, _keys(k0, w), :])
        cost = jnp.maximum(z, 0.0) + jnp.log2(1.0 + jnp.exp2(-jnp.abs(z)))
        if vis is not None:
            cost = jnp.where(vis, cost, 0.0)
        if keep is not None:
            cost = cost * keep
        cb = cost.astype(BF16)
        parts = []
        for c in reversed(range(w // cs)):
            sl = slice(c * cs, (c + 1) * cs)
            inner = _dot(cb[:, sl], later)
            parts.append(inner + rsum)
            rsum = rsum + inner[:, 0:1] + cb[:, c * cs:c * cs + 1].astype(F32)
        arg = z - cost - jnp.concatenate(parts[::-1], axis=1)
        if bias is not None:
            arg = arg + bias
        wgt = jnp.exp2(arg)
        if vis is not None:
            wgt = jnp.where(vis, wgt, 0.0)
        acc = acc + _dot(wgt.astype(BF16), x_v[0, _keys(k0, w), :])
        return rsum, acc

    def knorm(j, best):
        kk = x_k[0, _rows(j, TQ), :].astype(F32)
        return jnp.maximum(best, jnp.sum(kk * kk, axis=-1, keepdims=True))

    kn2 = jnp.max(lax.fori_loop(0, lp // TQ, knorm, jnp.zeros((TQ, 1), F32)), axis=0, keepdims=True)

    def emit(qi, acc):
        o_ref[0, _rows(qi, TQ), :] = _merge_heads(acc[:TQ], acc[TQ:]).astype(o_ref.dtype)

    def group_fn(qis, mwin):
        qss = [_stack_masked(x_q[0, _rows(qi, TQ), :]) for qi in qis]
        pres = [qi + 1 - mwin * g for qi in qis]
        zero = (jnp.zeros((2 * TQ, 1), F32), jnp.zeros((2 * TQ, LANES), F32))
        if mwin == 0:
            for qi, qs, pre in zip(qis, qss, pres):
                emit(qi, step(zero, qs, 0, vis=_first_window_mask("strict", qi * TQ, w, pre * TQ))[1])
            return
        carries = [step(zero, qs, qi * TQ + TQ - w, keep=dkeep_ref[...], bias=dbias_ref[...])
                   for qi, qs in zip(qis, qss)]

        def rest(cs):
            out = []
            for qs, pre, carry in zip(qss, pres, cs):
                for t in range(mwin - 1, 0, -1):
                    carry = step(carry, qs, (pre + (t - 1) * g) * TQ)
                span = _span_mask(w, pre * TQ)
                out.append(step(carry, qs, 0, keep=jnp.where(span, 1.0, 0.0),
                                bias=jnp.where(span, 0.0, NEG_INF)))
            return tuple(out)

        slack = [jnp.max(jnp.sqrt(jnp.sum(jnp.square(qs.astype(F32)), axis=-1, keepdims=True) * kn2)
                         - carry[0]) for qs, carry in zip(qss, carries)]
        live = functools.reduce(jnp.maximum, slack) > -F32_EXP2_UNDERFLOW
        carries = lax.cond(live, rest, lambda cs: cs, tuple(carries))
        for qi, carry in zip(qis, carries):
            emit(qi, carry[1])

    _for_each_query_group(lp // TQ, w, SB_Q_GROUP, group_fn)


def _sb_attn(sb):
    b, lp, _ = sb.shape
    nb = SB_W // LANES
    return _attn_call(_sb_attn_kernel, "sb_attn", b, lp, nb,
                      [_col_spec(lp, 0), _col_spec(lp, nb), _col_spec(lp, 2 * nb)],
                      (sb, sb, sb), n_masks=2)


def _silu(x):
    return x / (1.0 + jnp.exp(-x))


def _mixer_residual(h_ref, a_ref, b_ref, wa_ref, wb_ref):
    return h_ref[...] + _dot(a_ref[...], wa_ref[...]) + _dot(b_ref[...], wb_ref[...])


def _ffn_kernel(h_ref, a_ref, b_ref, wa_ref, wb_ref, g_ref, wg_ref, wu_ref, wd_ref, o_ref, hn_ref, acc_ref):
    f = pl.program_id(1)

    @pl.when(f == 0)
    def _():
        hmid = _mixer_residual(h_ref, a_ref, b_ref, wa_ref, wb_ref)
        hn_ref[...] = _rms(hmid, g_ref[...]).astype(BF16)
        acc_ref[...] = hmid

    hn = hn_ref[...]
    a = _silu(_dot(hn, wg_ref[...])) * _dot(hn, wu_ref[...])
    acc_ref[...] += _dot(a.astype(BF16), wd_ref[...])

    @pl.when(f == pl.num_programs(1) - 1)
    def _():
        o_ref[...] = acc_ref[...]


def _ffn(h, mix_a, mix_b, wa, wb, g, wg, wu, wd):
    t, d = h.shape
    ff = wg.shape[1]
    na, nb = mix_a.shape[1], mix_b.shape[1]
    tm = _pick(t, (512, 384, 256, 128))
    tf = _pick(ff, (1408, 512, 256, 128))
    return pl.pallas_call(
        _ffn_kernel,
        grid=(t // tm, ff // tf),
        in_specs=[pl.BlockSpec((tm, d), lambda i, f: (i, 0)),
                  pl.BlockSpec((tm, na), lambda i, f: (i, 0)),
                  pl.BlockSpec((tm, nb), lambda i, f: (i, 0)),
                  pl.BlockSpec((na, d), lambda i, f: (0, 0)),
                  pl.BlockSpec((nb, d), lambda i, f: (0, 0)),
                  pl.BlockSpec((1, d), lambda i, f: (0, 0)),
                  pl.BlockSpec((d, tf), lambda i, f: (0, f)),
                  pl.BlockSpec((d, tf), lambda i, f: (0, f)),
                  pl.BlockSpec((tf, d), lambda i, f: (f, 0))],
        out_specs=pl.BlockSpec((tm, d), lambda i, f: (i, 0)),
        out_shape=jax.ShapeDtypeStruct((t, d), F32),
        scratch_shapes=[pltpu.VMEM((tm, d), BF16), pltpu.VMEM((tm, d), F32)],
        compiler_params=_cparams(("parallel", "arbitrary")),
        name="ffn",
    )(h, mix_a, mix_b, wa, wb, g, wg, wu, wd)


def _router_gates(hn, wr_ref):
    a, b, c = _split3(hn)
    wa, wb, wc = wr_ref[0], wr_ref[1], wr_ref[2]
    logits = (_dot(a, wa) + _dot(a, wb) + _dot(b, wa) + _dot(a, wc) + _dot(b, wb) + _dot(c, wa))
    lane = lax.broadcasted_iota(jnp.int32, logits.shape, 1).astype(F32)
    logits = jnp.where(lane < N_EXPERTS, logits, -jnp.inf)
    v1 = jnp.max(logits, axis=-1, keepdims=True)
    i1 = jnp.min(jnp.where(logits == v1, lane, float(LANES)), axis=-1, keepdims=True)
    rest = jnp.where(lane == i1, -jnp.inf, logits)
    v2 = jnp.max(rest, axis=-1, keepdims=True)
    i2 = jnp.min(jnp.where(rest == v2, lane, float(LANES)), axis=-1, keepdims=True)
    e2 = jnp.exp(v2 - v1)
    den = 1.0 + e2
    gate = jnp.where(lane == i1, 1.0 / den, 0.0) + jnp.where(lane == i2, e2 / den, 0.0)
    sel = jnp.where((lane == i1) | (lane == i2), 1.0, 0.0)
    return gate, sel


MOE_TM = 512
MOE_CHUNK = 256
MOE_ALIGN = 16
MOE_ROWS = 1024
SEL_LANE = N_EXPERTS
RANK_LANE = 2 * N_EXPERTS
T_ROWS = 16


def _route_kernel(h_ref, a_ref, b_ref, wa_ref, wb_ref, g_ref, wr_ref,
                  hmid_ref, hn_ref, route_ref, selt_ref, rankt_ref, cnt_ref):
    tm = h_ref.shape[0]
    hmid = _mixer_residual(h_ref, a_ref, b_ref, wa_ref, wb_ref)
    hmid_ref[...] = hmid
    hn = _rms(hmid, g_ref[...])
    hn_ref[...] = hn.astype(BF16)
    gate, sel = _router_gates(hn, wr_ref)
    selb = sel.astype(BF16)
    r = lax.broadcasted_iota(jnp.int32, (tm, tm), 0)
    c = lax.broadcasted_iota(jnp.int32, (tm, tm), 1)
    rank = _dot(jnp.where(c < r, 1.0, 0.0).astype(BF16), selb)
    route_ref[...] = gate + pltpu.roll(sel, SEL_LANE, 1) + pltpu.roll(rank, RANK_LANE, 1)
    er = lax.broadcasted_iota(jnp.int32, (T_ROWS, LANES), 0)
    el = lax.broadcasted_iota(jnp.int32, (T_ROWS, LANES), 1)
    pick = jnp.where((er == el) & (er < N_EXPERTS), 1.0, 0.0).astype(BF16)
    selt = _dot_nt(pick, selb)
    selt_ref[0] = selt
    rankt_ref[0] = _dot(selt.astype(BF16), jnp.where(r < c, 1.0, 0.0).astype(BF16))
    cnt_ref[0] = jnp.sum(sel, axis=0, keepdims=True)


def _moe_route(h, mix_a, mix_b, wa, wb, g, wr3):
    t, d = h.shape
    na, nb = mix_a.shape[1], mix_b.shape[1]
    tm = _pick(t, (MOE_TM, 384, 256, 128))
    nt = t // tm
    return pl.pallas_call(
        _route_kernel,
        grid=(nt,),
        in_specs=[pl.BlockSpec((tm, d), lambda i: (i, 0)),
                  pl.BlockSpec((tm, na), lambda i: (i, 0)),
                  pl.BlockSpec((tm, nb), lambda i: (i, 0)),
                  pl.BlockSpec((na, d), lambda i: (0, 0)),
                  pl.BlockSpec((nb, d), lambda i: (0, 0)),
                  pl.BlockSpec((1, d), lambda i: (0, 0)),
                  pl.BlockSpec((3, d, LANES), lambda i: (0, 0, 0))],
        out_specs=[pl.BlockSpec((tm, d), lambda i: (i, 0)),
                   pl.BlockSpec((tm, d), lambda i: (i, 0)),
                   pl.BlockSpec((tm, LANES), lambda i: (i, 0)),
                   pl.BlockSpec((1, T_ROWS, tm), lambda i: (i, 0, 0)),
                   pl.BlockSpec((1, T_ROWS, tm), lambda i: (i, 0, 0)),
                   pl.BlockSpec((1, 1, LANES), lambda i: (i, 0, 0))],
        out_shape=[jax.ShapeDtypeStruct((t, d), F32),
                   jax.ShapeDtypeStruct((t, d), BF16),
                   jax.ShapeDtypeStruct((t, LANES), F32),
                   jax.ShapeDtypeStruct((nt, T_ROWS, tm), F32),
                   jax.ShapeDtypeStruct((nt, T_ROWS, tm), F32),
                   jax.ShapeDtypeStruct((nt, 1, LANES), F32)],
        compiler_params=_cparams(("parallel",)),
        name="moe_route",
    )(h, mix_a, mix_b, wa, wb, g, wr3)


def _moe_plan(cnt, n_row_tiles):
    cnt_al = (cnt + MOE_ALIGN - 1) // MOE_ALIGN * MOE_ALIGN
    tot = jnp.sum(cnt_al, axis=0)
    cap = (tot + MOE_CHUNK + MOE_ROWS - 1) // MOE_ROWS * MOE_ROWS
    base = jnp.cumsum(cap) - cap
    offs = base[None, :] + jnp.cumsum(cnt_al, axis=0) - cnt_al
    starts = jnp.arange(n_row_tiles, dtype=jnp.int32) * MOE_ROWS
    tile_e = jnp.clip(jnp.sum(starts[:, None] >= base[None, :], axis=1) - 1, 0, N_EXPERTS - 1)
    tile_ok = starts < (base + tot)[tile_e]
    return (offs.reshape(-1).astype(jnp.int32), tile_e.astype(jnp.int32), tile_ok.astype(jnp.int32))


def _chunk_copies(n_chunks):
    return [(e, c) for e in range(N_EXPERTS) for c in range(n_chunks)]


def _dispatch_kernel(offs_ref, cnts_ref, hn_ref, selt_ref, rankt_ref, xs_in_ref, xs_ref, stage, sems):
    del xs_in_ref
    i = pl.program_id(0)
    tm = hn_ref.shape[0]
    hn = hn_ref[...]
    ridx = lax.broadcasted_iota(jnp.int32, (MOE_CHUNK, tm), 0).astype(F32)

    def copy(e, c):
        slot = e * (pl.cdiv(tm, MOE_CHUNK)) + c
        row0 = pl.multiple_of(offs_ref[i * N_EXPERTS + e] + c * MOE_CHUNK, MOE_ALIGN)
        return pltpu.make_async_copy(stage.at[slot], xs_ref.at[pl.ds(row0, MOE_CHUNK)], sems.at[slot])

    for e, c in _chunk_copies(pl.cdiv(tm, MOE_CHUNK)):
        @pl.when(c * MOE_CHUNK < cnts_ref[i * N_EXPERTS + e])
        def _(e=e, c=c):
            hit = (rankt_ref[0, e:e + 1, :] - float(c * MOE_CHUNK) == ridx) & (selt_ref[0, e:e + 1, :] > 0.5)
            rows = _dot(jnp.where(hit, 1.0, 0.0).astype(BF16), hn)
            stage[e * (pl.cdiv(tm, MOE_CHUNK)) + c] = rows.astype(BF16)
            copy(e, c).start()

    for e, c in _chunk_copies(pl.cdiv(tm, MOE_CHUNK)):
        @pl.when(c * MOE_CHUNK < cnts_ref[i * N_EXPERTS + e])
        def _(e=e, c=c):
            copy(e, c).wait()


def _moe_dispatch(offs, cnts, hn, selt, rankt, n_rows):
    t, d = hn.shape
    nt, _, tm = selt.shape
    n_slots = N_EXPERTS * (pl.cdiv(tm, MOE_CHUNK))
    xs0 = jnp.zeros((n_rows, d), BF16)
    grid_spec = pltpu.PrefetchScalarGridSpec(
        num_scalar_prefetch=2,
        grid=(nt,),
        in_specs=[pl.BlockSpec((tm, d), lambda i, *_: (i, 0)),
                  pl.BlockSpec((1, T_ROWS, tm), lambda i, *_: (i, 0, 0)),
                  pl.BlockSpec((1, T_ROWS, tm), lambda i, *_: (i, 0, 0)),
                  pl.BlockSpec(memory_space=pl.ANY)],
        out_specs=pl.BlockSpec(memory_space=pl.ANY),
        scratch_shapes=[pltpu.VMEM((n_slots, MOE_CHUNK, d), BF16), pltpu.SemaphoreType.DMA((n_slots,))],
    )
    return pl.pallas_call(
        _dispatch_kernel,
        grid_spec=grid_spec,
        out_shape=jax.ShapeDtypeStruct((n_rows, d), BF16),
        input_output_aliases={5: 0},
        compiler_params=_cparams(("arbitrary",)),
        name="moe_dispatch",
    )(offs, cnts, hn, selt, rankt, xs0)


def _expert_kernel(te_ref, ok_ref, xs_ref, wg_ref, wu_ref, wd_ref, ys_ref, acc_ref):
    s = pl.program_id(0)
    f = pl.program_id(1)
    ok = ok_ref[s] > 0

    @pl.when(ok & (f == 0))
    def _():
        acc_ref[...] = jnp.zeros_like(acc_ref)

    @pl.when(ok)
    def _():
        x = xs_ref[...]
        a = _silu(_dot(x, wg_ref[0].astype(BF16))) * _dot(x, wu_ref[0].astype(BF16))
        acc_ref[...] += _dot(a.astype(BF16), wd_ref[0].astype(BF16))

    last = f == pl.num_programs(1) - 1

    @pl.when(ok & last)
    def _():
        ys_ref[...] = acc_ref[...].astype(ys_ref.dtype)

    @pl.when(jnp.logical_not(ok) & last)
    def _():
        ys_ref[...] = jnp.zeros_like(ys_ref)


def _moe_experts(tile_e, tile_ok, xs, wg, wu, wd):
    n_rows, d = xs.shape
    ff = wg.shape[2]
    tf = _pick(ff, (512, 256, 128))
    nf = ff // tf

    def f_idx(s, f, ok_ref):
        return jnp.where(ok_ref[s] > 0, f, nf - 1)

    grid_spec = pltpu.PrefetchScalarGridSpec(
        num_scalar_prefetch=2,
        grid=(n_rows // MOE_ROWS, nf),
        in_specs=[pl.BlockSpec((MOE_ROWS, d), lambda s, f, te, ok: (s, 0)),
                  pl.BlockSpec((1, d, tf), lambda s, f, te, ok: (te[s], 0, f_idx(s, f, ok))),
                  pl.BlockSpec((1, d, tf), lambda s, f, te, ok: (te[s], 0, f_idx(s, f, ok))),
                  pl.BlockSpec((1, tf, d), lambda s, f, te, ok: (te[s], f_idx(s, f, ok), 0))],
        out_specs=pl.BlockSpec((MOE_ROWS, d), lambda s, f, te, ok: (s, 0)),
        scratch_shapes=[pltpu.VMEM((MOE_ROWS, d), F32)],
    )
    return pl.pallas_call(
        _expert_kernel,
        grid_spec=grid_spec,
        out_shape=jax.ShapeDtypeStruct((n_rows, d), BF16),
        compiler_params=_cparams(("parallel", "arbitrary")),
        name="moe_experts",
    )(tile_e, tile_ok, xs, wg, wu, wd)


def _combine_kernel(offs_ref, cnts_ref, h_ref, route_ref, gout_ref, ys_ref, o_ref, stage, sems):
    i = pl.program_id(0)
    tm = h_ref.shape[0]
    n_chunks = pl.cdiv(tm, MOE_CHUNK)
    ridx = lax.broadcasted_iota(jnp.int32, (tm, MOE_CHUNK), 1).astype(F32)

    def copy(e, c):
        slot = e * n_chunks + c
        row0 = pl.multiple_of(offs_ref[i * N_EXPERTS + e] + c * MOE_CHUNK, MOE_ALIGN)
        return pltpu.make_async_copy(ys_ref.at[pl.ds(row0, MOE_CHUNK)], stage.at[slot], sems.at[slot])

    for e, c in _chunk_copies(n_chunks):
        @pl.when(c * MOE_CHUNK < cnts_ref[i * N_EXPERTS + e])
        def _(e=e, c=c):
            copy(e, c).start()

    o_ref[...] = h_ref[...]
    for e, c in _chunk_copies(n_chunks):
        @pl.when(c * MOE_CHUNK < cnts_ref[i * N_EXPERTS + e])
        def _(e=e, c=c):
            copy(e, c).wait()
            gate = route_ref[:, e:e + 1]
            sel = route_ref[:, SEL_LANE + e:SEL_LANE + e + 1]
            rank = route_ref[:, RANK_LANE + e:RANK_LANE + e + 1]
            hit = (rank - float(c * MOE_CHUNK) == ridx) & (sel > 0.5)
            back = _dot(jnp.where(hit, 1.0, 0.0).astype(BF16), stage[e * n_chunks + c])
            o_ref[...] += gate * back

    o_ref[...] = _rms(o_ref[...], gout_ref[...])


def _moe_combine(offs, cnts, h, route, ys, g_out):
    t, d = h.shape
    tm = _pick(t, (MOE_TM, 384, 256, 128))
    n_slots = N_EXPERTS * (pl.cdiv(tm, MOE_CHUNK))
    grid_spec = pltpu.PrefetchScalarGridSpec(
        num_scalar_prefetch=2,
        grid=(t // tm,),
        in_specs=[pl.BlockSpec((tm, d), lambda i, *_: (i, 0)),
                  pl.BlockSpec((tm, LANES), lambda i, *_: (i, 0)),
                  pl.BlockSpec((1, d), lambda i, *_: (0, 0)),
                  pl.BlockSpec(memory_space=pl.ANY)],
        out_specs=pl.BlockSpec((tm, d), lambda i, *_: (i, 0)),
        scratch_shapes=[pltpu.VMEM((n_slots, MOE_CHUNK, d), BF16), pltpu.SemaphoreType.DMA((n_slots,))],
    )
    return pl.pallas_call(
        _combine_kernel,
        grid_spec=grid_spec,
        out_shape=jax.ShapeDtypeStruct((t, d), F32),
        compiler_params=_cparams(("arbitrary",)),
        name="moe_combine",
    )(offs, cnts, h, route, g_out, ys)


def _moe(h, mix_a, mix_b, wa, wb, g, wr3, wg, wu, wd, g_out):
    t, d = h.shape
    h, hn, route, selt, rankt, cnt = _moe_route(h, mix_a, mix_b, wa, wb, g, wr3)
    nt = cnt.shape[0]
    cnt = cnt[:, 0, :N_EXPERTS].astype(jnp.int32)
    worst = TOP_K * t + nt * N_EXPERTS * (MOE_ALIGN - 1) + N_EXPERTS * (MOE_CHUNK + MOE_ROWS - 1)
    n_row_tiles = -(-worst // MOE_ROWS)
    offs, tile_e, tile_ok = _moe_plan(cnt, n_row_tiles)
    cnts = cnt.reshape(-1)
    xs = _moe_dispatch(offs, cnts, hn, selt, rankt, n_row_tiles * MOE_ROWS)
    ys = _moe_experts(tile_e, tile_ok, xs, wg, wu, wd)
    return _moe_combine(offs, cnts, h, route, ys, g_out)


def _rope_table(lp, rot_dim, theta, width, offset, scale):
    half = rot_dim // 2
    pos = (jnp.arange(lp, dtype=jnp.int32) - FIRST).astype(F32)
    inv = theta ** (-jnp.arange(half, dtype=F32) * 2.0 / rot_dim)
    ang = pos[:, None] * inv[None, :]
    cos, sin = jnp.cos(ang), jnp.sin(ang)
    ones = jnp.ones((lp, 1), F32)
    zeros = jnp.zeros((lp, 1), F32)

    def unit(first, second, fill):
        parts = [jnp.tile(fill, (1, offset)), first, second,
                 jnp.tile(fill, (1, width - offset - rot_dim))]
        return jnp.tile(jnp.concatenate(parts, axis=1), (1, LANES // width))

    c = unit(cos, cos, ones)
    s1 = unit(-sin, jnp.zeros_like(sin), zeros)
    s2 = unit(jnp.zeros_like(sin), sin, zeros)
    return jnp.concatenate([c, s1, s2], axis=1) * scale


def _pad_heads(w, heads, used, width):
    k = w.shape[0]
    w = w.reshape(k, heads, used)
    return jnp.pad(w, ((0, 0), (0, 0), (0, width - used))).reshape(k, heads * width)


def kernel(x, meta_tokens, ln_mix_e, w_in_e, ln_mla_q_e, w_mla_uq_e, ln_mla_kv_e, w_mla_ukv_e,
           w_out_e, ln_ffn_e, w_ffn_gate_e, w_ffn_up_e, w_ffn_down_e, ln_mix_o, w_in_o, b_forget_o,
           diff_lq1_o, diff_lk1_o, diff_lq2_o, diff_lk2_o, diff_subln_o, w_out_o, ln_ffn_o,
           w_router_o, w_moe_gate_o, w_moe_up_o, w_moe_down_o, ln_final):
    b, seq, d = x.shape
    assert d == D_MODEL and seq % TQ == 0
    lp = PAD + seq
    t = b * lp
    depth = 2

    meta = jnp.broadcast_to(meta_tokens.astype(x.dtype)[None], (b, N_META, d))
    h = jnp.concatenate([jnp.zeros((b, FIRST, d), x.dtype), meta, x], axis=1).reshape(t, d)

    row = lambda v: v.reshape(1, -1).astype(F32)
    sb_scale = HEAD_DIM ** -0.5
    tab_mla_q = _rope_table(lp, MLA_ROPE, MLA_THETA, LANES, MLA_NOPE,
                            (MLA_NOPE + MLA_ROPE) ** -0.5 * LOG2E)
    tab_mla_k = _rope_table(lp, MLA_ROPE, MLA_THETA, LANES, MLA_NOPE, 1.0)
    tab_diff_q = _rope_table(lp, ROT_DIM, ROPE_THETA, DIFF_DIM, 0, LOG2E)
    tab_diff_k = _rope_table(lp, ROT_DIM, ROPE_THETA, DIFF_DIM, 0, 1.0)

    for i in range(depth):
        j = i // 2
        if i % 2 == 0:
            w_in = w_in_e[j]
            lat_w = MLA_Q_RANK + MLA_KV_RANK + MLA_ROPE
            w_cat = jnp.concatenate(
                [w_in[:, :SB_W] * sb_scale, w_in[:, SB_W:3 * SB_W + lat_w],
                 jnp.zeros((d, 4 * LANES - lat_w), F32)], axis=1).astype(BF16)
            sb, lat = _even_proj(h, row(ln_mix_e[j]), w_cat)
            sb = sb.reshape(b, lp, -1)
            lat = lat.reshape(b, lp, -1)

            wq = _pad_heads(w_mla_uq_e[j], MLA_HEADS, MLA_NOPE + MLA_ROPE, LANES).astype(BF16)
            ukv = w_mla_ukv_e[j].reshape(MLA_KV_RANK, MLA_HEADS, MLA_NOPE + MLA_V)
            wk_nope = _pad_heads(ukv[:, :, :MLA_NOPE].reshape(MLA_KV_RANK, -1), MLA_HEADS, MLA_NOPE, LANES)
            place = jnp.pad(jnp.eye(MLA_ROPE, dtype=F32), ((0, 0), (MLA_NOPE, LANES - MLA_NOPE - MLA_ROPE)))
            wk_rope = jnp.tile(place, (1, MLA_HEADS))
            wk = jnp.concatenate(
                [wk_nope, wk_rope, jnp.zeros((LANES - MLA_ROPE, MLA_HEADS * LANES), F32)], axis=0).astype(BF16)
            wv = ukv[:, :, MLA_NOPE:].reshape(MLA_KV_RANK, -1).astype(BF16)
            mq, mk, mv = _mla_up(lat, row(ln_mla_q_e[j]), row(ln_mla_kv_e[j]), wq, wk, wv,
                                 tab_mla_q, tab_mla_k)

            a_out = _sb_attn(sb).reshape(t, -1)
            b_out = _mla_attn(mq, mk, mv).reshape(t, -1)
            w_out = w_out_e[j].astype(BF16)
            h = _ffn(h, a_out, b_out, w_out[:SB_W], w_out[SB_W:], row(ln_ffn_e[j]),
                     w_ffn_gate_e[j].astype(BF16), w_ffn_up_e[j].astype(BF16), w_ffn_down_e[j].astype(BF16))
        else:
            lam_init = 0.8 - 0.6 * math.exp(-0.3 * i)
            w_in = w_in_o[j]
            wf = w_in[:, N_ODD_MAIN:]
            w_cat = jnp.concatenate(
                [w_in[:, :DIFF_W] * sb_scale, w_in[:, DIFF_W:3 * DIFF_W],
                 w_in[:, 3 * DIFF_W:3 * DIFF_W + FOX_W] * sb_scale, w_in[:, 3 * DIFF_W + FOX_W:N_ODD_MAIN],
                 wf, jnp.zeros((d, LANES - FOX_HEADS), F32)], axis=1).astype(BF16)
            wft = jnp.pad(wf.T, ((0, F_ROWS - FOX_HEADS), (0, 0))).astype(BF16)
            bf = b_forget_o[j].astype(F32)
            bcol = jnp.pad(bf, (0, LANES - FOX_HEADS)).reshape(1, LANES)
            brow = jnp.pad(bf, (0, F_ROWS - FOX_HEADS)).reshape(F_ROWS, 1)
            main, fcol, frow = _odd_proj(h.reshape(b, lp, d), row(ln_mix_o[j]), w_cat, wft, bcol, brow,
                                         tab_diff_q, tab_diff_k)
            ccol, crow = _fox_cumsum(fcol, frow)
            crow4 = crow[:, :FOX_HEADS].reshape(b, FOX_HEADS // 2, 2, lp)
            lam_vecs = jnp.stack([diff_lq1_o[j], diff_lk1_o[j], diff_lq2_o[j], diff_lk2_o[j]]).astype(F32)
            c_out = _diff_attn(main, lam_vecs, row(diff_subln_o[j]), lam_init).reshape(t, -1)
            d_out = _fox_attn(main, ccol, crow4).reshape(t, -1)
            w_out = w_out_o[j].astype(BF16)
            wr = jnp.pad(w_router_o[j].astype(F32), ((0, 0), (0, LANES - N_EXPERTS)))
            wr3 = jnp.stack(_split3(wr))
            h = _moe(h, c_out, d_out, w_out[:DIFF_W], w_out[DIFF_W:], row(ln_ffn_o[j]), wr3,
                     w_moe_gate_o[j], w_moe_up_o[j], w_moe_down_o[j], row(ln_final))

    return h.reshape(b, lp, d)[:, PAD:]
```

```python
import functools
import math

import jax
import jax.numpy as jnp
from jax import lax
from jax.experimental import pallas as pl
from jax.experimental.pallas import tpu as pltpu

F32 = jnp.float32
BF16 = jnp.bfloat16

D_MODEL = 1024
CHUNK = 64
N_META = 16
EPS = 1e-6
LOG2E = 1.4426950408889634
NEG_INF = -1e30

HEAD_DIM = 64
SB_HEADS = 8
MLA_HEADS = 8
MLA_NOPE = 64
MLA_ROPE = 32
MLA_V = 64
MLA_Q_RANK = 256
MLA_KV_RANK = 128
MLA_THETA = 10000.0
DIFF_HEADS = 4
DIFF_DIM = 64
FOX_HEADS = 8
ROPE_THETA = 500000.0
ROT_DIM = HEAD_DIM // 4
N_EXPERTS = 8
TOP_K = 2

SB_W = SB_HEADS * HEAD_DIM
DIFF_W = DIFF_HEADS * 2 * DIFF_DIM
FOX_W = FOX_HEADS * HEAD_DIM

LANES = 128
PAD = 128
FIRST = PAD - N_META
TQ = 128
KEY_WINDOW = 512
Q_GROUP = 4
SB_Q_GROUP = 4
SB_KEY_WINDOW = 512
F32_EXP2_UNDERFLOW = 160.0
SUFFIX_CHUNK = 256
CHUNK_SHIFT = CHUNK.bit_length() - 1
VMEM_LIMIT = 56 * 1024 * 1024


def _cparams(sem):
    return pltpu.CompilerParams(dimension_semantics=sem, vmem_limit_bytes=VMEM_LIMIT)


def _pick(n, cands):
    for c in cands:
        if n % c == 0:
            return c
    raise ValueError(f"no tile in {cands} divides {n}")


def _rms(x, g):
    return x * lax.rsqrt(jnp.mean(x * x, axis=-1, keepdims=True) + EPS) * g


def _dot(a, b):
    return jnp.dot(a, b, preferred_element_type=F32)


def _dot_nt(a, b):
    return lax.dot_general(a, b, (((1,), (1,)), ((), ())), preferred_element_type=F32)


def _rope_chunk(y, tab, shift):
    c = tab[:, 0:LANES]
    s1 = tab[:, LANES:2 * LANES]
    s2 = tab[:, 2 * LANES:3 * LANES]
    return (y * c + pltpu.roll(y, LANES - shift, 1) * s1 + pltpu.roll(y, shift, 1) * s2)


def _split3(x):
    a = x.astype(BF16)
    r = x - a.astype(F32)
    b = r.astype(BF16)
    c = (r - b.astype(F32)).astype(BF16)
    return a, b, c


def _even_proj_kernel(h_ref, g_ref, w_ref, sb_ref, lat_ref):
    hn = _rms(h_ref[...], g_ref[...]).astype(BF16)
    y = _dot(hn, w_ref[...])
    n_sb = sb_ref.shape[-1]
    sb_ref[:, :SB_W] = (y[:, :SB_W] * LOG2E).astype(BF16)
    sb_ref[:, SB_W:] = y[:, SB_W:n_sb].astype(BF16)
    lat_ref[...] = y[:, n_sb:].astype(BF16)


def _even_proj(h, g, w):
    t, d = h.shape
    n = w.shape[1]
    n_sb = 3 * SB_W
    tm = _pick(t, (512, 384, 256, 128))
    return pl.pallas_call(
        _even_proj_kernel,
        grid=(t // tm,),
        in_specs=[pl.BlockSpec((tm, d), lambda i: (i, 0)),
                  pl.BlockSpec((1, d), lambda i: (0, 0)),
                  pl.BlockSpec((d, n), lambda i: (0, 0))],
        out_specs=[pl.BlockSpec((tm, n_sb), lambda i: (i, 0)),
                   pl.BlockSpec((tm, n - n_sb), lambda i: (i, 0))],
        out_shape=[jax.ShapeDtypeStruct((t, n_sb), BF16),
                   jax.ShapeDtypeStruct((t, n - n_sb), BF16)],
        compiler_params=_cparams(("parallel",)),
        name="even_proj",
    )(h, g, w)


def _mla_up_kernel(lat_ref, gq_ref, gkv_ref, wq_ref, wk_ref, wv_ref, tq_ref, tk_ref,
                   q_ref, k_ref, v_ref):
    lat = lat_ref[0]
    nq = _rms(lat[:, :MLA_Q_RANK].astype(F32), gq_ref[...]).astype(BF16)
    nkv = _rms(lat[:, MLA_Q_RANK:MLA_Q_RANK + MLA_KV_RANK].astype(F32), gkv_ref[...]).astype(BF16)
    yq = _dot(nq, wq_ref[...])
    xk = jnp.concatenate([nkv, lat[:, MLA_Q_RANK + MLA_KV_RANK:]], axis=1)
    yk = _dot(xk, wk_ref[...])
    v_ref[0] = _dot(nkv, wv_ref[...]).astype(BF16)
    tabq = tq_ref[...]
    tabk = tk_ref[...]
    for hd in range(MLA_HEADS):
        sl = slice(hd * LANES, (hd + 1) * LANES)
        q_ref[0, :, sl] = _rope_chunk(yq[:, sl], tabq, MLA_ROPE // 2).astype(BF16)
        k_ref[0, :, sl] = _rope_chunk(yk[:, sl], tabk, MLA_ROPE // 2).astype(BF16)


def _mla_up(lat, gq, gkv, wq, wk, wv, tabq, tabk):
    b, lp, nl = lat.shape
    tm = _pick(lp, (384, 128))
    nqk = MLA_HEADS * LANES
    nv = MLA_HEADS * MLA_V
    full = lambda shp: pl.BlockSpec(shp, lambda bi, i: (0,) * len(shp))
    return pl.pallas_call(
        _mla_up_kernel,
        grid=(b, lp // tm),
        in_specs=[pl.BlockSpec((1, tm, nl), lambda bi, i: (bi, i, 0)),
                  full(gq.shape), full(gkv.shape), full(wq.shape), full(wk.shape), full(wv.shape),
                  pl.BlockSpec((tm, 3 * LANES), lambda bi, i: (i, 0)),
                  pl.BlockSpec((tm, 3 * LANES), lambda bi, i: (i, 0))],
        out_specs=[pl.BlockSpec((1, tm, nqk), lambda bi, i: (bi, i, 0)),
                   pl.BlockSpec((1, tm, nqk), lambda bi, i: (bi, i, 0)),
                   pl.BlockSpec((1, tm, nv), lambda bi, i: (bi, i, 0))],
        out_shape=[jax.ShapeDtypeStruct((b, lp, nqk), BF16),
                   jax.ShapeDtypeStruct((b, lp, nqk), BF16),
                   jax.ShapeDtypeStruct((b, lp, nv), BF16)],
        compiler_params=_cparams(("parallel", "parallel")),
        name="mla_up",
    )(lat, gq, gkv, wq, wk, wv, tabq, tabk)


N_ODD_MAIN = 3 * DIFF_W + 3 * FOX_W
F_ROWS = 16


def _odd_proj_kernel(h_ref, g_ref, w_ref, wft_ref, bcol_ref, brow_ref, tabq_ref, tabk_ref,
                     main_ref, fcol_ref, frow_ref):
    hn = _rms(h_ref[0], g_ref[...]).astype(BF16)
    y = _dot(hn, w_ref[...])
    n_rope = DIFF_W // LANES
    for c in range(2 * n_rope):
        sl = slice(c * LANES, (c + 1) * LANES)
        tab = tabq_ref[...] if c < n_rope else tabk_ref[...]
        main_ref[0, :, sl] = _rope_chunk(y[:, sl], tab, ROT_DIM // 2).astype(BF16)
    fq = slice(3 * DIFF_W, 3 * DIFF_W + FOX_W)
    main_ref[0, :, 2 * DIFF_W:3 * DIFF_W] = y[:, 2 * DIFF_W:3 * DIFF_W].astype(BF16)
    main_ref[0, :, fq] = (y[:, fq] * LOG2E).astype(BF16)
    main_ref[0, :, 3 * DIFF_W + FOX_W:] = y[:, 3 * DIFF_W + FOX_W:N_ODD_MAIN].astype(BF16)
    fcol_ref[0] = y[:, N_ODD_MAIN:] + bcol_ref[...]
    frow_ref[0] = _dot_nt(wft_ref[...], hn) + brow_ref[...]


def _odd_proj(h3, g, w, wft, bcol, brow, tabq, tabk):
    b, lp, d = h3.shape
    tm = _pick(lp, (384, 128))
    full = lambda shp: pl.BlockSpec(shp, lambda bi, i: (0,) * len(shp))
    return pl.pallas_call(
        _odd_proj_kernel,
        grid=(b, lp // tm),
        in_specs=[pl.BlockSpec((1, tm, d), lambda bi, i: (bi, i, 0)),
                  full(g.shape), full(w.shape), full(wft.shape), full(bcol.shape), full(brow.shape),
                  pl.BlockSpec((tm, 3 * LANES), lambda bi, i: (i, 0)),
                  pl.BlockSpec((tm, 3 * LANES), lambda bi, i: (i, 0))],
        out_specs=[pl.BlockSpec((1, tm, N_ODD_MAIN), lambda bi, i: (bi, i, 0)),
                   pl.BlockSpec((1, tm, LANES), lambda bi, i: (bi, i, 0)),
                   pl.BlockSpec((1, F_ROWS, tm), lambda bi, i: (bi, 0, i))],
        out_shape=[jax.ShapeDtypeStruct((b, lp, N_ODD_MAIN), BF16),
                   jax.ShapeDtypeStruct((b, lp, LANES), F32),
                   jax.ShapeDtypeStruct((b, F_ROWS, lp), F32)],
        compiler_params=_cparams(("parallel", "parallel")),
        name="odd_proj",
    )(h3, g, w, wft, bcol, brow, tabq, tabk)


def _log_sigmoid(x):
    return jnp.minimum(x, 0.0) - jnp.log(1.0 + jnp.exp(-jnp.abs(x)))


def _fox_cumsum_kernel(fcol_ref, frow_ref, ccol_ref, crow_ref):
    lp = fcol_ref.shape[1]
    blk = LANES
    r = lax.broadcasted_iota(jnp.int32, (blk, blk), 0)
    c = lax.broadcasted_iota(jnp.int32, (blk, blk), 1)
    tri_l = (c <= r).astype(BF16)
    tri_u = (r <= c).astype(BF16)
    row_id = lax.broadcasted_iota(jnp.int32, (blk, LANES), 0)
    col_id = lax.broadcasted_iota(jnp.int32, (F_ROWS, blk), 1)
    carry_c = jnp.zeros((1, LANES), F32)
    carry_r = jnp.zeros((F_ROWS, 1), F32)
    for j in range(lp // blk):
        sl = slice(j * blk, (j + 1) * blk)
        ls = _log_sigmoid(fcol_ref[0, sl, :])
        ls = jnp.where(row_id + j * blk >= FIRST, ls, 0.0)
        a, b, c3 = _split3(ls)
        cs = _dot(tri_l, a) + _dot(tri_l, b) + _dot(tri_l, c3) + carry_c
        ccol_ref[0, sl, :] = cs * LOG2E
        carry_c = cs[blk - 1:blk, :]
        lr = _log_sigmoid(frow_ref[0, :, sl])
        lr = jnp.where(col_id + j * blk >= FIRST, lr, 0.0)
        a, b, c3 = _split3(lr)
        cr = _dot(a, tri_u) + _dot(b, tri_u) + _dot(c3, tri_u) + carry_r
        crow_ref[0, :, sl] = cr * LOG2E
        carry_r = cr[:, blk - 1:blk]


def _fox_cumsum(fcol, frow):
    b, lp, _ = fcol.shape
    return pl.pallas_call(
        _fox_cumsum_kernel,
        grid=(b,),
        in_specs=[pl.BlockSpec((1, lp, LANES), lambda bi: (bi, 0, 0)),
                  pl.BlockSpec((1, F_ROWS, lp), lambda bi: (bi, 0, 0))],
        out_specs=[pl.BlockSpec((1, lp, LANES), lambda bi: (bi, 0, 0)),
                   pl.BlockSpec((1, F_ROWS, lp), lambda bi: (bi, 0, 0))],
        out_shape=[jax.ShapeDtypeStruct((b, lp, LANES), F32),
                   jax.ShapeDtypeStruct((b, F_ROWS, lp), F32)],
        compiler_params=_cparams(("parallel",)),
        name="fox_cumsum",
    )(fcol, frow)


def _rows(i, n):
    if isinstance(i, int):
        return pl.ds(i * n, n)
    return pl.ds(pl.multiple_of(i * n, n), n)


def _keys(k0, w):
    if isinstance(k0, int):
        return pl.ds(k0, w)
    return pl.ds(pl.multiple_of(k0, TQ), w)


def _visible(kind, ks, qs):
    if kind == "chunk":
        return (ks >> CHUNK_SHIFT) <= (qs >> CHUNK_SHIFT)
    if kind == "causal":
        return ks <= qs
    return ks < qs


def _first_window_mask(kind, q0, w, lim):
    r = lax.broadcasted_iota(jnp.int32, (2 * TQ, w), 0)
    ks = lax.broadcasted_iota(jnp.int32, (2 * TQ, w), 1)
    return _visible(kind, ks, q0 + (r & (TQ - 1))) & (ks >= FIRST) & (ks < lim)


def _diag_window_mask(kind, w):
    r = lax.broadcasted_iota(jnp.int32, (2 * TQ, w), 0)
    ks = lax.broadcasted_iota(jnp.int32, (2 * TQ, w), 1) + (TQ - w)
    return _visible(kind, ks, r & (TQ - 1))


def _span_mask(w, lim):
    ks = lax.broadcasted_iota(jnp.int32, (1, w), 1)
    return (ks >= FIRST) & (ks < lim)


def _lane_halves():
    lane = lax.broadcasted_iota(jnp.int32, (TQ, LANES), 1)
    return lane < (LANES // 2)


def _stack_masked(q):
    lo = _lane_halves()
    z = jnp.zeros_like(q)
    return jnp.concatenate([jnp.where(lo, q, z), jnp.where(lo, z, q)], axis=0)


def _softmax_sweep(kind, w, prep, scores_fn, v_fn, dbias_ref):
    def step(carry, s, v):
        m, acc = carry
        m_new = jnp.maximum(m, jnp.max(s, axis=-1, keepdims=True))
        alpha = jnp.exp2(m - m_new)
        p = jnp.exp2(s - m_new)
        ones = jnp.ones((v.shape[0], LANES), BF16)
        acc = alpha * acc + _dot(p.astype(BF16), jnp.concatenate([v, ones], axis=1))
        return m_new, acc

    def run(qi, mwin):
        g = w // TQ
        ctx = prep(qi)
        q0 = qi * TQ
        pre = qi + 1 - mwin * g
        n0 = w
        carry = (jnp.full((2 * TQ, 1), NEG_INF, F32), jnp.zeros((2 * TQ, 2 * LANES), F32))
        s = scores_fn(ctx, 0, n0)
        if mwin == 0:
            s = jnp.where(_first_window_mask(kind, q0, n0, pre * TQ), s, NEG_INF)
            return step(carry, s, v_fn(0, n0))[1]
        carry = step(carry, s + jnp.where(_span_mask(n0, pre * TQ), 0.0, NEG_INF), v_fn(0, n0))
        for t in range(1, mwin):
            k0 = (pre + (t - 1) * g) * TQ
            carry = step(carry, scores_fn(ctx, k0, w), v_fn(k0, w))
        k0 = q0 + TQ - w
        return step(carry, scores_fn(ctx, k0, w) + dbias_ref[...], v_fn(k0, w))[1]

    return run


def _for_each_query_group(nq, w, q_group, group_fn):
    g = w // TQ
    one = jnp.sign(pl.program_id(0) + 1)
    for mwin in range(-(-nq // g)):
        lo, hi = mwin * g, min((mwin + 1) * g, nq)

        n_groups = (hi - lo) // q_group

        def body(it, _, mwin=mwin, lo=lo):
            qi = lo + q_group * it
            group_fn([qi + j for j in range(q_group)], mwin)
            return 0

        if n_groups:
            lax.fori_loop(0, n_groups * one, body, 0)
        for qi in range(lo + n_groups * q_group, hi):
            group_fn([qi], mwin)


def _for_each_query_block(nq, w, run, emit):
    def group_fn(qis, mwin):
        results = [run(qi, mwin) for qi in qis]
        for qi, res in zip(qis, results):
            emit(qi, res)

    _for_each_query_group(nq, w, Q_GROUP, group_fn)


def _key_window(lp, limit=None):
    return min(KEY_WINDOW if limit is None else limit, lp)


def _merge_heads(top, bottom):
    return jnp.where(_lane_halves(), top, bottom)


def _attn_call(kernel, name, b, lp, n_blocks, in_specs, args, n_masks=1, window=None):
    w = _key_window(lp, window)
    return pl.pallas_call(
        kernel,
        grid=(b, n_blocks),
        in_specs=in_specs,
        out_specs=pl.BlockSpec((1, lp, LANES), lambda bi, hp: (bi, 0, hp)),
        out_shape=jax.ShapeDtypeStruct((b, lp, n_blocks * LANES), BF16),
        scratch_shapes=[pltpu.VMEM((2 * TQ, w), F32)] * n_masks,
        compiler_params=_cparams(("parallel", "parallel")),
        name=name,
    )(*args)


def _normalised(acc):
    return acc[:, :LANES] / acc[:, LANES:]


def _col_spec(lp, off):
    return pl.BlockSpec((1, lp, LANES), lambda bi, hp: (bi, 0, off + hp))


def _mla_attn_kernel(q_ref, k_ref, v_ref, o_ref, dbias_ref):
    lp = q_ref.shape[1]
    w = _key_window(lp)
    dbias_ref[...] = jnp.where(_diag_window_mask("chunk", w), 0.0, NEG_INF)

    def prep(qi):
        return q_ref[0, _rows(qi, TQ), 0:LANES], q_ref[0, _rows(qi, TQ), LANES:2 * LANES]

    def scores(ctx, k0, n):
        qa, qb = ctx
        ka = k_ref[0, _keys(k0, n), 0:LANES]
        kb = k_ref[0, _keys(k0, n), LANES:2 * LANES]
        return jnp.concatenate([_dot_nt(qa, ka), _dot_nt(qb, kb)], axis=0)

    run = _softmax_sweep("chunk", w, prep, scores, lambda k0, n: v_ref[0, _keys(k0, n), :], dbias_ref)

    def emit(qi, acc):
        o = _normalised(acc)
        o_ref[0, _rows(qi, TQ), :] = _merge_heads(o[:TQ], o[TQ:]).astype(o_ref.dtype)

    _for_each_query_block(lp // TQ, w, run, emit)


def _mla_attn(q, k, v):
    b, lp, _ = q.shape
    wide = lambda: pl.BlockSpec((1, lp, 2 * LANES), lambda bi, hp: (bi, 0, hp))
    return _attn_call(_mla_attn_kernel, "mla_attn", b, lp, MLA_HEADS // 2,
                      [wide(), wide(), _col_spec(lp, 0)], (q, k, v))


def _diff_attn_kernel(lam_init, x_q, x_k, x_v, lam_ref, gsub_ref, o_ref, dbias_ref):
    lp = x_q.shape[1]
    w = _key_window(lp)
    dbias_ref[...] = jnp.where(_diag_window_mask("chunk", w), 0.0, NEG_INF)
    lv = lam_ref[...]
    lam = (jnp.exp(jnp.sum(lv[0:1] * lv[1:2], axis=-1, keepdims=True))
           - jnp.exp(jnp.sum(lv[2:3] * lv[3:4], axis=-1, keepdims=True)) + lam_init)
    gsub = gsub_ref[...]

    def prep(qi):
        return _stack_masked(x_q[0, _rows(qi, TQ), :])

    def scores(qs, k0, n):
        return _dot_nt(qs, x_k[0, _keys(k0, n), :])

    run = _softmax_sweep("chunk", w, prep, scores, lambda k0, n: x_v[0, _keys(k0, n), :], dbias_ref)

    def emit(qi, acc):
        o = _normalised(acc)
        o = o[:TQ] - lam * o[TQ:]
        o_ref[0, _rows(qi, TQ), :] = (_rms(o, gsub) * (1.0 - lam_init)).astype(o_ref.dtype)

    _for_each_query_block(lp // TQ, w, run, emit)


def _diff_attn(main, lam_vecs, gsub, lam_init):
    b, lp, _ = main.shape
    nb = DIFF_W // LANES
    full = lambda shp: pl.BlockSpec(shp, lambda bi, hp: (0,) * len(shp))
    return _attn_call(functools.partial(_diff_attn_kernel, lam_init), "diff_attn", b, lp, DIFF_HEADS,
                      [_col_spec(lp, 0), _col_spec(lp, nb), _col_spec(lp, 2 * nb),
                       full(lam_vecs.shape), full(gsub.shape)],
                      (main, main, main, lam_vecs, gsub))


def _fox_attn_kernel(x_q, x_k, x_v, ccol_ref, crow_ref, o_ref, dbias_ref):
    lp = x_q.shape[1]
    w = _key_window(lp)
    dbias_ref[...] = jnp.where(_diag_window_mask("causal", w), 0.0, NEG_INF)
    hp = pl.program_id(1)
    lane = lax.broadcasted_iota(jnp.int32, (TQ, LANES), 1)

    def prep(qi):
        cc = ccol_ref[0, _rows(qi, TQ), :]
        cq0 = jnp.sum(jnp.where(lane == 2 * hp, cc, 0.0), axis=-1, keepdims=True)
        cq1 = jnp.sum(jnp.where(lane == 2 * hp + 1, cc, 0.0), axis=-1, keepdims=True)
        return _stack_masked(x_q[0, _rows(qi, TQ), :]), cq0, cq1

    def scores(ctx, k0, n):
        qs, cq0, cq1 = ctx
        s = _dot_nt(qs, x_k[0, _keys(k0, n), :])
        ck = crow_ref[0, 0, :, _keys(k0, n)]
        return s + jnp.concatenate([cq0 - ck[0:1, :], cq1 - ck[1:2, :]], axis=0)

    run = _softmax_sweep("causal", w, prep, scores, lambda k0, n: x_v[0, _keys(k0, n), :], dbias_ref)

    def emit(qi, acc):
        o = _normalised(acc)
        o_ref[0, _rows(qi, TQ), :] = _merge_heads(o[:TQ], o[TQ:]).astype(o_ref.dtype)

    _for_each_query_block(lp // TQ, w, run, emit)


def _fox_attn(main, ccol, crow4):
    b, lp, _ = main.shape
    off = 3 * DIFF_W // LANES
    nb = FOX_W // LANES
    return _attn_call(_fox_attn_kernel, "fox_attn", b, lp, nb,
                      [_col_spec(lp, off), _col_spec(lp, off + nb), _col_spec(lp, off + 2 * nb),
                       pl.BlockSpec((1, lp, LANES), lambda bi, hp: (bi, 0, 0)),
                       pl.BlockSpec((1, 1, 2, lp), lambda bi, hp: (bi, hp, 0, 0))],
                      (main, main, main, ccol, crow4))


def _sb_attn_kernel(x_q, x_k, x_v, o_ref, dkeep_ref, dbias_ref):
    lp = x_q.shape[1]
    w = _key_window(lp, SB_KEY_WINDOW)
    g = w // TQ
    cs = SUFFIX_CHUNK if w % SUFFIX_CHUNK == 0 else TQ
    kr = lax.broadcasted_iota(jnp.int32, (cs, cs), 0)
    kc = lax.broadcasted_iota(jnp.int32, (cs, cs), 1)
    later = jnp.where(kr > kc, 1.0, 0.0).astype(BF16)
    dvis = _diag_window_mask("strict", w)
    dkeep_ref[...] = jnp.where(dvis, 1.0, 0.0)
    dbias_ref[...] = jnp.where(dvis, 0.0, NEG_INF)

    def step(carry, qs, k0, keep=None, bias=None, vis=None):
        rsum, acc = carry
        z = _dot_nt(qs, x_k[0, _keys(k0, w), :])
        cost = jnp.maximum(z, 0.0) + jnp.log2(1.0 + jnp.exp2(-jnp.abs(z)))
        if vis is not None:
            cost = jnp.where(vis, cost, 0.0)
        if keep is not None:
            cost = cost * keep
        cb = cost.astype(BF16)
        parts = []
        for c in reversed(range(w // cs)):
            sl = slice(c * cs, (c + 1) * cs)
            inner = _dot(cb[:, sl], later)
            parts.append(inner + rsum)
            rsum = rsum + inner[:, 0:1] + cb[:, c * cs:c * cs + 1].astype(F32)
        arg = z - cost - jnp.concatenate(parts[::-1], axis=1)
        if bias is not None:
            arg = arg + bias
        wgt = jnp.exp2(arg)
        if vis is not None:
            wgt = jnp.where(vis, wgt, 0.0)
        acc = acc + _dot(wgt.astype(BF16), x_v[0, _keys(k0, w), :])
        return rsum, acc

    def knorm(j, best):
        kk = x_k[0, _rows(j, TQ), :].astype(F32)
        return jnp.maximum(best, jnp.sum(kk * kk, axis=-1, keepdims=True))

    kn2 = jnp.max(lax.fori_loop(0, lp // TQ, knorm, jnp.zeros((TQ, 1), F32)), axis=0, keepdims=True)

    def emit(qi, acc):
        o_ref[0, _rows(qi, TQ), :] = _merge_heads(acc[:TQ], acc[TQ:]).astype(o_ref.dtype)

    def group_fn(qis, mwin):
        qss = [_stack_masked(x_q[0, _rows(qi, TQ), :]) for qi in qis]
        pres = [qi + 1 - mwin * g for qi in qis]
        zero = (jnp.zeros((2 * TQ, 1), F32), jnp.zeros((2 * TQ, LANES), F32))
        if mwin == 0:
            for qi, qs, pre in zip(qis, qss, pres):
                emit(qi, step(zero, qs, 0, vis=_first_window_mask("strict", qi * TQ, w, pre * TQ))[1])
            return
        carries = [step(zero, qs, qi * TQ + TQ - w, keep=dkeep_ref[...], bias=dbias_ref[...])
                   for qi, qs in zip(qis, qss)]

        def rest(cs):
            out = []
            for qs, pre, carry in zip(qss, pres, cs):
                for t in range(mwin - 1, 0, -1):
                    carry = step(carry, qs, (pre + (t - 1) * g) * TQ)
                span = _span_mask(w, pre * TQ)
                out.append(step(carry, qs, 0, keep=jnp.where(span, 1.0, 0.0),
                                bias=jnp.where(span, 0.0, NEG_INF)))
            return tuple(out)

        slack = [jnp.max(jnp.sqrt(jnp.sum(jnp.square(qs.astype(F32)), axis=-1, keepdims=True) * kn2)
                         - carry[0]) for qs, carry in zip(qss, carries)]
        live = functools.reduce(jnp.maximum, slack) > -F32_EXP2_UNDERFLOW
        carries = lax.cond(live, rest, lambda cs: cs, tuple(carries))
        for qi, carry in zip(qis, carries):
            emit(qi, carry[1])

    _for_each_query_group(lp // TQ, w, SB_Q_GROUP, group_fn)


def _sb_attn(sb):
    b, lp, _ = sb.shape
    nb = SB_W // LANES
    return _attn_call(_sb_attn_kernel, "sb_attn", b, lp, nb,
                      [_col_spec(lp, 0), _col_spec(lp, nb), _col_spec(lp, 2 * nb)],
                      (sb, sb, sb), n_masks=2, window=SB_KEY_WINDOW)


def _silu(x):
    return x / (1.0 + jnp.exp(-x))


def _mixer_residual(h_ref, a_ref, b_ref, wa_ref, wb_ref):
    return h_ref[...] + _dot(a_ref[...], wa_ref[...]) + _dot(b_ref[...], wb_ref[...])


def _out_proj_kernel(h_ref, a_ref, b_ref, wa_ref, wb_ref, o_ref):
    o_ref[...] = _mixer_residual(h_ref, a_ref, b_ref, wa_ref, wb_ref)


def _out_proj(h, mix_a, mix_b, wa, wb):
    t, d = h.shape
    tm = _pick(t, (512, 384, 256, 128))
    na, nb = mix_a.shape[1], mix_b.shape[1]
    return pl.pallas_call(
        _out_proj_kernel,
        grid=(t // tm,),
        in_specs=[pl.BlockSpec((tm, d), lambda i: (i, 0)),
                  pl.BlockSpec((tm, na), lambda i: (i, 0)),
                  pl.BlockSpec((tm, nb), lambda i: (i, 0)),
                  pl.BlockSpec((na, d), lambda i: (0, 0)),
                  pl.BlockSpec((nb, d), lambda i: (0, 0))],
        out_specs=pl.BlockSpec((tm, d), lambda i: (i, 0)),
        out_shape=jax.ShapeDtypeStruct((t, d), F32),
        compiler_params=_cparams(("parallel",)),
        name="out_proj",
    )(h, mix_a, mix_b, wa, wb)


def _ffn_kernel(h_ref, a_ref, b_ref, wa_ref, wb_ref, g_ref, wg_ref, wu_ref, wd_ref, o_ref, hn_ref, acc_ref):
    f = pl.program_id(1)

    @pl.when(f == 0)
    def _():
        hmid = _mixer_residual(h_ref, a_ref, b_ref, wa_ref, wb_ref)
        hn_ref[...] = _rms(hmid, g_ref[...]).astype(BF16)
        acc_ref[...] = hmid

    hn = hn_ref[...]
    a = _silu(_dot(hn, wg_ref[...])) * _dot(hn, wu_ref[...])
    acc_ref[...] += _dot(a.astype(BF16), wd_ref[...])

    @pl.when(f == pl.num_programs(1) - 1)
    def _():
        o_ref[...] = acc_ref[...]


def _ffn(h, mix_a, mix_b, wa, wb, g, wg, wu, wd):
    t, d = h.shape
    ff = wg.shape[1]
    na, nb = mix_a.shape[1], mix_b.shape[1]
    tm = _pick(t, (512, 384, 256, 128))
    tf = _pick(ff, (1408, 512, 256, 128))
    return pl.pallas_call(
        _ffn_kernel,
        grid=(t // tm, ff // tf),
        in_specs=[pl.BlockSpec((tm, d), lambda i, f: (i, 0)),
                  pl.BlockSpec((tm, na), lambda i, f: (i, 0)),
                  pl.BlockSpec((tm, nb), lambda i, f: (i, 0)),
                  pl.BlockSpec((na, d), lambda i, f: (0, 0)),
                  pl.BlockSpec((nb, d), lambda i, f: (0, 0)),
                  pl.BlockSpec((1, d), lambda i, f: (0, 0)),
                  pl.BlockSpec((d, tf), lambda i, f: (0, f)),
                  pl.BlockSpec((d, tf), lambda i, f: (0, f)),
                  pl.BlockSpec((tf, d), lambda i, f: (f, 0))],
        out_specs=pl.BlockSpec((tm, d), lambda i, f: (i, 0)),
        out_shape=jax.ShapeDtypeStruct((t, d), F32),
        scratch_shapes=[pltpu.VMEM((tm, d), BF16), pltpu.VMEM((tm, d), F32)],
        compiler_params=_cparams(("parallel", "arbitrary")),
        name="ffn",
    )(h, mix_a, mix_b, wa, wb, g, wg, wu, wd)


def _router_gates(hn, wr_ref):
    a, b, c = _split3(hn)
    w_ab = jnp.concatenate([wr_ref[0], wr_ref[1]], axis=1)
    ya = _dot(a, jnp.concatenate([w_ab, wr_ref[2]], axis=1))
    yb = _dot(b, w_ab)
    logits = (ya[:, :LANES] + ya[:, LANES:2 * LANES] + yb[:, :LANES] + ya[:, 2 * LANES:] + yb[:, LANES:]
              + _dot(c, wr_ref[0]))
    lane = lax.broadcasted_iota(jnp.int32, logits.shape, 1).astype(F32)
    logits = jnp.where(lane < N_EXPERTS, logits, -jnp.inf)
    v1 = jnp.max(logits, axis=-1, keepdims=True)
    i1 = jnp.min(jnp.where(logits == v1, lane, float(LANES)), axis=-1, keepdims=True)
    rest = jnp.where(lane == i1, -jnp.inf, logits)
    v2 = jnp.max(rest, axis=-1, keepdims=True)
    i2 = jnp.min(jnp.where(rest == v2, lane, float(LANES)), axis=-1, keepdims=True)
    e2 = jnp.exp(v2 - v1)
    den = 1.0 + e2
    gate = jnp.where(lane == i1, 1.0 / den, 0.0) + jnp.where(lane == i2, e2 / den, 0.0)
    sel = jnp.where((lane == i1) | (lane == i2), 1.0, 0.0)
    return gate, sel


MOE_TM = 512
MOE_CHUNK = 256
MOE_ALIGN = 16
MOE_ROWS = 1024
SEL_LANE = N_EXPERTS
RANK_LANE = 2 * N_EXPERTS
T_ROWS = 16


def _route_kernel(h_ref, g_ref, wr_ref, hn_ref, route_ref, selt_ref, rankt_ref, cnt_ref):
    tm = h_ref.shape[0]
    hn = _rms(h_ref[...], g_ref[...])
    hn_ref[...] = hn.astype(BF16)
    gate, sel = _router_gates(hn, wr_ref)
    selb = sel.astype(BF16)
    r = lax.broadcasted_iota(jnp.int32, (tm, tm), 0)
    c = lax.broadcasted_iota(jnp.int32, (tm, tm), 1)
    rank = _dot(jnp.where(c < r, 1.0, 0.0).astype(BF16), selb)
    route_ref[...] = gate + pltpu.roll(sel, SEL_LANE, 1) + pltpu.roll(rank, RANK_LANE, 1)
    er = lax.broadcasted_iota(jnp.int32, (T_ROWS, LANES), 0)
    el = lax.broadcasted_iota(jnp.int32, (T_ROWS, LANES), 1)
    pick = jnp.where((er == el) & (er < N_EXPERTS), 1.0, 0.0).astype(BF16)
    selt = _dot_nt(pick, selb)
    selt_ref[0] = selt
    rankt_ref[0] = _dot(selt.astype(BF16), jnp.where(r < c, 1.0, 0.0).astype(BF16))
    cnt_ref[0] = jnp.sum(sel, axis=0, keepdims=True)


def _moe_route(h, g, wr3):
    t, d = h.shape
    tm = _pick(t, (MOE_TM, 384, 256, 128))
    nt = t // tm
    return pl.pallas_call(
        _route_kernel,
        grid=(nt,),
        in_specs=[pl.BlockSpec((tm, d), lambda i: (i, 0)),
                  pl.BlockSpec((1, d), lambda i: (0, 0)),
                  pl.BlockSpec((3, d, LANES), lambda i: (0, 0, 0))],
        out_specs=[pl.BlockSpec((tm, d), lambda i: (i, 0)),
                   pl.BlockSpec((tm, LANES), lambda i: (i, 0)),
                   pl.BlockSpec((1, T_ROWS, tm), lambda i: (i, 0, 0)),
                   pl.BlockSpec((1, T_ROWS, tm), lambda i: (i, 0, 0)),
                   pl.BlockSpec((1, 1, LANES), lambda i: (i, 0, 0))],
        out_shape=[jax.ShapeDtypeStruct((t, d), BF16),
                   jax.ShapeDtypeStruct((t, LANES), F32),
                   jax.ShapeDtypeStruct((nt, T_ROWS, tm), F32),
                   jax.ShapeDtypeStruct((nt, T_ROWS, tm), F32),
                   jax.ShapeDtypeStruct((nt, 1, LANES), F32)],
        compiler_params=_cparams(("parallel",)),
        name="moe_route",
    )(h, g, wr3)


def _moe_plan(cnt, n_row_tiles):
    cnt_al = (cnt + MOE_ALIGN - 1) // MOE_ALIGN * MOE_ALIGN
    tot = jnp.sum(cnt_al, axis=0)
    cap = (tot + MOE_CHUNK + MOE_ROWS - 1) // MOE_ROWS * MOE_ROWS
    base = jnp.cumsum(cap) - cap
    offs = base[None, :] + jnp.cumsum(cnt_al, axis=0) - cnt_al
    starts = jnp.arange(n_row_tiles, dtype=jnp.int32) * MOE_ROWS
    tile_e = jnp.clip(jnp.sum(starts[:, None] >= base[None, :], axis=1) - 1, 0, N_EXPERTS - 1)
    tile_ok = starts < (base + tot)[tile_e]
    return (offs.reshape(-1).astype(jnp.int32), tile_e.astype(jnp.int32), tile_ok.astype(jnp.int32))


def _chunk_copies(n_chunks):
    return [(e, c) for e in range(N_EXPERTS) for c in range(n_chunks)]


def _dispatch_kernel(offs_ref, cnts_ref, hn_ref, selt_ref, rankt_ref, xs_in_ref, xs_ref, stage, sems):
    del xs_in_ref
    i = pl.program_id(0)
    tm = hn_ref.shape[0]
    hn = hn_ref[...]
    ridx = lax.broadcasted_iota(jnp.int32, (MOE_CHUNK, tm), 0).astype(F32)

    def copy(e, c):
        slot = e * (pl.cdiv(tm, MOE_CHUNK)) + c
        row0 = pl.multiple_of(offs_ref[i * N_EXPERTS + e] + c * MOE_CHUNK, MOE_ALIGN)
        return pltpu.make_async_copy(stage.at[slot], xs_ref.at[pl.ds(row0, MOE_CHUNK)], sems.at[slot])

    for e, c in _chunk_copies(pl.cdiv(tm, MOE_CHUNK)):
        @pl.when(c * MOE_CHUNK < cnts_ref[i * N_EXPERTS + e])
        def _(e=e, c=c):
            hit = (rankt_ref[0, e:e + 1, :] - float(c * MOE_CHUNK) == ridx) & (selt_ref[0, e:e + 1, :] > 0.5)
            rows = _dot(jnp.where(hit, 1.0, 0.0).astype(BF16), hn)
            stage[e * (pl.cdiv(tm, MOE_CHUNK)) + c] = rows.astype(BF16)
            copy(e, c).start()

    for e, c in _chunk_copies(pl.cdiv(tm, MOE_CHUNK)):
        @pl.when(c * MOE_CHUNK < cnts_ref[i * N_EXPERTS + e])
        def _(e=e, c=c):
            copy(e, c).wait()


def _moe_dispatch(offs, cnts, hn, selt, rankt, n_rows):
    t, d = hn.shape
    nt, _, tm = selt.shape
    n_slots = N_EXPERTS * (pl.cdiv(tm, MOE_CHUNK))
    xs0 = jnp.zeros((n_rows, d), BF16)
    grid_spec = pltpu.PrefetchScalarGridSpec(
        num_scalar_prefetch=2,
        grid=(nt,),
        in_specs=[pl.BlockSpec((tm, d), lambda i, *_: (i, 0)),
                  pl.BlockSpec((1, T_ROWS, tm), lambda i, *_: (i, 0, 0)),
                  pl.BlockSpec((1, T_ROWS, tm), lambda i, *_: (i, 0, 0)),
                  pl.BlockSpec(memory_space=pl.ANY)],
        out_specs=pl.BlockSpec(memory_space=pl.ANY),
        scratch_shapes=[pltpu.VMEM((n_slots, MOE_CHUNK, d), BF16), pltpu.SemaphoreType.DMA((n_slots,))],
    )
    return pl.pallas_call(
        _dispatch_kernel,
        grid_spec=grid_spec,
        out_shape=jax.ShapeDtypeStruct((n_rows, d), BF16),
        input_output_aliases={5: 0},
        compiler_params=_cparams(("arbitrary",)),
        name="moe_dispatch",
    )(offs, cnts, hn, selt, rankt, xs0)


def _expert_kernel(te_ref, ok_ref, xs_ref, wg_ref, wu_ref, wd_ref, ys_ref, acc_ref):
    s = pl.program_id(0)
    f = pl.program_id(1)
    ok = ok_ref[s] > 0

    @pl.when(ok & (f == 0))
    def _():
        acc_ref[...] = jnp.zeros_like(acc_ref)

    @pl.when(ok)
    def _():
        x = xs_ref[...]
        a = _silu(_dot(x, wg_ref[0].astype(BF16))) * _dot(x, wu_ref[0].astype(BF16))
        acc_ref[...] += _dot(a.astype(BF16), wd_ref[0].astype(BF16))

    last = f == pl.num_programs(1) - 1

    @pl.when(ok & last)
    def _():
        ys_ref[...] = acc_ref[...].astype(ys_ref.dtype)

    @pl.when(jnp.logical_not(ok) & last)
    def _():
        ys_ref[...] = jnp.zeros_like(ys_ref)


def _moe_experts(tile_e, tile_ok, xs, wg, wu, wd):
    n_rows, d = xs.shape
    ff = wg.shape[2]
    tf = _pick(ff, (512, 256, 128))
    nf = ff // tf

    def f_idx(s, f, ok_ref):
        return jnp.where(ok_ref[s] > 0, f, nf - 1)

    grid_spec = pltpu.PrefetchScalarGridSpec(
        num_scalar_prefetch=2,
        grid=(n_rows // MOE_ROWS, nf),
        in_specs=[pl.BlockSpec((MOE_ROWS, d), lambda s, f, te, ok: (s, 0)),
                  pl.BlockSpec((1, d, tf), lambda s, f, te, ok: (te[s], 0, f_idx(s, f, ok))),
                  pl.BlockSpec((1, d, tf), lambda s, f, te, ok: (te[s], 0, f_idx(s, f, ok))),
                  pl.BlockSpec((1, tf, d), lambda s, f, te, ok: (te[s], f_idx(s, f, ok), 0))],
        out_specs=pl.BlockSpec((MOE_ROWS, d), lambda s, f, te, ok: (s, 0)),
        scratch_shapes=[pltpu.VMEM((MOE_ROWS, d), F32)],
    )
    return pl.pallas_call(
        _expert_kernel,
        grid_spec=grid_spec,
        out_shape=jax.ShapeDtypeStruct((n_rows, d), BF16),
        compiler_params=_cparams(("parallel", "arbitrary")),
        name="moe_experts",
    )(tile_e, tile_ok, xs, wg, wu, wd)


def _combine_kernel(offs_ref, cnts_ref, h_ref, route_ref, gout_ref, ys_ref, o_ref, stage, sems):
    i = pl.program_id(0)
    tm = h_ref.shape[0]
    n_chunks = pl.cdiv(tm, MOE_CHUNK)
    ridx = lax.broadcasted_iota(jnp.int32, (tm, MOE_CHUNK), 1).astype(F32)

    def copy(e, c):
        slot = e * n_chunks + c
        row0 = pl.multiple_of(offs_ref[i * N_EXPERTS + e] + c * MOE_CHUNK, MOE_ALIGN)
        return pltpu.make_async_copy(ys_ref.at[pl.ds(row0, MOE_CHUNK)], stage.at[slot], sems.at[slot])

    for e, c in _chunk_copies(n_chunks):
        @pl.when(c * MOE_CHUNK < cnts_ref[i * N_EXPERTS + e])
        def _(e=e, c=c):
            copy(e, c).start()

    o_ref[...] = h_ref[...]
    for e, c in _chunk_copies(n_chunks):
        @pl.when(c * MOE_CHUNK < cnts_ref[i * N_EXPERTS + e])
        def _(e=e, c=c):
            copy(e, c).wait()
            gate = route_ref[:, e:e + 1]
            sel = route_ref[:, SEL_LANE + e:SEL_LANE + e + 1]
            rank = route_ref[:, RANK_LANE + e:RANK_LANE + e + 1]
            hit = (rank - float(c * MOE_CHUNK) == ridx) & (sel > 0.5)
            back = _dot(jnp.where(hit, 1.0, 0.0).astype(BF16), stage[e * n_chunks + c])
            o_ref[...] += gate * back

    o_ref[...] = _rms(o_ref[...], gout_ref[...])


def _moe_combine(offs, cnts, h, route, ys, g_out):
    t, d = h.shape
    tm = _pick(t, (MOE_TM, 384, 256, 128))
    n_slots = N_EXPERTS * (pl.cdiv(tm, MOE_CHUNK))
    grid_spec = pltpu.PrefetchScalarGridSpec(
        num_scalar_prefetch=2,
        grid=(t // tm,),
        in_specs=[pl.BlockSpec((tm, d), lambda i, *_: (i, 0)),
                  pl.BlockSpec((tm, LANES), lambda i, *_: (i, 0)),
                  pl.BlockSpec((1, d), lambda i, *_: (0, 0)),
                  pl.BlockSpec(memory_space=pl.ANY)],
        out_specs=pl.BlockSpec((tm, d), lambda i, *_: (i, 0)),
        scratch_shapes=[pltpu.VMEM((n_slots, MOE_CHUNK, d), BF16), pltpu.SemaphoreType.DMA((n_slots,))],
    )
    return pl.pallas_call(
        _combine_kernel,
        grid_spec=grid_spec,
        out_shape=jax.ShapeDtypeStruct((t, d), F32),
        compiler_params=_cparams(("arbitrary",)),
        name="moe_combine",
    )(offs, cnts, h, route, g_out, ys)


def _moe(h, g, wr3, wg, wu, wd, g_out):
    t, d = h.shape
    hn, route, selt, rankt, cnt = _moe_route(h, g, wr3)
    nt = cnt.shape[0]
    cnt = cnt[:, 0, :N_EXPERTS].astype(jnp.int32)
    worst = TOP_K * t + nt * N_EXPERTS * (MOE_ALIGN - 1) + N_EXPERTS * (MOE_CHUNK + MOE_ROWS - 1)
    n_row_tiles = -(-worst // MOE_ROWS)
    offs, tile_e, tile_ok = _moe_plan(cnt, n_row_tiles)
    cnts = cnt.reshape(-1)
    xs = _moe_dispatch(offs, cnts, hn, selt, rankt, n_row_tiles * MOE_ROWS)
    ys = _moe_experts(tile_e, tile_ok, xs, wg, wu, wd)
    return _moe_combine(offs, cnts, h, route, ys, g_out)


def _rope_table(lp, rot_dim, theta, width, offset, scale):
    half = rot_dim // 2
    pos = (jnp.arange(lp, dtype=jnp.int32) - FIRST).astype(F32)
    inv = theta ** (-jnp.arange(half, dtype=F32) * 2.0 / rot_dim)
    ang = pos[:, None] * inv[None, :]
    cos, sin = jnp.cos(ang), jnp.sin(ang)
    ones = jnp.ones((lp, 1), F32)
    zeros = jnp.zeros((lp, 1), F32)

    def unit(first, second, fill):
        parts = [jnp.tile(fill, (1, offset)), first, second,
                 jnp.tile(fill, (1, width - offset - rot_dim))]
        return jnp.tile(jnp.concatenate(parts, axis=1), (1, LANES // width))

    c = unit(cos, cos, ones)
    s1 = unit(-sin, jnp.zeros_like(sin), zeros)
    s2 = unit(jnp.zeros_like(sin), sin, zeros)
    return jnp.concatenate([c, s1, s2], axis=1) * scale


def _pad_heads(w, heads, used, width):
    k = w.shape[0]
    w = w.reshape(k, heads, used)
    return jnp.pad(w, ((0, 0), (0, 0), (0, width - used))).reshape(k, heads * width)


def kernel(x, meta_tokens, ln_mix_e, w_in_e, ln_mla_q_e, w_mla_uq_e, ln_mla_kv_e, w_mla_ukv_e,
           w_out_e, ln_ffn_e, w_ffn_gate_e, w_ffn_up_e, w_ffn_down_e, ln_mix_o, w_in_o, b_forget_o,
           diff_lq1_o, diff_lk1_o, diff_lq2_o, diff_lk2_o, diff_subln_o, w_out_o, ln_ffn_o,
           w_router_o, w_moe_gate_o, w_moe_up_o, w_moe_down_o, ln_final):
    b, seq, d = x.shape
    assert d == D_MODEL and seq % TQ == 0
    lp = PAD + seq
    t = b * lp
    depth = 2

    meta = jnp.broadcast_to(meta_tokens.astype(x.dtype)[None], (b, N_META, d))
    h = jnp.concatenate([jnp.zeros((b, FIRST, d), x.dtype), meta, x], axis=1).reshape(t, d)

    row = lambda v: v.reshape(1, -1).astype(F32)
    sb_scale = HEAD_DIM ** -0.5
    tab_mla_q = _rope_table(lp, MLA_ROPE, MLA_THETA, LANES, MLA_NOPE,
                            (MLA_NOPE + MLA_ROPE) ** -0.5 * LOG2E)
    tab_mla_k = _rope_table(lp, MLA_ROPE, MLA_THETA, LANES, MLA_NOPE, 1.0)
    tab_diff_q = _rope_table(lp, ROT_DIM, ROPE_THETA, DIFF_DIM, 0, LOG2E)
    tab_diff_k = _rope_table(lp, ROT_DIM, ROPE_THETA, DIFF_DIM, 0, 1.0)

    for i in range(depth):
        j = i // 2
        if i % 2 == 0:
            w_in = w_in_e[j]
            lat_w = MLA_Q_RANK + MLA_KV_RANK + MLA_ROPE
            w_cat = jnp.concatenate(
                [w_in[:, :SB_W] * sb_scale, w_in[:, SB_W:3 * SB_W + lat_w],
                 jnp.zeros((d, 4 * LANES - lat_w), F32)], axis=1).astype(BF16)
            sb, lat = _even_proj(h, row(ln_mix_e[j]), w_cat)
            sb = sb.reshape(b, lp, -1)
            lat = lat.reshape(b, lp, -1)

            wq = _pad_heads(w_mla_uq_e[j], MLA_HEADS, MLA_NOPE + MLA_ROPE, LANES).astype(BF16)
            ukv = w_mla_ukv_e[j].reshape(MLA_KV_RANK, MLA_HEADS, MLA_NOPE + MLA_V)
            wk_nope = _pad_heads(ukv[:, :, :MLA_NOPE].reshape(MLA_KV_RANK, -1), MLA_HEADS, MLA_NOPE, LANES)
            place = jnp.pad(jnp.eye(MLA_ROPE, dtype=F32), ((0, 0), (MLA_NOPE, LANES - MLA_NOPE - MLA_ROPE)))
            wk_rope = jnp.tile(place, (1, MLA_HEADS))
            wk = jnp.concatenate(
                [wk_nope, wk_rope, jnp.zeros((LANES - MLA_ROPE, MLA_HEADS * LANES), F32)], axis=0).astype(BF16)
            wv = ukv[:, :, MLA_NOPE:].reshape(MLA_KV_RANK, -1).astype(BF16)
            mq, mk, mv = _mla_up(lat, row(ln_mla_q_e[j]), row(ln_mla_kv_e[j]), wq, wk, wv,
                                 tab_mla_q, tab_mla_k)

            a_out = _sb_attn(sb).reshape(t, -1)
            b_out = _mla_attn(mq, mk, mv).reshape(t, -1)
            w_out = w_out_e[j].astype(BF16)
            h = _ffn(h, a_out, b_out, w_out[:SB_W], w_out[SB_W:], row(ln_ffn_e[j]),
                     w_ffn_gate_e[j].astype(BF16), w_ffn_up_e[j].astype(BF16), w_ffn_down_e[j].astype(BF16))
        else:
            lam_init = 0.8 - 0.6 * math.exp(-0.3 * i)
            w_in = w_in_o[j]
            wf = w_in[:, N_ODD_MAIN:]
            w_cat = jnp.concatenate(
                [w_in[:, :DIFF_W] * sb_scale, w_in[:, DIFF_W:3 * DIFF_W],
                 w_in[:, 3 * DIFF_W:3 * DIFF_W + FOX_W] * sb_scale, w_in[:, 3 * DIFF_W + FOX_W:N_ODD_MAIN],
                 wf, jnp.zeros((d, LANES - FOX_HEADS), F32)], axis=1).astype(BF16)
            wft = jnp.pad(wf.T, ((0, F_ROWS - FOX_HEADS), (0, 0))).astype(BF16)
            bf = b_forget_o[j].astype(F32)
            bcol = jnp.pad(bf, (0, LANES - FOX_HEADS)).reshape(1, LANES)
            brow = jnp.pad(bf, (0, F_ROWS - FOX_HEADS)).reshape(F_ROWS, 1)
            main, fcol, frow = _odd_proj(h.reshape(b, lp, d), row(ln_mix_o[j]), w_cat, wft, bcol, brow,
                                         tab_diff_q, tab_diff_k)
            ccol, crow = _fox_cumsum(fcol, frow)
            crow4 = crow[:, :FOX_HEADS].reshape(b, FOX_HEADS // 2, 2, lp)
            lam_vecs = jnp.stack([diff_lq1_o[j], diff_lk1_o[j], diff_lq2_o[j], diff_lk2_o[j]]).astype(F32)
            c_out = _diff_attn(main, lam_vecs, row(diff_subln_o[j]), lam_init).reshape(t, -1)
            d_out = _fox_attn(main, ccol, crow4).reshape(t, -1)
            w_out = w_out_o[j].astype(BF16)
            wr = jnp.pad(w_router_o[j].astype(F32), ((0, 0), (0, LANES - N_EXPERTS)))
            wr3 = jnp.stack(_split3(wr))
            h = _out_proj(h, c_out, d_out, w_out[:DIFF_W], w_out[DIFF_W:])
            h = _moe(h, row(ln_ffn_o[j]), wr3, w_moe_gate_o[j], w_moe_up_o[j], w_moe_down_o[j],
                     row(ln_final))

    return h.reshape(b, lp, d)[:, PAD:]
```

```python
import functools
import math

import jax
import jax.numpy as jnp
from jax import lax
from jax.experimental import pallas as pl
from jax.experimental.pallas import tpu as pltpu

F32 = jnp.float32
BF16 = jnp.bfloat16

D_MODEL = 1024
CHUNK = 64
N_META = 16
EPS = 1e-6
LOG2E = 1.4426950408889634
NEG_INF = -1e30

HEAD_DIM = 64
SB_HEADS = 8
MLA_HEADS = 8
MLA_NOPE = 64
MLA_ROPE = 32
MLA_V = 64
MLA_Q_RANK = 256
MLA_KV_RANK = 128
MLA_THETA = 10000.0
DIFF_HEADS = 4
DIFF_DIM = 64
FOX_HEADS = 8
ROPE_THETA = 500000.0
ROT_DIM = HEAD_DIM // 4
N_EXPERTS = 8
TOP_K = 2

SB_W = SB_HEADS * HEAD_DIM
DIFF_W = DIFF_HEADS * 2 * DIFF_DIM
FOX_W = FOX_HEADS * HEAD_DIM

LANES = 128
PAD = 128
FIRST = PAD - N_META
TQ = 128
KEY_WINDOW = 512
Q_GROUP = 4
SB_Q_GROUP = 2
SB_KEY_WINDOW = 512
F32_EXP2_UNDERFLOW = 160.0
SUFFIX_CHUNK = 256
CHUNK_SHIFT = CHUNK.bit_length() - 1
VMEM_LIMIT = 56 * 1024 * 1024


def _cparams(sem):
    return pltpu.CompilerParams(dimension_semantics=sem, vmem_limit_bytes=VMEM_LIMIT)


def _pick(n, cands):
    for c in cands:
        if n % c == 0:
            return c
    raise ValueError(f"no tile in {cands} divides {n}")


def _rms(x, g):
    return x * lax.rsqrt(jnp.mean(x * x, axis=-1, keepdims=True) + EPS) * g


def _dot(a, b):
    return jnp.dot(a, b, preferred_element_type=F32)


def _dot_nt(a, b):
    return lax.dot_general(a, b, (((1,), (1,)), ((), ())), preferred_element_type=F32)


def _rope_chunk(y, tab, shift):
    c = tab[:, 0:LANES]
    s1 = tab[:, LANES:2 * LANES]
    s2 = tab[:, 2 * LANES:3 * LANES]
    return (y * c + pltpu.roll(y, LANES - shift, 1) * s1 + pltpu.roll(y, shift, 1) * s2)


def _split3(x):
    a = x.astype(BF16)
    r = x - a.astype(F32)
    b = r.astype(BF16)
    c = (r - b.astype(F32)).astype(BF16)
    return a, b, c


def _even_proj_kernel(h_ref, g_ref, w_ref, sb_ref, lat_ref):
    hn = _rms(h_ref[...], g_ref[...]).astype(BF16)
    y = _dot(hn, w_ref[...])
    n_sb = sb_ref.shape[-1]
    sb_ref[:, :SB_W] = (y[:, :SB_W] * LOG2E).astype(BF16)
    sb_ref[:, SB_W:] = y[:, SB_W:n_sb].astype(BF16)
    lat_ref[...] = y[:, n_sb:].astype(BF16)


def _even_proj(h, g, w):
    t, d = h.shape
    n = w.shape[1]
    n_sb = 3 * SB_W
    tm = _pick(t, (512, 384, 256, 128))
    return pl.pallas_call(
        _even_proj_kernel,
        grid=(t // tm,),
        in_specs=[pl.BlockSpec((tm, d), lambda i: (i, 0)),
                  pl.BlockSpec((1, d), lambda i: (0, 0)),
                  pl.BlockSpec((d, n), lambda i: (0, 0))],
        out_specs=[pl.BlockSpec((tm, n_sb), lambda i: (i, 0)),
                   pl.BlockSpec((tm, n - n_sb), lambda i: (i, 0))],
        out_shape=[jax.ShapeDtypeStruct((t, n_sb), BF16),
                   jax.ShapeDtypeStruct((t, n - n_sb), BF16)],
        compiler_params=_cparams(("parallel",)),
        name="even_proj",
    )(h, g, w)


def _mla_up_kernel(lat_ref, gq_ref, gkv_ref, wq_ref, wk_ref, wv_ref, tq_ref, tk_ref,
                   q_ref, k_ref, v_ref):
    lat = lat_ref[0]
    nq = _rms(lat[:, :MLA_Q_RANK].astype(F32), gq_ref[...]).astype(BF16)
    nkv = _rms(lat[:, MLA_Q_RANK:MLA_Q_RANK + MLA_KV_RANK].astype(F32), gkv_ref[...]).astype(BF16)
    yq = _dot(nq, wq_ref[...])
    xk = jnp.concatenate([nkv, lat[:, MLA_Q_RANK + MLA_KV_RANK:]], axis=1)
    yk = _dot(xk, wk_ref[...])
    v_ref[0] = _dot(nkv, wv_ref[...]).astype(BF16)
    tabq = tq_ref[...]
    tabk = tk_ref[...]
    for hd in range(MLA_HEADS):
        sl = slice(hd * LANES, (hd + 1) * LANES)
        q_ref[0, :, sl] = _rope_chunk(yq[:, sl], tabq, MLA_ROPE // 2).astype(BF16)
        k_ref[0, :, sl] = _rope_chunk(yk[:, sl], tabk, MLA_ROPE // 2).astype(BF16)


def _mla_up(lat, gq, gkv, wq, wk, wv, tabq, tabk):
    b, lp, nl = lat.shape
    tm = _pick(lp, (384, 128))
    nqk = MLA_HEADS * LANES
    nv = MLA_HEADS * MLA_V
    full = lambda shp: pl.BlockSpec(shp, lambda bi, i: (0,) * len(shp))
    return pl.pallas_call(
        _mla_up_kernel,
        grid=(b, lp // tm),
        in_specs=[pl.BlockSpec((1, tm, nl), lambda bi, i: (bi, i, 0)),
                  full(gq.shape), full(gkv.shape), full(wq.shape), full(wk.shape), full(wv.shape),
                  pl.BlockSpec((tm, 3 * LANES), lambda bi, i: (i, 0)),
                  pl.BlockSpec((tm, 3 * LANES), lambda bi, i: (i, 0))],
        out_specs=[pl.BlockSpec((1, tm, nqk), lambda bi, i: (bi, i, 0)),
                   pl.BlockSpec((1, tm, nqk), lambda bi, i: (bi, i, 0)),
                   pl.BlockSpec((1, tm, nv), lambda bi, i: (bi, i, 0))],
        out_shape=[jax.ShapeDtypeStruct((b, lp, nqk), BF16),
                   jax.ShapeDtypeStruct((b, lp, nqk), BF16),
                   jax.ShapeDtypeStruct((b, lp, nv), BF16)],
        compiler_params=_cparams(("parallel", "parallel")),
        name="mla_up",
    )(lat, gq, gkv, wq, wk, wv, tabq, tabk)


N_ODD_MAIN = 3 * DIFF_W + 3 * FOX_W
F_ROWS = 16


def _odd_proj_kernel(h_ref, g_ref, w_ref, wft_ref, bcol_ref, brow_ref, tabq_ref, tabk_ref,
                     main_ref, fcol_ref, frow_ref):
    hn = _rms(h_ref[0], g_ref[...]).astype(BF16)
    y = _dot(hn, w_ref[...])
    n_rope = DIFF_W // LANES
    for c in range(2 * n_rope):
        sl = slice(c * LANES, (c + 1) * LANES)
        tab = tabq_ref[...] if c < n_rope else tabk_ref[...]
        main_ref[0, :, sl] = _rope_chunk(y[:, sl], tab, ROT_DIM // 2).astype(BF16)
    fq = slice(3 * DIFF_W, 3 * DIFF_W + FOX_W)
    main_ref[0, :, 2 * DIFF_W:3 * DIFF_W] = y[:, 2 * DIFF_W:3 * DIFF_W].astype(BF16)
    main_ref[0, :, fq] = (y[:, fq] * LOG2E).astype(BF16)
    main_ref[0, :, 3 * DIFF_W + FOX_W:] = y[:, 3 * DIFF_W + FOX_W:N_ODD_MAIN].astype(BF16)
    fcol_ref[0] = y[:, N_ODD_MAIN:] + bcol_ref[...]
    frow_ref[0] = _dot_nt(wft_ref[...], hn) + brow_ref[...]


def _odd_proj(h3, g, w, wft, bcol, brow, tabq, tabk):
    b, lp, d = h3.shape
    tm = _pick(lp, (384, 128))
    full = lambda shp: pl.BlockSpec(shp, lambda bi, i: (0,) * len(shp))
    return pl.pallas_call(
        _odd_proj_kernel,
        grid=(b, lp // tm),
        in_specs=[pl.BlockSpec((1, tm, d), lambda bi, i: (bi, i, 0)),
                  full(g.shape), full(w.shape), full(wft.shape), full(bcol.shape), full(brow.shape),
                  pl.BlockSpec((tm, 3 * LANES), lambda bi, i: (i, 0)),
                  pl.BlockSpec((tm, 3 * LANES), lambda bi, i: (i, 0))],
        out_specs=[pl.BlockSpec((1, tm, N_ODD_MAIN), lambda bi, i: (bi, i, 0)),
                   pl.BlockSpec((1, tm, LANES), lambda bi, i: (bi, i, 0)),
                   pl.BlockSpec((1, F_ROWS, tm), lambda bi, i: (bi, 0, i))],
        out_shape=[jax.ShapeDtypeStruct((b, lp, N_ODD_MAIN), BF16),
                   jax.ShapeDtypeStruct((b, lp, LANES), F32),
                   jax.ShapeDtypeStruct((b, F_ROWS, lp), F32)],
        compiler_params=_cparams(("parallel", "parallel")),
        name="odd_proj",
    )(h3, g, w, wft, bcol, brow, tabq, tabk)


def _log_sigmoid(x):
    return jnp.minimum(x, 0.0) - jnp.log(1.0 + jnp.exp(-jnp.abs(x)))


def _fox_cumsum_kernel(fcol_ref, frow_ref, ccol_ref, crow_ref):
    lp = fcol_ref.shape[1]
    blk = LANES
    r = lax.broadcasted_iota(jnp.int32, (blk, blk), 0)
    c = lax.broadcasted_iota(jnp.int32, (blk, blk), 1)
    tri_l = (c <= r).astype(BF16)
    tri_u = (r <= c).astype(BF16)
    row_id = lax.broadcasted_iota(jnp.int32, (blk, LANES), 0)
    col_id = lax.broadcasted_iota(jnp.int32, (F_ROWS, blk), 1)
    carry_c = jnp.zeros((1, LANES), F32)
    carry_r = jnp.zeros((F_ROWS, 1), F32)
    for j in range(lp // blk):
        sl = slice(j * blk, (j + 1) * blk)
        ls = _log_sigmoid(fcol_ref[0, sl, :])
        ls = jnp.where(row_id + j * blk >= FIRST, ls, 0.0)
        a, b, c3 = _split3(ls)
        cs = _dot(tri_l, a) + _dot(tri_l, b) + _dot(tri_l, c3) + carry_c
        ccol_ref[0, sl, :] = cs * LOG2E
        carry_c = cs[blk - 1:blk, :]
        lr = _log_sigmoid(frow_ref[0, :, sl])
        lr = jnp.where(col_id + j * blk >= FIRST, lr, 0.0)
        a, b, c3 = _split3(lr)
        cr = _dot(a, tri_u) + _dot(b, tri_u) + _dot(c3, tri_u) + carry_r
        crow_ref[0, :, sl] = cr * LOG2E
        carry_r = cr[:, blk - 1:blk]


def _fox_cumsum(fcol, frow):
    b, lp, _ = fcol.shape
    return pl.pallas_call(
        _fox_cumsum_kernel,
        grid=(b,),
        in_specs=[pl.BlockSpec((1, lp, LANES), lambda bi: (bi, 0, 0)),
                  pl.BlockSpec((1, F_ROWS, lp), lambda bi: (bi, 0, 0))],
        out_specs=[pl.BlockSpec((1, lp, LANES), lambda bi: (bi, 0, 0)),
                   pl.BlockSpec((1, F_ROWS, lp), lambda bi: (bi, 0, 0))],
        out_shape=[jax.ShapeDtypeStruct((b, lp, LANES), F32),
                   jax.ShapeDtypeStruct((b, F_ROWS, lp), F32)],
        compiler_params=_cparams(("parallel",)),
        name="fox_cumsum",
    )(fcol, frow)


def _rows(i, n):
    if isinstance(i, int):
        return pl.ds(i * n, n)
    return pl.ds(pl.multiple_of(i * n, n), n)


def _keys(k0, w):
    if isinstance(k0, int):
        return pl.ds(k0, w)
    return pl.ds(pl.multiple_of(k0, TQ), w)


def _visible(kind, ks, qs):
    if kind == "chunk":
        return (ks >> CHUNK_SHIFT) <= (qs >> CHUNK_SHIFT)
    if kind == "causal":
        return ks <= qs
    return ks < qs


def _first_window_mask(kind, q0, w, lim):
    r = lax.broadcasted_iota(jnp.int32, (2 * TQ, w), 0)
    ks = lax.broadcasted_iota(jnp.int32, (2 * TQ, w), 1)
    return _visible(kind, ks, q0 + (r & (TQ - 1))) & (ks >= FIRST) & (ks < lim)


def _diag_window_mask(kind, w):
    r = lax.broadcasted_iota(jnp.int32, (2 * TQ, w), 0)
    ks = lax.broadcasted_iota(jnp.int32, (2 * TQ, w), 1) + (TQ - w)
    return _visible(kind, ks, r & (TQ - 1))


def _span_mask(w, lim):
    ks = lax.broadcasted_iota(jnp.int32, (1, w), 1)
    return (ks >= FIRST) & (ks < lim)


def _lane_halves():
    lane = lax.broadcasted_iota(jnp.int32, (TQ, LANES), 1)
    return lane < (LANES // 2)


def _stack_masked(q):
    lo = _lane_halves()
    z = jnp.zeros_like(q)
    return jnp.concatenate([jnp.where(lo, q, z), jnp.where(lo, z, q)], axis=0)


def _softmax_sweep(kind, w, prep, scores_fn, v_fn, dbias_ref):
    def step(carry, s, v):
        m, acc = carry
        m_new = jnp.maximum(m, jnp.max(s, axis=-1, keepdims=True))
        alpha = jnp.exp2(m - m_new)
        p = jnp.exp2(s - m_new)
        ones = jnp.ones((v.shape[0], LANES), BF16)
        acc = alpha * acc + _dot(p.astype(BF16), jnp.concatenate([v, ones], axis=1))
        return m_new, acc

    def run(qi, mwin):
        g = w // TQ
        ctx = prep(qi)
        q0 = qi * TQ
        pre = qi + 1 - mwin * g
        n0 = w
        carry = (jnp.full((2 * TQ, 1), NEG_INF, F32), jnp.zeros((2 * TQ, 2 * LANES), F32))
        s = scores_fn(ctx, 0, n0)
        if mwin == 0:
            s = jnp.where(_first_window_mask(kind, q0, n0, pre * TQ), s, NEG_INF)
            return step(carry, s, v_fn(0, n0))[1]
        carry = step(carry, s + jnp.where(_span_mask(n0, pre * TQ), 0.0, NEG_INF), v_fn(0, n0))
        for t in range(1, mwin):
            k0 = (pre + (t - 1) * g) * TQ
            carry = step(carry, scores_fn(ctx, k0, w), v_fn(k0, w))
        k0 = q0 + TQ - w
        return step(carry, scores_fn(ctx, k0, w) + dbias_ref[...], v_fn(k0, w))[1]

    return run


def _for_each_query_group(nq, w, q_group, group_fn):
    g = w // TQ
    one = jnp.sign(pl.program_id(0) + 1)
    for mwin in range(-(-nq // g)):
        lo, hi = mwin * g, min((mwin + 1) * g, nq)

        n_groups = (hi - lo) // q_group

        def body(it, _, mwin=mwin, lo=lo):
            qi = lo + q_group * it
            group_fn([qi + j for j in range(q_group)], mwin)
            return 0

        if n_groups:
            lax.fori_loop(0, n_groups * one, body, 0)
        for qi in range(lo + n_groups * q_group, hi):
            group_fn([qi], mwin)


def _for_each_query_block(nq, w, run, emit):
    def group_fn(qis, mwin):
        results = [run(qi, mwin) for qi in qis]
        for qi, res in zip(qis, results):
            emit(qi, res)

    _for_each_query_group(nq, w, Q_GROUP, group_fn)


def _key_window(lp, limit=None):
    return min(KEY_WINDOW if limit is None else limit, lp)


def _merge_heads(top, bottom):
    return jnp.where(_lane_halves(), top, bottom)


def _attn_call(kernel, name, b, lp, n_blocks, in_specs, args, n_masks=1, window=None):
    w = _key_window(lp, window)
    return pl.pallas_call(
        kernel,
        grid=(b, n_blocks),
        in_specs=in_specs,
        out_specs=pl.BlockSpec((1, lp, LANES), lambda bi, hp: (bi, 0, hp)),
        out_shape=jax.ShapeDtypeStruct((b, lp, n_blocks * LANES), BF16),
        scratch_shapes=[pltpu.VMEM((2 * TQ, w), F32)] * n_masks,
        compiler_params=_cparams(("parallel", "parallel")),
        name=name,
    )(*args)


def _normalised(acc):
    return acc[:, :LANES] / acc[:, LANES:]


def _col_spec(lp, off):
    return pl.BlockSpec((1, lp, LANES), lambda bi, hp: (bi, 0, off + hp))


def _mla_attn_kernel(q_ref, k_ref, v_ref, o_ref, dbias_ref):
    lp = q_ref.shape[1]
    w = _key_window(lp)
    dbias_ref[...] = jnp.where(_diag_window_mask("chunk", w), 0.0, NEG_INF)

    def prep(qi):
        return q_ref[0, _rows(qi, TQ), 0:LANES], q_ref[0, _rows(qi, TQ), LANES:2 * LANES]

    def scores(ctx, k0, n):
        qa, qb = ctx
        ka = k_ref[0, _keys(k0, n), 0:LANES]
        kb = k_ref[0, _keys(k0, n), LANES:2 * LANES]
        return jnp.concatenate([_dot_nt(qa, ka), _dot_nt(qb, kb)], axis=0)

    run = _softmax_sweep("chunk", w, prep, scores, lambda k0, n: v_ref[0, _keys(k0, n), :], dbias_ref)

    def emit(qi, acc):
        o = _normalised(acc)
        o_ref[0, _rows(qi, TQ), :] = _merge_heads(o[:TQ], o[TQ:]).astype(o_ref.dtype)

    _for_each_query_block(lp // TQ, w, run, emit)


def _mla_attn(q, k, v):
    b, lp, _ = q.shape
    wide = lambda: pl.BlockSpec((1, lp, 2 * LANES), lambda bi, hp: (bi, 0, hp))
    return _attn_call(_mla_attn_kernel, "mla_attn", b, lp, MLA_HEADS // 2,
                      [wide(), wide(), _col_spec(lp, 0)], (q, k, v))


def _diff_attn_kernel(lam_init, x_q, x_k, x_v, lam_ref, gsub_ref, o_ref, dbias_ref):
    lp = x_q.shape[1]
    w = _key_window(lp)
    dbias_ref[...] = jnp.where(_diag_window_mask("chunk", w), 0.0, NEG_INF)
    lv = lam_ref[...]
    lam = (jnp.exp(jnp.sum(lv[0:1] * lv[1:2], axis=-1, keepdims=True))
           - jnp.exp(jnp.sum(lv[2:3] * lv[3:4], axis=-1, keepdims=True)) + lam_init)
    gsub = gsub_ref[...]

    def prep(qi):
        return _stack_masked(x_q[0, _rows(qi, TQ), :])

    def scores(qs, k0, n):
        return _dot_nt(qs, x_k[0, _keys(k0, n), :])

    run = _softmax_sweep("chunk", w, prep, scores, lambda k0, n: x_v[0, _keys(k0, n), :], dbias_ref)

    def emit(qi, acc):
        o = _normalised(acc)
        o = o[:TQ] - lam * o[TQ:]
        o_ref[0, _rows(qi, TQ), :] = (_rms(o, gsub) * (1.0 - lam_init)).astype(o_ref.dtype)

    _for_each_query_block(lp // TQ, w, run, emit)


def _diff_attn(main, lam_vecs, gsub, lam_init):
    b, lp, _ = main.shape
    nb = DIFF_W // LANES
    full = lambda shp: pl.BlockSpec(shp, lambda bi, hp: (0,) * len(shp))
    return _attn_call(functools.partial(_diff_attn_kernel, lam_init), "diff_attn", b, lp, DIFF_HEADS,
                      [_col_spec(lp, 0), _col_spec(lp, nb), _col_spec(lp, 2 * nb),
                       full(lam_vecs.shape), full(gsub.shape)],
                      (main, main, main, lam_vecs, gsub))


def _fox_attn_kernel(x_q, x_k, x_v, ccol_ref, crow_ref, o_ref, dbias_ref):
    lp = x_q.shape[1]
    w = _key_window(lp)
    dbias_ref[...] = jnp.where(_diag_window_mask("causal", w), 0.0, NEG_INF)
    hp = pl.program_id(1)
    lane = lax.broadcasted_iota(jnp.int32, (TQ, LANES), 1)

    def prep(qi):
        cc = ccol_ref[0, _rows(qi, TQ), :]
        cq0 = jnp.sum(jnp.where(lane == 2 * hp, cc, 0.0), axis=-1, keepdims=True)
        cq1 = jnp.sum(jnp.where(lane == 2 * hp + 1, cc, 0.0), axis=-1, keepdims=True)
        return _stack_masked(x_q[0, _rows(qi, TQ), :]), cq0, cq1

    def scores(ctx, k0, n):
        qs, cq0, cq1 = ctx
        s = _dot_nt(qs, x_k[0, _keys(k0, n), :])
        ck = crow_ref[0, 0, :, _keys(k0, n)]
        return s + jnp.concatenate([cq0 - ck[0:1, :], cq1 - ck[1:2, :]], axis=0)

    run = _softmax_sweep("causal", w, prep, scores, lambda k0, n: x_v[0, _keys(k0, n), :], dbias_ref)

    def emit(qi, acc):
        o = _normalised(acc)
        o_ref[0, _rows(qi, TQ), :] = _merge_heads(o[:TQ], o[TQ:]).astype(o_ref.dtype)

    _for_each_query_block(lp // TQ, w, run, emit)


def _fox_attn(main, ccol, crow4):
    b, lp, _ = main.shape
    off = 3 * DIFF_W // LANES
    nb = FOX_W // LANES
    return _attn_call(_fox_attn_kernel, "fox_attn", b, lp, nb,
                      [_col_spec(lp, off), _col_spec(lp, off + nb), _col_spec(lp, off + 2 * nb),
                       pl.BlockSpec((1, lp, LANES), lambda bi, hp: (bi, 0, 0)),
                       pl.BlockSpec((1, 1, 2, lp), lambda bi, hp: (bi, hp, 0, 0))],
                      (main, main, main, ccol, crow4))


def _sb_attn_kernel(x_q, x_k, x_v, o_ref, dkeep_ref, dbias_ref):
    lp = x_q.shape[1]
    w = _key_window(lp, SB_KEY_WINDOW)
    g = w // TQ
    cs = SUFFIX_CHUNK if w % SUFFIX_CHUNK == 0 else TQ
    kr = lax.broadcasted_iota(jnp.int32, (cs, cs), 0)
    kc = lax.broadcasted_iota(jnp.int32, (cs, cs), 1)
    later = jnp.where(kr > kc, 1.0, 0.0).astype(BF16)
    dvis = _diag_window_mask("strict", w)
    dkeep_ref[...] = jnp.where(dvis, 1.0, 0.0)
    dbias_ref[...] = jnp.where(dvis, 0.0, NEG_INF)

    def step(carry, qs, k0, keep=None, bias=None, vis=None):
        rsum, acc = carry
        z = _dot_nt(qs, x_k[0, _keys(k0, w), :])
        cost = jnp.maximum(z, 0.0) + jnp.log2(1.0 + jnp.exp2(-jnp.abs(z)))
        if vis is not None:
            cost = jnp.where(vis, cost, 0.0)
        if keep is not None:
            cost = cost * keep
        cb = cost.astype(BF16)
        parts = []
        for c in reversed(range(w // cs)):
            sl = slice(c * cs, (c + 1) * cs)
            inner = _dot(cb[:, sl], later)
            parts.append(inner + rsum)
            rsum = rsum + inner[:, 0:1] + cb[:, c * cs:c * cs + 1].astype(F32)
        arg = z - cost - jnp.concatenate(parts[::-1], axis=1)
        if bias is not None:
            arg = arg + bias
        wgt = jnp.exp2(arg)
        if vis is not None:
            wgt = jnp.where(vis, wgt, 0.0)
        acc = acc + _dot(wgt.astype(BF16), x_v[0, _keys(k0, w), :])
        return rsum, acc

    def knorm(j, best):
        kk = x_k[0, _rows(j, TQ), :].astype(F32)
        return jnp.maximum(best, jnp.sum(kk * kk, axis=-1, keepdims=True))

    kn2 = jnp.max(lax.fori_loop(0, lp // TQ, knorm, jnp.zeros((TQ, 1), F32)), axis=0, keepdims=True)

    def emit(qi, acc):
        o_ref[0, _rows(qi, TQ), :] = _merge_heads(acc[:TQ], acc[TQ:]).astype(o_ref.dtype)

    def group_fn(qis, mwin):
        qss = [_stack_masked(x_q[0, _rows(qi, TQ), :]) for qi in qis]
        pres = [qi + 1 - mwin * g for qi in qis]
        zero = (jnp.zeros((2 * TQ, 1), F32), jnp.zeros((2 * TQ, LANES), F32))
        if mwin == 0:
            for qi, qs, pre in zip(qis, qss, pres):
                emit(qi, step(zero, qs, 0, vis=_first_window_mask("strict", qi * TQ, w, pre * TQ))[1])
            return
        carries = [step(zero, qs, qi * TQ + TQ - w, keep=dkeep_ref[...], bias=dbias_ref[...])
                   for qi, qs in zip(qis, qss)]

        def rest(cs):
            out = []
            for qs, pre, carry in zip(qss, pres, cs):
                for t in range(mwin - 1, 0, -1):
                    carry = step(carry, qs, (pre + (t - 1) * g) * TQ)
                span = _span_mask(w, pre * TQ)
                out.append(step(carry, qs, 0, keep=jnp.where(span, 1.0, 0.0),
                                bias=jnp.where(span, 0.0, NEG_INF)))
            return tuple(out)

        slack = [jnp.max(jnp.sqrt(jnp.sum(jnp.square(qs.astype(F32)), axis=-1, keepdims=True) * kn2)
                         - carry[0]) for qs, carry in zip(qss, carries)]
        live = functools.reduce(jnp.maximum, slack) > -F32_EXP2_UNDERFLOW
        carries = lax.cond(live, rest, lambda cs: cs, tuple(carries))
        for qi, carry in zip(qis, carries):
            emit(qi, carry[1])

    _for_each_query_group(lp // TQ, w, SB_Q_GROUP, group_fn)


def _sb_attn(sb):
    b, lp, _ = sb.shape
    nb = SB_W // LANES
    return _attn_call(_sb_attn_kernel, "sb_attn", b, lp, nb,
                      [_col_spec(lp, 0), _col_spec(lp, nb), _col_spec(lp, 2 * nb)],
                      (sb, sb, sb), n_masks=2, window=SB_KEY_WINDOW)


def _silu(x):
    return x / (1.0 + jnp.exp(-x))


def _mixer_residual(h_ref, a_ref, b_ref, wa_ref, wb_ref):
    return h_ref[...] + _dot(a_ref[...], wa_ref[...]) + _dot(b_ref[...], wb_ref[...])


def _out_proj_kernel(h_ref, a_ref, b_ref, wa_ref, wb_ref, o_ref):
    o_ref[...] = _mixer_residual(h_ref, a_ref, b_ref, wa_ref, wb_ref)


def _out_proj(h, mix_a, mix_b, wa, wb):
    t, d = h.shape
    tm = _pick(t, (512, 384, 256, 128))
    na, nb = mix_a.shape[1], mix_b.shape[1]
    return pl.pallas_call(
        _out_proj_kernel,
        grid=(t // tm,),
        in_specs=[pl.BlockSpec((tm, d), lambda i: (i, 0)),
                  pl.BlockSpec((tm, na), lambda i: (i, 0)),
                  pl.BlockSpec((tm, nb), lambda i: (i, 0)),
                  pl.BlockSpec((na, d), lambda i: (0, 0)),
                  pl.BlockSpec((nb, d), lambda i: (0, 0))],
        out_specs=pl.BlockSpec((tm, d), lambda i: (i, 0)),
        out_shape=jax.ShapeDtypeStruct((t, d), F32),
        compiler_params=_cparams(("parallel",)),
        name="out_proj",
    )(h, mix_a, mix_b, wa, wb)


def _ffn_kernel(h_ref, a_ref, b_ref, wa_ref, wb_ref, g_ref, wg_ref, wu_ref, wd_ref, o_ref, hn_ref, acc_ref):
    f = pl.program_id(1)

    @pl.when(f == 0)
    def _():
        hmid = _mixer_residual(h_ref, a_ref, b_ref, wa_ref, wb_ref)
        hn_ref[...] = _rms(hmid, g_ref[...]).astype(BF16)
        acc_ref[...] = hmid

    hn = hn_ref[...]
    a = _silu(_dot(hn, wg_ref[...])) * _dot(hn, wu_ref[...])
    acc_ref[...] += _dot(a.astype(BF16), wd_ref[...])

    @pl.when(f == pl.num_programs(1) - 1)
    def _():
        o_ref[...] = acc_ref[...]


def _ffn(h, mix_a, mix_b, wa, wb, g, wg, wu, wd):
    t, d = h.shape
    ff = wg.shape[1]
    na, nb = mix_a.shape[1], mix_b.shape[1]
    tm = _pick(t, (512, 384, 256, 128))
    tf = _pick(ff, (1408, 512, 256, 128))
    return pl.pallas_call(
        _ffn_kernel,
        grid=(t // tm, ff // tf),
        in_specs=[pl.BlockSpec((tm, d), lambda i, f: (i, 0)),
                  pl.BlockSpec((tm, na), lambda i, f: (i, 0)),
                  pl.BlockSpec((tm, nb), lambda i, f: (i, 0)),
                  pl.BlockSpec((na, d), lambda i, f: (0, 0)),
                  pl.BlockSpec((nb, d), lambda i, f: (0, 0)),
                  pl.BlockSpec((1, d), lambda i, f: (0, 0)),
                  pl.BlockSpec((d, tf), lambda i, f: (0, f)),
                  pl.BlockSpec((d, tf), lambda i, f: (0, f)),
                  pl.BlockSpec((tf, d), lambda i, f: (f, 0))],
        out_specs=pl.BlockSpec((tm, d), lambda i, f: (i, 0)),
        out_shape=jax.ShapeDtypeStruct((t, d), F32),
        scratch_shapes=[pltpu.VMEM((tm, d), BF16), pltpu.VMEM((tm, d), F32)],
        compiler_params=_cparams(("parallel", "arbitrary")),
        name="ffn",
    )(h, mix_a, mix_b, wa, wb, g, wg, wu, wd)


def _router_gates(hn, wr_ref):
    a, b, c = _split3(hn)
    w_ab = jnp.concatenate([wr_ref[0], wr_ref[1]], axis=1)
    ya = _dot(a, jnp.concatenate([w_ab, wr_ref[2]], axis=1))
    yb = _dot(b, w_ab)
    logits = (ya[:, :LANES] + ya[:, LANES:2 * LANES] + yb[:, :LANES] + ya[:, 2 * LANES:] + yb[:, LANES:]
              + _dot(c, wr_ref[0]))
    lane = lax.broadcasted_iota(jnp.int32, logits.shape, 1).astype(F32)
    logits = jnp.where(lane < N_EXPERTS, logits, -jnp.inf)
    v1 = jnp.max(logits, axis=-1, keepdims=True)
    i1 = jnp.min(jnp.where(logits == v1, lane, float(LANES)), axis=-1, keepdims=True)
    rest = jnp.where(lane == i1, -jnp.inf, logits)
    v2 = jnp.max(rest, axis=-1, keepdims=True)
    i2 = jnp.min(jnp.where(rest == v2, lane, float(LANES)), axis=-1, keepdims=True)
    e2 = jnp.exp(v2 - v1)
    den = 1.0 + e2
    gate = jnp.where(lane == i1, 1.0 / den, 0.0) + jnp.where(lane == i2, e2 / den, 0.0)
    sel = jnp.where((lane == i1) | (lane == i2), 1.0, 0.0)
    return gate, sel


MOE_TM = 512
MOE_CHUNK = 256
MOE_ALIGN = 16
MOE_ROWS = 1024
SEL_LANE = N_EXPERTS
RANK_LANE = 2 * N_EXPERTS
T_ROWS = 16


def _route_kernel(h_ref, g_ref, wr_ref, hn_ref, route_ref, selt_ref, rankt_ref, cnt_ref):
    tm = h_ref.shape[0]
    hn = _rms(h_ref[...], g_ref[...])
    hn_ref[...] = hn.astype(BF16)
    gate, sel = _router_gates(hn, wr_ref)
    selb = sel.astype(BF16)
    r = lax.broadcasted_iota(jnp.int32, (tm, tm), 0)
    c = lax.broadcasted_iota(jnp.int32, (tm, tm), 1)
    rank = _dot(jnp.where(c < r, 1.0, 0.0).astype(BF16), selb)
    route_ref[...] = gate + pltpu.roll(sel, SEL_LANE, 1) + pltpu.roll(rank, RANK_LANE, 1)
    er = lax.broadcasted_iota(jnp.int32, (T_ROWS, LANES), 0)
    el = lax.broadcasted_iota(jnp.int32, (T_ROWS, LANES), 1)
    pick = jnp.where((er == el) & (er < N_EXPERTS), 1.0, 0.0).astype(BF16)
    selt = _dot_nt(pick, selb)
    selt_ref[0] = selt
    rankt_ref[0] = _dot(selt.astype(BF16), jnp.where(r < c, 1.0, 0.0).astype(BF16))
    cnt_ref[0] = jnp.sum(sel, axis=0, keepdims=True)


def _moe_route(h, g, wr3):
    t, d = h.shape
    tm = _pick(t, (MOE_TM, 384, 256, 128))
    nt = t // tm
    return pl.pallas_call(
        _route_kernel,
        grid=(nt,),
        in_specs=[pl.BlockSpec((tm, d), lambda i: (i, 0)),
                  pl.BlockSpec((1, d), lambda i: (0, 0)),
                  pl.BlockSpec((3, d, LANES), lambda i: (0, 0, 0))],
        out_specs=[pl.BlockSpec((tm, d), lambda i: (i, 0)),
                   pl.BlockSpec((tm, LANES), lambda i: (i, 0)),
                   pl.BlockSpec((1, T_ROWS, tm), lambda i: (i, 0, 0)),
                   pl.BlockSpec((1, T_ROWS, tm), lambda i: (i, 0, 0)),
                   pl.BlockSpec((1, 1, LANES), lambda i: (i, 0, 0))],
        out_shape=[jax.ShapeDtypeStruct((t, d), BF16),
                   jax.ShapeDtypeStruct((t, LANES), F32),
                   jax.ShapeDtypeStruct((nt, T_ROWS, tm), F32),
                   jax.ShapeDtypeStruct((nt, T_ROWS, tm), F32),
                   jax.ShapeDtypeStruct((nt, 1, LANES), F32)],
        compiler_params=_cparams(("parallel",)),
        name="moe_route",
    )(h, g, wr3)


def _moe_plan(cnt, n_row_tiles):
    cnt_al = (cnt + MOE_ALIGN - 1) // MOE_ALIGN * MOE_ALIGN
    tot = jnp.sum(cnt_al, axis=0)
    cap = (tot + MOE_CHUNK + MOE_ROWS - 1) // MOE_ROWS * MOE_ROWS
    base = jnp.cumsum(cap) - cap
    offs = base[None, :] + jnp.cumsum(cnt_al, axis=0) - cnt_al
    starts = jnp.arange(n_row_tiles, dtype=jnp.int32) * MOE_ROWS
    tile_e = jnp.clip(jnp.sum(starts[:, None] >= base[None, :], axis=1) - 1, 0, N_EXPERTS - 1)
    tile_ok = starts < (base + tot)[tile_e]
    return (offs.reshape(-1).astype(jnp.int32), tile_e.astype(jnp.int32), tile_ok.astype(jnp.int32))


def _chunk_copies(n_chunks):
    return [(e, c) for e in range(N_EXPERTS) for c in range(n_chunks)]


def _dispatch_kernel(offs_ref, cnts_ref, hn_ref, selt_ref, rankt_ref, xs_in_ref, xs_ref, stage, sems):
    del xs_in_ref
    i = pl.program_id(0)
    tm = hn_ref.shape[0]
    hn = hn_ref[...]
    ridx = lax.broadcasted_iota(jnp.int32, (MOE_CHUNK, tm), 0).astype(F32)

    def copy(e, c):
        slot = e * (pl.cdiv(tm, MOE_CHUNK)) + c
        row0 = pl.multiple_of(offs_ref[i * N_EXPERTS + e] + c * MOE_CHUNK, MOE_ALIGN)
        return pltpu.make_async_copy(stage.at[slot], xs_ref.at[pl.ds(row0, MOE_CHUNK)], sems.at[slot])

    for e, c in _chunk_copies(pl.cdiv(tm, MOE_CHUNK)):
        @pl.when(c * MOE_CHUNK < cnts_ref[i * N_EXPERTS + e])
        def _(e=e, c=c):
            hit = (rankt_ref[0, e:e + 1, :] - float(c * MOE_CHUNK) == ridx) & (selt_ref[0, e:e + 1, :] > 0.5)
            rows = _dot(jnp.where(hit, 1.0, 0.0).astype(BF16), hn)
            stage[e * (pl.cdiv(tm, MOE_CHUNK)) + c] = rows.astype(BF16)
            copy(e, c).start()

    for e, c in _chunk_copies(pl.cdiv(tm, MOE_CHUNK)):
        @pl.when(c * MOE_CHUNK < cnts_ref[i * N_EXPERTS + e])
        def _(e=e, c=c):
            copy(e, c).wait()


def _moe_dispatch(offs, cnts, hn, selt, rankt, n_rows):
    t, d = hn.shape
    nt, _, tm = selt.shape
    n_slots = N_EXPERTS * (pl.cdiv(tm, MOE_CHUNK))
    xs0 = jnp.zeros((n_rows, d), BF16)
    grid_spec = pltpu.PrefetchScalarGridSpec(
        num_scalar_prefetch=2,
        grid=(nt,),
        in_specs=[pl.BlockSpec((tm, d), lambda i, *_: (i, 0)),
                  pl.BlockSpec((1, T_ROWS, tm), lambda i, *_: (i, 0, 0)),
                  pl.BlockSpec((1, T_ROWS, tm), lambda i, *_: (i, 0, 0)),
                  pl.BlockSpec(memory_space=pl.ANY)],
        out_specs=pl.BlockSpec(memory_space=pl.ANY),
        scratch_shapes=[pltpu.VMEM((n_slots, MOE_CHUNK, d), BF16), pltpu.SemaphoreType.DMA((n_slots,))],
    )
    return pl.pallas_call(
        _dispatch_kernel,
        grid_spec=grid_spec,
        out_shape=jax.ShapeDtypeStruct((n_rows, d), BF16),
        input_output_aliases={5: 0},
        compiler_params=_cparams(("arbitrary",)),
        name="moe_dispatch",
    )(offs, cnts, hn, selt, rankt, xs0)


def _expert_kernel(te_ref, ok_ref, xs_ref, wg_ref, wu_ref, wd_ref, ys_ref, acc_ref):
    s = pl.program_id(0)
    f = pl.program_id(1)
    ok = ok_ref[s] > 0

    @pl.when(ok & (f == 0))
    def _():
        acc_ref[...] = jnp.zeros_like(acc_ref)

    @pl.when(ok)
    def _():
        x = xs_ref[...]
        a = _silu(_dot(x, wg_ref[0].astype(BF16))) * _dot(x, wu_ref[0].astype(BF16))
        acc_ref[...] += _dot(a.astype(BF16), wd_ref[0].astype(BF16))

    last = f == pl.num_programs(1) - 1

    @pl.when(ok & last)
    def _():
        ys_ref[...] = acc_ref[...].astype(ys_ref.dtype)

    @pl.when(jnp.logical_not(ok) & last)
    def _():
        ys_ref[...] = jnp.zeros_like(ys_ref)


def _moe_experts(tile_e, tile_ok, xs, wg, wu, wd):
    n_rows, d = xs.shape
    ff = wg.shape[2]
    tf = _pick(ff, (512, 256, 128))
    nf = ff // tf

    def f_idx(s, f, ok_ref):
        return jnp.where(ok_ref[s] > 0, f, nf - 1)

    grid_spec = pltpu.PrefetchScalarGridSpec(
        num_scalar_prefetch=2,
        grid=(n_rows // MOE_ROWS, nf),
        in_specs=[pl.BlockSpec((MOE_ROWS, d), lambda s, f, te, ok: (s, 0)),
                  pl.BlockSpec((1, d, tf), lambda s, f, te, ok: (te[s], 0, f_idx(s, f, ok))),
                  pl.BlockSpec((1, d, tf), lambda s, f, te, ok: (te[s], 0, f_idx(s, f, ok))),
                  pl.BlockSpec((1, tf, d), lambda s, f, te, ok: (te[s], f_idx(s, f, ok), 0))],
        out_specs=pl.BlockSpec((MOE_ROWS, d), lambda s, f, te, ok: (s, 0)),
        scratch_shapes=[pltpu.VMEM((MOE_ROWS, d), F32)],
    )
    return pl.pallas_call(
        _expert_kernel,
        grid_spec=grid_spec,
        out_shape=jax.ShapeDtypeStruct((n_rows, d), BF16),
        compiler_params=_cparams(("parallel", "arbitrary")),
        name="moe_experts",
    )(tile_e, tile_ok, xs, wg, wu, wd)


def _combine_kernel(offs_ref, cnts_ref, h_ref, route_ref, gout_ref, ys_ref, o_ref, stage, sems):
    i = pl.program_id(0)
    tm = h_ref.shape[0]
    n_chunks = pl.cdiv(tm, MOE_CHUNK)
    ridx = lax.broadcasted_iota(jnp.int32, (tm, MOE_CHUNK), 1).astype(F32)

    def copy(e, c):
        slot = e * n_chunks + c
        row0 = pl.multiple_of(offs_ref[i * N_EXPERTS + e] + c * MOE_CHUNK, MOE_ALIGN)
        return pltpu.make_async_copy(ys_ref.at[pl.ds(row0, MOE_CHUNK)], stage.at[slot], sems.at[slot])

    for e, c in _chunk_copies(n_chunks):
        @pl.when(c * MOE_CHUNK < cnts_ref[i * N_EXPERTS + e])
        def _(e=e, c=c):
            copy(e, c).start()

    o_ref[...] = h_ref[...]
    for e, c in _chunk_copies(n_chunks):
        @pl.when(c * MOE_CHUNK < cnts_ref[i * N_EXPERTS + e])
        def _(e=e, c=c):
            copy(e, c).wait()
            gate = route_ref[:, e:e + 1]
            sel = route_ref[:, SEL_LANE + e:SEL_LANE + e + 1]
            rank = route_ref[:, RANK_LANE + e:RANK_LANE + e + 1]
            hit = (rank - float(c * MOE_CHUNK) == ridx) & (sel > 0.5)
            back = _dot(jnp.where(hit, 1.0, 0.0).astype(BF16), stage[e * n_chunks + c])
            o_ref[...] += gate * back

    o_ref[...] = _rms(o_ref[...], gout_ref[...])


def _moe_combine(offs, cnts, h, route, ys, g_out):
    t, d = h.shape
    tm = _pick(t, (MOE_TM, 384, 256, 128))
    n_slots = N_EXPERTS * (pl.cdiv(tm, MOE_CHUNK))
    grid_spec = pltpu.PrefetchScalarGridSpec(
        num_scalar_prefetch=2,
        grid=(t // tm,),
        in_specs=[pl.BlockSpec((tm, d), lambda i, *_: (i, 0)),
                  pl.BlockSpec((tm, LANES), lambda i, *_: (i, 0)),
                  pl.BlockSpec((1, d), lambda i, *_: (0, 0)),
                  pl.BlockSpec(memory_space=pl.ANY)],
        out_specs=pl.BlockSpec((tm, d), lambda i, *_: (i, 0)),
        scratch_shapes=[pltpu.VMEM((n_slots, MOE_CHUNK, d), BF16), pltpu.SemaphoreType.DMA((n_slots,))],
    )
    return pl.pallas_call(
        _combine_kernel,
        grid_spec=grid_spec,
        out_shape=jax.ShapeDtypeStruct((t, d), F32),
        compiler_params=_cparams(("arbitrary",)),
        name="moe_combine",
    )(offs, cnts, h, route, g_out, ys)


def _moe(h, g, wr3, wg, wu, wd, g_out):
    t, d = h.shape
    hn, route, selt, rankt, cnt = _moe_route(h, g, wr3)
    nt = cnt.shape[0]
    cnt = cnt[:, 0, :N_EXPERTS].astype(jnp.int32)
    worst = TOP_K * t + nt * N_EXPERTS * (MOE_ALIGN - 1) + N_EXPERTS * (MOE_CHUNK + MOE_ROWS - 1)
    n_row_tiles = -(-worst // MOE_ROWS)
    offs, tile_e, tile_ok = _moe_plan(cnt, n_row_tiles)
    cnts = cnt.reshape(-1)
    xs = _moe_dispatch(offs, cnts, hn, selt, rankt, n_row_tiles * MOE_ROWS)
    ys = _moe_experts(tile_e, tile_ok, xs, wg, wu, wd)
    return _moe_combine(offs, cnts, h, route, ys, g_out)


def _rope_table(lp, rot_dim, theta, width, offset, scale):
    half = rot_dim // 2
    pos = (jnp.arange(lp, dtype=jnp.int32) - FIRST).astype(F32)
    inv = theta ** (-jnp.arange(half, dtype=F32) * 2.0 / rot_dim)
    ang = pos[:, None] * inv[None, :]
    cos, sin = jnp.cos(ang), jnp.sin(ang)
    ones = jnp.ones((lp, 1), F32)
    zeros = jnp.zeros((lp, 1), F32)

    def unit(first, second, fill):
        parts = [jnp.tile(fill, (1, offset)), first, second,
                 jnp.tile(fill, (1, width - offset - rot_dim))]
        return jnp.tile(jnp.concatenate(parts, axis=1), (1, LANES // width))

    c = unit(cos, cos, ones)
    s1 = unit(-sin, jnp.zeros_like(sin), zeros)
    s2 = unit(jnp.zeros_like(sin), sin, zeros)
    return jnp.concatenate([c, s1, s2], axis=1) * scale


def _pad_heads(w, heads, used, width):
    k = w.shape[0]
    w = w.reshape(k, heads, used)
    return jnp.pad(w, ((0, 0), (0, 0), (0, width - used))).reshape(k, heads * width)


def kernel(x, meta_tokens, ln_mix_e, w_in_e, ln_mla_q_e, w_mla_uq_e, ln_mla_kv_e, w_mla_ukv_e,
           w_out_e, ln_ffn_e, w_ffn_gate_e, w_ffn_up_e, w_ffn_down_e, ln_mix_o, w_in_o, b_forget_o,
           diff_lq1_o, diff_lk1_o, diff_lq2_o, diff_lk2_o, diff_subln_o, w_out_o, ln_ffn_o,
           w_router_o, w_moe_gate_o, w_moe_up_o, w_moe_down_o, ln_final):
    b, seq, d = x.shape
    assert d == D_MODEL and seq % TQ == 0
    lp = PAD + seq
    t = b * lp
    depth = 2

    meta = jnp.broadcast_to(meta_tokens.astype(x.dtype)[None], (b, N_META, d))
    h = jnp.concatenate([jnp.zeros((b, FIRST, d), x.dtype), meta, x], axis=1).reshape(t, d)

    row = lambda v: v.reshape(1, -1).astype(F32)
    sb_scale = HEAD_DIM ** -0.5
    tab_mla_q = _rope_table(lp, MLA_ROPE, MLA_THETA, LANES, MLA_NOPE,
                            (MLA_NOPE + MLA_ROPE) ** -0.5 * LOG2E)
    tab_mla_k = _rope_table(lp, MLA_ROPE, MLA_THETA, LANES, MLA_NOPE, 1.0)
    tab_diff_q = _rope_table(lp, ROT_DIM, ROPE_THETA, DIFF_DIM, 0, LOG2E)
    tab_diff_k = _rope_table(lp, ROT_DIM, ROPE_THETA, DIFF_DIM, 0, 1.0)

    for i in range(depth):
        j = i // 2
        if i % 2 == 0:
            w_in = w_in_e[j]
            lat_w = MLA_Q_RANK + MLA_KV_RANK + MLA_ROPE
            w_cat = jnp.concatenate(
                [w_in[:, :SB_W] * sb_scale, w_in[:, SB_W:3 * SB_W + lat_w],
                 jnp.zeros((d, 4 * LANES - lat_w), F32)], axis=1).astype(BF16)
            sb, lat = _even_proj(h, row(ln_mix_e[j]), w_cat)
            sb = sb.reshape(b, lp, -1)
            lat = lat.reshape(b, lp, -1)

            wq = _pad_heads(w_mla_uq_e[j], MLA_HEADS, MLA_NOPE + MLA_ROPE, LANES).astype(BF16)
            ukv = w_mla_ukv_e[j].reshape(MLA_KV_RANK, MLA_HEADS, MLA_NOPE + MLA_V)
            wk_nope = _pad_heads(ukv[:, :, :MLA_NOPE].reshape(MLA_KV_RANK, -1), MLA_HEADS, MLA_NOPE, LANES)
            place = jnp.pad(jnp.eye(MLA_ROPE, dtype=F32), ((0, 0), (MLA_NOPE, LANES - MLA_NOPE - MLA_ROPE)))
            wk_rope = jnp.tile(place, (1, MLA_HEADS))
            wk = jnp.concatenate(
                [wk_nope, wk_rope, jnp.zeros((LANES - MLA_ROPE, MLA_HEADS * LANES), F32)], axis=0).astype(BF16)
            wv = ukv[:, :, MLA_NOPE:].reshape(MLA_KV_RANK, -1).astype(BF16)
            mq, mk, mv = _mla_up(lat, row(ln_mla_q_e[j]), row(ln_mla_kv_e[j]), wq, wk, wv,
                                 tab_mla_q, tab_mla_k)

            a_out = _sb_attn(sb).reshape(t, -1)
            b_out = _mla_attn(mq, mk, mv).reshape(t, -1)
            w_out = w_out_e[j].astype(BF16)
            h = _ffn(h, a_out, b_out, w_out[:SB_W], w_out[SB_W:], row(ln_ffn_e[j]),
                     w_ffn_gate_e[j].astype(BF16), w_ffn_up_e[j].astype(BF16), w_ffn_down_e[j].astype(BF16))
        else:
            lam_init = 0.8 - 0.6 * math.exp(-0.3 * i)
            w_in = w_in_o[j]
            wf = w_in[:, N_ODD_MAIN:]
            w_cat = jnp.concatenate(
                [w_in[:, :DIFF_W] * sb_scale, w_in[:, DIFF_W:3 * DIFF_W],
                 w_in[:, 3 * DIFF_W:3 * DIFF_W + FOX_W] * sb_scale, w_in[:, 3 * DIFF_W + FOX_W:N_ODD_MAIN],
                 wf, jnp.zeros((d, LANES - FOX_HEADS), F32)], axis=1).astype(BF16)
            wft = jnp.pad(wf.T, ((0, F_ROWS - FOX_HEADS), (0, 0))).astype(BF16)
            bf = b_forget_o[j].astype(F32)
            bcol = jnp.pad(bf, (0, LANES - FOX_HEADS)).reshape(1, LANES)
            brow = jnp.pad(bf, (0, F_ROWS - FOX_HEADS)).reshape(F_ROWS, 1)
            main, fcol, frow = _odd_proj(h.reshape(b, lp, d), row(ln_mix_o[j]), w_cat, wft, bcol, brow,
                                         tab_diff_q, tab_diff_k)
            ccol, crow = _fox_cumsum(fcol, frow)
            crow4 = crow[:, :FOX_HEADS].reshape(b, FOX_HEADS // 2, 2, lp)
            lam_vecs = jnp.stack([diff_lq1_o[j], diff_lk1_o[j], diff_lq2_o[j], diff_lk2_o[j]]).astype(F32)
            c_out = _diff_attn(main, lam_vecs, row(diff_subln_o[j]), lam_init).reshape(t, -1)
            d_out = _fox_attn(main, ccol, crow4).reshape(t, -1)
            w_out = w_out_o[j].astype(BF16)
            wr = jnp.pad(w_router_o[j].astype(F32), ((0, 0), (0, LANES - N_EXPERTS)))
            wr3 = jnp.stack(_split3(wr))
            h = _out_proj(h, c_out, d_out, w_out[:DIFF_W], w_out[DIFF_W:])
            h = _moe(h, row(ln_ffn_o[j]), wr3, w_moe_gate_o[j], w_moe_up_o[j], w_moe_down_o[j],
                     row(ln_final))

    return h.reshape(b, lp, d)[:, PAD:]
```

```python
import functools
import math

import jax
import jax.numpy as jnp
from jax import lax
from jax.experimental import pallas as pl
from jax.experimental.pallas import tpu as pltpu

F32 = jnp.float32
BF16 = jnp.bfloat16

D_MODEL = 1024
CHUNK = 64
N_META = 16
EPS = 1e-6
LOG2E = 1.4426950408889634
NEG_INF = -1e30

HEAD_DIM = 64
SB_HEADS = 8
MLA_HEADS = 8
MLA_NOPE = 64
MLA_ROPE = 32
MLA_V = 64
MLA_Q_RANK = 256
MLA_KV_RANK = 128
MLA_THETA = 10000.0
DIFF_HEADS = 4
DIFF_DIM = 64
FOX_HEADS = 8
ROPE_THETA = 500000.0
ROT_DIM = HEAD_DIM // 4
N_EXPERTS = 8
TOP_K = 2

SB_W = SB_HEADS * HEAD_DIM
DIFF_W = DIFF_HEADS * 2 * DIFF_DIM
FOX_W = FOX_HEADS * HEAD_DIM

LANES = 128
PAD = 128
FIRST = PAD - N_META
TQ = 128
KEY_WINDOW = 512
Q_GROUP = 4
SB_Q_GROUP = 2
SB_KEY_WINDOW = 512
F32_EXP2_UNDERFLOW = 160.0
SUFFIX_CHUNK = 256
CHUNK_SHIFT = CHUNK.bit_length() - 1
VMEM_LIMIT = 56 * 1024 * 1024


def _cparams(sem):
    return pltpu.CompilerParams(dimension_semantics=sem, vmem_limit_bytes=VMEM_LIMIT)


def _pick(n, cands):
    for c in cands:
        if n % c == 0:
            return c
    raise ValueError(f"no tile in {cands} divides {n}")


def _rms(x, g):
    return x * lax.rsqrt(jnp.mean(x * x, axis=-1, keepdims=True) + EPS) * g


def _dot(a, b):
    return jnp.dot(a, b, preferred_element_type=F32)


def _dot_nt(a, b):
    return lax.dot_general(a, b, (((1,), (1,)), ((), ())), preferred_element_type=F32)


def _rope_chunk(y, tab, shift):
    c = tab[:, 0:LANES]
    s1 = tab[:, LANES:2 * LANES]
    s2 = tab[:, 2 * LANES:3 * LANES]
    return (y * c + pltpu.roll(y, LANES - shift, 1) * s1 + pltpu.roll(y, shift, 1) * s2)


def _split3(x):
    a = x.astype(BF16)
    r = x - a.astype(F32)
    b = r.astype(BF16)
    c = (r - b.astype(F32)).astype(BF16)
    return a, b, c


def _even_proj_kernel(h_ref, g_ref, w_ref, sb_ref, lat_ref):
    hn = _rms(h_ref[...], g_ref[...]).astype(BF16)
    y = _dot(hn, w_ref[...])
    n_sb = sb_ref.shape[-1]
    sb_ref[:, :SB_W] = (y[:, :SB_W] * LOG2E).astype(BF16)
    sb_ref[:, SB_W:] = y[:, SB_W:n_sb].astype(BF16)
    lat_ref[...] = y[:, n_sb:].astype(BF16)


def _even_proj(h, g, w):
    t, d = h.shape
    n = w.shape[1]
    n_sb = 3 * SB_W
    tm = _pick(t, (512, 384, 256, 128))
    return pl.pallas_call(
        _even_proj_kernel,
        grid=(t // tm,),
        in_specs=[pl.BlockSpec((tm, d), lambda i: (i, 0)),
                  pl.BlockSpec((1, d), lambda i: (0, 0)),
                  pl.BlockSpec((d, n), lambda i: (0, 0))],
        out_specs=[pl.BlockSpec((tm, n_sb), lambda i: (i, 0)),
                   pl.BlockSpec((tm, n - n_sb), lambda i: (i, 0))],
        out_shape=[jax.ShapeDtypeStruct((t, n_sb), BF16),
                   jax.ShapeDtypeStruct((t, n - n_sb), BF16)],
        compiler_params=_cparams(("parallel",)),
        name="even_proj",
    )(h, g, w)


def _mla_up_kernel(lat_ref, gq_ref, gkv_ref, wq_ref, wk_ref, wv_ref, tq_ref, tk_ref,
                   q_ref, k_ref, v_ref):
    lat = lat_ref[0]
    nq = _rms(lat[:, :MLA_Q_RANK].astype(F32), gq_ref[...]).astype(BF16)
    nkv = _rms(lat[:, MLA_Q_RANK:MLA_Q_RANK + MLA_KV_RANK].astype(F32), gkv_ref[...]).astype(BF16)
    yq = _dot(nq, wq_ref[...])
    xk = jnp.concatenate([nkv, lat[:, MLA_Q_RANK + MLA_KV_RANK:]], axis=1)
    yk = _dot(xk, wk_ref[...])
    v_ref[0] = _dot(nkv, wv_ref[...]).astype(BF16)
    tabq = tq_ref[...]
    tabk = tk_ref[...]
    for hd in range(MLA_HEADS):
        sl = slice(hd * LANES, (hd + 1) * LANES)
        q_ref[0, :, sl] = _rope_chunk(yq[:, sl], tabq, MLA_ROPE // 2).astype(BF16)
        k_ref[0, :, sl] = _rope_chunk(yk[:, sl], tabk, MLA_ROPE // 2).astype(BF16)


def _mla_up(lat, gq, gkv, wq, wk, wv, tabq, tabk):
    b, lp, nl = lat.shape
    tm = _pick(lp, (384, 128))
    nqk = MLA_HEADS * LANES
    nv = MLA_HEADS * MLA_V
    full = lambda shp: pl.BlockSpec(shp, lambda bi, i: (0,) * len(shp))
    return pl.pallas_call(
        _mla_up_kernel,
        grid=(b, lp // tm),
        in_specs=[pl.BlockSpec((1, tm, nl), lambda bi, i: (bi, i, 0)),
                  full(gq.shape), full(gkv.shape), full(wq.shape), full(wk.shape), full(wv.shape),
                  pl.BlockSpec((tm, 3 * LANES), lambda bi, i: (i, 0)),
                  pl.BlockSpec((tm, 3 * LANES), lambda bi, i: (i, 0))],
        out_specs=[pl.BlockSpec((1, tm, nqk), lambda bi, i: (bi, i, 0)),
                   pl.BlockSpec((1, tm, nqk), lambda bi, i: (bi, i, 0)),
                   pl.BlockSpec((1, tm, nv), lambda bi, i: (bi, i, 0))],
        out_shape=[jax.ShapeDtypeStruct((b, lp, nqk), BF16),
                   jax.ShapeDtypeStruct((b, lp, nqk), BF16),
                   jax.ShapeDtypeStruct((b, lp, nv), BF16)],
        compiler_params=_cparams(("parallel", "parallel")),
        name="mla_up",
    )(lat, gq, gkv, wq, wk, wv, tabq, tabk)


N_ODD_MAIN = 3 * DIFF_W + 3 * FOX_W
F_ROWS = 16


def _odd_proj_kernel(h_ref, g_ref, w_ref, wft_ref, bcol_ref, brow_ref, tabq_ref, tabk_ref,
                     main_ref, fcol_ref, frow_ref):
    hn = _rms(h_ref[0], g_ref[...]).astype(BF16)
    y = _dot(hn, w_ref[...])
    n_rope = DIFF_W // LANES
    for c in range(2 * n_rope):
        sl = slice(c * LANES, (c + 1) * LANES)
        tab = tabq_ref[...] if c < n_rope else tabk_ref[...]
        main_ref[0, :, sl] = _rope_chunk(y[:, sl], tab, ROT_DIM // 2).astype(BF16)
    fq = slice(3 * DIFF_W, 3 * DIFF_W + FOX_W)
    main_ref[0, :, 2 * DIFF_W:3 * DIFF_W] = y[:, 2 * DIFF_W:3 * DIFF_W].astype(BF16)
    main_ref[0, :, fq] = (y[:, fq] * LOG2E).astype(BF16)
    main_ref[0, :, 3 * DIFF_W + FOX_W:] = y[:, 3 * DIFF_W + FOX_W:N_ODD_MAIN].astype(BF16)
    fcol_ref[0] = y[:, N_ODD_MAIN:] + bcol_ref[...]
    frow_ref[0] = _dot_nt(wft_ref[...], hn) + brow_ref[...]


def _odd_proj(h3, g, w, wft, bcol, brow, tabq, tabk):
    b, lp, d = h3.shape
    tm = _pick(lp, (384, 128))
    full = lambda shp: pl.BlockSpec(shp, lambda bi, i: (0,) * len(shp))
    return pl.pallas_call(
        _odd_proj_kernel,
        grid=(b, lp // tm),
        in_specs=[pl.BlockSpec((1, tm, d), lambda bi, i: (bi, i, 0)),
                  full(g.shape), full(w.shape), full(wft.shape), full(bcol.shape), full(brow.shape),
                  pl.BlockSpec((tm, 3 * LANES), lambda bi, i: (i, 0)),
                  pl.BlockSpec((tm, 3 * LANES), lambda bi, i: (i, 0))],
        out_specs=[pl.BlockSpec((1, tm, N_ODD_MAIN), lambda bi, i: (bi, i, 0)),
                   pl.BlockSpec((1, tm, LANES), lambda bi, i: (bi, i, 0)),
                   pl.BlockSpec((1, F_ROWS, tm), lambda bi, i: (bi, 0, i))],
        out_shape=[jax.ShapeDtypeStruct((b, lp, N_ODD_MAIN), BF16),
                   jax.ShapeDtypeStruct((b, lp, LANES), F32),
                   jax.ShapeDtypeStruct((b, F_ROWS, lp), F32)],
        compiler_params=_cparams(("parallel", "parallel")),
        name="odd_proj",
    )(h3, g, w, wft, bcol, brow, tabq, tabk)


def _log_sigmoid(x):
    return jnp.minimum(x, 0.0) - jnp.log(1.0 + jnp.exp(-jnp.abs(x)))


def _fox_cumsum_kernel(fcol_ref, frow_ref, ccol_ref, crow_ref):
    lp = fcol_ref.shape[1]
    blk = LANES
    r = lax.broadcasted_iota(jnp.int32, (blk, blk), 0)
    c = lax.broadcasted_iota(jnp.int32, (blk, blk), 1)
    tri_l = (c <= r).astype(BF16)
    tri_u = (r <= c).astype(BF16)
    row_id = lax.broadcasted_iota(jnp.int32, (blk, LANES), 0)
    col_id = lax.broadcasted_iota(jnp.int32, (F_ROWS, blk), 1)
    carry_c = jnp.zeros((1, LANES), F32)
    carry_r = jnp.zeros((F_ROWS, 1), F32)
    for j in range(lp // blk):
        sl = slice(j * blk, (j + 1) * blk)
        ls = _log_sigmoid(fcol_ref[0, sl, :])
        ls = jnp.where(row_id + j * blk >= FIRST, ls, 0.0)
        a, b, c3 = _split3(ls)
        cs = _dot(tri_l, a) + _dot(tri_l, b) + _dot(tri_l, c3) + carry_c
        ccol_ref[0, sl, :] = cs * LOG2E
        carry_c = cs[blk - 1:blk, :]
        lr = _log_sigmoid(frow_ref[0, :, sl])
        lr = jnp.where(col_id + j * blk >= FIRST, lr, 0.0)
        a, b, c3 = _split3(lr)
        cr = _dot(a, tri_u) + _dot(b, tri_u) + _dot(c3, tri_u) + carry_r
        crow_ref[0, :, sl] = cr * LOG2E
        carry_r = cr[:, blk - 1:blk]


def _fox_cumsum(fcol, frow):
    b, lp, _ = fcol.shape
    return pl.pallas_call(
        _fox_cumsum_kernel,
        grid=(b,),
        in_specs=[pl.BlockSpec((1, lp, LANES), lambda bi: (bi, 0, 0)),
                  pl.BlockSpec((1, F_ROWS, lp), lambda bi: (bi, 0, 0))],
        out_specs=[pl.BlockSpec((1, lp, LANES), lambda bi: (bi, 0, 0)),
                   pl.BlockSpec((1, F_ROWS, lp), lambda bi: (bi, 0, 0))],
        out_shape=[jax.ShapeDtypeStruct((b, lp, LANES), F32),
                   jax.ShapeDtypeStruct((b, F_ROWS, lp), F32)],
        compiler_params=_cparams(("parallel",)),
        name="fox_cumsum",
    )(fcol, frow)


def _rows(i, n):
    if isinstance(i, int):
        return pl.ds(i * n, n)
    return pl.ds(pl.multiple_of(i * n, n), n)


def _keys(k0, w):
    if isinstance(k0, int):
        return pl.ds(k0, w)
    return pl.ds(pl.multiple_of(k0, TQ), w)


def _visible(kind, ks, qs):
    if kind == "chunk":
        return (ks >> CHUNK_SHIFT) <= (qs >> CHUNK_SHIFT)
    if kind == "causal":
        return ks <= qs
    return ks < qs


def _first_window_mask(kind, q0, w, lim):
    r = lax.broadcasted_iota(jnp.int32, (2 * TQ, w), 0)
    ks = lax.broadcasted_iota(jnp.int32, (2 * TQ, w), 1)
    return _visible(kind, ks, q0 + (r & (TQ - 1))) & (ks >= FIRST) & (ks < lim)


def _diag_window_mask(kind, w):
    r = lax.broadcasted_iota(jnp.int32, (2 * TQ, w), 0)
    ks = lax.broadcasted_iota(jnp.int32, (2 * TQ, w), 1) + (TQ - w)
    return _visible(kind, ks, r & (TQ - 1))


def _span_mask(w, lim):
    ks = lax.broadcasted_iota(jnp.int32, (1, w), 1)
    return (ks >= FIRST) & (ks < lim)


def _lane_halves():
    lane = lax.broadcasted_iota(jnp.int32, (TQ, LANES), 1)
    return lane < (LANES // 2)


def _stack_masked(q):
    lo = _lane_halves()
    z = jnp.zeros_like(q)
    return jnp.concatenate([jnp.where(lo, q, z), jnp.where(lo, z, q)], axis=0)


def _softmax_sweep(kind, w, prep, scores_fn, v_fn, dbias_ref):
    def step(carry, s, v):
        m, acc = carry
        m_new = jnp.maximum(m, jnp.max(s, axis=-1, keepdims=True))
        alpha = jnp.exp2(m - m_new)
        p = jnp.exp2(s - m_new)
        ones = jnp.ones((v.shape[0], LANES), BF16)
        acc = alpha * acc + _dot(p.astype(BF16), jnp.concatenate([v, ones], axis=1))
        return m_new, acc

    def run(qi, mwin):
        g = w // TQ
        ctx = prep(qi)
        q0 = qi * TQ
        pre = qi + 1 - mwin * g
        n0 = w
        carry = (jnp.full((2 * TQ, 1), NEG_INF, F32), jnp.zeros((2 * TQ, 2 * LANES), F32))
        s = scores_fn(ctx, 0, n0)
        if mwin == 0:
            s = jnp.where(_first_window_mask(kind, q0, n0, pre * TQ), s, NEG_INF)
            return step(carry, s, v_fn(0, n0))[1]
        carry = step(carry, s + jnp.where(_span_mask(n0, pre * TQ), 0.0, NEG_INF), v_fn(0, n0))
        for t in range(1, mwin):
            k0 = (pre + (t - 1) * g) * TQ
            carry = step(carry, scores_fn(ctx, k0, w), v_fn(k0, w))
        k0 = q0 + TQ - w
        return step(carry, scores_fn(ctx, k0, w) + dbias_ref[...], v_fn(k0, w))[1]

    return run


def _for_each_query_group(nq, w, q_group, group_fn):
    g = w // TQ
    one = jnp.sign(pl.program_id(0) + 1)
    for mwin in range(-(-nq // g)):
        lo, hi = mwin * g, min((mwin + 1) * g, nq)

        n_groups = (hi - lo) // q_group

        def body(it, _, mwin=mwin, lo=lo):
            qi = lo + q_group * it
            group_fn([qi + j for j in range(q_group)], mwin)
            return 0

        if n_groups:
            lax.fori_loop(0, n_groups * one, body, 0)
        for qi in range(lo + n_groups * q_group, hi):
            group_fn([qi], mwin)


def _for_each_query_block(nq, w, run, emit):
    def group_fn(qis, mwin):
        results = [run(qi, mwin) for qi in qis]
        for qi, res in zip(qis, results):
            emit(qi, res)

    _for_each_query_group(nq, w, Q_GROUP, group_fn)


def _key_window(lp, limit=None):
    return min(KEY_WINDOW if limit is None else limit, lp)


def _merge_heads(top, bottom):
    return jnp.where(_lane_halves(), top, bottom)


def _attn_call(kernel, name, b, lp, n_blocks, in_specs, args, n_masks=1, window=None):
    w = _key_window(lp, window)
    return pl.pallas_call(
        kernel,
        grid=(b, n_blocks),
        in_specs=in_specs,
        out_specs=pl.BlockSpec((1, lp, LANES), lambda bi, hp: (bi, 0, hp)),
        out_shape=jax.ShapeDtypeStruct((b, lp, n_blocks * LANES), BF16),
        scratch_shapes=[pltpu.VMEM((2 * TQ, w), F32)] * n_masks,
        compiler_params=_cparams(("parallel", "parallel")),
        name=name,
    )(*args)


def _normalised(acc):
    return acc[:, :LANES] / acc[:, LANES:]


def _col_spec(lp, off):
    return pl.BlockSpec((1, lp, LANES), lambda bi, hp: (bi, 0, off + hp))


def _mla_attn_kernel(q_ref, k_ref, v_ref, o_ref, dbias_ref):
    lp = q_ref.shape[1]
    w = _key_window(lp)
    dbias_ref[...] = jnp.where(_diag_window_mask("chunk", w), 0.0, NEG_INF)

    def prep(qi):
        return q_ref[0, _rows(qi, TQ), 0:LANES], q_ref[0, _rows(qi, TQ), LANES:2 * LANES]

    def scores(ctx, k0, n):
        qa, qb = ctx
        ka = k_ref[0, _keys(k0, n), 0:LANES]
        kb = k_ref[0, _keys(k0, n), LANES:2 * LANES]
        return jnp.concatenate([_dot_nt(qa, ka), _dot_nt(qb, kb)], axis=0)

    run = _softmax_sweep("chunk", w, prep, scores, lambda k0, n: v_ref[0, _keys(k0, n), :], dbias_ref)

    def emit(qi, acc):
        o = _normalised(acc)
        o_ref[0, _rows(qi, TQ), :] = _merge_heads(o[:TQ], o[TQ:]).astype(o_ref.dtype)

    _for_each_query_block(lp // TQ, w, run, emit)


def _mla_attn(q, k, v):
    b, lp, _ = q.shape
    wide = lambda: pl.BlockSpec((1, lp, 2 * LANES), lambda bi, hp: (bi, 0, hp))
    return _attn_call(_mla_attn_kernel, "mla_attn", b, lp, MLA_HEADS // 2,
                      [wide(), wide(), _col_spec(lp, 0)], (q, k, v))


def _diff_attn_kernel(lam_init, x_q, x_k, x_v, lam_ref, gsub_ref, o_ref, dbias_ref):
    lp = x_q.shape[1]
    w = _key_window(lp)
    dbias_ref[...] = jnp.where(_diag_window_mask("chunk", w), 0.0, NEG_INF)
    lv = lam_ref[...]
    lam = (jnp.exp(jnp.sum(lv[0:1] * lv[1:2], axis=-1, keepdims=True))
           - jnp.exp(jnp.sum(lv[2:3] * lv[3:4], axis=-1, keepdims=True)) + lam_init)
    gsub = gsub_ref[...]

    def prep(qi):
        return _stack_masked(x_q[0, _rows(qi, TQ), :])

    def scores(qs, k0, n):
        return _dot_nt(qs, x_k[0, _keys(k0, n), :])

    run = _softmax_sweep("chunk", w, prep, scores, lambda k0, n: x_v[0, _keys(k0, n), :], dbias_ref)

    def emit(qi, acc):
        o = _normalised(acc)
        o = o[:TQ] - lam * o[TQ:]
        o_ref[0, _rows(qi, TQ), :] = (_rms(o, gsub) * (1.0 - lam_init)).astype(o_ref.dtype)

    _for_each_query_block(lp // TQ, w, run, emit)


def _diff_attn(main, lam_vecs, gsub, lam_init):
    b, lp, _ = main.shape
    nb = DIFF_W // LANES
    full = lambda shp: pl.BlockSpec(shp, lambda bi, hp: (0,) * len(shp))
    return _attn_call(functools.partial(_diff_attn_kernel, lam_init), "diff_attn", b, lp, DIFF_HEADS,
                      [_col_spec(lp, 0), _col_spec(lp, nb), _col_spec(lp, 2 * nb),
                       full(lam_vecs.shape), full(gsub.shape)],
                      (main, main, main, lam_vecs, gsub))


def _fox_attn_kernel(x_q, x_k, x_v, ccol_ref, crow_ref, o_ref, dbias_ref):
    lp = x_q.shape[1]
    w = _key_window(lp)
    dbias_ref[...] = jnp.where(_diag_window_mask("causal", w), 0.0, NEG_INF)
    hp = pl.program_id(1)
    lane = lax.broadcasted_iota(jnp.int32, (TQ, LANES), 1)

    def prep(qi):
        cc = ccol_ref[0, _rows(qi, TQ), :]
        cq0 = jnp.sum(jnp.where(lane == 2 * hp, cc, 0.0), axis=-1, keepdims=True)
        cq1 = jnp.sum(jnp.where(lane == 2 * hp + 1, cc, 0.0), axis=-1, keepdims=True)
        return _stack_masked(x_q[0, _rows(qi, TQ), :]), cq0, cq1

    def scores(ctx, k0, n):
        qs, cq0, cq1 = ctx
        s = _dot_nt(qs, x_k[0, _keys(k0, n), :])
        ck = crow_ref[0, 0, :, _keys(k0, n)]
        return s + jnp.concatenate([cq0 - ck[0:1, :], cq1 - ck[1:2, :]], axis=0)

    run = _softmax_sweep("causal", w, prep, scores, lambda k0, n: x_v[0, _keys(k0, n), :], dbias_ref)

    def emit(qi, acc):
        o = _normalised(acc)
        o_ref[0, _rows(qi, TQ), :] = _merge_heads(o[:TQ], o[TQ:]).astype(o_ref.dtype)

    _for_each_query_block(lp // TQ, w, run, emit)


def _fox_attn(main, ccol, crow4):
    b, lp, _ = main.shape
    off = 3 * DIFF_W // LANES
    nb = FOX_W // LANES
    return _attn_call(_fox_attn_kernel, "fox_attn", b, lp, nb,
                      [_col_spec(lp, off), _col_spec(lp, off + nb), _col_spec(lp, off + 2 * nb),
                       pl.BlockSpec((1, lp, LANES), lambda bi, hp: (bi, 0, 0)),
                       pl.BlockSpec((1, 1, 2, lp), lambda bi, hp: (bi, hp, 0, 0))],
                      (main, main, main, ccol, crow4))


def _sb_attn_kernel(x_q, x_k, x_v, o_ref, dkeep_ref, dbias_ref):
    lp = x_q.shape[1]
    w = _key_window(lp, SB_KEY_WINDOW)
    g = w // TQ
    cs = SUFFIX_CHUNK if w % SUFFIX_CHUNK == 0 else TQ
    kr = lax.broadcasted_iota(jnp.int32, (cs, cs), 0)
    kc = lax.broadcasted_iota(jnp.int32, (cs, cs), 1)
    later = jnp.where(kr > kc, 1.0, 0.0).astype(BF16)
    dvis = _diag_window_mask("strict", w)
    dkeep_ref[...] = jnp.where(dvis, 1.0, 0.0)
    dbias_ref[...] = jnp.where(dvis, 0.0, NEG_INF)

    def step(carry, qs, k0, keep=None, bias=None, vis=None):
        rsum, acc = carry
        z = _dot_nt(qs, x_k[0, _keys(k0, w), :])
        cost = jnp.maximum(z, 0.0) + jnp.log2(1.0 + jnp.exp2(-jnp.abs(z)))
        if vis is not None:
            cost = jnp.where(vis, cost, 0.0)
        if keep is not None:
            cost = cost * keep
        cb = cost.astype(BF16)
        parts = []
        for c in reversed(range(w // cs)):
            sl = slice(c * cs, (c + 1) * cs)
            inner = _dot(cb[:, sl], later)
            parts.append(inner + rsum)
            rsum = rsum + inner[:, 0:1] + cb[:, c * cs:c * cs + 1].astype(F32)
        arg = z - cost - jnp.concatenate(parts[::-1], axis=1)
        if bias is not None:
            arg = arg + bias
        wgt = jnp.exp2(arg)
        if vis is not None:
            wgt = jnp.where(vis, wgt, 0.0)
        acc = acc + _dot(wgt.astype(BF16), x_v[0, _keys(k0, w), :])
        return rsum, acc

    def knorm(j, best):
        kk = x_k[0, _rows(j, TQ), :].astype(F32)
        return jnp.maximum(best, jnp.sum(kk * kk, axis=-1, keepdims=True))

    kn2 = jnp.max(lax.fori_loop(0, lp // TQ, knorm, jnp.zeros((TQ, 1), F32)), axis=0, keepdims=True)

    def emit(qi, acc):
        o_ref[0, _rows(qi, TQ), :] = _merge_heads(acc[:TQ], acc[TQ:]).astype(o_ref.dtype)

    zero = (jnp.zeros((2 * TQ, 1), F32), jnp.zeros((2 * TQ, LANES), F32))

    def near_group(qis):
        for qi in qis:
            qs = _stack_masked(x_q[0, _rows(qi, TQ), :])
            vis = _first_window_mask("strict", qi * TQ, w, (qi + 1) * TQ)
            emit(qi, step(zero, qs, 0, vis=vis)[1])

    def far_group(qis):
        qss = [_stack_masked(x_q[0, _rows(qi, TQ), :]) for qi in qis]
        mwins = [qi // g for qi in qis]
        pres = [qi + 1 - mwin * g for qi, mwin in zip(qis, mwins)]
        carries = [step(zero, qs, qi * TQ + TQ - w, keep=dkeep_ref[...], bias=dbias_ref[...])
                   for qi, qs in zip(qis, qss)]

        def rest(cs):
            out = []
            for qs, mwin, pre, carry in zip(qss, mwins, pres, cs):
                def mid(it, c, qs=qs, mwin=mwin, pre=pre):
                    return step(c, qs, (pre + (mwin - 2 - it) * g) * TQ)

                carry = lax.fori_loop(0, mwin - 1, mid, carry)
                span = _span_mask(w, pre * TQ)
                out.append(step(carry, qs, 0, keep=jnp.where(span, 1.0, 0.0),
                                bias=jnp.where(span, 0.0, NEG_INF)))
            return tuple(out)

        slack = [jnp.max(jnp.sqrt(jnp.sum(jnp.square(qs.astype(F32)), axis=-1, keepdims=True) * kn2)
                         - carry[0]) for qs, carry in zip(qss, carries)]
        live = functools.reduce(jnp.maximum, slack) > -F32_EXP2_UNDERFLOW
        carries = lax.cond(live, rest, lambda cs: cs, tuple(carries))
        for qi, carry in zip(qis, carries):
            emit(qi, carry[1])

    def sweep(lo, hi, group):
        n_groups = (hi - lo) // SB_Q_GROUP

        def body(it, _):
            qi = lo + SB_Q_GROUP * it
            group([qi + j for j in range(SB_Q_GROUP)])
            return 0

        lax.fori_loop(0, n_groups, body, 0)
        for qi in range(lo + n_groups * SB_Q_GROUP, hi):
            group([qi])

    nq = lp // TQ
    sweep(0, min(g, nq), near_group)
    sweep(min(g, nq), nq, far_group)


def _sb_attn(sb):
    b, lp, _ = sb.shape
    nb = SB_W // LANES
    return _attn_call(_sb_attn_kernel, "sb_attn", b, lp, nb,
                      [_col_spec(lp, 0), _col_spec(lp, nb), _col_spec(lp, 2 * nb)],
                      (sb, sb, sb), n_masks=2, window=SB_KEY_WINDOW)


def _silu(x):
    return x / (1.0 + jnp.exp(-x))


def _mixer_residual(h_ref, a_ref, b_ref, wa_ref, wb_ref):
    return h_ref[...] + _dot(a_ref[...], wa_ref[...]) + _dot(b_ref[...], wb_ref[...])


def _out_proj_kernel(h_ref, a_ref, b_ref, wa_ref, wb_ref, o_ref):
    o_ref[...] = _mixer_residual(h_ref, a_ref, b_ref, wa_ref, wb_ref)


def _out_proj(h, mix_a, mix_b, wa, wb):
    t, d = h.shape
    tm = _pick(t, (512, 384, 256, 128))
    na, nb = mix_a.shape[1], mix_b.shape[1]
    return pl.pallas_call(
        _out_proj_kernel,
        grid=(t // tm,),
        in_specs=[pl.BlockSpec((tm, d), lambda i: (i, 0)),
                  pl.BlockSpec((tm, na), lambda i: (i, 0)),
                  pl.BlockSpec((tm, nb), lambda i: (i, 0)),
                  pl.BlockSpec((na, d), lambda i: (0, 0)),
                  pl.BlockSpec((nb, d), lambda i: (0, 0))],
        out_specs=pl.BlockSpec((tm, d), lambda i: (i, 0)),
        out_shape=jax.ShapeDtypeStruct((t, d), F32),
        compiler_params=_cparams(("parallel",)),
        name="out_proj",
    )(h, mix_a, mix_b, wa, wb)


def _ffn_kernel(h_ref, a_ref, b_ref, wa_ref, wb_ref, g_ref, wg_ref, wu_ref, wd_ref, o_ref, hn_ref, acc_ref):
    f = pl.program_id(1)

    @pl.when(f == 0)
    def _():
        hmid = _mixer_residual(h_ref, a_ref, b_ref, wa_ref, wb_ref)
        hn_ref[...] = _rms(hmid, g_ref[...]).astype(BF16)
        acc_ref[...] = hmid

    hn = hn_ref[...]
    a = _silu(_dot(hn, wg_ref[...])) * _dot(hn, wu_ref[...])
    acc_ref[...] += _dot(a.astype(BF16), wd_ref[...])

    @pl.when(f == pl.num_programs(1) - 1)
    def _():
        o_ref[...] = acc_ref[...]


def _ffn(h, mix_a, mix_b, wa, wb, g, wg, wu, wd):
    t, d = h.shape
    ff = wg.shape[1]
    na, nb = mix_a.shape[1], mix_b.shape[1]
    tm = _pick(t, (512, 384, 256, 128))
    tf = _pick(ff, (1408, 512, 256, 128))
    return pl.pallas_call(
        _ffn_kernel,
        grid=(t // tm, ff // tf),
        in_specs=[pl.BlockSpec((tm, d), lambda i, f: (i, 0)),
                  pl.BlockSpec((tm, na), lambda i, f: (i, 0)),
                  pl.BlockSpec((tm, nb), lambda i, f: (i, 0)),
                  pl.BlockSpec((na, d), lambda i, f: (0, 0)),
                  pl.BlockSpec((nb, d), lambda i, f: (0, 0)),
                  pl.BlockSpec((1, d), lambda i, f: (0, 0)),
                  pl.BlockSpec((d, tf), lambda i, f: (0, f)),
                  pl.BlockSpec((d, tf), lambda i, f: (0, f)),
                  pl.BlockSpec((tf, d), lambda i, f: (f, 0))],
        out_specs=pl.BlockSpec((tm, d), lambda i, f: (i, 0)),
        out_shape=jax.ShapeDtypeStruct((t, d), F32),
        scratch_shapes=[pltpu.VMEM((tm, d), BF16), pltpu.VMEM((tm, d), F32)],
        compiler_params=_cparams(("parallel", "arbitrary")),
        name="ffn",
    )(h, mix_a, mix_b, wa, wb, g, wg, wu, wd)


def _router_gates(hn, wr_ref):
    a, b, c = _split3(hn)
    w_ab = jnp.concatenate([wr_ref[0], wr_ref[1]], axis=1)
    ya = _dot(a, jnp.concatenate([w_ab, wr_ref[2]], axis=1))
    yb = _dot(b, w_ab)
    logits = (ya[:, :LANES] + ya[:, LANES:2 * LANES] + yb[:, :LANES] + ya[:, 2 * LANES:] + yb[:, LANES:]
              + _dot(c, wr_ref[0]))
    lane = lax.broadcasted_iota(jnp.int32, logits.shape, 1).astype(F32)
    logits = jnp.where(lane < N_EXPERTS, logits, -jnp.inf)
    v1 = jnp.max(logits, axis=-1, keepdims=True)
    i1 = jnp.min(jnp.where(logits == v1, lane, float(LANES)), axis=-1, keepdims=True)
    rest = jnp.where(lane == i1, -jnp.inf, logits)
    v2 = jnp.max(rest, axis=-1, keepdims=True)
    i2 = jnp.min(jnp.where(rest == v2, lane, float(LANES)), axis=-1, keepdims=True)
    e2 = jnp.exp(v2 - v1)
    den = 1.0 + e2
    gate = jnp.where(lane == i1, 1.0 / den, 0.0) + jnp.where(lane == i2, e2 / den, 0.0)
    sel = jnp.where((lane == i1) | (lane == i2), 1.0, 0.0)
    return gate, sel


MOE_TM = 512
MOE_CHUNK = 256
MOE_ALIGN = 16
MOE_ROWS = 1024
SEL_LANE = N_EXPERTS
RANK_LANE = 2 * N_EXPERTS
T_ROWS = 16


def _route_kernel(h_ref, g_ref, wr_ref, hn_ref, route_ref, selt_ref, rankt_ref, cnt_ref):
    tm = h_ref.shape[0]
    hn = _rms(h_ref[...], g_ref[...])
    hn_ref[...] = hn.astype(BF16)
    gate, sel = _router_gates(hn, wr_ref)
    selb = sel.astype(BF16)
    r = lax.broadcasted_iota(jnp.int32, (tm, tm), 0)
    c = lax.broadcasted_iota(jnp.int32, (tm, tm), 1)
    rank = _dot(jnp.where(c < r, 1.0, 0.0).astype(BF16), selb)
    route_ref[...] = gate + pltpu.roll(sel, SEL_LANE, 1) + pltpu.roll(rank, RANK_LANE, 1)
    er = lax.broadcasted_iota(jnp.int32, (T_ROWS, LANES), 0)
    el = lax.broadcasted_iota(jnp.int32, (T_ROWS, LANES), 1)
    pick = jnp.where((er == el) & (er < N_EXPERTS), 1.0, 0.0).astype(BF16)
    selt = _dot_nt(pick, selb)
    selt_ref[0] = selt
    rankt_ref[0] = _dot(selt.astype(BF16), jnp.where(r < c, 1.0, 0.0).astype(BF16))
    cnt_ref[0] = jnp.sum(sel, axis=0, keepdims=True)


def _moe_route(h, g, wr3):
    t, d = h.shape
    tm = _pick(t, (MOE_TM, 384, 256, 128))
    nt = t // tm
    return pl.pallas_call(
        _route_kernel,
        grid=(nt,),
        in_specs=[pl.BlockSpec((tm, d), lambda i: (i, 0)),
                  pl.BlockSpec((1, d), lambda i: (0, 0)),
                  pl.BlockSpec((3, d, LANES), lambda i: (0, 0, 0))],
        out_specs=[pl.BlockSpec((tm, d), lambda i: (i, 0)),
                   pl.BlockSpec((tm, LANES), lambda i: (i, 0)),
                   pl.BlockSpec((1, T_ROWS, tm), lambda i: (i, 0, 0)),
                   pl.BlockSpec((1, T_ROWS, tm), lambda i: (i, 0, 0)),
                   pl.BlockSpec((1, 1, LANES), lambda i: (i, 0, 0))],
        out_shape=[jax.ShapeDtypeStruct((t, d), BF16),
                   jax.ShapeDtypeStruct((t, LANES), F32),
                   jax.ShapeDtypeStruct((nt, T_ROWS, tm), F32),
                   jax.ShapeDtypeStruct((nt, T_ROWS, tm), F32),
                   jax.ShapeDtypeStruct((nt, 1, LANES), F32)],
        compiler_params=_cparams(("parallel",)),
        name="moe_route",
    )(h, g, wr3)


def _moe_plan(cnt, n_row_tiles):
    cnt_al = (cnt + MOE_ALIGN - 1) // MOE_ALIGN * MOE_ALIGN
    tot = jnp.sum(cnt_al, axis=0)
    cap = (tot + MOE_CHUNK + MOE_ROWS - 1) // MOE_ROWS * MOE_ROWS
    base = jnp.cumsum(cap) - cap
    offs = base[None, :] + jnp.cumsum(cnt_al, axis=0) - cnt_al
    starts = jnp.arange(n_row_tiles, dtype=jnp.int32) * MOE_ROWS
    tile_e = jnp.clip(jnp.sum(starts[:, None] >= base[None, :], axis=1) - 1, 0, N_EXPERTS - 1)
    tile_ok = starts < (base + tot)[tile_e]
    return (offs.reshape(-1).astype(jnp.int32), tile_e.astype(jnp.int32), tile_ok.astype(jnp.int32))


def _chunk_copies(n_chunks):
    return [(e, c) for e in range(N_EXPERTS) for c in range(n_chunks)]


def _dispatch_kernel(offs_ref, cnts_ref, hn_ref, selt_ref, rankt_ref, xs_in_ref, xs_ref, stage, sems):
    del xs_in_ref
    i = pl.program_id(0)
    tm = hn_ref.shape[0]
    hn = hn_ref[...]
    ridx = lax.broadcasted_iota(jnp.int32, (MOE_CHUNK, tm), 0).astype(F32)

    def copy(e, c):
        slot = e * (pl.cdiv(tm, MOE_CHUNK)) + c
        row0 = pl.multiple_of(offs_ref[i * N_EXPERTS + e] + c * MOE_CHUNK, MOE_ALIGN)
        return pltpu.make_async_copy(stage.at[slot], xs_ref.at[pl.ds(row0, MOE_CHUNK)], sems.at[slot])

    for e, c in _chunk_copies(pl.cdiv(tm, MOE_CHUNK)):
        @pl.when(c * MOE_CHUNK < cnts_ref[i * N_EXPERTS + e])
        def _(e=e, c=c):
            hit = (rankt_ref[0, e:e + 1, :] - float(c * MOE_CHUNK) == ridx) & (selt_ref[0, e:e + 1, :] > 0.5)
            rows = _dot(jnp.where(hit, 1.0, 0.0).astype(BF16), hn)
            stage[e * (pl.cdiv(tm, MOE_CHUNK)) + c] = rows.astype(BF16)
            copy(e, c).start()

    for e, c in _chunk_copies(pl.cdiv(tm, MOE_CHUNK)):
        @pl.when(c * MOE_CHUNK < cnts_ref[i * N_EXPERTS + e])
        def _(e=e, c=c):
            copy(e, c).wait()


def _moe_dispatch(offs, cnts, hn, selt, rankt, n_rows):
    t, d = hn.shape
    nt, _, tm = selt.shape
    n_slots = N_EXPERTS * (pl.cdiv(tm, MOE_CHUNK))
    xs0 = jnp.zeros((n_rows, d), BF16)
    grid_spec = pltpu.PrefetchScalarGridSpec(
        num_scalar_prefetch=2,
        grid=(nt,),
        in_specs=[pl.BlockSpec((tm, d), lambda i, *_: (i, 0)),
                  pl.BlockSpec((1, T_ROWS, tm), lambda i, *_: (i, 0, 0)),
                  pl.BlockSpec((1, T_ROWS, tm), lambda i, *_: (i, 0, 0)),
                  pl.BlockSpec(memory_space=pl.ANY)],
        out_specs=pl.BlockSpec(memory_space=pl.ANY),
        scratch_shapes=[pltpu.VMEM((n_slots, MOE_CHUNK, d), BF16), pltpu.SemaphoreType.DMA((n_slots,))],
    )
    return pl.pallas_call(
        _dispatch_kernel,
        grid_spec=grid_spec,
        out_shape=jax.ShapeDtypeStruct((n_rows, d), BF16),
        input_output_aliases={5: 0},
        compiler_params=_cparams(("arbitrary",)),
        name="moe_dispatch",
    )(offs, cnts, hn, selt, rankt, xs0)


def _expert_kernel(te_ref, ok_ref, xs_ref, wg_ref, wu_ref, wd_ref, ys_ref, acc_ref):
    s = pl.program_id(0)
    f = pl.program_id(1)
    ok = ok_ref[s] > 0

    @pl.when(ok & (f == 0))
    def _():
        acc_ref[...] = jnp.zeros_like(acc_ref)

    @pl.when(ok)
    def _():
        x = xs_ref[...]
        a = _silu(_dot(x, wg_ref[0].astype(BF16))) * _dot(x, wu_ref[0].astype(BF16))
        acc_ref[...] += _dot(a.astype(BF16), wd_ref[0].astype(BF16))

    last = f == pl.num_programs(1) - 1

    @pl.when(ok & last)
    def _():
        ys_ref[...] = acc_ref[...].astype(ys_ref.dtype)

    @pl.when(jnp.logical_not(ok) & last)
    def _():
        ys_ref[...] = jnp.zeros_like(ys_ref)


def _moe_experts(tile_e, tile_ok, xs, wg, wu, wd):
    n_rows, d = xs.shape
    ff = wg.shape[2]
    tf = _pick(ff, (512, 256, 128))
    nf = ff // tf

    def f_idx(s, f, ok_ref):
        return jnp.where(ok_ref[s] > 0, f, nf - 1)

    grid_spec = pltpu.PrefetchScalarGridSpec(
        num_scalar_prefetch=2,
        grid=(n_rows // MOE_ROWS, nf),
        in_specs=[pl.BlockSpec((MOE_ROWS, d), lambda s, f, te, ok: (s, 0)),
                  pl.BlockSpec((1, d, tf), lambda s, f, te, ok: (te[s], 0, f_idx(s, f, ok))),
                  pl.BlockSpec((1, d, tf), lambda s, f, te, ok: (te[s], 0, f_idx(s, f, ok))),
                  pl.BlockSpec((1, tf, d), lambda s, f, te, ok: (te[s], f_idx(s, f, ok), 0))],
        out_specs=pl.BlockSpec((MOE_ROWS, d), lambda s, f, te, ok: (s, 0)),
        scratch_shapes=[pltpu.VMEM((MOE_ROWS, d), F32)],
    )
    return pl.pallas_call(
        _expert_kernel,
        grid_spec=grid_spec,
        out_shape=jax.ShapeDtypeStruct((n_rows, d), BF16),
        compiler_params=_cparams(("parallel", "arbitrary")),
        name="moe_experts",
    )(tile_e, tile_ok, xs, wg, wu, wd)


def _combine_kernel(offs_ref, cnts_ref, h_ref, route_ref, gout_ref, ys_ref, o_ref, stage, sems):
    i = pl.program_id(0)
    tm = h_ref.shape[0]
    n_chunks = pl.cdiv(tm, MOE_CHUNK)
    ridx = lax.broadcasted_iota(jnp.int32, (tm, MOE_CHUNK), 1).astype(F32)

    def copy(e, c):
        slot = e * n_chunks + c
        row0 = pl.multiple_of(offs_ref[i * N_EXPERTS + e] + c * MOE_CHUNK, MOE_ALIGN)
        return pltpu.make_async_copy(ys_ref.at[pl.ds(row0, MOE_CHUNK)], stage.at[slot], sems.at[slot])

    for e, c in _chunk_copies(n_chunks):
        @pl.when(c * MOE_CHUNK < cnts_ref[i * N_EXPERTS + e])
        def _(e=e, c=c):
            copy(e, c).start()

    o_ref[...] = h_ref[...]
    for e, c in _chunk_copies(n_chunks):
        @pl.when(c * MOE_CHUNK < cnts_ref[i * N_EXPERTS + e])
        def _(e=e, c=c):
            copy(e, c).wait()
            gate = route_ref[:, e:e + 1]
            sel = route_ref[:, SEL_LANE + e:SEL_LANE + e + 1]
            rank = route_ref[:, RANK_LANE + e:RANK_LANE + e + 1]
            hit = (rank - float(c * MOE_CHUNK) == ridx) & (sel > 0.5)
            back = _dot(jnp.where(hit, 1.0, 0.0).astype(BF16), stage[e * n_chunks + c])
            o_ref[...] += gate * back

    o_ref[...] = _rms(o_ref[...], gout_ref[...])


def _moe_combine(offs, cnts, h, route, ys, g_out):
    t, d = h.shape
    tm = _pick(t, (MOE_TM, 384, 256, 128))
    n_slots = N_EXPERTS * (pl.cdiv(tm, MOE_CHUNK))
    grid_spec = pltpu.PrefetchScalarGridSpec(
        num_scalar_prefetch=2,
        grid=(t // tm,),
        in_specs=[pl.BlockSpec((tm, d), lambda i, *_: (i, 0)),
                  pl.BlockSpec((tm, LANES), lambda i, *_: (i, 0)),
                  pl.BlockSpec((1, d), lambda i, *_: (0, 0)),
                  pl.BlockSpec(memory_space=pl.ANY)],
        out_specs=pl.BlockSpec((tm, d), lambda i, *_: (i, 0)),
        scratch_shapes=[pltpu.VMEM((n_slots, MOE_CHUNK, d), BF16), pltpu.SemaphoreType.DMA((n_slots,))],
    )
    return pl.pallas_call(
        _combine_kernel,
        grid_spec=grid_spec,
        out_shape=jax.ShapeDtypeStruct((t, d), F32),
        compiler_params=_cparams(("arbitrary",)),
        name="moe_combine",
    )(offs, cnts, h, route, g_out, ys)


def _moe(h, g, wr3, wg, wu, wd, g_out):
    t, d = h.shape
    hn, route, selt, rankt, cnt = _moe_route(h, g, wr3)
    nt = cnt.shape[0]
    cnt = cnt[:, 0, :N_EXPERTS].astype(jnp.int32)
    worst = TOP_K * t + nt * N_EXPERTS * (MOE_ALIGN - 1) + N_EXPERTS * (MOE_CHUNK + MOE_ROWS - 1)
    n_row_tiles = -(-worst // MOE_ROWS)
    offs, tile_e, tile_ok = _moe_plan(cnt, n_row_tiles)
    cnts = cnt.reshape(-1)
    xs = _moe_dispatch(offs, cnts, hn, selt, rankt, n_row_tiles * MOE_ROWS)
    ys = _moe_experts(tile_e, tile_ok, xs, wg, wu, wd)
    return _moe_combine(offs, cnts, h, route, ys, g_out)


def _rope_table(lp, rot_dim, theta, width, offset, scale):
    half = rot_dim // 2
    pos = (jnp.arange(lp, dtype=jnp.int32) - FIRST).astype(F32)
    inv = theta ** (-jnp.arange(half, dtype=F32) * 2.0 / rot_dim)
    ang = pos[:, None] * inv[None, :]
    cos, sin = jnp.cos(ang), jnp.sin(ang)
    ones = jnp.ones((lp, 1), F32)
    zeros = jnp.zeros((lp, 1), F32)

    def unit(first, second, fill):
        parts = [jnp.tile(fill, (1, offset)), first, second,
                 jnp.tile(fill, (1, width - offset - rot_dim))]
        return jnp.tile(jnp.concatenate(parts, axis=1), (1, LANES // width))

    c = unit(cos, cos, ones)
    s1 = unit(-sin, jnp.zeros_like(sin), zeros)
    s2 = unit(jnp.zeros_like(sin), sin, zeros)
    return jnp.concatenate([c, s1, s2], axis=1) * scale


def _pad_heads(w, heads, used, width):
    k = w.shape[0]
    w = w.reshape(k, heads, used)
    return jnp.pad(w, ((0, 0), (0, 0), (0, width - used))).reshape(k, heads * width)


def kernel(x, meta_tokens, ln_mix_e, w_in_e, ln_mla_q_e, w_mla_uq_e, ln_mla_kv_e, w_mla_ukv_e,
           w_out_e, ln_ffn_e, w_ffn_gate_e, w_ffn_up_e, w_ffn_down_e, ln_mix_o, w_in_o, b_forget_o,
           diff_lq1_o, diff_lk1_o, diff_lq2_o, diff_lk2_o, diff_subln_o, w_out_o, ln_ffn_o,
           w_router_o, w_moe_gate_o, w_moe_up_o, w_moe_down_o, ln_final):
    b, seq, d = x.shape
    assert d == D_MODEL and seq % TQ == 0
    lp = PAD + seq
    t = b * lp
    depth = 2

    meta = jnp.broadcast_to(meta_tokens.astype(x.dtype)[None], (b, N_META, d))
    h = jnp.concatenate([jnp.zeros((b, FIRST, d), x.dtype), meta, x], axis=1).reshape(t, d)

    row = lambda v: v.reshape(1, -1).astype(F32)
    sb_scale = HEAD_DIM ** -0.5
    tab_mla_q = _rope_table(lp, MLA_ROPE, MLA_THETA, LANES, MLA_NOPE,
                            (MLA_NOPE + MLA_ROPE) ** -0.5 * LOG2E)
    tab_mla_k = _rope_table(lp, MLA_ROPE, MLA_THETA, LANES, MLA_NOPE, 1.0)
    tab_diff_q = _rope_table(lp, ROT_DIM, ROPE_THETA, DIFF_DIM, 0, LOG2E)
    tab_diff_k = _rope_table(lp, ROT_DIM, ROPE_THETA, DIFF_DIM, 0, 1.0)

    for i in range(depth):
        j = i // 2
        if i % 2 == 0:
            w_in = w_in_e[j]
            lat_w = MLA_Q_RANK + MLA_KV_RANK + MLA_ROPE
            w_cat = jnp.concatenate(
                [w_in[:, :SB_W] * sb_scale, w_in[:, SB_W:3 * SB_W + lat_w],
                 jnp.zeros((d, 4 * LANES - lat_w), F32)], axis=1).astype(BF16)
            sb, lat = _even_proj(h, row(ln_mix_e[j]), w_cat)
            sb = sb.reshape(b, lp, -1)
            lat = lat.reshape(b, lp, -1)

            wq = _pad_heads(w_mla_uq_e[j], MLA_HEADS, MLA_NOPE + MLA_ROPE, LANES).astype(BF16)
            ukv = w_mla_ukv_e[j].reshape(MLA_KV_RANK, MLA_HEADS, MLA_NOPE + MLA_V)
            wk_nope = _pad_heads(ukv[:, :, :MLA_NOPE].reshape(MLA_KV_RANK, -1), MLA_HEADS, MLA_NOPE, LANES)
            place = jnp.pad(jnp.eye(MLA_ROPE, dtype=F32), ((0, 0), (MLA_NOPE, LANES - MLA_NOPE - MLA_ROPE)))
            wk_rope = jnp.tile(place, (1, MLA_HEADS))
            wk = jnp.concatenate(
                [wk_nope, wk_rope, jnp.zeros((LANES - MLA_ROPE, MLA_HEADS * LANES), F32)], axis=0).astype(BF16)
            wv = ukv[:, :, MLA_NOPE:].reshape(MLA_KV_RANK, -1).astype(BF16)
            mq, mk, mv = _mla_up(lat, row(ln_mla_q_e[j]), row(ln_mla_kv_e[j]), wq, wk, wv,
                                 tab_mla_q, tab_mla_k)

            a_out = _sb_attn(sb).reshape(t, -1)
            b_out = _mla_attn(mq, mk, mv).reshape(t, -1)
            w_out = w_out_e[j].astype(BF16)
            h = _ffn(h, a_out, b_out, w_out[:SB_W], w_out[SB_W:], row(ln_ffn_e[j]),
                     w_ffn_gate_e[j].astype(BF16), w_ffn_up_e[j].astype(BF16), w_ffn_down_e[j].astype(BF16))
        else:
            lam_init = 0.8 - 0.6 * math.exp(-0.3 * i)
            w_in = w_in_o[j]
            wf = w_in[:, N_ODD_MAIN:]
            w_cat = jnp.concatenate(
                [w_in[:, :DIFF_W] * sb_scale, w_in[:, DIFF_W:3 * DIFF_W],
                 w_in[:, 3 * DIFF_W:3 * DIFF_W + FOX_W] * sb_scale, w_in[:, 3 * DIFF_W + FOX_W:N_ODD_MAIN],
                 wf, jnp.zeros((d, LANES - FOX_HEADS), F32)], axis=1).astype(BF16)
            wft = jnp.pad(wf.T, ((0, F_ROWS - FOX_HEADS), (0, 0))).astype(BF16)
            bf = b_forget_o[j].astype(F32)
            bcol = jnp.pad(bf, (0, LANES - FOX_HEADS)).reshape(1, LANES)
            brow = jnp.pad(bf, (0, F_ROWS - FOX_HEADS)).reshape(F_ROWS, 1)
            main, fcol, frow = _odd_proj(h.reshape(b, lp, d), row(ln_mix_o[j]), w_cat, wft, bcol, brow,
                                         tab_diff_q, tab_diff_k)
            ccol, crow = _fox_cumsum(fcol, frow)
            crow4 = crow[:, :FOX_HEADS].reshape(b, FOX_HEADS // 2, 2, lp)
            lam_vecs = jnp.stack([diff_lq1_o[j], diff_lk1_o[j], diff_lq2_o[j], diff_lk2_o[j]]).astype(F32)
            c_out = _diff_attn(main, lam_vecs, row(diff_subln_o[j]), lam_init).reshape(t, -1)
            d_out = _fox_attn(main, ccol, crow4).reshape(t, -1)
            w_out = w_out_o[j].astype(BF16)
            wr = jnp.pad(w_router_o[j].astype(F32), ((0, 0), (0, LANES - N_EXPERTS)))
            wr3 = jnp.stack(_split3(wr))
            h = _out_proj(h, c_out, d_out, w_out[:DIFF_W], w_out[DIFF_W:])
            h = _moe(h, row(ln_ffn_o[j]), wr3, w_moe_gate_o[j], w_moe_up_o[j], w_moe_down_o[j],
                     row(ln_final))

    return h.reshape(b, lp, d)[:, PAD:]
```

```python
import functools
import math

import jax
import jax.numpy as jnp
from jax import lax
from jax.experimental import pallas as pl
from jax.experimental.pallas import tpu as pltpu

F32 = jnp.float32
BF16 = jnp.bfloat16

D_MODEL = 1024
CHUNK = 64
N_META = 16
EPS = 1e-6
LOG2E = 1.4426950408889634
NEG_INF = -1e30

HEAD_DIM = 64
SB_HEADS = 8
MLA_HEADS = 8
MLA_NOPE = 64
MLA_ROPE = 32
MLA_V = 64
MLA_Q_RANK = 256
MLA_KV_RANK = 128
MLA_THETA = 10000.0
DIFF_HEADS = 4
DIFF_DIM = 64
FOX_HEADS = 8
ROPE_THETA = 500000.0
ROT_DIM = HEAD_DIM // 4
N_EXPERTS = 8
TOP_K = 2

SB_W = SB_HEADS * HEAD_DIM
DIFF_W = DIFF_HEADS * 2 * DIFF_DIM
FOX_W = FOX_HEADS * HEAD_DIM

LANES = 128
PAD = 128
FIRST = PAD - N_META
TQ = 128
KEY_WINDOW = 512
Q_GROUP = 4
SB_Q_GROUP = 4
SB_KEY_WINDOW = 512
F32_EXP2_UNDERFLOW = 160.0
SUFFIX_CHUNK = 256
CHUNK_SHIFT = CHUNK.bit_length() - 1
VMEM_LIMIT = 56 * 1024 * 1024


def _cparams(sem):
    return pltpu.CompilerParams(dimension_semantics=sem, vmem_limit_bytes=VMEM_LIMIT)


def _pick(n, cands):
    for c in cands:
        if n % c == 0:
            return c
    raise ValueError(f"no tile in {cands} divides {n}")


def _rms(x, g):
    return x * lax.rsqrt(jnp.mean(x * x, axis=-1, keepdims=True) + EPS) * g


def _dot(a, b):
    return jnp.dot(a, b, preferred_element_type=F32)


def _dot_nt(a, b):
    return lax.dot_general(a, b, (((1,), (1,)), ((), ())), preferred_element_type=F32)


def _rope_chunk(y, tab, shift):
    c = tab[:, 0:LANES]
    s1 = tab[:, LANES:2 * LANES]
    s2 = tab[:, 2 * LANES:3 * LANES]
    return (y * c + pltpu.roll(y, LANES - shift, 1) * s1 + pltpu.roll(y, shift, 1) * s2)


def _split3(x):
    a = x.astype(BF16)
    r = x - a.astype(F32)
    b = r.astype(BF16)
    c = (r - b.astype(F32)).astype(BF16)
    return a, b, c


def _even_proj_kernel(h_ref, g_ref, w_ref, sb_ref, lat_ref):
    hn = _rms(h_ref[...], g_ref[...]).astype(BF16)
    y = _dot(hn, w_ref[...])
    n_sb = sb_ref.shape[-1]
    sb_ref[:, :SB_W] = (y[:, :SB_W] * LOG2E).astype(BF16)
    sb_ref[:, SB_W:] = y[:, SB_W:n_sb].astype(BF16)
    lat_ref[...] = y[:, n_sb:].astype(BF16)


def _even_proj(h, g, w):
    t, d = h.shape
    n = w.shape[1]
    n_sb = 3 * SB_W
    tm = _pick(t, (512, 384, 256, 128))
    return pl.pallas_call(
        _even_proj_kernel,
        grid=(t // tm,),
        in_specs=[pl.BlockSpec((tm, d), lambda i: (i, 0)),
                  pl.BlockSpec((1, d), lambda i: (0, 0)),
                  pl.BlockSpec((d, n), lambda i: (0, 0))],
        out_specs=[pl.BlockSpec((tm, n_sb), lambda i: (i, 0)),
                   pl.BlockSpec((tm, n - n_sb), lambda i: (i, 0))],
        out_shape=[jax.ShapeDtypeStruct((t, n_sb), BF16),
                   jax.ShapeDtypeStruct((t, n - n_sb), BF16)],
        compiler_params=_cparams(("parallel",)),
        name="even_proj",
    )(h, g, w)


def _mla_up_kernel(lat_ref, gq_ref, gkv_ref, wq_ref, wk_ref, wv_ref, tq_ref, tk_ref,
                   q_ref, k_ref, v_ref):
    lat = lat_ref[0]
    nq = _rms(lat[:, :MLA_Q_RANK].astype(F32), gq_ref[...]).astype(BF16)
    nkv = _rms(lat[:, MLA_Q_RANK:MLA_Q_RANK + MLA_KV_RANK].astype(F32), gkv_ref[...]).astype(BF16)
    yq = _dot(nq, wq_ref[...])
    xk = jnp.concatenate([nkv, lat[:, MLA_Q_RANK + MLA_KV_RANK:]], axis=1)
    yk = _dot(xk, wk_ref[...])
    v_ref[0] = _dot(nkv, wv_ref[...]).astype(BF16)
    tabq = tq_ref[...]
    tabk = tk_ref[...]
    for hd in range(MLA_HEADS):
        sl = slice(hd * LANES, (hd + 1) * LANES)
        q_ref[0, :, sl] = _rope_chunk(yq[:, sl], tabq, MLA_ROPE // 2).astype(BF16)
        k_ref[0, :, sl] = _rope_chunk(yk[:, sl], tabk, MLA_ROPE // 2).astype(BF16)


def _mla_up(lat, gq, gkv, wq, wk, wv, tabq, tabk):
    b, lp, nl = lat.shape
    tm = _pick(lp, (384, 128))
    nqk = MLA_HEADS * LANES
    nv = MLA_HEADS * MLA_V
    full = lambda shp: pl.BlockSpec(shp, lambda bi, i: (0,) * len(shp))
    return pl.pallas_call(
        _mla_up_kernel,
        grid=(b, lp // tm),
        in_specs=[pl.BlockSpec((1, tm, nl), lambda bi, i: (bi, i, 0)),
                  full(gq.shape), full(gkv.shape), full(wq.shape), full(wk.shape), full(wv.shape),
                  pl.BlockSpec((tm, 3 * LANES), lambda bi, i: (i, 0)),
                  pl.BlockSpec((tm, 3 * LANES), lambda bi, i: (i, 0))],
        out_specs=[pl.BlockSpec((1, tm, nqk), lambda bi, i: (bi, i, 0)),
                   pl.BlockSpec((1, tm, nqk), lambda bi, i: (bi, i, 0)),
                   pl.BlockSpec((1, tm, nv), lambda bi, i: (bi, i, 0))],
        out_shape=[jax.ShapeDtypeStruct((b, lp, nqk), BF16),
                   jax.ShapeDtypeStruct((b, lp, nqk), BF16),
                   jax.ShapeDtypeStruct((b, lp, nv), BF16)],
        compiler_params=_cparams(("parallel", "parallel")),
        name="mla_up",
    )(lat, gq, gkv, wq, wk, wv, tabq, tabk)


N_ODD_MAIN = 3 * DIFF_W + 3 * FOX_W
F_ROWS = 16


def _odd_proj_kernel(h_ref, g_ref, w_ref, wft_ref, bcol_ref, brow_ref, tabq_ref, tabk_ref,
                     main_ref, fcol_ref, frow_ref):
    hn = _rms(h_ref[0], g_ref[...]).astype(BF16)
    y = _dot(hn, w_ref[...])
    n_rope = DIFF_W // LANES
    for c in range(2 * n_rope):
        sl = slice(c * LANES, (c + 1) * LANES)
        tab = tabq_ref[...] if c < n_rope else tabk_ref[...]
        main_ref[0, :, sl] = _rope_chunk(y[:, sl], tab, ROT_DIM // 2).astype(BF16)
    fq = slice(3 * DIFF_W, 3 * DIFF_W + FOX_W)
    main_ref[0, :, 2 * DIFF_W:3 * DIFF_W] = y[:, 2 * DIFF_W:3 * DIFF_W].astype(BF16)
    main_ref[0, :, fq] = (y[:, fq] * LOG2E).astype(BF16)
    main_ref[0, :, 3 * DIFF_W + FOX_W:] = y[:, 3 * DIFF_W + FOX_W:N_ODD_MAIN].astype(BF16)
    fcol_ref[0] = y[:, N_ODD_MAIN:] + bcol_ref[...]
    frow_ref[0] = _dot_nt(wft_ref[...], hn) + brow_ref[...]


def _odd_proj(h3, g, w, wft, bcol, brow, tabq, tabk):
    b, lp, d = h3.shape
    tm = _pick(lp, (384, 128))
    full = lambda shp: pl.BlockSpec(shp, lambda bi, i: (0,) * len(shp))
    return pl.pallas_call(
        _odd_proj_kernel,
        grid=(b, lp // tm),
        in_specs=[pl.BlockSpec((1, tm, d), lambda bi, i: (bi, i, 0)),
                  full(g.shape), full(w.shape), full(wft.shape), full(bcol.shape), full(brow.shape),
                  pl.BlockSpec((tm, 3 * LANES), lambda bi, i: (i, 0)),
                  pl.BlockSpec((tm, 3 * LANES), lambda bi, i: (i, 0))],
        out_specs=[pl.BlockSpec((1, tm, N_ODD_MAIN), lambda bi, i: (bi, i, 0)),
                   pl.BlockSpec((1, tm, LANES), lambda bi, i: (bi, i, 0)),
                   pl.BlockSpec((1, F_ROWS, tm), lambda bi, i: (bi, 0, i))],
        out_shape=[jax.ShapeDtypeStruct((b, lp, N_ODD_MAIN), BF16),
                   jax.ShapeDtypeStruct((b, lp, LANES), F32),
                   jax.ShapeDtypeStruct((b, F_ROWS, lp), F32)],
        compiler_params=_cparams(("parallel", "parallel")),
        name="odd_proj",
    )(h3, g, w, wft, bcol, brow, tabq, tabk)


def _log_sigmoid(x):
    return jnp.minimum(x, 0.0) - jnp.log(1.0 + jnp.exp(-jnp.abs(x)))


def _fox_cumsum_kernel(fcol_ref, frow_ref, ccol_ref, crow_ref):
    lp = fcol_ref.shape[1]
    blk = LANES
    r = lax.broadcasted_iota(jnp.int32, (blk, blk), 0)
    c = lax.broadcasted_iota(jnp.int32, (blk, blk), 1)
    tri_l = (c <= r).astype(BF16)
    tri_u = (r <= c).astype(BF16)
    row_id = lax.broadcasted_iota(jnp.int32, (blk, LANES), 0)
    col_id = lax.broadcasted_iota(jnp.int32, (F_ROWS, blk), 1)
    carry_c = jnp.zeros((1, LANES), F32)
    carry_r = jnp.zeros((F_ROWS, 1), F32)
    for j in range(lp // blk):
        sl = slice(j * blk, (j + 1) * blk)
        ls = _log_sigmoid(fcol_ref[0, sl, :])
        ls = jnp.where(row_id + j * blk >= FIRST, ls, 0.0)
        a, b, c3 = _split3(ls)
        cs = _dot(tri_l, a) + _dot(tri_l, b) + _dot(tri_l, c3) + carry_c
        ccol_ref[0, sl, :] = cs * LOG2E
        carry_c = cs[blk - 1:blk, :]
        lr = _log_sigmoid(frow_ref[0, :, sl])
        lr = jnp.where(col_id + j * blk >= FIRST, lr, 0.0)
        a, b, c3 = _split3(lr)
        cr = _dot(a, tri_u) + _dot(b, tri_u) + _dot(c3, tri_u) + carry_r
        crow_ref[0, :, sl] = cr * LOG2E
        carry_r = cr[:, blk - 1:blk]


def _fox_cumsum(fcol, frow):
    b, lp, _ = fcol.shape
    return pl.pallas_call(
        _fox_cumsum_kernel,
        grid=(b,),
        in_specs=[pl.BlockSpec((1, lp, LANES), lambda bi: (bi, 0, 0)),
                  pl.BlockSpec((1, F_ROWS, lp), lambda bi: (bi, 0, 0))],
        out_specs=[pl.BlockSpec((1, lp, LANES), lambda bi: (bi, 0, 0)),
                   pl.BlockSpec((1, F_ROWS, lp), lambda bi: (bi, 0, 0))],
        out_shape=[jax.ShapeDtypeStruct((b, lp, LANES), F32),
                   jax.ShapeDtypeStruct((b, F_ROWS, lp), F32)],
        compiler_params=_cparams(("parallel",)),
        name="fox_cumsum",
    )(fcol, frow)


def _rows(i, n):
    if isinstance(i, int):
        return pl.ds(i * n, n)
    return pl.ds(pl.multiple_of(i * n, n), n)


def _keys(k0, w):
    if isinstance(k0, int):
        return pl.ds(k0, w)
    return pl.ds(pl.multiple_of(k0, TQ), w)


def _visible(kind, ks, qs):
    if kind == "chunk":
        return (ks >> CHUNK_SHIFT) <= (qs >> CHUNK_SHIFT)
    if kind == "causal":
        return ks <= qs
    return ks < qs


def _first_window_mask(kind, q0, w, lim):
    r = lax.broadcasted_iota(jnp.int32, (2 * TQ, w), 0)
    ks = lax.broadcasted_iota(jnp.int32, (2 * TQ, w), 1)
    return _visible(kind, ks, q0 + (r & (TQ - 1))) & (ks >= FIRST) & (ks < lim)


def _diag_window_mask(kind, w):
    r = lax.broadcasted_iota(jnp.int32, (2 * TQ, w), 0)
    ks = lax.broadcasted_iota(jnp.int32, (2 * TQ, w), 1) + (TQ - w)
    return _visible(kind, ks, r & (TQ - 1))


def _span_mask(w, lim):
    ks = lax.broadcasted_iota(jnp.int32, (1, w), 1)
    return (ks >= FIRST) & (ks < lim)


def _lane_halves():
    lane = lax.broadcasted_iota(jnp.int32, (TQ, LANES), 1)
    return lane < (LANES // 2)


def _stack_masked(q):
    lo = _lane_halves()
    z = jnp.zeros_like(q)
    return jnp.concatenate([jnp.where(lo, q, z), jnp.where(lo, z, q)], axis=0)


def _softmax_sweep(kind, w, prep, scores_fn, v_fn, dbias_ref):
    def step(carry, s, v):
        m, acc = carry
        m_new = jnp.maximum(m, jnp.max(s, axis=-1, keepdims=True))
        alpha = jnp.exp2(m - m_new)
        p = jnp.exp2(s - m_new)
        ones = jnp.ones((v.shape[0], LANES), BF16)
        acc = alpha * acc + _dot(p.astype(BF16), jnp.concatenate([v, ones], axis=1))
        return m_new, acc

    def run(qi, mwin):
        g = w // TQ
        ctx = prep(qi)
        q0 = qi * TQ
        pre = qi + 1 - mwin * g
        n0 = w
        carry = (jnp.full((2 * TQ, 1), NEG_INF, F32), jnp.zeros((2 * TQ, 2 * LANES), F32))
        s = scores_fn(ctx, 0, n0)
        if mwin == 0:
            s = jnp.where(_first_window_mask(kind, q0, n0, pre * TQ), s, NEG_INF)
            return step(carry, s, v_fn(0, n0))[1]
        carry = step(carry, s + jnp.where(_span_mask(n0, pre * TQ), 0.0, NEG_INF), v_fn(0, n0))
        for t in range(1, mwin):
            k0 = (pre + (t - 1) * g) * TQ
            carry = step(carry, scores_fn(ctx, k0, w), v_fn(k0, w))
        k0 = q0 + TQ - w
        return step(carry, scores_fn(ctx, k0, w) + dbias_ref[...], v_fn(k0, w))[1]

    return run


def _for_each_query_group(nq, w, q_group, group_fn):
    g = w // TQ
    one = jnp.sign(pl.program_id(0) + 1)
    for mwin in range(-(-nq // g)):
        lo, hi = mwin * g, min((mwin + 1) * g, nq)

        n_groups = (hi - lo) // q_group

        def body(it, _, mwin=mwin, lo=lo):
            qi = lo + q_group * it
            group_fn([qi + j for j in range(q_group)], mwin)
            return 0

        if n_groups:
            lax.fori_loop(0, n_groups * one, body, 0)
        for qi in range(lo + n_groups * q_group, hi):
            group_fn([qi], mwin)


def _for_each_query_block(nq, w, run, emit):
    def group_fn(qis, mwin):
        results = [run(qi, mwin) for qi in qis]
        for qi, res in zip(qis, results):
            emit(qi, res)

    _for_each_query_group(nq, w, Q_GROUP, group_fn)


def _key_window(lp, limit=None):
    return min(KEY_WINDOW if limit is None else limit, lp)


def _merge_heads(top, bottom):
    return jnp.where(_lane_halves(), top, bottom)


def _attn_call(kernel, name, b, lp, n_blocks, in_specs, args, n_masks=1, window=None):
    w = _key_window(lp, window)
    return pl.pallas_call(
        kernel,
        grid=(b, n_blocks),
        in_specs=in_specs,
        out_specs=pl.BlockSpec((1, lp, LANES), lambda bi, hp: (bi, 0, hp)),
        out_shape=jax.ShapeDtypeStruct((b, lp, n_blocks * LANES), BF16),
        scratch_shapes=[pltpu.VMEM((2 * TQ, w), F32)] * n_masks,
        compiler_params=_cparams(("parallel", "parallel")),
        name=name,
    )(*args)


def _normalised(acc):
    return acc[:, :LANES] / acc[:, LANES:]


def _col_spec(lp, off):
    return pl.BlockSpec((1, lp, LANES), lambda bi, hp: (bi, 0, off + hp))


def _mla_attn_kernel(q_ref, k_ref, v_ref, o_ref, dbias_ref):
    lp = q_ref.shape[1]
    w = _key_window(lp)
    dbias_ref[...] = jnp.where(_diag_window_mask("chunk", w), 0.0, NEG_INF)

    def prep(qi):
        return q_ref[0, _rows(qi, TQ), 0:LANES], q_ref[0, _rows(qi, TQ), LANES:2 * LANES]

    def scores(ctx, k0, n):
        qa, qb = ctx
        ka = k_ref[0, _keys(k0, n), 0:LANES]
        kb = k_ref[0, _keys(k0, n), LANES:2 * LANES]
        return jnp.concatenate([_dot_nt(qa, ka), _dot_nt(qb, kb)], axis=0)

    run = _softmax_sweep("chunk", w, prep, scores, lambda k0, n: v_ref[0, _keys(k0, n), :], dbias_ref)

    def emit(qi, acc):
        o = _normalised(acc)
        o_ref[0, _rows(qi, TQ), :] = _merge_heads(o[:TQ], o[TQ:]).astype(o_ref.dtype)

    _for_each_query_block(lp // TQ, w, run, emit)


def _mla_attn(q, k, v):
    b, lp, _ = q.shape
    wide = lambda: pl.BlockSpec((1, lp, 2 * LANES), lambda bi, hp: (bi, 0, hp))
    return _attn_call(_mla_attn_kernel, "mla_attn", b, lp, MLA_HEADS // 2,
                      [wide(), wide(), _col_spec(lp, 0)], (q, k, v))


def _diff_attn_kernel(lam_init, x_q, x_k, x_v, lam_ref, gsub_ref, o_ref, dbias_ref):
    lp = x_q.shape[1]
    w = _key_window(lp)
    dbias_ref[...] = jnp.where(_diag_window_mask("chunk", w), 0.0, NEG_INF)
    lv = lam_ref[...]
    lam = (jnp.exp(jnp.sum(lv[0:1] * lv[1:2], axis=-1, keepdims=True))
           - jnp.exp(jnp.sum(lv[2:3] * lv[3:4], axis=-1, keepdims=True)) + lam_init)
    gsub = gsub_ref[...]

    def prep(qi):
        return _stack_masked(x_q[0, _rows(qi, TQ), :])

    def scores(qs, k0, n):
        return _dot_nt(qs, x_k[0, _keys(k0, n), :])

    run = _softmax_sweep("chunk", w, prep, scores, lambda k0, n: x_v[0, _keys(k0, n), :], dbias_ref)

    def emit(qi, acc):
        o = _normalised(acc)
        o = o[:TQ] - lam * o[TQ:]
        o_ref[0, _rows(qi, TQ), :] = (_rms(o, gsub) * (1.0 - lam_init)).astype(o_ref.dtype)

    _for_each_query_block(lp // TQ, w, run, emit)


def _diff_attn(main, lam_vecs, gsub, lam_init):
    b, lp, _ = main.shape
    nb = DIFF_W // LANES
    full = lambda shp: pl.BlockSpec(shp, lambda bi, hp: (0,) * len(shp))
    return _attn_call(functools.partial(_diff_attn_kernel, lam_init), "diff_attn", b, lp, DIFF_HEADS,
                      [_col_spec(lp, 0), _col_spec(lp, nb), _col_spec(lp, 2 * nb),
                       full(lam_vecs.shape), full(gsub.shape)],
                      (main, main, main, lam_vecs, gsub))


def _fox_attn_kernel(x_q, x_k, x_v, ccol_ref, crow_ref, o_ref, dbias_ref):
    lp = x_q.shape[1]
    w = _key_window(lp)
    dbias_ref[...] = jnp.where(_diag_window_mask("causal", w), 0.0, NEG_INF)
    hp = pl.program_id(1)
    lane = lax.broadcasted_iota(jnp.int32, (TQ, LANES), 1)

    def prep(qi):
        cc = ccol_ref[0, _rows(qi, TQ), :]
        cq0 = jnp.sum(jnp.where(lane == 2 * hp, cc, 0.0), axis=-1, keepdims=True)
        cq1 = jnp.sum(jnp.where(lane == 2 * hp + 1, cc, 0.0), axis=-1, keepdims=True)
        return _stack_masked(x_q[0, _rows(qi, TQ), :]), cq0, cq1

    def scores(ctx, k0, n):
        qs, cq0, cq1 = ctx
        s = _dot_nt(qs, x_k[0, _keys(k0, n), :])
        ck = crow_ref[0, 0, :, _keys(k0, n)]
        return s + jnp.concatenate([cq0 - ck[0:1, :], cq1 - ck[1:2, :]], axis=0)

    run = _softmax_sweep("causal", w, prep, scores, lambda k0, n: x_v[0, _keys(k0, n), :], dbias_ref)

    def emit(qi, acc):
        o = _normalised(acc)
        o_ref[0, _rows(qi, TQ), :] = _merge_heads(o[:TQ], o[TQ:]).astype(o_ref.dtype)

    _for_each_query_block(lp // TQ, w, run, emit)


def _fox_attn(main, ccol, crow4):
    b, lp, _ = main.shape
    off = 3 * DIFF_W // LANES
    nb = FOX_W // LANES
    return _attn_call(_fox_attn_kernel, "fox_attn", b, lp, nb,
                      [_col_spec(lp, off), _col_spec(lp, off + nb), _col_spec(lp, off + 2 * nb),
                       pl.BlockSpec((1, lp, LANES), lambda bi, hp: (bi, 0, 0)),
                       pl.BlockSpec((1, 1, 2, lp), lambda bi, hp: (bi, hp, 0, 0))],
                      (main, main, main, ccol, crow4))


def _sb_attn_kernel(x_q, x_k, x_v, o_ref, dkeep_ref, dbias_ref):
    lp = x_q.shape[1]
    w = _key_window(lp, SB_KEY_WINDOW)
    g = w // TQ
    cs = SUFFIX_CHUNK if w % SUFFIX_CHUNK == 0 else TQ
    kr = lax.broadcasted_iota(jnp.int32, (cs, cs), 0)
    kc = lax.broadcasted_iota(jnp.int32, (cs, cs), 1)
    later = jnp.where(kr > kc, 1.0, 0.0).astype(BF16)
    dvis = _diag_window_mask("strict", w)
    dkeep_ref[...] = jnp.where(dvis, 1.0, 0.0)
    dbias_ref[...] = jnp.where(dvis, 0.0, NEG_INF)

    def step(carry, qs, k0, keep=None, bias=None, vis=None):
        rsum, acc = carry
        z = _dot_nt(qs, x_k[0, _keys(k0, w), :])
        cost = jnp.maximum(z, 0.0) + jnp.log2(1.0 + jnp.exp2(-jnp.abs(z)))
        if vis is not None:
            cost = jnp.where(vis, cost, 0.0)
        if keep is not None:
            cost = cost * keep
        cb = cost.astype(BF16)
        parts = []
        for c in reversed(range(w // cs)):
            sl = slice(c * cs, (c + 1) * cs)
            inner = _dot(cb[:, sl], later)
            parts.append(inner + rsum)
            rsum = rsum + inner[:, 0:1] + cb[:, c * cs:c * cs + 1].astype(F32)
        arg = z - cost - jnp.concatenate(parts[::-1], axis=1)
        if bias is not None:
            arg = arg + bias
        wgt = jnp.exp2(arg)
        if vis is not None:
            wgt = jnp.where(vis, wgt, 0.0)
        acc = acc + _dot(wgt.astype(BF16), x_v[0, _keys(k0, w), :])
        return rsum, acc

    def knorm(j, best):
        kk = x_k[0, _rows(j, TQ), :].astype(F32)
        return jnp.maximum(best, jnp.sum(kk * kk, axis=-1, keepdims=True))

    kn2 = jnp.max(lax.fori_loop(0, lp // TQ, knorm, jnp.zeros((TQ, 1), F32)), axis=0, keepdims=True)

    def emit(qi, acc):
        o_ref[0, _rows(qi, TQ), :] = _merge_heads(acc[:TQ], acc[TQ:]).astype(o_ref.dtype)

    zero = (jnp.zeros((2 * TQ, 1), F32), jnp.zeros((2 * TQ, LANES), F32))

    def near_group(qis):
        for qi in qis:
            qs = _stack_masked(x_q[0, _rows(qi, TQ), :])
            vis = _first_window_mask("strict", qi * TQ, w, (qi + 1) * TQ)
            emit(qi, step(zero, qs, 0, vis=vis)[1])

    def far_group(qis):
        qss = [_stack_masked(x_q[0, _rows(qi, TQ), :]) for qi in qis]
        mwins = [qi // g for qi in qis]
        pres = [qi + 1 - mwin * g for qi, mwin in zip(qis, mwins)]
        carries = [step(zero, qs, qi * TQ + TQ - w, keep=dkeep_ref[...], bias=dbias_ref[...])
                   for qi, qs in zip(qis, qss)]

        def rest(cs):
            out = []
            for qs, mwin, pre, carry in zip(qss, mwins, pres, cs):
                def mid(it, c, qs=qs, mwin=mwin, pre=pre):
                    return step(c, qs, (pre + (mwin - 2 - it) * g) * TQ)

                carry = lax.fori_loop(0, mwin - 1, mid, carry)
                span = _span_mask(w, pre * TQ)
                out.append(step(carry, qs, 0, keep=jnp.where(span, 1.0, 0.0),
                                bias=jnp.where(span, 0.0, NEG_INF)))
            return tuple(out)

        slack = [jnp.max(jnp.sqrt(jnp.sum(jnp.square(qs.astype(F32)), axis=-1, keepdims=True) * kn2)
                         - carry[0]) for qs, carry in zip(qss, carries)]
        live = functools.reduce(jnp.maximum, slack) > -F32_EXP2_UNDERFLOW
        carries = lax.cond(live, rest, lambda cs: cs, tuple(carries))
        for qi, carry in zip(qis, carries):
            emit(qi, carry[1])

    def sweep(lo, hi, group):
        n_groups = (hi - lo) // SB_Q_GROUP

        def body(it, _):
            qi = lo + SB_Q_GROUP * it
            group([qi + j for j in range(SB_Q_GROUP)])
            return 0

        lax.fori_loop(0, n_groups, body, 0)
        for qi in range(lo + n_groups * SB_Q_GROUP, hi):
            group([qi])

    nq = lp // TQ
    sweep(0, min(g, nq), near_group)
    sweep(min(g, nq), nq, far_group)


def _sb_attn(sb):
    b, lp, _ = sb.shape
    nb = SB_W // LANES
    return _attn_call(_sb_attn_kernel, "sb_attn", b, lp, nb,
                      [_col_spec(lp, 0), _col_spec(lp, nb), _col_spec(lp, 2 * nb)],
                      (sb, sb, sb), n_masks=2, window=SB_KEY_WINDOW)


def _silu(x):
    return x / (1.0 + jnp.exp(-x))


def _mixer_residual(h_ref, a_ref, b_ref, wa_ref, wb_ref):
    return h_ref[...] + _dot(a_ref[...], wa_ref[...]) + _dot(b_ref[...], wb_ref[...])


def _out_proj_kernel(h_ref, a_ref, b_ref, wa_ref, wb_ref, o_ref):
    o_ref[...] = _mixer_residual(h_ref, a_ref, b_ref, wa_ref, wb_ref)


def _out_proj(h, mix_a, mix_b, wa, wb):
    t, d = h.shape
    tm = _pick(t, (512, 384, 256, 128))
    na, nb = mix_a.shape[1], mix_b.shape[1]
    return pl.pallas_call(
        _out_proj_kernel,
        grid=(t // tm,),
        in_specs=[pl.BlockSpec((tm, d), lambda i: (i, 0)),
                  pl.BlockSpec((tm, na), lambda i: (i, 0)),
                  pl.BlockSpec((tm, nb), lambda i: (i, 0)),
                  pl.BlockSpec((na, d), lambda i: (0, 0)),
                  pl.BlockSpec((nb, d), lambda i: (0, 0))],
        out_specs=pl.BlockSpec((tm, d), lambda i: (i, 0)),
        out_shape=jax.ShapeDtypeStruct((t, d), F32),
        compiler_params=_cparams(("parallel",)),
        name="out_proj",
    )(h, mix_a, mix_b, wa, wb)


def _ffn_kernel(h_ref, a_ref, b_ref, wa_ref, wb_ref, g_ref, wg_ref, wu_ref, wd_ref, o_ref, hn_ref, acc_ref):
    f = pl.program_id(1)

    @pl.when(f == 0)
    def _():
        hmid = _mixer_residual(h_ref, a_ref, b_ref, wa_ref, wb_ref)
        hn_ref[...] = _rms(hmid, g_ref[...]).astype(BF16)
        acc_ref[...] = hmid

    hn = hn_ref[...]
    a = _silu(_dot(hn, wg_ref[...])) * _dot(hn, wu_ref[...])
    acc_ref[...] += _dot(a.astype(BF16), wd_ref[...])

    @pl.when(f == pl.num_programs(1) - 1)
    def _():
        o_ref[...] = acc_ref[...]


def _ffn(h, mix_a, mix_b, wa, wb, g, wg, wu, wd):
    t, d = h.shape
    ff = wg.shape[1]
    na, nb = mix_a.shape[1], mix_b.shape[1]
    tm = _pick(t, (512, 384, 256, 128))
    tf = _pick(ff, (1408, 512, 256, 128))
    return pl.pallas_call(
        _ffn_kernel,
        grid=(t // tm, ff // tf),
        in_specs=[pl.BlockSpec((tm, d), lambda i, f: (i, 0)),
                  pl.BlockSpec((tm, na), lambda i, f: (i, 0)),
                  pl.BlockSpec((tm, nb), lambda i, f: (i, 0)),
                  pl.BlockSpec((na, d), lambda i, f: (0, 0)),
                  pl.BlockSpec((nb, d), lambda i, f: (0, 0)),
                  pl.BlockSpec((1, d), lambda i, f: (0, 0)),
                  pl.BlockSpec((d, tf), lambda i, f: (0, f)),
                  pl.BlockSpec((d, tf), lambda i, f: (0, f)),
                  pl.BlockSpec((tf, d), lambda i, f: (f, 0))],
        out_specs=pl.BlockSpec((tm, d), lambda i, f: (i, 0)),
        out_shape=jax.ShapeDtypeStruct((t, d), F32),
        scratch_shapes=[pltpu.VMEM((tm, d), BF16), pltpu.VMEM((tm, d), F32)],
        compiler_params=_cparams(("parallel", "arbitrary")),
        name="ffn",
    )(h, mix_a, mix_b, wa, wb, g, wg, wu, wd)


def _router_gates(hn, wr_ref):
    a, b, c = _split3(hn)
    w_ab = jnp.concatenate([wr_ref[0], wr_ref[1]], axis=1)
    ya = _dot(a, jnp.concatenate([w_ab, wr_ref[2]], axis=1))
    yb = _dot(b, w_ab)
    logits = (ya[:, :LANES] + ya[:, LANES:2 * LANES] + yb[:, :LANES] + ya[:, 2 * LANES:] + yb[:, LANES:]
              + _dot(c, wr_ref[0]))
    lane = lax.broadcasted_iota(jnp.int32, logits.shape, 1).astype(F32)
    logits = jnp.where(lane < N_EXPERTS, logits, -jnp.inf)
    v1 = jnp.max(logits, axis=-1, keepdims=True)
    i1 = jnp.min(jnp.where(logits == v1, lane, float(LANES)), axis=-1, keepdims=True)
    rest = jnp.where(lane == i1, -jnp.inf, logits)
    v2 = jnp.max(rest, axis=-1, keepdims=True)
    i2 = jnp.min(jnp.where(rest == v2, lane, float(LANES)), axis=-1, keepdims=True)
    e2 = jnp.exp(v2 - v1)
    den = 1.0 + e2
    gate = jnp.where(lane == i1, 1.0 / den, 0.0) + jnp.where(lane == i2, e2 / den, 0.0)
    sel = jnp.where((lane == i1) | (lane == i2), 1.0, 0.0)
    return gate, sel


MOE_TM = 512
MOE_CHUNK = 256
MOE_ALIGN = 16
MOE_ROWS = 1024
SEL_LANE = N_EXPERTS
RANK_LANE = 2 * N_EXPERTS
T_ROWS = 16


def _route_kernel(h_ref, g_ref, wr_ref, hn_ref, route_ref, selt_ref, rankt_ref, cnt_ref):
    tm = h_ref.shape[0]
    hn = _rms(h_ref[...], g_ref[...])
    hn_ref[...] = hn.astype(BF16)
    gate, sel = _router_gates(hn, wr_ref)
    selb = sel.astype(BF16)
    r = lax.broadcasted_iota(jnp.int32, (tm, tm), 0)
    c = lax.broadcasted_iota(jnp.int32, (tm, tm), 1)
    rank = _dot(jnp.where(c < r, 1.0, 0.0).astype(BF16), selb)
    route_ref[...] = gate + pltpu.roll(sel, SEL_LANE, 1) + pltpu.roll(rank, RANK_LANE, 1)
    er = lax.broadcasted_iota(jnp.int32, (T_ROWS, LANES), 0)
    el = lax.broadcasted_iota(jnp.int32, (T_ROWS, LANES), 1)
    pick = jnp.where((er == el) & (er < N_EXPERTS), 1.0, 0.0).astype(BF16)
    selt = _dot_nt(pick, selb)
    selt_ref[0] = selt
    rankt_ref[0] = _dot(selt.astype(BF16), jnp.where(r < c, 1.0, 0.0).astype(BF16))
    cnt_ref[0] = jnp.sum(sel, axis=0, keepdims=True)


def _moe_route(h, g, wr3):
    t, d = h.shape
    tm = _pick(t, (MOE_TM, 384, 256, 128))
    nt = t // tm
    return pl.pallas_call(
        _route_kernel,
        grid=(nt,),
        in_specs=[pl.BlockSpec((tm, d), lambda i: (i, 0)),
                  pl.BlockSpec((1, d), lambda i: (0, 0)),
                  pl.BlockSpec((3, d, LANES), lambda i: (0, 0, 0))],
        out_specs=[pl.BlockSpec((tm, d), lambda i: (i, 0)),
                   pl.BlockSpec((tm, LANES), lambda i: (i, 0)),
                   pl.BlockSpec((1, T_ROWS, tm), lambda i: (i, 0, 0)),
                   pl.BlockSpec((1, T_ROWS, tm), lambda i: (i, 0, 0)),
                   pl.BlockSpec((1, 1, LANES), lambda i: (i, 0, 0))],
        out_shape=[jax.ShapeDtypeStruct((t, d), BF16),
                   jax.ShapeDtypeStruct((t, LANES), F32),
                   jax.ShapeDtypeStruct((nt, T_ROWS, tm), F32),
                   jax.ShapeDtypeStruct((nt, T_ROWS, tm), F32),
                   jax.ShapeDtypeStruct((nt, 1, LANES), F32)],
        compiler_params=_cparams(("parallel",)),
        name="moe_route",
    )(h, g, wr3)


def _moe_plan(cnt, n_row_tiles):
    cnt_al = (cnt + MOE_ALIGN - 1) // MOE_ALIGN * MOE_ALIGN
    tot = jnp.sum(cnt_al, axis=0)
    cap = (tot + MOE_CHUNK + MOE_ROWS - 1) // MOE_ROWS * MOE_ROWS
    base = jnp.cumsum(cap) - cap
    offs = base[None, :] + jnp.cumsum(cnt_al, axis=0) - cnt_al
    starts = jnp.arange(n_row_tiles, dtype=jnp.int32) * MOE_ROWS
    tile_e = jnp.clip(jnp.sum(starts[:, None] >= base[None, :], axis=1) - 1, 0, N_EXPERTS - 1)
    tile_ok = starts < (base + tot)[tile_e]
    return (offs.reshape(-1).astype(jnp.int32), tile_e.astype(jnp.int32), tile_ok.astype(jnp.int32))


def _chunk_copies(n_chunks):
    return [(e, c) for e in range(N_EXPERTS) for c in range(n_chunks)]


def _dispatch_kernel(offs_ref, cnts_ref, hn_ref, selt_ref, rankt_ref, xs_in_ref, xs_ref, stage, sems):
    del xs_in_ref
    i = pl.program_id(0)
    tm = hn_ref.shape[0]
    hn = hn_ref[...]
    ridx = lax.broadcasted_iota(jnp.int32, (MOE_CHUNK, tm), 0).astype(F32)

    def copy(e, c):
        slot = e * (pl.cdiv(tm, MOE_CHUNK)) + c
        row0 = pl.multiple_of(offs_ref[i * N_EXPERTS + e] + c * MOE_CHUNK, MOE_ALIGN)
        return pltpu.make_async_copy(stage.at[slot], xs_ref.at[pl.ds(row0, MOE_CHUNK)], sems.at[slot])

    for e, c in _chunk_copies(pl.cdiv(tm, MOE_CHUNK)):
        @pl.when(c * MOE_CHUNK < cnts_ref[i * N_EXPERTS + e])
        def _(e=e, c=c):
            hit = (rankt_ref[0, e:e + 1, :] - float(c * MOE_CHUNK) == ridx) & (selt_ref[0, e:e + 1, :] > 0.5)
            rows = _dot(jnp.where(hit, 1.0, 0.0).astype(BF16), hn)
            stage[e * (pl.cdiv(tm, MOE_CHUNK)) + c] = rows.astype(BF16)
            copy(e, c).start()

    for e, c in _chunk_copies(pl.cdiv(tm, MOE_CHUNK)):
        @pl.when(c * MOE_CHUNK < cnts_ref[i * N_EXPERTS + e])
        def _(e=e, c=c):
            copy(e, c).wait()


def _moe_dispatch(offs, cnts, hn, selt, rankt, n_rows):
    t, d = hn.shape
    nt, _, tm = selt.shape
    n_slots = N_EXPERTS * (pl.cdiv(tm, MOE_CHUNK))
    xs0 = jnp.zeros((n_rows, d), BF16)
    grid_spec = pltpu.PrefetchScalarGridSpec(
        num_scalar_prefetch=2,
        grid=(nt,),
        in_specs=[pl.BlockSpec((tm, d), lambda i, *_: (i, 0)),
                  pl.BlockSpec((1, T_ROWS, tm), lambda i, *_: (i, 0, 0)),
                  pl.BlockSpec((1, T_ROWS, tm), lambda i, *_: (i, 0, 0)),
                  pl.BlockSpec(memory_space=pl.ANY)],
        out_specs=pl.BlockSpec(memory_space=pl.ANY),
        scratch_shapes=[pltpu.VMEM((n_slots, MOE_CHUNK, d), BF16), pltpu.SemaphoreType.DMA((n_slots,))],
    )
    return pl.pallas_call(
        _dispatch_kernel,
        grid_spec=grid_spec,
        out_shape=jax.ShapeDtypeStruct((n_rows, d), BF16),
        input_output_aliases={5: 0},
        compiler_params=_cparams(("arbitrary",)),
        name="moe_dispatch",
    )(offs, cnts, hn, selt, rankt, xs0)


def _expert_kernel(te_ref, ok_ref, xs_ref, wg_ref, wu_ref, wd_ref, ys_ref, acc_ref):
    s = pl.program_id(0)
    f = pl.program_id(1)
    ok = ok_ref[s] > 0

    @pl.when(ok & (f == 0))
    def _():
        acc_ref[...] = jnp.zeros_like(acc_ref)

    @pl.when(ok)
    def _():
        x = xs_ref[...]
        a = _silu(_dot(x, wg_ref[0].astype(BF16))) * _dot(x, wu_ref[0].astype(BF16))
        acc_ref[...] += _dot(a.astype(BF16), wd_ref[0].astype(BF16))

    last = f == pl.num_programs(1) - 1

    @pl.when(ok & last)
    def _():
        ys_ref[...] = acc_ref[...].astype(ys_ref.dtype)

    @pl.when(jnp.logical_not(ok) & last)
    def _():
        ys_ref[...] = jnp.zeros_like(ys_ref)


def _moe_experts(tile_e, tile_ok, xs, wg, wu, wd):
    n_rows, d = xs.shape
    ff = wg.shape[2]
    tf = _pick(ff, (512, 256, 128))
    nf = ff // tf

    def f_idx(s, f, ok_ref):
        return jnp.where(ok_ref[s] > 0, f, nf - 1)

    grid_spec = pltpu.PrefetchScalarGridSpec(
        num_scalar_prefetch=2,
        grid=(n_rows // MOE_ROWS, nf),
        in_specs=[pl.BlockSpec((MOE_ROWS, d), lambda s, f, te, ok: (s, 0)),
                  pl.BlockSpec((1, d, tf), lambda s, f, te, ok: (te[s], 0, f_idx(s, f, ok))),
                  pl.BlockSpec((1, d, tf), lambda s, f, te, ok: (te[s], 0, f_idx(s, f, ok))),
                  pl.BlockSpec((1, tf, d), lambda s, f, te, ok: (te[s], f_idx(s, f, ok), 0))],
        out_specs=pl.BlockSpec((MOE_ROWS, d), lambda s, f, te, ok: (s, 0)),
        scratch_shapes=[pltpu.VMEM((MOE_ROWS, d), F32)],
    )
    return pl.pallas_call(
        _expert_kernel,
        grid_spec=grid_spec,
        out_shape=jax.ShapeDtypeStruct((n_rows, d), BF16),
        compiler_params=_cparams(("parallel", "arbitrary")),
        name="moe_experts",
    )(tile_e, tile_ok, xs, wg, wu, wd)


def _combine_kernel(offs_ref, cnts_ref, h_ref, route_ref, gout_ref, ys_ref, o_ref, stage, sems):
    i = pl.program_id(0)
    tm = h_ref.shape[0]
    n_chunks = pl.cdiv(tm, MOE_CHUNK)
    ridx = lax.broadcasted_iota(jnp.int32, (tm, MOE_CHUNK), 1).astype(F32)

    def copy(e, c):
        slot = e * n_chunks + c
        row0 = pl.multiple_of(offs_ref[i * N_EXPERTS + e] + c * MOE_CHUNK, MOE_ALIGN)
        return pltpu.make_async_copy(ys_ref.at[pl.ds(row0, MOE_CHUNK)], stage.at[slot], sems.at[slot])

    for e, c in _chunk_copies(n_chunks):
        @pl.when(c * MOE_CHUNK < cnts_ref[i * N_EXPERTS + e])
        def _(e=e, c=c):
            copy(e, c).start()

    o_ref[...] = h_ref[...]
    for e, c in _chunk_copies(n_chunks):
        @pl.when(c * MOE_CHUNK < cnts_ref[i * N_EXPERTS + e])
        def _(e=e, c=c):
            copy(e, c).wait()
            gate = route_ref[:, e:e + 1]
            sel = route_ref[:, SEL_LANE + e:SEL_LANE + e + 1]
            rank = route_ref[:, RANK_LANE + e:RANK_LANE + e + 1]
            hit = (rank - float(c * MOE_CHUNK) == ridx) & (sel > 0.5)
            back = _dot(jnp.where(hit, 1.0, 0.0).astype(BF16), stage[e * n_chunks + c])
            o_ref[...] += gate * back

    o_ref[...] = _rms(o_ref[...], gout_ref[...])


def _moe_combine(offs, cnts, h, route, ys, g_out):
    t, d = h.shape
    tm = _pick(t, (MOE_TM, 384, 256, 128))
    n_slots = N_EXPERTS * (pl.cdiv(tm, MOE_CHUNK))
    grid_spec = pltpu.PrefetchScalarGridSpec(
        num_scalar_prefetch=2,
        grid=(t // tm,),
        in_specs=[pl.BlockSpec((tm, d), lambda i, *_: (i, 0)),
                  pl.BlockSpec((tm, LANES), lambda i, *_: (i, 0)),
                  pl.BlockSpec((1, d), lambda i, *_: (0, 0)),
                  pl.BlockSpec(memory_space=pl.ANY)],
        out_specs=pl.BlockSpec((tm, d), lambda i, *_: (i, 0)),
        scratch_shapes=[pltpu.VMEM((n_slots, MOE_CHUNK, d), BF16), pltpu.SemaphoreType.DMA((n_slots,))],
    )
    return pl.pallas_call(
        _combine_kernel,
        grid_spec=grid_spec,
        out_shape=jax.ShapeDtypeStruct((t, d), F32),
        compiler_params=_cparams(("arbitrary",)),
        name="moe_combine",
    )(offs, cnts, h, route, g_out, ys)


def _moe(h, g, wr3, wg, wu, wd, g_out):
    t, d = h.shape
    hn, route, selt, rankt, cnt = _moe_route(h, g, wr3)
    nt = cnt.shape[0]
    cnt = cnt[:, 0, :N_EXPERTS].astype(jnp.int32)
    worst = TOP_K * t + nt * N_EXPERTS * (MOE_ALIGN - 1) + N_EXPERTS * (MOE_CHUNK + MOE_ROWS - 1)
    n_row_tiles = -(-worst // MOE_ROWS)
    offs, tile_e, tile_ok = _moe_plan(cnt, n_row_tiles)
    cnts = cnt.reshape(-1)
    xs = _moe_dispatch(offs, cnts, hn, selt, rankt, n_row_tiles * MOE_ROWS)
    ys = _moe_experts(tile_e, tile_ok, xs, wg, wu, wd)
    return _moe_combine(offs, cnts, h, route, ys, g_out)


def _rope_table(lp, rot_dim, theta, width, offset, scale):
    half = rot_dim // 2
    pos = (jnp.arange(lp, dtype=jnp.int32) - FIRST).astype(F32)
    inv = theta ** (-jnp.arange(half, dtype=F32) * 2.0 / rot_dim)
    ang = pos[:, None] * inv[None, :]
    cos, sin = jnp.cos(ang), jnp.sin(ang)
    ones = jnp.ones((lp, 1), F32)
    zeros = jnp.zeros((lp, 1), F32)

    def unit(first, second, fill):
        parts = [jnp.tile(fill, (1, offset)), first, second,
                 jnp.tile(fill, (1, width - offset - rot_dim))]
        return jnp.tile(jnp.concatenate(parts, axis=1), (1, LANES // width))

    c = unit(cos, cos, ones)
    s1 = unit(-sin, jnp.zeros_like(sin), zeros)
    s2 = unit(jnp.zeros_like(sin), sin, zeros)
    return jnp.concatenate([c, s1, s2], axis=1) * scale


def _pad_heads(w, heads, used, width):
    k = w.shape[0]
    w = w.reshape(k, heads, used)
    return jnp.pad(w, ((0, 0), (0, 0), (0, width - used))).reshape(k, heads * width)


def kernel(x, meta_tokens, ln_mix_e, w_in_e, ln_mla_q_e, w_mla_uq_e, ln_mla_kv_e, w_mla_ukv_e,
           w_out_e, ln_ffn_e, w_ffn_gate_e, w_ffn_up_e, w_ffn_down_e, ln_mix_o, w_in_o, b_forget_o,
           diff_lq1_o, diff_lk1_o, diff_lq2_o, diff_lk2_o, diff_subln_o, w_out_o, ln_ffn_o,
           w_router_o, w_moe_gate_o, w_moe_up_o, w_moe_down_o, ln_final):
    b, seq, d = x.shape
    assert d == D_MODEL and seq % TQ == 0
    lp = PAD + seq
    t = b * lp
    depth = 2

    meta = jnp.broadcast_to(meta_tokens.astype(x.dtype)[None], (b, N_META, d))
    h = jnp.concatenate([jnp.zeros((b, FIRST, d), x.dtype), meta, x], axis=1).reshape(t, d)

    row = lambda v: v.reshape(1, -1).astype(F32)
    sb_scale = HEAD_DIM ** -0.5
    tab_mla_q = _rope_table(lp, MLA_ROPE, MLA_THETA, LANES, MLA_NOPE,
                            (MLA_NOPE + MLA_ROPE) ** -0.5 * LOG2E)
    tab_mla_k = _rope_table(lp, MLA_ROPE, MLA_THETA, LANES, MLA_NOPE, 1.0)
    tab_diff_q = _rope_table(lp, ROT_DIM, ROPE_THETA, DIFF_DIM, 0, LOG2E)
    tab_diff_k = _rope_table(lp, ROT_DIM, ROPE_THETA, DIFF_DIM, 0, 1.0)

    for i in range(depth):
        j = i // 2
        if i % 2 == 0:
            w_in = w_in_e[j]
            lat_w = MLA_Q_RANK + MLA_KV_RANK + MLA_ROPE
            w_cat = jnp.concatenate(
                [w_in[:, :SB_W] * sb_scale, w_in[:, SB_W:3 * SB_W + lat_w],
                 jnp.zeros((d, 4 * LANES - lat_w), F32)], axis=1).astype(BF16)
            sb, lat = _even_proj(h, row(ln_mix_e[j]), w_cat)
            sb = sb.reshape(b, lp, -1)
            lat = lat.reshape(b, lp, -1)

            wq = _pad_heads(w_mla_uq_e[j], MLA_HEADS, MLA_NOPE + MLA_ROPE, LANES).astype(BF16)
            ukv = w_mla_ukv_e[j].reshape(MLA_KV_RANK, MLA_HEADS, MLA_NOPE + MLA_V)
            wk_nope = _pad_heads(ukv[:, :, :MLA_NOPE].reshape(MLA_KV_RANK, -1), MLA_HEADS, MLA_NOPE, LANES)
            place = jnp.pad(jnp.eye(MLA_ROPE, dtype=F32), ((0, 0), (MLA_NOPE, LANES - MLA_NOPE - MLA_ROPE)))
            wk_rope = jnp.tile(place, (1, MLA_HEADS))
            wk = jnp.concatenate(
                [wk_nope, wk_rope, jnp.zeros((LANES - MLA_ROPE, MLA_HEADS * LANES), F32)], axis=0).astype(BF16)
            wv = ukv[:, :, MLA_NOPE:].reshape(MLA_KV_RANK, -1).astype(BF16)
            mq, mk, mv = _mla_up(lat, row(ln_mla_q_e[j]), row(ln_mla_kv_e[j]), wq, wk, wv,
                                 tab_mla_q, tab_mla_k)

            a_out = _sb_attn(sb).reshape(t, -1)
            b_out = _mla_attn(mq, mk, mv).reshape(t, -1)
            w_out = w_out_e[j].astype(BF16)
            h = _ffn(h, a_out, b_out, w_out[:SB_W], w_out[SB_W:], row(ln_ffn_e[j]),
                     w_ffn_gate_e[j].astype(BF16), w_ffn_up_e[j].astype(BF16), w_ffn_down_e[j].astype(BF16))
        else:
            lam_init = 0.8 - 0.6 * math.exp(-0.3 * i)
            w_in = w_in_o[j]
            wf = w_in[:, N_ODD_MAIN:]
            w_cat = jnp.concatenate(
                [w_in[:, :DIFF_W] * sb_scale, w_in[:, DIFF_W:3 * DIFF_W],
                 w_in[:, 3 * DIFF_W:3 * DIFF_W + FOX_W] * sb_scale, w_in[:, 3 * DIFF_W + FOX_W:N_ODD_MAIN],
                 wf, jnp.zeros((d, LANES - FOX_HEADS), F32)], axis=1).astype(BF16)
            wft = jnp.pad(wf.T, ((0, F_ROWS - FOX_HEADS), (0, 0))).astype(BF16)
            bf = b_forget_o[j].astype(F32)
            bcol = jnp.pad(bf, (0, LANES - FOX_HEADS)).reshape(1, LANES)
            brow = jnp.pad(bf, (0, F_ROWS - FOX_HEADS)).reshape(F_ROWS, 1)
            main, fcol, frow = _odd_proj(h.reshape(b, lp, d), row(ln_mix_o[j]), w_cat, wft, bcol, brow,
                                         tab_diff_q, tab_diff_k)
            ccol, crow = _fox_cumsum(fcol, frow)
            crow4 = crow[:, :FOX_HEADS].reshape(b, FOX_HEADS // 2, 2, lp)
            lam_vecs = jnp.stack([diff_lq1_o[j], diff_lk1_o[j], diff_lq2_o[j], diff_lk2_o[j]]).astype(F32)
            c_out = _diff_attn(main, lam_vecs, row(diff_subln_o[j]), lam_init).reshape(t, -1)
            d_out = _fox_attn(main, ccol, crow4).reshape(t, -1)
            w_out = w_out_o[j].astype(BF16)
            wr = jnp.pad(w_router_o[j].astype(F32), ((0, 0), (0, LANES - N_EXPERTS)))
            wr3 = jnp.stack(_split3(wr))
            h = _out_proj(h, c_out, d_out, w_out[:DIFF_W], w_out[DIFF_W:])
            h = _moe(h, row(ln_ffn_o[j]), wr3, w_moe_gate_o[j], w_moe_up_o[j], w_moe_down_o[j],
                     row(ln_final))

    return h.reshape(b, lp, d)[:, PAD:]
```
